```python
import functools
import jax
import jax.numpy as jnp
from jax import lax
import numpy as np

D_MODEL = 1024
BATCH = 8
SEQ = 2048
DEPTH = 1
DEC_BATCH = 32
DEC_SEQ = 8
PAST_LEN = 16384
PAGE_SIZE = 128

POOL_WIDTH = D_MODEL // 2
POOL_WINDOWS = (2, 4, 8, 16)
N_POOL_GROUPS = len(POOL_WINDOWS)
POOL_GROUP = POOL_WIDTH // N_POOL_GROUPS
POOL_HIST = max(POOL_WINDOWS) - 1
N_HEADS = 8
HEAD_DIM = 64
ATTN_WIDTH = N_HEADS * HEAD_DIM
MOBA_BLOCK = 256
MOBA_TOPK = 3
Q_BLOCK = 128
ATTN_SCALE = HEAD_DIM ** -0.5
MIX_IN = POOL_WIDTH + 3 * ATTN_WIDTH
MIX_WIDTH = POOL_WIDTH + ATTN_WIDTH
N_GROUPS = 4
EXPERTS_PER_GROUP = 8
N_EXPERTS = N_GROUPS * EXPERTS_PER_GROUP
EXPERT_TOPK = 2
D_FF_EXPERT = D_MODEL // 2
ROW_BLOCK = 128
EPS = 1e-6

kernel_name = 'hymba_pool_moba_hmoe_step'


def _rmsnorm(x, g):
    xf = x.astype(jnp.float32)
    y = xf * lax.rsqrt(jnp.mean(xf * xf, axis=-1, keepdims=True) + EPS)
    return (y * g.astype(jnp.float32)).astype(x.dtype)


def _alibi_slopes():
    return jnp.exp2(-8.0 * (jnp.arange(N_HEADS, dtype=jnp.float32) + 1.0) / N_HEADS)


def _pool_mix(u_ext, pos0, w_pool, pool_scale):
    f32 = jnp.float32
    B, Lx, _ = u_ext.shape
    L = Lx - POOL_HIST
    uf = u_ext.astype(f32)
    cz = jnp.concatenate([jnp.zeros((B, 1, POOL_WIDTH), f32), jnp.cumsum(uf, axis=1)], axis=1)
    end = cz[:, POOL_HIST + 1:]
    u = uf[:, POOL_HIST:]
    pos = pos0 + jnp.arange(L)
    groups = []
    for g, w in enumerate(POOL_WINDOWS):
        ch = slice(g * POOL_GROUP, (g + 1) * POOL_GROUP)
        win_sum = end[..., ch] - cz[:, POOL_HIST + 1 - w:POOL_HIST + 1 - w + L, ch]
        count = jnp.minimum(pos + 1, w).astype(f32)[None, :, None]
        groups.append(win_sum / count - u[..., ch])
    pooled = jnp.stack(groups, axis=2)
    y = jnp.einsum('blgc,gcd->blgd', pooled, w_pool.astype(f32)).reshape(B, L, POOL_WIDTH)
    return y * pool_scale.astype(f32)


def _moba_prompt(q, k, v):
    f32 = jnp.float32
    B, T = q.shape[:2]
    S = MOBA_BLOCK
    NB = -(-T // S)
    pad = NB * S - T
    kp = jnp.pad(k.astype(f32), ((0, 0), (0, pad), (0, 0), (0, 0)))
    vp = jnp.pad(v.astype(f32), ((0, 0), (0, pad), (0, 0), (0, 0)))
    kb = kp.reshape(B, NB, S, N_HEADS, HEAD_DIM).transpose(0, 3, 1, 2, 4)
    vb = vp.reshape(B, NB, S, N_HEADS, HEAD_DIM).transpose(0, 3, 1, 2, 4)
    kmean = kb.mean(axis=3)
    ktop = min(MOBA_TOPK, NB - 1)
    slopes = _alibi_slopes()
    nq = T // Q_BLOCK
    qb = q.astype(f32).reshape(B, nq, Q_BLOCK, N_HEADS, HEAD_DIM).transpose(1, 0, 3, 2, 4)
    bi = jnp.arange(B)[:, None, None, None]
    hi = jnp.arange(N_HEADS)[None, :, None, None]
    s_idx = jnp.arange(S)

    def one_block(args):
        qc, c = args
        t = c * Q_BLOCK + jnp.arange(Q_BLOCK)
        cb = (c * Q_BLOCK) // S
        ko = lax.dynamic_index_in_dim(kb, cb, axis=2, keepdims=False)
        vo = lax.dynamic_index_in_dim(vb, cb, axis=2, keepdims=False)
        so = cb * S + s_idx
        lo = jnp.einsum('bhqd,bhsd->bhqs', qc, ko) * ATTN_SCALE - slopes[:, None, None] * (t[:, None] - so[None, :])
        lo = jnp.where(so[None, :] <= t[:, None], lo, -jnp.inf)
        if ktop == 0:
            return jnp.einsum('bhqs,bhsd->bhqd', jax.nn.softmax(lo, axis=-1), vo)
        gate = jnp.einsum('bhqd,bhnd->bhqn', qc, kmean)
        gate = jnp.where(jnp.arange(NB) < cb, gate, -jnp.inf)
        _, sel = lax.top_k(gate, ktop)
        ks = kb[bi, hi, sel]
        vs = vb[bi, hi, sel]
        ps = sel[..., None] * S + s_idx
        ls = jnp.einsum('bhqd,bhqjsd->bhqjs', qc, ks) * ATTN_SCALE - slopes[:, None, None, None] * (t[:, None, None] - ps)
        ls = jnp.where((sel < cb)[..., None], ls, -jnp.inf)
        n_sel = ktop * S
        p = jax.nn.softmax(jnp.concatenate([ls.reshape(B, N_HEADS, Q_BLOCK, n_sel), lo], axis=-1), axis=-1)
        p_sel = p[..., :n_sel].reshape(B, N_HEADS, Q_BLOCK, ktop, S)
        return jnp.einsum('bhqjs,bhqjsd->bhqd', p_sel, vs) + jnp.einsum('bhqs,bhsd->bhqd', p[..., n_sel:], vo)

    out = lax.map(one_block, (qb, jnp.arange(nq)))
    return out.transpose(1, 0, 3, 2, 4).reshape(B, T, ATTN_WIDTH)


def _moba_sample(q, k, v, cache_k, cache_v, page_table, layer):
    f32 = jnp.float32
    B, L = q.shape[:2]
    slopes = _alibi_slopes()
    ppb = MOBA_BLOCK // PAGE_SIZE
    cb = PAST_LEN // MOBA_BLOCK
    n_own = PAST_LEN - cb * MOBA_BLOCK
    ktop = min(MOBA_TOPK, cb)
    t = PAST_LEN + jnp.arange(L)
    qh = q.astype(f32).transpose(0, 2, 1, 3)
    logit_parts, value_parts = [], []
    if ktop > 0:
        past = cache_k[layer, page_table[:, :cb * ppb]].astype(f32)
        kmean = past.reshape(B, cb, MOBA_BLOCK, N_HEADS, HEAD_DIM).mean(axis=2)
        gate = jnp.einsum('bhqd,bnhd->bhqn', qh, kmean)
        _, sel = lax.top_k(gate, ktop)
        bi5 = jnp.arange(B)[:, None, None, None, None]
        hi5 = jnp.arange(N_HEADS)[None, :, None, None, None]
        phys = page_table[bi5, sel[..., None] * ppb + jnp.arange(ppb)]
        n_sel = ktop * MOBA_BLOCK
        ks = cache_k[layer, phys, :, hi5].astype(f32).reshape(B, N_HEADS, L, n_sel, HEAD_DIM)
        vs = cache_v[layer, phys, :, hi5].astype(f32).reshape(B, N_HEADS, L, n_sel, HEAD_DIM)
        ps = (sel[..., None] * MOBA_BLOCK + jnp.arange(MOBA_BLOCK)).reshape(B, N_HEADS, L, n_sel)
        ls = jnp.einsum('bhqd,bhqsd->bhqs', qh, ks) * ATTN_SCALE - slopes[:, None, None] * (t[:, None] - ps)
        logit_parts.append(ls)
        value_parts.append(('bhqs,bhqsd->bhqd', vs))
    if n_own > 0:
        own_pages = page_table[:, cb * ppb:cb * ppb + n_own // PAGE_SIZE]
        ko = cache_k[layer, own_pages].astype(f32).reshape(B, n_own, N_HEADS, HEAD_DIM)
        vo = cache_v[layer, own_pages].astype(f32).reshape(B, n_own, N_HEADS, HEAD_DIM)
        po = cb * MOBA_BLOCK + jnp.arange(n_own)
        lo = jnp.einsum('bhqd,bshd->bhqs', qh, ko) * ATTN_SCALE - slopes[:, None, None] * (t[:, None] - po[None, :])
        logit_parts.append(lo)
        value_parts.append(('bhqs,bshd->bhqd', vo))
    kf = k.astype(f32)
    ln = jnp.einsum('bhqd,bshd->bhqs', qh, kf) * ATTN_SCALE - slopes[:, None, None] * (t[:, None] - t[None, :])
    ln = jnp.where(t[None, :] <= t[:, None], ln, -jnp.inf)
    logit_parts.append(ln)
    value_parts.append(('bhqs,bshd->bhqd', v.astype(f32)))
    p = jax.nn.softmax(jnp.concatenate(logit_parts, axis=-1), axis=-1)
    out = None
    off = 0
    for lp, (eq, val) in zip(logit_parts, value_parts):
        n = lp.shape[-1]
        term = jnp.einsum(eq, p[..., off:off + n], val)
        out = term if out is None else out + term
        off += n
    return out.transpose(0, 2, 1, 3).reshape(B, L, ATTN_WIDTH)


def _routed_experts(x, expert, weight, w_gate, w_up, w_down):
    N, D = x.shape
    A = expert.shape[0]
    token = jnp.arange(A) // EXPERT_TOPK
    order = jnp.argsort(expert)
    e_s, tok_s, w_s = expert[order], token[order], weight[order]
    counts = jnp.zeros((N_EXPERTS,), jnp.int32).at[expert].add(1)
    start = jnp.cumsum(counts) - counts
    padded = (counts + ROW_BLOCK - 1) // ROW_BLOCK * ROW_BLOCK
    pend = jnp.cumsum(padded)
    pstart = pend - padded
    dest = pstart[e_s] + jnp.arange(A) - start[e_s]
    n_blocks = A // ROW_BLOCK + N_EXPERTS
    row_tok = jnp.zeros((n_blocks * ROW_BLOCK,), jnp.int32).at[dest].set(tok_s)
    blk_e = jnp.minimum(jnp.searchsorted(pend, jnp.arange(n_blocks) * ROW_BLOCK, side='right'), N_EXPERTS - 1)
    xr = x[row_tok].reshape(n_blocks, ROW_BLOCK, D)

    def expert_block(args):
        xb, e = args
        return (jax.nn.silu(xb @ w_gate[e]) * (xb @ w_up[e])) @ w_down[e]

    yr = lax.map(expert_block, (xr, blk_e)).reshape(n_blocks * ROW_BLOCK, D)
    contrib = yr[dest].astype(jnp.float32) * w_s[:, None]
    return jax.ops.segment_sum(contrib, tok_s, num_segments=N)


def _hier_moe(h, w_group, b_group, w_expert, b_expert, w_gate, w_up, w_down):
    f32 = jnp.float32
    B, L, D = h.shape
    x = h.reshape(B * L, D)
    xf = x.astype(f32)
    g_prob = jax.nn.softmax(xf @ w_group.astype(f32) + b_group.astype(f32), axis=-1)
    g_w, g_idx = lax.top_k(g_prob, 1)
    e_logits = (xf @ w_expert.astype(f32) + b_expert.astype(f32)).reshape(-1, N_GROUPS, EXPERTS_PER_GROUP)
    e_in = e_logits[jnp.arange(B * L), g_idx[:, 0]]
    e_top, e_loc = lax.top_k(e_in, EXPERT_TOPK)
    weights = jax.nn.softmax(e_top, axis=-1) * g_w
    experts = g_idx * EXPERTS_PER_GROUP + e_loc
    y = _routed_experts(x, experts.reshape(-1), weights.reshape(-1), w_gate, w_up, w_down)
    return y.reshape(B, L, D).astype(h.dtype)


def _layer(x, c, pool_hist, pos0, attend, w_ada, b_ada, g_attn_norm, w_in, g_q, g_k, w_pool, pool_scale,
           w_out, g_ffn_norm, w_group, b_group, w_expert, b_expert, w_gate, w_up, w_down):
    B, L, _ = x.shape
    mod = (jax.nn.silu(c) @ w_ada + b_ada)[:, None, :]
    sh1, sc1, gt1, sh2, sc2, gt2 = jnp.split(mod, 6, axis=-1)
    h = _rmsnorm(x, g_attn_norm) * (1 + sc1) + sh1
    z = h @ w_in
    u, q, k, v = jnp.split(z, [POOL_WIDTH, POOL_WIDTH + ATTN_WIDTH, POOL_WIDTH + 2 * ATTN_WIDTH], axis=-1)
    q = _rmsnorm(q.reshape(B, L, N_HEADS, HEAD_DIM), g_q)
    k = _rmsnorm(k.reshape(B, L, N_HEADS, HEAD_DIM), g_k)
    v = v.reshape(B, L, N_HEADS, HEAD_DIM)
    u_ext = jnp.concatenate([pool_hist.astype(u.dtype), u], axis=1)
    y_pool = _pool_mix(u_ext, pos0, w_pool, pool_scale)
    y_attn = attend(q, k, v)
    mix = jnp.concatenate([y_pool.astype(x.dtype), y_attn.astype(x.dtype)], axis=-1) @ w_out
    x = x + gt1 * mix
    h2 = _rmsnorm(x, g_ffn_norm) * (1 + sc2) + sh2
    x = x + gt2 * _hier_moe(h2, w_group, b_group, w_expert, b_expert, w_gate, w_up, w_down)
    return x, k, v, u_ext[:, -POOL_HIST:]


def setup_inputs(seed: int = 0) -> dict:
    key = jax.random.key(seed)
    ks = jax.random.split(key, 32)
    f32 = jnp.float32
    D = D_MODEL
    n_pages = PAST_LEN // PAGE_SIZE
    n_used = DEC_BATCH * n_pages
    n_phys = n_used + max(1, n_used // 4)

    def nrm(k, shape, scale):
        return jax.random.normal(k, shape, f32) * scale

    return {
        'x_prompt': nrm(ks[0], (BATCH, SEQ, D), 1.0),
        'x_sample': nrm(ks[1], (DEC_BATCH, DEC_SEQ, D), 1.0),
        'cache_k': nrm(ks[2], (DEPTH, n_phys, PAGE_SIZE, N_HEADS, HEAD_DIM), 1.0),
        'cache_v': nrm(ks[3], (DEPTH, n_phys, PAGE_SIZE, N_HEADS, HEAD_DIM), 1.0),
        'state_pool': nrm(ks[4], (DEPTH, DEC_BATCH, POOL_HIST, POOL_WIDTH), 1.0),
        'page_table': jax.random.permutation(ks[5], n_phys)[:n_used].reshape(DEC_BATCH, n_pages).astype(jnp.int32),
        'c_prompt': nrm(ks[6], (BATCH, D), 1.0),
        'c_sample': nrm(ks[7], (DEC_BATCH, D), 1.0),
        'w_ada': nrm(ks[8], (DEPTH, D, 6 * D), 0.5 * D ** -0.5),
        'b_ada': nrm(ks[9], (DEPTH, 6 * D), 0.01),
        'g_attn_norm': 1.0 + nrm(ks[10], (DEPTH, D), 0.02),
        'w_in': nrm(ks[11], (DEPTH, D, MIX_IN), D ** -0.5),
        'g_q': 1.0 + nrm(ks[12], (DEPTH, HEAD_DIM), 0.02),
        'g_k': 1.0 + nrm(ks[13], (DEPTH, HEAD_DIM), 0.02),
        'w_pool': nrm(ks[14], (DEPTH, N_POOL_GROUPS, POOL_GROUP, POOL_GROUP), POOL_GROUP ** -0.5),
        'pool_scale': 1.0 + nrm(ks[15], (DEPTH, POOL_WIDTH), 0.02),
        'w_out': nrm(ks[16], (DEPTH, MIX_WIDTH, D), MIX_WIDTH ** -0.5),
        'g_ffn_norm': 1.0 + nrm(ks[17], (DEPTH, D), 0.02),
        'w_group': nrm(ks[18], (DEPTH, D, N_GROUPS), D ** -0.5),
        'b_group': nrm(ks[19], (DEPTH, N_GROUPS), 0.01),
        'w_expert': nrm(ks[20], (DEPTH, D, N_EXPERTS), D ** -0.5),
        'b_expert': nrm(ks[21], (DEPTH, N_EXPERTS), 0.01),
        'w_gate': nrm(ks[22], (DEPTH, N_EXPERTS, D, D_FF_EXPERT), D ** -0.5),
        'w_up': nrm(ks[23], (DEPTH, N_EXPERTS, D, D_FF_EXPERT), D ** -0.5),
        'w_down': nrm(ks[24], (DEPTH, N_EXPERTS, D_FF_EXPERT, D), D_FF_EXPERT ** -0.5),
    }


def reference(x_prompt, x_sample, cache_k, cache_v, state_pool, page_table, c_prompt, c_sample,
              w_ada, b_ada, g_attn_norm, w_in, g_q, g_k, w_pool, pool_scale, w_out, g_ffn_norm,
              w_group, b_group, w_expert, b_expert, w_gate, w_up, w_down):
    hp, hs = x_prompt, x_sample
    kp_l, vp_l, pp_l, ks_l, vs_l, ps_l = [], [], [], [], [], []
    hist0 = jnp.zeros((x_prompt.shape[0], POOL_HIST, POOL_WIDTH), x_prompt.dtype)
    for l in range(DEPTH):
        lw = (w_ada[l], b_ada[l], g_attn_norm[l], w_in[l], g_q[l], g_k[l], w_pool[l], pool_scale[l],
              w_out[l], g_ffn_norm[l], w_group[l], b_group[l], w_expert[l], b_expert[l],
              w_gate[l], w_up[l], w_down[l])
        hp, kp, vp, pp = _layer(hp, c_prompt, hist0, 0, _moba_prompt, *lw)
        attend_s = functools.partial(_moba_sample, cache_k=cache_k, cache_v=cache_v,
                                     page_table=page_table, layer=l)
        hs, ksm, vsm, psm = _layer(hs, c_sample, state_pool[l], PAST_LEN, attend_s, *lw)
        kp_l.append(kp)
        vp_l.append(vp)
        pp_l.append(pp)
        ks_l.append(ksm)
        vs_l.append(vsm)
        ps_l.append(psm)
    return (hp, hs, jnp.stack(kp_l), jnp.stack(vp_l), jnp.stack(pp_l), jnp.stack(ks_l), jnp.stack(vs_l), jnp.stack(ps_l))
```

```python
import functools

import jax
import jax.numpy as jnp
from jax import lax
from jax.experimental import pallas as pl
from jax.experimental.pallas import tpu as pltpu

F32 = jnp.float32
BF16 = jnp.bfloat16
I32 = jnp.int32

D_MODEL = 1024
BATCH = 8
SEQ = 2048
DEC_BATCH = 32
DEC_SEQ = 8
PAST_LEN = 16384
PAGE_SIZE = 128
POOL_WIDTH = 512
POOL_WINDOWS = (2, 4, 8, 16)
POOL_GROUP = 128
POOL_HIST = 15
HALO = 16
N_HEADS = 8
HEAD_DIM = 64
ATTN_WIDTH = 512
MOBA_BLOCK = 256
MOBA_TOPK = 3
ATTN_SCALE = HEAD_DIM ** -0.5
MIX_IN = POOL_WIDTH + 3 * ATTN_WIDTH
N_GROUPS = 4
EXPERTS_PER_GROUP = 8
N_EXPERTS = 32
EXPERT_TOPK = 2
D_FF = 512
EPS = 1e-6
NEG = -1e30

LANES = 128
ROW_TILE = 512
MOE_ROWS = 256
PAGES_PER_STEP = 16
VMEM_LIMIT = 56 * 1024 * 1024

_NT = (((1,), (1,)), ((), ()))


def _dot(a, b):
    return jnp.dot(a, b, preferred_element_type=F32)


def _dot_nt(a, b):
    return lax.dot_general(a, b, _NT, preferred_element_type=F32)


def _split_dot(a, b01):
    hi = a.astype(BF16)
    lo = (a - hi.astype(F32)).astype(BF16)
    return _dot(hi, b01) + _dot(lo, b01)


def _rms_mod(x, g, sc, sh):
    ms = jnp.mean(x * x, axis=-1, keepdims=True)
    return x * lax.rsqrt(ms + EPS) * g * (1.0 + sc) + sh


def _ada_kernel(c_ref, w_ref, b_ref, o_ref):
    c = c_ref[...]
    a = c / (1.0 + jnp.exp(-c))
    o_ref[...] = _dot(a.astype(BF16), w_ref[...].astype(BF16)) + b_ref[...]


def _ada(c_all, w_ada, b_ada):
    n = c_all.shape[0]
    tn = 1536
    return pl.pallas_call(
        _ada_kernel,
        grid=(6 * D_MODEL // tn,),
        in_specs=[pl.BlockSpec((n, D_MODEL), lambda j: (0, 0)),
                  pl.BlockSpec((D_MODEL, tn), lambda j: (0, j)),
                  pl.BlockSpec((1, tn), lambda j: (0, j))],
        out_specs=pl.BlockSpec((n, tn), lambda j: (0, j)),
        out_shape=jax.ShapeDtypeStruct((n, 6 * D_MODEL), F32),
        compiler_params=pltpu.CompilerParams(vmem_limit_bytes=VMEM_LIMIT),
        name="ada",
    )(c_all, w_ada, b_ada.reshape(1, -1))


def _inproj_core(x, sh1, sc1, g, win_ref, bd_ref, gq_ref, gk_ref):
    h = _rms_mod(x, g, sc1, sh1)
    z = _dot(h.astype(BF16), win_ref[...])
    u = z[:, 0:POOL_WIDTH]
    q = z[:, POOL_WIDTH:POOL_WIDTH + ATTN_WIDTH]
    k = z[:, POOL_WIDTH + ATTN_WIDTH:POOL_WIDTH + 2 * ATTN_WIDTH]
    v = z[:, POOL_WIDTH + 2 * ATTN_WIDTH:]
    bd = bd_ref[...]
    q = q * lax.rsqrt(_split_dot(q * q, bd) * (1.0 / HEAD_DIM) + EPS) * gq_ref[...]
    k = k * lax.rsqrt(_split_dot(k * k, bd) * (1.0 / HEAD_DIM) + EPS) * gk_ref[...]
    return u, q, k, v


def _window_sum(e, w):
    s = e
    sh = 1
    while sh < w:
        s = s + pltpu.roll(s, sh, axis=0)
        sh *= 2
    return s


def _inproj_prompt_kernel(x_ref, mod_ref, g_ref, win_ref, bd_ref, gq_ref, gk_ref, wpool_ref, ps_ref,
                          ypool_ref, q_ref, k_ref, v_ref, tail_ref, ext_ref):
    t = pl.program_id(1)
    nt = pl.num_programs(1)
    x = x_ref[0]
    sh1 = mod_ref[0, :, 0:D_MODEL]
    sc1 = mod_ref[0, :, D_MODEL:2 * D_MODEL]
    u, q, k, v = _inproj_core(x, sh1, sc1, g_ref[...], win_ref, bd_ref, gq_ref, gk_ref)
    q_ref[0] = (q * ATTN_SCALE).astype(BF16)
    k_ref[0] = k
    v_ref[0] = v

    @pl.when(t == 0)
    def _():
        ext_ref[0:HALO, :] = jnp.zeros((HALO, POOL_WIDTH), F32)

    ext_ref[HALO:, :] = u
    pos = t * ROW_TILE + lax.broadcasted_iota(I32, (ROW_TILE, 1), 0)
    for gi, w in enumerate(POOL_WINDOWS):
        cols = slice(gi * POOL_GROUP, (gi + 1) * POOL_GROUP)
        win = _window_sum(ext_ref[:, cols], w)[HALO:]
        inv_cnt = 1.0 / jnp.minimum(pos + 1, w).astype(F32)
        pooled = win * inv_cnt - u[:, cols]
        y = _dot(pooled.astype(BF16), wpool_ref[gi]) * ps_ref[:, cols]
        ypool_ref[0, :, cols] = y.astype(BF16)
    last = u[ROW_TILE - HALO:, :]
    ext_ref[0:HALO, :] = last

    @pl.when(t == nt - 1)
    def _():
        tail_ref[0] = last


def _inproj_prompt(x, mod_p, g1, win, bd, gq, gk, wpool, ps):
    B, T, D = x.shape
    nt = T // ROW_TILE
    const2 = lambda b, t: (0, 0)
    tok = lambda b, t: (b, t, 0)
    return pl.pallas_call(
        _inproj_prompt_kernel,
        grid=(B, nt),
        in_specs=[pl.BlockSpec((1, ROW_TILE, D), tok),
                  pl.BlockSpec((1, 1, 6 * D), lambda b, t: (b, 0, 0)),
                  pl.BlockSpec((1, D), const2),
                  pl.BlockSpec((D, MIX_IN), const2),
                  pl.BlockSpec((ATTN_WIDTH, ATTN_WIDTH), const2),
                  pl.BlockSpec((1, ATTN_WIDTH), const2),
                  pl.BlockSpec((1, ATTN_WIDTH), const2),
                  pl.BlockSpec((4, POOL_GROUP, POOL_GROUP), lambda b, t: (0, 0, 0)),
                  pl.BlockSpec((1, POOL_WIDTH), const2)],
        out_specs=[pl.BlockSpec((1, ROW_TILE, POOL_WIDTH), tok),
                   pl.BlockSpec((1, ROW_TILE, ATTN_WIDTH), tok),
                   pl.BlockSpec((1, ROW_TILE, ATTN_WIDTH), tok),
                   pl.BlockSpec((1, ROW_TILE, ATTN_WIDTH), tok),
                   pl.BlockSpec((1, HALO, POOL_WIDTH), lambda b, t: (b, 0, 0))],
        out_shape=[jax.ShapeDtypeStruct((B, T, POOL_WIDTH), BF16),
                   jax.ShapeDtypeStruct((B, T, ATTN_WIDTH), BF16),
                   jax.ShapeDtypeStruct((B, T, ATTN_WIDTH), F32),
                   jax.ShapeDtypeStruct((B, T, ATTN_WIDTH), F32),
                   jax.ShapeDtypeStruct((B, HALO, POOL_WIDTH), F32)],
        scratch_shapes=[pltpu.VMEM((HALO + ROW_TILE, POOL_WIDTH), F32)],
        compiler_params=pltpu.CompilerParams(
            dimension_semantics=("arbitrary", "arbitrary"), vmem_limit_bytes=VMEM_LIMIT),
        name="inproj_prompt",
    )(x, mod_p, g1, win, bd, gq, gk, wpool, ps)


def _inproj_sample_kernel(x_ref, sh_ref, sc_ref, g_ref, win_ref, bd_ref, gq_ref, gk_ref, hist_ref, wpool_ref,
                          ps_ref, ypool_ref, q_ref, k_ref, v_ref, tail_ref, ext_ref):
    n = DEC_BATCH * DEC_SEQ
    ext_rows = HALO + DEC_SEQ
    u, q, k, v = _inproj_core(x_ref[...], sh_ref[...], sc_ref[...], g_ref[...], win_ref, bd_ref, gq_ref, gk_ref)
    q_ref[...] = q * ATTN_SCALE
    k_ref[...] = k
    v_ref[...] = v
    ext_ref[:, 0:HALO, :] = hist_ref[...]
    ext_ref[:, HALO:, :] = u.reshape(DEC_BATCH, DEC_SEQ, POOL_WIDTH)
    tail_ref[...] = ext_ref[:, ext_rows - HALO:, :]
    pos = PAST_LEN + lax.broadcasted_iota(I32, (DEC_BATCH, DEC_SEQ, 1), 1).reshape(n, 1)
    for gi, w in enumerate(POOL_WINDOWS):
        cols = slice(gi * POOL_GROUP, (gi + 1) * POOL_GROUP)
        e = ext_ref[:, :, cols].reshape(DEC_BATCH * ext_rows, POOL_GROUP)
        win = _window_sum(e, w).reshape(DEC_BATCH, ext_rows, POOL_GROUP)[:, HALO:, :].reshape(n, POOL_GROUP)
        inv_cnt = 1.0 / jnp.minimum(pos + 1, w).astype(F32)
        pooled = win * inv_cnt - u[:, cols]
        y = _dot(pooled.astype(BF16), wpool_ref[gi]) * ps_ref[:, cols]
        ypool_ref[:, cols] = y.astype(BF16)


def _inproj_sample(x, sh1, sc1, g1, win, bd, gq, gk, hist, wpool, ps):
    n = x.shape[0]
    return pl.pallas_call(
        _inproj_sample_kernel,
        out_shape=[jax.ShapeDtypeStruct((n, POOL_WIDTH), BF16),
                   jax.ShapeDtypeStruct((n, ATTN_WIDTH), F32),
                   jax.ShapeDtypeStruct((n, ATTN_WIDTH), F32),
                   jax.ShapeDtypeStruct((n, ATTN_WIDTH), F32),
                   jax.ShapeDtypeStruct((DEC_BATCH, HALO, POOL_WIDTH), F32)],
        scratch_shapes=[pltpu.VMEM((DEC_BATCH, HALO + DEC_SEQ, POOL_WIDTH), F32)],
        compiler_params=pltpu.CompilerParams(vmem_limit_bytes=VMEM_LIMIT),
        name="inproj_sample",
    )(x, sh1, sc1, g1, win, bd, gq, gk, hist, wpool, ps)


def _moba_prompt_kernel(slopes_ref, q_ref, k_ref, v_ref, o_ref,
                        kb_ref, vb_ref, kmf_ref, m_ref, l_ref, acc_ref):
    S = MOBA_BLOCK
    nb = SEQ // S
    hp = pl.program_id(1)
    cb = pl.program_id(2)
    lane = lax.broadcasted_iota(I32, (1, LANES), 1)
    first = lane < HEAD_DIM

    @pl.when(cb == 0)
    def _():
        kf = k_ref[0]
        kb_ref[...] = kf.astype(BF16)
        vb_ref[...] = v_ref[0].astype(BF16)
        kmf_ref[...] = jnp.zeros((LANES, LANES), F32)
        for j in range(nb):
            mean = jnp.sum(kf[j * S:(j + 1) * S], axis=0, keepdims=True) * (1.0 / S)
            kmf_ref[j:j + 1, :] = jnp.where(first, mean, 0.0)
            kmf_ref[nb + j:nb + j + 1, :] = jnp.where(first, 0.0, mean)

    q2 = q_ref[0]
    zero = jnp.zeros_like(q2)
    qm = (jnp.where(first, q2, zero), jnp.where(first, zero, q2))

    gt = _dot_nt(kmf_ref[...].astype(BF16), q2)[0:2 * nb]
    row = lax.broadcasted_iota(I32, (2 * nb, S), 0)
    blk = row % nb
    cnt = jnp.zeros((2 * nb, S), I32)
    for m in range(nb):
        gm = jnp.where(row < nb, gt[m:m + 1, :], gt[nb + m:nb + m + 1, :])
        beats = (gm > gt) | ((gm == gt) & (m < blk))
        cnt = cnt + jnp.where(beats & (m < cb), 1, 0)
    sel = ((cnt < MOBA_TOPK) & (blk < cb)).astype(F32)
    selq = jnp.concatenate([sel, jnp.zeros((LANES - 2 * nb, S), F32)], axis=0).T

    r_i = lax.broadcasted_iota(I32, (S, S), 0)
    c_i = lax.broadcasted_iota(I32, (S, S), 1)
    rel = (c_i - r_i).astype(F32)
    causal = c_i <= r_i

    def scores(h, kblk, joff):
        slope = slopes_ref[2 * hp + h]
        return _dot_nt(qm[h], kblk) + slope * (rel + joff)

    own = pl.multiple_of(cb * S, S)
    k_own = kb_ref[pl.ds(own, S), :]
    v_own = vb_ref[pl.ds(own, S), :]
    for h in range(2):
        s = jnp.where(causal, scores(h, k_own, 0.0), NEG)
        m = jnp.max(s, axis=-1, keepdims=True)
        p = jnp.exp(s - m)
        m_ref[h] = m
        l_ref[h] = jnp.sum(p, axis=-1, keepdims=True)
        acc_ref[h] = _dot(p.astype(BF16), v_own)

    for j in range(nb - 1):
        @pl.when(j < cb)
        def _():
            kblk = kb_ref[j * S:(j + 1) * S, :]
            vblk = vb_ref[j * S:(j + 1) * S, :]
            joff = ((j - cb) * S).astype(F32)
            for h in range(2):
                keep = selq[:, h * nb + j:h * nb + j + 1] > 0.5
                s = jnp.where(keep, scores(h, kblk, joff), NEG)
                m_old = m_ref[h]
                m_new = jnp.maximum(m_old, jnp.max(s, axis=-1, keepdims=True))
                alpha = jnp.exp(m_old - m_new)
                p = jnp.exp(s - m_new)
                l_ref[h] = alpha * l_ref[h] + jnp.sum(p, axis=-1, keepdims=True)
                acc_ref[h] = alpha * acc_ref[h] + _dot(p.astype(BF16), vblk)
                m_ref[h] = m_new

    o0 = acc_ref[0] * (1.0 / l_ref[0])
    o1 = acc_ref[1] * (1.0 / l_ref[1])
    o_ref[0] = jnp.where(first, o0, o1).astype(BF16)


def _moba_prompt(slopes, q, k, v):
    B, T, _ = q.shape
    S = MOBA_BLOCK
    return pl.pallas_call(
        _moba_prompt_kernel,
        grid=(B, N_HEADS // 2, T // S),
        in_specs=[pl.BlockSpec(memory_space=pltpu.SMEM),
                  pl.BlockSpec((1, S, LANES), lambda b, hp, c: (b, c, hp)),
                  pl.BlockSpec((1, T, LANES), lambda b, hp, c: (b, 0, hp)),
                  pl.BlockSpec((1, T, LANES), lambda b, hp, c: (b, 0, hp))],
        out_specs=pl.BlockSpec((1, S, LANES), lambda b, hp, c: (b, c, hp)),
        out_shape=jax.ShapeDtypeStruct((B, T, ATTN_WIDTH), BF16),
        scratch_shapes=[pltpu.VMEM((T, LANES), BF16),
                        pltpu.VMEM((T, LANES), BF16),
                        pltpu.VMEM((LANES, LANES), F32),
                        pltpu.VMEM((2, S, 1), F32),
                        pltpu.VMEM((2, S, 1), F32),
                        pltpu.VMEM((2, S, LANES), F32)],
        compiler_params=pltpu.CompilerParams(
            dimension_semantics=("arbitrary", "arbitrary", "arbitrary"), vmem_limit_bytes=VMEM_LIMIT),
        name="moba_prompt",
    )(slopes, q, k, v)


def _moba_sample_kernel(pt_ref, q_ref, kn_ref, vn_ref, slope_ref, lq_ref, *rest):
    P = PAGES_PER_STEP
    kpages = rest[0:P]
    vpages = rest[P:2 * P]
    o_ref = rest[2 * P]
    s_ref, km_ref, qp_ref, kc_ref, vc_ref, acc_ref, sel_ref, linv_ref = rest[2 * P + 1:]
    S = MOBA_BLOCK
    n_blocks = PAST_LEN // S
    chunk = P * PAGE_SIZE
    n_chunks = PAST_LEN // chunk
    st = pl.program_id(1)
    slope = slope_ref[...]
    lq = lq_ref[...]

    @pl.when(st == 0)
    def _():
        q8 = q_ref[0]
        head = lax.broadcasted_iota(I32, (DEC_SEQ, ATTN_WIDTH), 1) // HEAD_DIM
        parts = [jnp.where(head == h, q8, 0.0) for h in range(N_HEADS)]
        parts.append(jnp.zeros((LANES - N_HEADS * DEC_SEQ, ATTN_WIDTH), F32))
        qp_ref[...] = jnp.concatenate(parts, axis=0).astype(BF16)

    @pl.when(st < n_chunks)
    def _():
        for r in range(P):
            kc_ref[r * PAGE_SIZE:(r + 1) * PAGE_SIZE, :] = kpages[r][0].astype(BF16)
        ppb = S // PAGE_SIZE
        for r2 in range(P // ppb):
            tot = None
            for r in range(r2 * ppb, (r2 + 1) * ppb):
                part = jnp.sum(kpages[r][0], axis=0, keepdims=True)
                tot = part if tot is None else tot + part
            km_ref[pl.ds(st * (P // ppb) + r2, 1), :] = tot * (1.0 / S)
        sc = _dot_nt(kc_ref[...], qp_ref[...])
        keypos = st * chunk + lax.broadcasted_iota(I32, (chunk, 1), 0)
        dist = (keypos - PAST_LEN).astype(F32) - lq
        s_ref[pl.ds(pl.multiple_of(st * chunk, chunk), chunk), :] = sc + slope * dist

    @pl.when(st == n_chunks)
    def _():
        gt = _dot_nt(km_ref[...].astype(BF16), qp_ref[...])
        blk = lax.broadcasted_iota(I32, (n_blocks, LANES), 0)
        cnt = jnp.zeros((n_blocks, LANES), I32)
        for m in range(n_blocks):
            gm = gt[m:m + 1, :]
            beats = (gm > gt) | ((gm == gt) & (m < blk))
            cnt = cnt + jnp.where(beats, 1, 0)
        sel_ref[...] = (cnt < min(MOBA_TOPK, n_blocks)).astype(F32)

        pad = 2 * DEC_SEQ
        rown = lax.broadcasted_iota(I32, (pad, 1), 0)
        kn = jnp.concatenate([kn_ref[0], jnp.zeros((pad - DEC_SEQ, ATTN_WIDTH), F32)], axis=0)
        sn = _dot_nt(kn.astype(BF16), qp_ref[...]) + slope * (rown.astype(F32) - lq)
        validn = rown.astype(F32) <= lq
        sn = jnp.where(validn, sn, NEG)

        def max_body(n, mx):
            blkv = s_ref[pl.ds(pl.multiple_of(n * S, S), S), :]
            keep = sel_ref[pl.ds(n, 1), :] > 0.5
            return jnp.maximum(mx, jnp.max(jnp.where(keep, blkv, NEG), axis=0, keepdims=True))

        mx = lax.fori_loop(0, n_blocks, max_body, jnp.max(sn, axis=0, keepdims=True))

        def exp_body(n, lsum):
            idx = pl.ds(pl.multiple_of(n * S, S), S)
            keep = sel_ref[pl.ds(n, 1), :] > 0.5
            p = jnp.exp(jnp.where(keep, s_ref[idx, :] - mx, NEG))
            s_ref[idx, :] = p
            return lsum + jnp.sum(p, axis=0, keepdims=True)

        pn = jnp.exp(sn - mx)
        lsum = lax.fori_loop(0, n_blocks, exp_body, jnp.sum(pn, axis=0, keepdims=True))
        linv_ref[...] = 1.0 / lsum
        pn_t = jnp.concatenate([pn, jnp.zeros((LANES - pad, LANES), F32)], axis=0).T
        vn = jnp.concatenate([vn_ref[0], jnp.zeros((LANES - DEC_SEQ, ATTN_WIDTH), F32)], axis=0)
        acc_ref[...] = _dot(pn_t.astype(BF16), vn.astype(BF16))

    @pl.when(st >= n_chunks)
    def _():
        c = st - n_chunks
        for r in range(P):
            vc_ref[r * PAGE_SIZE:(r + 1) * PAGE_SIZE, :] = vpages[r][0].astype(BF16)
        p_t = s_ref[pl.ds(pl.multiple_of(c * chunk, chunk), chunk), :].T
        acc_ref[...] += _dot(p_t.astype(BF16), vc_ref[...])

    @pl.when(st == 2 * n_chunks - 1)
    def _():
        linv_col = jnp.broadcast_to(linv_ref[...], (LANES, LANES)).T[:, 0:1]
        o = acc_ref[...] * linv_col
        head = lax.broadcasted_iota(I32, (DEC_SEQ, ATTN_WIDTH), 1) // HEAD_DIM
        res = jnp.zeros((DEC_SEQ, ATTN_WIDTH), F32)
        for h in range(N_HEADS):
            res = res + jnp.where(head == h, o[h * DEC_SEQ:(h + 1) * DEC_SEQ, :], 0.0)
        o_ref[0] = res.astype(BF16)


def _moba_sample(page_table, q, kn, vn, slope_lane, lq_lane, cache_k, cache_v):
    P = PAGES_PER_STEP
    n_chunks = PAST_LEN // (P * PAGE_SIZE)
    B = DEC_BATCH

    def kmap(r):
        return lambda b, s, pt: (pt[b, jnp.minimum(s, n_chunks - 1) * P + r], 0, 0)

    def vmap(r):
        return lambda b, s, pt: (pt[b, jnp.maximum(s - n_chunks, 0) * P + r], 0, 0)

    row3 = lambda b, s, pt: (b, 0, 0)
    const2 = lambda b, s, pt: (0, 0)
    page = (1, PAGE_SIZE, ATTN_WIDTH)
    in_specs = [pl.BlockSpec((1, DEC_SEQ, ATTN_WIDTH), row3),
                pl.BlockSpec((1, DEC_SEQ, ATTN_WIDTH), row3),
                pl.BlockSpec((1, DEC_SEQ, ATTN_WIDTH), row3),
                pl.BlockSpec((1, LANES), const2),
                pl.BlockSpec((1, LANES), const2)]
    in_specs += [pl.BlockSpec(page, kmap(r)) for r in range(P)]
    in_specs += [pl.BlockSpec(page, vmap(r)) for r in range(P)]
    grid_spec = pltpu.PrefetchScalarGridSpec(
        num_scalar_prefetch=1,
        grid=(B, 2 * n_chunks),
        in_specs=in_specs,
        out_specs=pl.BlockSpec((1, DEC_SEQ, ATTN_WIDTH), row3),
        scratch_shapes=[pltpu.VMEM((PAST_LEN, LANES), F32),
                        pltpu.VMEM((PAST_LEN // MOBA_BLOCK, ATTN_WIDTH), F32),
                        pltpu.VMEM((LANES, ATTN_WIDTH), BF16),
                        pltpu.VMEM((P * PAGE_SIZE, ATTN_WIDTH), BF16),
                        pltpu.VMEM((P * PAGE_SIZE, ATTN_WIDTH), BF16),
                        pltpu.VMEM((LANES, ATTN_WIDTH), F32),
                        pltpu.VMEM((PAST_LEN // MOBA_BLOCK, LANES), F32),
                        pltpu.VMEM((1, LANES), F32)])
    return pl.pallas_call(
        _moba_sample_kernel,
        grid_spec=grid_spec,
        out_shape=jax.ShapeDtypeStruct((B, DEC_SEQ, ATTN_WIDTH), BF16),
        compiler_params=pltpu.CompilerParams(
            dimension_semantics=("arbitrary", "arbitrary"), vmem_limit_bytes=VMEM_LIMIT),
        name="moba_sample",
    )(page_table, q, kn, vn, slope_lane, lq_lane, *([cache_k] * P), *([cache_v] * P))


def _outproj_kernel(yp_ref, ya_ref, x_ref, gt1_ref, sh2_ref, sc2_ref, g2_ref, wout_ref, wr_ref, wrhi_ref, br_ref,
                    x1_ref, h2_ref, eid_ref, ew_ref):
    rows = x_ref.shape[0]
    mix = _dot(yp_ref[...], wout_ref[0:POOL_WIDTH, :]) + _dot(ya_ref[...], wout_ref[POOL_WIDTH:, :])
    x1 = x_ref[...] + gt1_ref[...] * mix
    x1_ref[...] = x1
    h2 = _rms_mod(x1, g2_ref[...], sc2_ref[...], sh2_ref[...])
    hh = h2.astype(BF16)
    h2_ref[...] = hh
    hl = (h2 - hh.astype(F32)).astype(BF16)
    both = _dot(hh, wr_ref[...])
    lt = (both[:, 0:LANES] + both[:, LANES:] + _dot(hl, wrhi_ref[...])).T + br_ref[...]
    row8 = lax.broadcasted_iota(I32, (8, rows), 0)
    g8 = lt[0:8]
    gmax = jnp.max(g8, axis=0, keepdims=True)
    gsum = jnp.sum(jnp.exp(g8 - gmax), axis=0, keepdims=True)
    g_w = 1.0 / gsum
    g_idx = jnp.min(jnp.where(g8 == gmax, row8, 8), axis=0, keepdims=True)
    e_in = jnp.zeros((8, rows), F32)
    for g in range(N_GROUPS):
        e_in = e_in + jnp.where(g_idx == g, lt[8 + 8 * g:16 + 8 * g], 0.0)
    m1 = jnp.max(e_in, axis=0, keepdims=True)
    i1 = jnp.min(jnp.where(e_in == m1, row8, 8), axis=0, keepdims=True)
    e_rest = jnp.where(row8 == i1, NEG, e_in)
    m2 = jnp.max(e_rest, axis=0, keepdims=True)
    i2 = jnp.min(jnp.where(e_rest == m2, row8, 8), axis=0, keepdims=True)
    r = jnp.exp(m2 - m1)
    w1 = g_w / (1.0 + r)
    w2 = g_w * r / (1.0 + r)
    base = g_idx * EXPERTS_PER_GROUP
    eid_ref[...] = jnp.where(row8 == 0, base + i1, jnp.where(row8 == 1, base + i2, 0))
    ew_ref[...] = jnp.where(row8 == 0, w1, jnp.where(row8 == 1, w2, 0.0))


def _outproj(yp, ya, x, gt1, sh2, sc2, g2, wout, wr, wrhi, br, per_batch_rows):
    n = x.shape[0]
    tile = min(ROW_TILE, n)
    tok = lambda i: (i, 0)
    const2 = lambda i: (0, 0)
    if per_batch_rows is None:
        mod_spec = pl.BlockSpec((tile, D_MODEL), tok)
    else:
        per = per_batch_rows // tile
        mod_spec = pl.BlockSpec((1, 1, D_MODEL), lambda i: (i // per, 0, 0))
    kern = _outproj_kernel if per_batch_rows is None else _outproj_kernel_batched
    return pl.pallas_call(
        kern,
        grid=(n // tile,),
        in_specs=[pl.BlockSpec((tile, POOL_WIDTH), tok),
                  pl.BlockSpec((tile, ATTN_WIDTH), tok),
                  pl.BlockSpec((tile, D_MODEL), tok),
                  mod_spec, mod_spec, mod_spec,
                  pl.BlockSpec((1, D_MODEL), const2),
                  pl.BlockSpec((2 * POOL_WIDTH, D_MODEL), const2),
                  pl.BlockSpec((D_MODEL, 2 * LANES), const2),
                  pl.BlockSpec((D_MODEL, LANES), const2),
                  pl.BlockSpec((LANES, 1), const2)],
        out_specs=[pl.BlockSpec((tile, D_MODEL), tok),
                   pl.BlockSpec((tile, D_MODEL), tok),
                   pl.BlockSpec((8, tile), lambda i: (0, i)),
                   pl.BlockSpec((8, tile), lambda i: (0, i))],
        out_shape=[jax.ShapeDtypeStruct((n, D_MODEL), F32),
                   jax.ShapeDtypeStruct((n, D_MODEL), BF16),
                   jax.ShapeDtypeStruct((8, n), I32),
                   jax.ShapeDtypeStruct((8, n), F32)],
        compiler_params=pltpu.CompilerParams(vmem_limit_bytes=VMEM_LIMIT),
        name="outproj",
    )(yp, ya, x, gt1, sh2, sc2, g2, wout, wr, wrhi, br)


class _Row0:
    def __init__(self, ref):
        self._ref = ref

    def __getitem__(self, idx):
        return self._ref[0]


def _outproj_kernel_batched(yp_ref, ya_ref, x_ref, gt1_ref, sh2_ref, sc2_ref, *rest):
    _outproj_kernel(yp_ref, ya_ref, x_ref, _Row0(gt1_ref), _Row0(sh2_ref), _Row0(sc2_ref), *rest)


def _moe_kernel(blk_e_ref, nact_ref, x_ref, wg_ref, wu_ref, wd_ref, y_ref, wgb_ref, wub_ref, wdb_ref):
    i = pl.program_id(0)
    prev = blk_e_ref[jnp.maximum(i - 1, 0)]
    active = i < nact_ref[0]

    @pl.when(active & ((i == 0) | (blk_e_ref[i] != prev)))
    def _():
        wgb_ref[...] = wg_ref[0].astype(BF16)
        wub_ref[...] = wu_ref[0].astype(BF16)
        wdb_ref[...] = wd_ref[0].astype(BF16)

    @pl.when(active)
    def _():
        x = x_ref[...]
        g = _dot(x, wgb_ref[...])
        u = _dot(x, wub_ref[...])
        a = g / (1.0 + jnp.exp(-g)) * u
        y_ref[...] = _dot(a.astype(BF16), wdb_ref[...])

    @pl.when(jnp.logical_not(active))
    def _():
        y_ref[...] = jnp.zeros_like(y_ref)


def _moe(blk_e, n_active, xr, w_gate, w_up, w_down):
    n_rows = xr.shape[0]
    n_blocks = n_rows // MOE_ROWS
    grid_spec = pltpu.PrefetchScalarGridSpec(
        num_scalar_prefetch=2,
        grid=(n_blocks,),
        in_specs=[pl.BlockSpec((MOE_ROWS, D_MODEL), lambda i, be, na: (i, 0)),
                  pl.BlockSpec((1, D_MODEL, D_FF), lambda i, be, na: (be[i], 0, 0)),
                  pl.BlockSpec((1, D_MODEL, D_FF), lambda i, be, na: (be[i], 0, 0)),
                  pl.BlockSpec((1, D_FF, D_MODEL), lambda i, be, na: (be[i], 0, 0))],
        out_specs=pl.BlockSpec((MOE_ROWS, D_MODEL), lambda i, be, na: (i, 0)),
        scratch_shapes=[pltpu.VMEM((D_MODEL, D_FF), BF16),
                        pltpu.VMEM((D_MODEL, D_FF), BF16),
                        pltpu.VMEM((D_FF, D_MODEL), BF16)])
    return pl.pallas_call(
        _moe_kernel,
        grid_spec=grid_spec,
        out_shape=jax.ShapeDtypeStruct((n_rows, D_MODEL), F32),
        compiler_params=pltpu.CompilerParams(
            dimension_semantics=("arbitrary",), vmem_limit_bytes=VMEM_LIMIT),
        name="moe",
    )(blk_e, n_active, xr, w_gate, w_up, w_down)


def _final_kernel(x1_ref, g_ref, w_ref, gt2_ref, y_ref):
    w = w_ref[...]
    moe = g_ref[:, 0:D_MODEL] * w[:, 0:1] + g_ref[:, D_MODEL:] * w[:, 1:2]
    y_ref[...] = x1_ref[...] + gt2_ref[...] * moe


def _final_kernel_batched(x1_ref, g_ref, w_ref, gt2_ref, y_ref):
    _final_kernel(x1_ref, g_ref, w_ref, _Row0(gt2_ref), y_ref)


def _final(x1, g2rows, w2, gt2, per_batch_rows):
    n = x1.shape[0]
    tile = min(ROW_TILE, n)
    tok = lambda i: (i, 0)
    if per_batch_rows is None:
        mod_spec = pl.BlockSpec((tile, D_MODEL), tok)
        kern = _final_kernel
    else:
        per = per_batch_rows // tile
        mod_spec = pl.BlockSpec((1, 1, D_MODEL), lambda i: (i // per, 0, 0))
        kern = _final_kernel_batched
    return pl.pallas_call(
        kern,
        grid=(n // tile,),
        in_specs=[pl.BlockSpec((tile, D_MODEL), tok),
                  pl.BlockSpec((tile, 2 * D_MODEL), tok),
                  pl.BlockSpec((tile, EXPERT_TOPK), tok),
                  mod_spec],
        out_specs=pl.BlockSpec((tile, D_MODEL), tok),
        out_shape=jax.ShapeDtypeStruct((n, D_MODEL), F32),
        compiler_params=pltpu.CompilerParams(vmem_limit_bytes=VMEM_LIMIT),
        name="final",
    )(x1, g2rows, w2, gt2)


def _dispatch(eid):
    A = eid.shape[0]
    n_blocks = A // MOE_ROWS + N_EXPERTS
    order = jnp.argsort(eid)
    e_s = eid[order]
    tok_s = (order // EXPERT_TOPK).astype(I32)
    counts = jnp.zeros((N_EXPERTS,), I32).at[eid].add(1)
    start = jnp.cumsum(counts) - counts
    padded = (counts + MOE_ROWS - 1) // MOE_ROWS * MOE_ROWS
    pend = jnp.cumsum(padded)
    pstart = pend - padded
    dest_s = (pstart[e_s] + jnp.arange(A, dtype=I32) - start[e_s]).astype(I32)
    row_tok = jnp.zeros((n_blocks * MOE_ROWS,), I32).at[dest_s].set(tok_s)
    dest = jnp.zeros((A,), I32).at[order].set(dest_s)
    blk_e = jnp.minimum(jnp.searchsorted(pend, jnp.arange(n_blocks, dtype=I32) * MOE_ROWS, side='right'),
                        N_EXPERTS - 1).astype(I32)
    n_active = (pend[-1] // MOE_ROWS).astype(I32).reshape(1)
    return row_tok, dest, blk_e, n_active


def kernel(x_prompt, x_sample, cache_k, cache_v, state_pool, page_table, c_prompt, c_sample, w_ada, b_ada,
           g_attn_norm, w_in, g_q, g_k, w_pool, pool_scale, w_out, g_ffn_norm, w_group, b_group, w_expert,
           b_expert, w_gate, w_up, w_down):
    D = D_MODEL
    B, T, _ = x_prompt.shape
    n_s = DEC_BATCH * DEC_SEQ
    n_p = B * T
    layer = 0

    win = w_in[layer].astype(BF16)
    wout = w_out[layer].astype(BF16)
    wpool = w_pool[layer].astype(BF16)
    g1 = g_attn_norm[layer].reshape(1, D)
    g2 = g_ffn_norm[layer].reshape(1, D)
    gq = jnp.tile(g_q[layer], N_HEADS).reshape(1, ATTN_WIDTH)
    gk = jnp.tile(g_k[layer], N_HEADS).reshape(1, ATTN_WIDTH)
    ps = pool_scale[layer].reshape(1, POOL_WIDTH)
    hd = jnp.arange(ATTN_WIDTH) // HEAD_DIM
    bd = (hd[:, None] == hd[None, :]).astype(BF16)
    slopes = jnp.exp2(-8.0 * (jnp.arange(N_HEADS, dtype=F32) + 1.0) / N_HEADS)
    qlane = jnp.arange(LANES)
    slope_lane = slopes[jnp.minimum(qlane // DEC_SEQ, N_HEADS - 1)].reshape(1, LANES)
    lq_lane = (qlane % DEC_SEQ).astype(F32).reshape(1, LANES)
    wr = jnp.zeros((D, LANES), F32).at[:, 0:N_GROUPS].set(w_group[layer]).at[:, 8:8 + N_EXPERTS].set(w_expert[layer])
    wr_hi = wr.astype(BF16)
    wr_lo = (wr - wr_hi.astype(F32)).astype(BF16)
    wr_both = jnp.concatenate([wr_hi, wr_lo], axis=1)
    br = jnp.zeros((LANES,), F32).at[0:N_GROUPS].set(b_group[layer]).at[N_GROUPS:8].set(NEG)
    br = br.at[8:8 + N_EXPERTS].set(b_expert[layer]).reshape(LANES, 1)

    mod = _ada(jnp.concatenate([c_prompt, c_sample], axis=0), w_ada[layer], b_ada[layer])
    mod_p = mod[:B].reshape(B, 1, 6 * D)
    mod_s = jnp.repeat(mod[B:], DEC_SEQ, axis=0)

    ypool_p, q_p, k_p, v_p, tail_p = _inproj_prompt(x_prompt, mod_p, g1, win, bd, gq, gk, wpool, ps)
    yattn_p = _moba_prompt(slopes, q_p, k_p, v_p)
    mp = lambda j: mod_p[:, :, j * D:(j + 1) * D]
    x1_p, h2_p, eid_p, ew_p = _outproj(ypool_p.reshape(n_p, POOL_WIDTH), yattn_p.reshape(n_p, ATTN_WIDTH),
                                       x_prompt.reshape(n_p, D), mp(2), mp(3), mp(4), g2, wout, wr_both, wr_hi, br, T)

    ms = lambda j: mod_s[:, j * D:(j + 1) * D]
    hist = jnp.concatenate([jnp.zeros((DEC_BATCH, HALO - POOL_HIST, POOL_WIDTH), F32), state_pool[layer]], axis=1)
    xs = x_sample.reshape(n_s, D)
    ypool_s, q_s, k_s, v_s, tail_s = _inproj_sample(xs, ms(0), ms(1), g1, win, bd, gq, gk, hist, wpool, ps)
    r3 = lambda a: a.reshape(DEC_BATCH, DEC_SEQ, ATTN_WIDTH)
    n_phys = cache_k.shape[1]
    ck = cache_k[layer].reshape(n_phys, PAGE_SIZE, ATTN_WIDTH)
    cv = cache_v[layer].reshape(n_phys, PAGE_SIZE, ATTN_WIDTH)
    yattn_s = _moba_sample(page_table, r3(q_s), r3(k_s), r3(v_s), slope_lane, lq_lane, ck, cv)
    x1_s, h2_s, eid_s, ew_s = _outproj(ypool_s, yattn_s.reshape(n_s, ATTN_WIDTH), xs, ms(2), ms(3), ms(4), g2,
                                       wout, wr_both, wr_hi, br, None)

    eid = jnp.concatenate([eid_p[0:2], eid_s[0:2]], axis=1).T.reshape(-1)
    ew = jnp.concatenate([ew_p[0:2], ew_s[0:2]], axis=1).T
    h2 = jnp.concatenate([h2_p, h2_s], axis=0)
    row_tok, dest, blk_e, n_active = _dispatch(eid)
    yr = _moe(blk_e, n_active, h2[row_tok], w_gate[layer], w_up[layer], w_down[layer])
    pair = yr[dest].reshape(n_p + n_s, 2 * D)
    y_p = _final(x1_p, pair[:n_p], ew[:n_p], mp(5), T)
    y_s = _final(x1_s, pair[n_p:], ew[n_p:], ms(5), None)

    k4 = lambda a, b, l: a.reshape(1, b, l, N_HEADS, HEAD_DIM)
    return (y_p.reshape(B, T, D), y_s.reshape(DEC_BATCH, DEC_SEQ, D),
            k4(k_p, B, T), k4(v_p, B, T), tail_p[None, :, HALO - POOL_HIST:, :],
            k4(k_s, DEC_BATCH, DEC_SEQ), k4(v_s, DEC_BATCH, DEC_SEQ), tail_s[None, :, HALO - POOL_HIST:, :])
```

```python
import functools

import jax
import jax.numpy as jnp
from jax import lax
from jax.experimental import pallas as pl
from jax.experimental.pallas import tpu as pltpu

F32 = jnp.float32
BF16 = jnp.bfloat16
I32 = jnp.int32

D_MODEL = 1024
BATCH = 8
SEQ = 2048
DEC_BATCH = 32
DEC_SEQ = 8
PAST_LEN = 16384
PAGE_SIZE = 128
POOL_WIDTH = 512
POOL_WINDOWS = (2, 4, 8, 16)
POOL_GROUP = 128
POOL_HIST = 15
HALO = 16
N_HEADS = 8
HEAD_DIM = 64
ATTN_WIDTH = 512
MOBA_BLOCK = 256
MOBA_TOPK = 3
ATTN_SCALE = HEAD_DIM ** -0.5
MIX_IN = POOL_WIDTH + 3 * ATTN_WIDTH
N_GROUPS = 4
EXPERTS_PER_GROUP = 8
N_EXPERTS = 32
EXPERT_TOPK = 2
D_FF = 512
EPS = 1e-6
NEG = -1e30

LANES = 128
ROW_TILE = 512
MOE_ROWS = 256
PAGES_PER_STEP = 16
VMEM_LIMIT = 56 * 1024 * 1024

_NT = (((1,), (1,)), ((), ()))


def _dot(a, b):
    return jnp.dot(a, b, preferred_element_type=F32)


def _dot_nt(a, b):
    return lax.dot_general(a, b, _NT, preferred_element_type=F32)


def _split_dot(a, b01):
    hi = a.astype(BF16)
    lo = (a - hi.astype(F32)).astype(BF16)
    return _dot(hi, b01) + _dot(lo, b01)


def _rms_mod(x, g, sc, sh):
    ms = jnp.mean(x * x, axis=-1, keepdims=True)
    return x * lax.rsqrt(ms + EPS) * g * (1.0 + sc) + sh


def _ada_kernel(c_ref, w_ref, b_ref, o_ref):
    c = c_ref[...]
    a = c / (1.0 + jnp.exp(-c))
    o_ref[...] = _dot(a.astype(BF16), w_ref[...].astype(BF16)) + b_ref[...]


def _ada(c_all, w_ada, b_ada):
    n = c_all.shape[0]
    tn = 1536
    return pl.pallas_call(
        _ada_kernel,
        grid=(6 * D_MODEL // tn,),
        in_specs=[pl.BlockSpec((n, D_MODEL), lambda j: (0, 0)),
                  pl.BlockSpec((D_MODEL, tn), lambda j: (0, j)),
                  pl.BlockSpec((1, tn), lambda j: (0, j))],
        out_specs=pl.BlockSpec((n, tn), lambda j: (0, j)),
        out_shape=jax.ShapeDtypeStruct((n, 6 * D_MODEL), F32),
        compiler_params=pltpu.CompilerParams(vmem_limit_bytes=VMEM_LIMIT),
        name="ada",
    )(c_all, w_ada, b_ada.reshape(1, -1))


def _inproj_core(x, sh1, sc1, g, win_ref, bd_ref, gq_ref, gk_ref):
    h = _rms_mod(x, g, sc1, sh1)
    z = _dot(h.astype(BF16), win_ref[...])
    u = z[:, 0:POOL_WIDTH]
    q = z[:, POOL_WIDTH:POOL_WIDTH + ATTN_WIDTH]
    k = z[:, POOL_WIDTH + ATTN_WIDTH:POOL_WIDTH + 2 * ATTN_WIDTH]
    v = z[:, POOL_WIDTH + 2 * ATTN_WIDTH:]
    bd = bd_ref[...]
    q = q * lax.rsqrt(_split_dot(q * q, bd) * (1.0 / HEAD_DIM) + EPS) * gq_ref[...]
    k = k * lax.rsqrt(_split_dot(k * k, bd) * (1.0 / HEAD_DIM) + EPS) * gk_ref[...]
    return u, q, k, v


def _window_sum(e, w):
    s = e
    sh = 1
    while sh < w:
        s = s + pltpu.roll(s, sh, axis=0)
        sh *= 2
    return s


def _inproj_prompt_kernel(x_ref, mod_ref, g_ref, win_ref, bd_ref, gq_ref, gk_ref, wpool_ref, ps_ref,
                          ypool_ref, q_ref, k_ref, v_ref, tail_ref, ext_ref):
    t = pl.program_id(1)
    nt = pl.num_programs(1)
    x = x_ref[0]
    sh1 = mod_ref[0, :, 0:D_MODEL]
    sc1 = mod_ref[0, :, D_MODEL:2 * D_MODEL]
    u, q, k, v = _inproj_core(x, sh1, sc1, g_ref[...], win_ref, bd_ref, gq_ref, gk_ref)
    q_ref[0] = (q * ATTN_SCALE).astype(BF16)
    k_ref[0] = k
    v_ref[0] = v

    @pl.when(t == 0)
    def _():
        ext_ref[0:HALO, :] = jnp.zeros((HALO, POOL_WIDTH), F32)

    ext_ref[HALO:, :] = u
    pos = t * ROW_TILE + lax.broadcasted_iota(I32, (ROW_TILE, 1), 0)
    for gi, w in enumerate(POOL_WINDOWS):
        cols = slice(gi * POOL_GROUP, (gi + 1) * POOL_GROUP)
        win = _window_sum(ext_ref[:, cols], w)[HALO:]
        inv_cnt = 1.0 / jnp.minimum(pos + 1, w).astype(F32)
        pooled = win * inv_cnt - u[:, cols]
        y = _dot(pooled.astype(BF16), wpool_ref[gi]) * ps_ref[:, cols]
        ypool_ref[0, :, cols] = y.astype(BF16)
    last = u[ROW_TILE - HALO:, :]
    ext_ref[0:HALO, :] = last

    @pl.when(t == nt - 1)
    def _():
        tail_ref[0] = last


def _inproj_prompt(x, mod_p, g1, win, bd, gq, gk, wpool, ps):
    B, T, D = x.shape
    nt = T // ROW_TILE
    const2 = lambda b, t: (0, 0)
    tok = lambda b, t: (b, t, 0)
    return pl.pallas_call(
        _inproj_prompt_kernel,
        grid=(B, nt),
        in_specs=[pl.BlockSpec((1, ROW_TILE, D), tok),
                  pl.BlockSpec((1, 1, 6 * D), lambda b, t: (b, 0, 0)),
                  pl.BlockSpec((1, D), const2),
                  pl.BlockSpec((D, MIX_IN), const2),
                  pl.BlockSpec((ATTN_WIDTH, ATTN_WIDTH), const2),
                  pl.BlockSpec((1, ATTN_WIDTH), const2),
                  pl.BlockSpec((1, ATTN_WIDTH), const2),
                  pl.BlockSpec((4, POOL_GROUP, POOL_GROUP), lambda b, t: (0, 0, 0)),
                  pl.BlockSpec((1, POOL_WIDTH), const2)],
        out_specs=[pl.BlockSpec((1, ROW_TILE, POOL_WIDTH), tok),
                   pl.BlockSpec((1, ROW_TILE, ATTN_WIDTH), tok),
                   pl.BlockSpec((1, ROW_TILE, ATTN_WIDTH), tok),
                   pl.BlockSpec((1, ROW_TILE, ATTN_WIDTH), tok),
                   pl.BlockSpec((1, HALO, POOL_WIDTH), lambda b, t: (b, 0, 0))],
        out_shape=[jax.ShapeDtypeStruct((B, T, POOL_WIDTH), BF16),
                   jax.ShapeDtypeStruct((B, T, ATTN_WIDTH), BF16),
                   jax.ShapeDtypeStruct((B, T, ATTN_WIDTH), F32),
                   jax.ShapeDtypeStruct((B, T, ATTN_WIDTH), F32),
                   jax.ShapeDtypeStruct((B, HALO, POOL_WIDTH), F32)],
        scratch_shapes=[pltpu.VMEM((HALO + ROW_TILE, POOL_WIDTH), F32)],
        compiler_params=pltpu.CompilerParams(
            dimension_semantics=("arbitrary", "arbitrary"), vmem_limit_bytes=VMEM_LIMIT),
        name="inproj_prompt",
    )(x, mod_p, g1, win, bd, gq, gk, wpool, ps)


def _inproj_sample_kernel(x_ref, sh_ref, sc_ref, g_ref, win_ref, bd_ref, gq_ref, gk_ref, hist_ref, wpool_ref,
                          ps_ref, ypool_ref, q_ref, k_ref, v_ref, tail_ref, ext_ref):
    n = DEC_BATCH * DEC_SEQ
    ext_rows = HALO + DEC_SEQ
    u, q, k, v = _inproj_core(x_ref[...], sh_ref[...], sc_ref[...], g_ref[...], win_ref, bd_ref, gq_ref, gk_ref)
    q_ref[...] = q * ATTN_SCALE
    k_ref[...] = k
    v_ref[...] = v
    ext_ref[:, 0:HALO, :] = hist_ref[...]
    ext_ref[:, HALO:, :] = u.reshape(DEC_BATCH, DEC_SEQ, POOL_WIDTH)
    tail_ref[...] = ext_ref[:, ext_rows - HALO:, :]
    pos = PAST_LEN + lax.broadcasted_iota(I32, (DEC_BATCH, DEC_SEQ, 1), 1).reshape(n, 1)
    for gi, w in enumerate(POOL_WINDOWS):
        cols = slice(gi * POOL_GROUP, (gi + 1) * POOL_GROUP)
        e = ext_ref[:, :, cols].reshape(DEC_BATCH * ext_rows, POOL_GROUP)
        win = _window_sum(e, w).reshape(DEC_BATCH, ext_rows, POOL_GROUP)[:, HALO:, :].reshape(n, POOL_GROUP)
        inv_cnt = 1.0 / jnp.minimum(pos + 1, w).astype(F32)
        pooled = win * inv_cnt - u[:, cols]
        y = _dot(pooled.astype(BF16), wpool_ref[gi]) * ps_ref[:, cols]
        ypool_ref[:, cols] = y.astype(BF16)


def _inproj_sample(x, sh1, sc1, g1, win, bd, gq, gk, hist, wpool, ps):
    n = x.shape[0]
    return pl.pallas_call(
        _inproj_sample_kernel,
        out_shape=[jax.ShapeDtypeStruct((n, POOL_WIDTH), BF16),
                   jax.ShapeDtypeStruct((n, ATTN_WIDTH), F32),
                   jax.ShapeDtypeStruct((n, ATTN_WIDTH), F32),
                   jax.ShapeDtypeStruct((n, ATTN_WIDTH), F32),
                   jax.ShapeDtypeStruct((DEC_BATCH, HALO, POOL_WIDTH), F32)],
        scratch_shapes=[pltpu.VMEM((DEC_BATCH, HALO + DEC_SEQ, POOL_WIDTH), F32)],
        compiler_params=pltpu.CompilerParams(vmem_limit_bytes=VMEM_LIMIT),
        name="inproj_sample",
    )(x, sh1, sc1, g1, win, bd, gq, gk, hist, wpool, ps)


def _moba_prompt_kernel(slopes_ref, q_ref, k_ref, v_ref, o_ref,
                        kb_ref, vb_ref, kmf_ref, m_ref, l_ref, acc_ref):
    S = MOBA_BLOCK
    nb = SEQ // S
    hp = pl.program_id(1)
    cb = pl.program_id(2)
    lane = lax.broadcasted_iota(I32, (1, LANES), 1)
    first = lane < HEAD_DIM

    @pl.when(cb == 0)
    def _():
        kf = k_ref[0]
        kb_ref[...] = kf.astype(BF16)
        vb_ref[...] = v_ref[0].astype(BF16)
        kmf_ref[...] = jnp.zeros((LANES, LANES), F32)
        for j in range(nb):
            mean = jnp.sum(kf[j * S:(j + 1) * S], axis=0, keepdims=True) * (1.0 / S)
            kmf_ref[j:j + 1, :] = jnp.where(first, mean, 0.0)
            kmf_ref[nb + j:nb + j + 1, :] = jnp.where(first, 0.0, mean)

    q2 = q_ref[0]
    zero = jnp.zeros_like(q2)
    qm = (jnp.where(first, q2, zero), jnp.where(first, zero, q2))

    gt = _dot_nt(kmf_ref[...].astype(BF16), q2)[0:2 * nb]
    row = lax.broadcasted_iota(I32, (2 * nb, S), 0)
    blk = row % nb
    cnt = jnp.zeros((2 * nb, S), I32)
    for m in range(nb):
        gm = jnp.where(row < nb, gt[m:m + 1, :], gt[nb + m:nb + m + 1, :])
        beats = (gm > gt) | ((gm == gt) & (m < blk))
        cnt = cnt + jnp.where(beats & (m < cb), 1, 0)
    sel = ((cnt < MOBA_TOPK) & (blk < cb)).astype(F32)
    selq = jnp.concatenate([sel, jnp.zeros((LANES - 2 * nb, S), F32)], axis=0).T

    r_i = lax.broadcasted_iota(I32, (S, S), 0)
    c_i = lax.broadcasted_iota(I32, (S, S), 1)
    rel = (c_i - r_i).astype(F32)
    causal = c_i <= r_i

    def scores(h, kblk, joff):
        slope = slopes_ref[2 * hp + h]
        return _dot_nt(qm[h], kblk) + slope * (rel + joff)

    own = pl.multiple_of(cb * S, S)
    k_own = kb_ref[pl.ds(own, S), :]
    v_own = vb_ref[pl.ds(own, S), :]
    for h in range(2):
        s = jnp.where(causal, scores(h, k_own, 0.0), NEG)
        m = jnp.max(s, axis=-1, keepdims=True)
        p = jnp.exp(s - m)
        m_ref[h] = m
        l_ref[h] = jnp.sum(p, axis=-1, keepdims=True)
        acc_ref[h] = _dot(p.astype(BF16), v_own)

    for j in range(nb - 1):
        @pl.when(j < cb)
        def _():
            kblk = kb_ref[j * S:(j + 1) * S, :]
            vblk = vb_ref[j * S:(j + 1) * S, :]
            joff = ((j - cb) * S).astype(F32)
            for h in range(2):
                keep = selq[:, h * nb + j:h * nb + j + 1] > 0.5
                s = jnp.where(keep, scores(h, kblk, joff), NEG)
                m_old = m_ref[h]
                m_new = jnp.maximum(m_old, jnp.max(s, axis=-1, keepdims=True))
                alpha = jnp.exp(m_old - m_new)
                p = jnp.exp(s - m_new)
                l_ref[h] = alpha * l_ref[h] + jnp.sum(p, axis=-1, keepdims=True)
                acc_ref[h] = alpha * acc_ref[h] + _dot(p.astype(BF16), vblk)
                m_ref[h] = m_new

    o0 = acc_ref[0] * (1.0 / l_ref[0])
    o1 = acc_ref[1] * (1.0 / l_ref[1])
    o_ref[0] = jnp.where(first, o0, o1).astype(BF16)


def _moba_prompt(slopes, q, k, v):
    B, T, _ = q.shape
    S = MOBA_BLOCK
    return pl.pallas_call(
        _moba_prompt_kernel,
        grid=(B, N_HEADS // 2, T // S),
        in_specs=[pl.BlockSpec(memory_space=pltpu.SMEM),
                  pl.BlockSpec((1, S, LANES), lambda b, hp, c: (b, c, hp)),
                  pl.BlockSpec((1, T, LANES), lambda b, hp, c: (b, 0, hp)),
                  pl.BlockSpec((1, T, LANES), lambda b, hp, c: (b, 0, hp))],
        out_specs=pl.BlockSpec((1, S, LANES), lambda b, hp, c: (b, c, hp)),
        out_shape=jax.ShapeDtypeStruct((B, T, ATTN_WIDTH), BF16),
        scratch_shapes=[pltpu.VMEM((T, LANES), BF16),
                        pltpu.VMEM((T, LANES), BF16),
                        pltpu.VMEM((LANES, LANES), F32),
                        pltpu.VMEM((2, S, 1), F32),
                        pltpu.VMEM((2, S, 1), F32),
                        pltpu.VMEM((2, S, LANES), F32)],
        compiler_params=pltpu.CompilerParams(
            dimension_semantics=("arbitrary", "arbitrary", "arbitrary"), vmem_limit_bytes=VMEM_LIMIT),
        name="moba_prompt",
    )(slopes, q, k, v)


def _moba_sample_kernel(pt_ref, q_ref, kn_ref, vn_ref, slope_ref, lq_ref, *rest):
    P = PAGES_PER_STEP
    kpages = rest[0:P]
    vpages = rest[P:2 * P]
    o_ref = rest[2 * P]
    s_ref, p_ref, qp_ref, kc_ref, vc_ref, acc_ref, gate_ref, linv_ref = rest[2 * P + 1:]
    S = MOBA_BLOCK
    n_blocks = PAST_LEN // S
    chunk = P * PAGE_SIZE
    n_chunks = PAST_LEN // chunk
    bpc = chunk // S
    st = pl.program_id(1)
    slope = slope_ref[...]
    lq = lq_ref[...]
    lane = lax.broadcasted_iota(I32, (1, LANES), 1)

    @pl.when(st == 0)
    def _():
        q8 = q_ref[0]
        head = lax.broadcasted_iota(I32, (DEC_SEQ, ATTN_WIDTH), 1) // HEAD_DIM
        parts = [jnp.where(head == h, q8, 0.0) for h in range(N_HEADS)]
        parts.append(jnp.zeros((LANES - N_HEADS * DEC_SEQ, ATTN_WIDTH), F32))
        qp_ref[...] = jnp.concatenate(parts, axis=0).astype(BF16)
        gate_ref[...] = jnp.zeros((LANES, LANES), F32)

    @pl.when(st < n_chunks)
    def _():
        for r in range(P):
            kc_ref[:, r * PAGE_SIZE:(r + 1) * PAGE_SIZE] = kpages[r][0].astype(BF16)
        sc = _dot(qp_ref[...], kc_ref[...])
        g = gate_ref[...]
        for r2 in range(bpc):
            gs = jnp.sum(sc[:, r2 * S:(r2 + 1) * S], axis=-1, keepdims=True) * (1.0 / S)
            g = jnp.where(lane == st * bpc + r2, gs, g)
        gate_ref[...] = g
        keypos = st * chunk + lax.broadcasted_iota(I32, (1, chunk), 1)
        dist = (keypos - PAST_LEN).astype(F32) - lq
        s_ref[st] = sc + slope * dist

    @pl.when(st == n_chunks)
    def _():
        gate = gate_ref[...]
        cnt = jnp.zeros((LANES, LANES), I32)
        for m in range(n_blocks):
            gm = gate[:, m:m + 1]
            beats = (gm > gate) | ((gm == gate) & (m < lane))
            cnt = cnt + jnp.where(beats, 1, 0)
        selq = ((cnt < min(MOBA_TOPK, n_blocks)) & (lane < n_blocks)).astype(F32)
        keeps = [selq[:, n:n + 1] > 0.5 for n in range(n_blocks)]

        kn = jnp.concatenate([kn_ref[0], jnp.zeros((LANES - DEC_SEQ, ATTN_WIDTH), F32)], axis=0)
        lane_f = lane.astype(F32)
        sn = _dot_nt(qp_ref[...], kn.astype(BF16)) + slope * (lane_f - lq)
        sn = jnp.where(lane_f <= lq, sn, NEG)

        mxv = sn
        for n in range(n_blocks):
            c, r2 = divmod(n, bpc)
            blk = jnp.where(keeps[n], s_ref[c, :, r2 * S:(r2 + 1) * S], NEG)
            for i in range(S // LANES):
                mxv = jnp.maximum(mxv, blk[:, i * LANES:(i + 1) * LANES])
        mx = jnp.max(mxv, axis=-1, keepdims=True)
        pn = jnp.exp(sn - mx)
        lsv = pn
        for n in range(n_blocks):
            c, r2 = divmod(n, bpc)
            p = jnp.exp(jnp.where(keeps[n], s_ref[c, :, r2 * S:(r2 + 1) * S] - mx, NEG))
            p_ref[c, :, r2 * S:(r2 + 1) * S] = p.astype(BF16)
            for i in range(S // LANES):
                lsv = lsv + p[:, i * LANES:(i + 1) * LANES]
        linv_ref[...] = 1.0 / jnp.sum(lsv, axis=-1, keepdims=True)
        vn = jnp.concatenate([vn_ref[0], jnp.zeros((LANES - DEC_SEQ, ATTN_WIDTH), F32)], axis=0)
        acc_ref[...] = _dot(pn.astype(BF16), vn.astype(BF16))

    @pl.when(st >= n_chunks)
    def _():
        for r in range(P):
            vc_ref[:, r * PAGE_SIZE:(r + 1) * PAGE_SIZE] = vpages[r][0].astype(BF16)
        acc_ref[...] += _dot_nt(p_ref[st - n_chunks], vc_ref[...])

    @pl.when(st == 2 * n_chunks - 1)
    def _():
        o = acc_ref[...] * linv_ref[...]
        head = lax.broadcasted_iota(I32, (DEC_SEQ, ATTN_WIDTH), 1) // HEAD_DIM
        res = jnp.zeros((DEC_SEQ, ATTN_WIDTH), F32)
        for h in range(N_HEADS):
            res = res + jnp.where(head == h, o[h * DEC_SEQ:(h + 1) * DEC_SEQ, :], 0.0)
        o_ref[0] = res.astype(BF16)


def _moba_sample(page_table, q, kn, vn, slope_col, lq_col, cache_kt, cache_vt):
    P = PAGES_PER_STEP
    chunk = P * PAGE_SIZE
    n_chunks = PAST_LEN // chunk
    B = DEC_BATCH

    def kmap(r):
        return lambda b, s, pt: (pt[b, jnp.minimum(s, n_chunks - 1) * P + r], 0, 0)

    def vmap(r):
        return lambda b, s, pt: (pt[b, jnp.maximum(s - n_chunks, 0) * P + r], 0, 0)

    row3 = lambda b, s, pt: (b, 0, 0)
    const2 = lambda b, s, pt: (0, 0)
    page = (1, ATTN_WIDTH, PAGE_SIZE)
    in_specs = [pl.BlockSpec((1, DEC_SEQ, ATTN_WIDTH), row3),
                pl.BlockSpec((1, DEC_SEQ, ATTN_WIDTH), row3),
                pl.BlockSpec((1, DEC_SEQ, ATTN_WIDTH), row3),
                pl.BlockSpec((LANES, 1), const2),
                pl.BlockSpec((LANES, 1), const2)]
    in_specs += [pl.BlockSpec(page, kmap(r)) for r in range(P)]
    in_specs += [pl.BlockSpec(page, vmap(r)) for r in range(P)]
    grid_spec = pltpu.PrefetchScalarGridSpec(
        num_scalar_prefetch=1,
        grid=(B, 2 * n_chunks),
        in_specs=in_specs,
        out_specs=pl.BlockSpec((1, DEC_SEQ, ATTN_WIDTH), row3),
        scratch_shapes=[pltpu.VMEM((n_chunks, LANES, chunk), F32),
                        pltpu.VMEM((n_chunks, LANES, chunk), BF16),
                        pltpu.VMEM((LANES, ATTN_WIDTH), BF16),
                        pltpu.VMEM((ATTN_WIDTH, chunk), BF16),
                        pltpu.VMEM((ATTN_WIDTH, chunk), BF16),
                        pltpu.VMEM((LANES, ATTN_WIDTH), F32),
                        pltpu.VMEM((LANES, LANES), F32),
                        pltpu.VMEM((LANES, 1), F32)])
    return pl.pallas_call(
        _moba_sample_kernel,
        grid_spec=grid_spec,
        out_shape=jax.ShapeDtypeStruct((B, DEC_SEQ, ATTN_WIDTH), BF16),
        compiler_params=pltpu.CompilerParams(
            dimension_semantics=("arbitrary", "arbitrary"), vmem_limit_bytes=VMEM_LIMIT),
        name="moba_sample",
    )(page_table, q, kn, vn, slope_col, lq_col, *([cache_kt] * P), *([cache_vt] * P))


def _outproj_kernel(yp_ref, ya_ref, x_ref, gt1_ref, sh2_ref, sc2_ref, g2_ref, wout_ref, wr_ref, wrhi_ref, br_ref,
                    x1_ref, h2_ref, eid_ref, ew_ref):
    rows = x_ref.shape[0]
    mix = _dot(yp_ref[...], wout_ref[0:POOL_WIDTH, :]) + _dot(ya_ref[...], wout_ref[POOL_WIDTH:, :])
    x1 = x_ref[...] + gt1_ref[...] * mix
    x1_ref[...] = x1
    h2 = _rms_mod(x1, g2_ref[...], sc2_ref[...], sh2_ref[...])
    hh = h2.astype(BF16)
    h2_ref[...] = hh
    hl = (h2 - hh.astype(F32)).astype(BF16)
    both = _dot(hh, wr_ref[...])
    lt = (both[:, 0:LANES] + both[:, LANES:] + _dot(hl, wrhi_ref[...])).T + br_ref[...]
    row8 = lax.broadcasted_iota(I32, (8, rows), 0)
    g8 = lt[0:8]
    gmax = jnp.max(g8, axis=0, keepdims=True)
    gsum = jnp.sum(jnp.exp(g8 - gmax), axis=0, keepdims=True)
    g_w = 1.0 / gsum
    g_idx = jnp.min(jnp.where(g8 == gmax, row8, 8), axis=0, keepdims=True)
    e_in = jnp.zeros((8, rows), F32)
    for g in range(N_GROUPS):
        e_in = e_in + jnp.where(g_idx == g, lt[8 + 8 * g:16 + 8 * g], 0.0)
    m1 = jnp.max(e_in, axis=0, keepdims=True)
    i1 = jnp.min(jnp.where(e_in == m1, row8, 8), axis=0, keepdims=True)
    e_rest = jnp.where(row8 == i1, NEG, e_in)
    m2 = jnp.max(e_rest, axis=0, keepdims=True)
    i2 = jnp.min(jnp.where(e_rest == m2, row8, 8), axis=0, keepdims=True)
    r = jnp.exp(m2 - m1)
    w1 = g_w / (1.0 + r)
    w2 = g_w * r / (1.0 + r)
    base = g_idx * EXPERTS_PER_GROUP
    eid_ref[...] = jnp.where(row8 == 0, base + i1, jnp.where(row8 == 1, base + i2, 0))
    ew_ref[...] = jnp.where(row8 == 0, w1, jnp.where(row8 == 1, w2, 0.0))


def _outproj(yp, ya, x, gt1, sh2, sc2, g2, wout, wr, wrhi, br, per_batch_rows):
    n = x.shape[0]
    tile = min(ROW_TILE, n)
    tok = lambda i: (i, 0)
    const2 = lambda i: (0, 0)
    if per_batch_rows is None:
        mod_spec = pl.BlockSpec((tile, D_MODEL), tok)
    else:
        per = per_batch_rows // tile
        mod_spec = pl.BlockSpec((1, 1, D_MODEL), lambda i: (i // per, 0, 0))
    kern = _outproj_kernel if per_batch_rows is None else _outproj_kernel_batched
    return pl.pallas_call(
        kern,
        grid=(n // tile,),
        in_specs=[pl.BlockSpec((tile, POOL_WIDTH), tok),
                  pl.BlockSpec((tile, ATTN_WIDTH), tok),
                  pl.BlockSpec((tile, D_MODEL), tok),
                  mod_spec, mod_spec, mod_spec,
                  pl.BlockSpec((1, D_MODEL), const2),
                  pl.BlockSpec((2 * POOL_WIDTH, D_MODEL), const2),
                  pl.BlockSpec((D_MODEL, 2 * LANES), const2),
                  pl.BlockSpec((D_MODEL, LANES), const2),
                  pl.BlockSpec((LANES, 1), const2)],
        out_specs=[pl.BlockSpec((tile, D_MODEL), tok),
                   pl.BlockSpec((tile, D_MODEL), tok),
                   pl.BlockSpec((8, tile), lambda i: (0, i)),
                   pl.BlockSpec((8, tile), lambda i: (0, i))],
        out_shape=[jax.ShapeDtypeStruct((n, D_MODEL), F32),
                   jax.ShapeDtypeStruct((n, D_MODEL), BF16),
                   jax.ShapeDtypeStruct((8, n), I32),
                   jax.ShapeDtypeStruct((8, n), F32)],
        compiler_params=pltpu.CompilerParams(vmem_limit_bytes=VMEM_LIMIT),
        name="outproj",
    )(yp, ya, x, gt1, sh2, sc2, g2, wout, wr, wrhi, br)


class _Row0:
    def __init__(self, ref):
        self._ref = ref

    def __getitem__(self, idx):
        return self._ref[0]


def _outproj_kernel_batched(yp_ref, ya_ref, x_ref, gt1_ref, sh2_ref, sc2_ref, *rest):
    _outproj_kernel(yp_ref, ya_ref, x_ref, _Row0(gt1_ref), _Row0(sh2_ref), _Row0(sc2_ref), *rest)


def _moe_kernel(blk_e_ref, nact_ref, x_ref, wg_ref, wu_ref, wd_ref, y_ref, wgb_ref, wub_ref, wdb_ref):
    i = pl.program_id(0)
    prev = blk_e_ref[jnp.maximum(i - 1, 0)]
    active = i < nact_ref[0]

    @pl.when(active & ((i == 0) | (blk_e_ref[i] != prev)))
    def _():
        wgb_ref[...] = wg_ref[0].astype(BF16)
        wub_ref[...] = wu_ref[0].astype(BF16)
        wdb_ref[...] = wd_ref[0].astype(BF16)

    @pl.when(active)
    def _():
        x = x_ref[...]
        g = _dot(x, wgb_ref[...])
        u = _dot(x, wub_ref[...])
        a = g / (1.0 + jnp.exp(-g)) * u
        y_ref[...] = _dot(a.astype(BF16), wdb_ref[...])

    @pl.when(jnp.logical_not(active))
    def _():
        y_ref[...] = jnp.zeros_like(y_ref)


def _moe(blk_e, n_active, xr, w_gate, w_up, w_down):
    n_rows = xr.shape[0]
    n_blocks = n_rows // MOE_ROWS
    grid_spec = pltpu.PrefetchScalarGridSpec(
        num_scalar_prefetch=2,
        grid=(n_blocks,),
        in_specs=[pl.BlockSpec((MOE_ROWS, D_MODEL), lambda i, be, na: (i, 0)),
                  pl.BlockSpec((1, D_MODEL, D_FF), lambda i, be, na: (be[i], 0, 0)),
                  pl.BlockSpec((1, D_MODEL, D_FF), lambda i, be, na: (be[i], 0, 0)),
                  pl.BlockSpec((1, D_FF, D_MODEL), lambda i, be, na: (be[i], 0, 0))],
        out_specs=pl.BlockSpec((MOE_ROWS, D_MODEL), lambda i, be, na: (i, 0)),
        scratch_shapes=[pltpu.VMEM((D_MODEL, D_FF), BF16),
                        pltpu.VMEM((D_MODEL, D_FF), BF16),
                        pltpu.VMEM((D_FF, D_MODEL), BF16)])
    return pl.pallas_call(
        _moe_kernel,
        grid_spec=grid_spec,
        out_shape=jax.ShapeDtypeStruct((n_rows, D_MODEL), F32),
        compiler_params=pltpu.CompilerParams(
            dimension_semantics=("arbitrary",), vmem_limit_bytes=VMEM_LIMIT),
        name="moe",
    )(blk_e, n_active, xr, w_gate, w_up, w_down)


def _final_kernel(x1_ref, o0_ref, o1_ref, w_ref, gt2_ref, y_ref):
    w = w_ref[...]
    moe = o0_ref[...] * w[:, 0:1] + o1_ref[...] * w[:, 1:2]
    y_ref[...] = x1_ref[...] + gt2_ref[...] * moe


def _final_kernel_batched(x1_ref, o0_ref, o1_ref, w_ref, gt2_ref, y_ref):
    _final_kernel(x1_ref, o0_ref, o1_ref, w_ref, _Row0(gt2_ref), y_ref)


def _final(x1, o0, o1, w2, gt2, per_batch_rows):
    n = x1.shape[0]
    tile = min(ROW_TILE, n)
    tok = lambda i: (i, 0)
    if per_batch_rows is None:
        mod_spec = pl.BlockSpec((tile, D_MODEL), tok)
        kern = _final_kernel
    else:
        per = per_batch_rows // tile
        mod_spec = pl.BlockSpec((1, 1, D_MODEL), lambda i: (i // per, 0, 0))
        kern = _final_kernel_batched
    return pl.pallas_call(
        kern,
        grid=(n // tile,),
        in_specs=[pl.BlockSpec((tile, D_MODEL), tok),
                  pl.BlockSpec((tile, D_MODEL), tok),
                  pl.BlockSpec((tile, D_MODEL), tok),
                  pl.BlockSpec((tile, EXPERT_TOPK), tok),
                  mod_spec],
        out_specs=pl.BlockSpec((tile, D_MODEL), tok),
        out_shape=jax.ShapeDtypeStruct((n, D_MODEL), F32),
        compiler_params=pltpu.CompilerParams(vmem_limit_bytes=VMEM_LIMIT),
        name="final",
    )(x1, o0, o1, w2, gt2)


def _dispatch(eid):
    A = eid.shape[0]
    n_blocks = A // MOE_ROWS + N_EXPERTS
    order = jnp.argsort(eid)
    e_s = eid[order]
    tok_s = (order // EXPERT_TOPK).astype(I32)
    counts = jnp.zeros((N_EXPERTS,), I32).at[eid].add(1)
    start = jnp.cumsum(counts) - counts
    padded = (counts + MOE_ROWS - 1) // MOE_ROWS * MOE_ROWS
    pend = jnp.cumsum(padded)
    pstart = pend - padded
    dest_s = (pstart[e_s] + jnp.arange(A, dtype=I32) - start[e_s]).astype(I32)
    row_tok = jnp.zeros((n_blocks * MOE_ROWS,), I32).at[dest_s].set(tok_s)
    dest = jnp.zeros((A,), I32).at[order].set(dest_s)
    blk_start = jnp.arange(n_blocks, dtype=I32) * MOE_ROWS
    blk_e = jnp.minimum(jnp.sum((pend[None, :] <= blk_start[:, None]).astype(I32), axis=1), N_EXPERTS - 1)
    n_active = (pend[-1] // MOE_ROWS).astype(I32).reshape(1)
    return row_tok, dest, blk_e, n_active


def kernel(x_prompt, x_sample, cache_k, cache_v, state_pool, page_table, c_prompt, c_sample, w_ada, b_ada,
           g_attn_norm, w_in, g_q, g_k, w_pool, pool_scale, w_out, g_ffn_norm, w_group, b_group, w_expert,
           b_expert, w_gate, w_up, w_down):
    D = D_MODEL
    B, T, _ = x_prompt.shape
    n_s = DEC_BATCH * DEC_SEQ
    n_p = B * T
    layer = 0

    win = w_in[layer].astype(BF16)
    wout = w_out[layer].astype(BF16)
    wpool = w_pool[layer].astype(BF16)
    g1 = g_attn_norm[layer].reshape(1, D)
    g2 = g_ffn_norm[layer].reshape(1, D)
    gq = jnp.tile(g_q[layer], N_HEADS).reshape(1, ATTN_WIDTH)
    gk = jnp.tile(g_k[layer], N_HEADS).reshape(1, ATTN_WIDTH)
    ps = pool_scale[layer].reshape(1, POOL_WIDTH)
    hd = jnp.arange(ATTN_WIDTH) // HEAD_DIM
    bd = (hd[:, None] == hd[None, :]).astype(BF16)
    slopes = jnp.exp2(-8.0 * (jnp.arange(N_HEADS, dtype=F32) + 1.0) / N_HEADS)
    qlane = jnp.arange(LANES)
    slope_col = slopes[jnp.minimum(qlane // DEC_SEQ, N_HEADS - 1)].reshape(LANES, 1)
    lq_col = (qlane % DEC_SEQ).astype(F32).reshape(LANES, 1)
    wr = jnp.zeros((D, LANES), F32).at[:, 0:N_GROUPS].set(w_group[layer]).at[:, 8:8 + N_EXPERTS].set(w_expert[layer])
    wr_hi = wr.astype(BF16)
    wr_lo = (wr - wr_hi.astype(F32)).astype(BF16)
    wr_both = jnp.concatenate([wr_hi, wr_lo], axis=1)
    br = jnp.zeros((LANES,), F32).at[0:N_GROUPS].set(b_group[layer]).at[N_GROUPS:8].set(NEG)
    br = br.at[8:8 + N_EXPERTS].set(b_expert[layer]).reshape(LANES, 1)

    mod = _ada(jnp.concatenate([c_prompt, c_sample], axis=0), w_ada[layer], b_ada[layer])
    mod_p = mod[:B].reshape(B, 1, 6 * D)
    mod_s = jnp.repeat(mod[B:], DEC_SEQ, axis=0)

    ypool_p, q_p, k_p, v_p, tail_p = _inproj_prompt(x_prompt, mod_p, g1, win, bd, gq, gk, wpool, ps)
    yattn_p = _moba_prompt(slopes, q_p, k_p, v_p)
    mp = lambda j: mod_p[:, :, j * D:(j + 1) * D]
    x1_p, h2_p, eid_p, ew_p = _outproj(ypool_p.reshape(n_p, POOL_WIDTH), yattn_p.reshape(n_p, ATTN_WIDTH),
                                       x_prompt.reshape(n_p, D), mp(2), mp(3), mp(4), g2, wout, wr_both, wr_hi, br, T)

    ms = lambda j: mod_s[:, j * D:(j + 1) * D]
    hist = jnp.concatenate([jnp.zeros((DEC_BATCH, HALO - POOL_HIST, POOL_WIDTH), F32), state_pool[layer]], axis=1)
    xs = x_sample.reshape(n_s, D)
    ypool_s, q_s, k_s, v_s, tail_s = _inproj_sample(xs, ms(0), ms(1), g1, win, bd, gq, gk, hist, wpool, ps)
    r3 = lambda a: a.reshape(DEC_BATCH, DEC_SEQ, ATTN_WIDTH)
    n_phys = cache_k.shape[1]
    ck = jnp.transpose(cache_k[layer], (0, 2, 3, 1)).reshape(n_phys, ATTN_WIDTH, PAGE_SIZE)
    cv = jnp.transpose(cache_v[layer], (0, 2, 3, 1)).reshape(n_phys, ATTN_WIDTH, PAGE_SIZE)
    yattn_s = _moba_sample(page_table, r3(q_s), r3(k_s), r3(v_s), slope_col, lq_col, ck, cv)
    x1_s, h2_s, eid_s, ew_s = _outproj(ypool_s, yattn_s.reshape(n_s, ATTN_WIDTH), xs, ms(2), ms(3), ms(4), g2,
                                       wout, wr_both, wr_hi, br, None)

    eid = jnp.concatenate([eid_p[0:2], eid_s[0:2]], axis=1).T.reshape(-1)
    ew = jnp.concatenate([ew_p[0:2], ew_s[0:2]], axis=1).T
    h2 = jnp.concatenate([h2_p, h2_s], axis=0)
    row_tok, dest, blk_e, n_active = _dispatch(eid)
    yr = _moe(blk_e, n_active, h2[row_tok], w_gate[layer], w_up[layer], w_down[layer])
    dest2 = dest.reshape(n_p + n_s, EXPERT_TOPK)
    y_p = _final(x1_p, yr[dest2[:n_p, 0]], yr[dest2[:n_p, 1]], ew[:n_p], mp(5), T)
    y_s = _final(x1_s, yr[dest2[n_p:, 0]], yr[dest2[n_p:, 1]], ew[n_p:], ms(5), None)

    k4 = lambda a, b, l: a.reshape(1, b, l, N_HEADS, HEAD_DIM)
    return (y_p.reshape(B, T, D), y_s.reshape(DEC_BATCH, DEC_SEQ, D),
            k4(k_p, B, T), k4(v_p, B, T), tail_p[None, :, HALO - POOL_HIST:, :],
            k4(k_s, DEC_BATCH, DEC_SEQ), k4(v_s, DEC_BATCH, DEC_SEQ), tail_s[None, :, HALO - POOL_HIST:, :])
```

```python
import functools

import jax
import jax.numpy as jnp
from jax import lax
from jax.experimental import pallas as pl
from jax.experimental.pallas import tpu as pltpu

F32 = jnp.float32
BF16 = jnp.bfloat16
I32 = jnp.int32

D_MODEL = 1024
BATCH = 8
SEQ = 2048
DEC_BATCH = 32
DEC_SEQ = 8
PAST_LEN = 16384
PAGE_SIZE = 128
POOL_WIDTH = 512
POOL_WINDOWS = (2, 4, 8, 16)
POOL_GROUP = 128
POOL_HIST = 15
HALO = 16
N_HEADS = 8
HEAD_DIM = 64
ATTN_WIDTH = 512
MOBA_BLOCK = 256
MOBA_TOPK = 3
ATTN_SCALE = HEAD_DIM ** -0.5
MIX_IN = POOL_WIDTH + 3 * ATTN_WIDTH
N_GROUPS = 4
EXPERTS_PER_GROUP = 8
N_EXPERTS = 32
EXPERT_TOPK = 2
D_FF = 512
EPS = 1e-6
NEG = -1e30

LANES = 128
ROW_TILE = 512
MOE_ROWS = 256
PAGES_PER_STEP = 16
VMEM_LIMIT = 56 * 1024 * 1024

_NT = (((1,), (1,)), ((), ()))


def _dot(a, b):
    return jnp.dot(a, b, preferred_element_type=F32)


def _dot_nt(a, b):
    return lax.dot_general(a, b, _NT, preferred_element_type=F32)


def _split_dot(a, b01):
    hi = a.astype(BF16)
    lo = (a - hi.astype(F32)).astype(BF16)
    return _dot(hi, b01) + _dot(lo, b01)


def _rms_mod(x, g, sc, sh):
    ms = jnp.mean(x * x, axis=-1, keepdims=True)
    return x * lax.rsqrt(ms + EPS) * g * (1.0 + sc) + sh


def _ada_kernel(c_ref, w_ref, b_ref, o_ref):
    c = c_ref[...]
    a = c / (1.0 + jnp.exp(-c))
    o_ref[...] = _dot(a.astype(BF16), w_ref[...].astype(BF16)) + b_ref[...]


def _ada(c_all, w_ada, b_ada):
    n = c_all.shape[0]
    tn = 1536
    return pl.pallas_call(
        _ada_kernel,
        grid=(6 * D_MODEL // tn,),
        in_specs=[pl.BlockSpec((n, D_MODEL), lambda j: (0, 0)),
                  pl.BlockSpec((D_MODEL, tn), lambda j: (0, j)),
                  pl.BlockSpec((1, tn), lambda j: (0, j))],
        out_specs=pl.BlockSpec((n, tn), lambda j: (0, j)),
        out_shape=jax.ShapeDtypeStruct((n, 6 * D_MODEL), F32),
        compiler_params=pltpu.CompilerParams(vmem_limit_bytes=VMEM_LIMIT),
        name="ada",
    )(c_all, w_ada, b_ada.reshape(1, -1))


def _inproj_core(x, sh1, sc1, g, win_ref, bd_ref, gq_ref, gk_ref):
    h = _rms_mod(x, g, sc1, sh1)
    z = _dot(h.astype(BF16), win_ref[...])
    u = z[:, 0:POOL_WIDTH]
    q = z[:, POOL_WIDTH:POOL_WIDTH + ATTN_WIDTH]
    k = z[:, POOL_WIDTH + ATTN_WIDTH:POOL_WIDTH + 2 * ATTN_WIDTH]
    v = z[:, POOL_WIDTH + 2 * ATTN_WIDTH:]
    bd = bd_ref[...]
    q = q * lax.rsqrt(_split_dot(q * q, bd) * (1.0 / HEAD_DIM) + EPS) * gq_ref[...]
    k = k * lax.rsqrt(_split_dot(k * k, bd) * (1.0 / HEAD_DIM) + EPS) * gk_ref[...]
    return u, q, k, v


def _window_sum(e, w):
    s = e
    sh = 1
    while sh < w:
        s = s + pltpu.roll(s, sh, axis=0)
        sh *= 2
    return s


def _inproj_prompt_kernel(x_ref, mod_ref, g_ref, win_ref, bd_ref, gq_ref, gk_ref, wpool_ref, ps_ref,
                          ypool_ref, q_ref, k_ref, v_ref, tail_ref, ext_ref):
    t = pl.program_id(1)
    nt = pl.num_programs(1)
    x = x_ref[0]
    sh1 = mod_ref[0, :, 0:D_MODEL]
    sc1 = mod_ref[0, :, D_MODEL:2 * D_MODEL]
    u, q, k, v = _inproj_core(x, sh1, sc1, g_ref[...], win_ref, bd_ref, gq_ref, gk_ref)
    q_ref[0] = (q * ATTN_SCALE).astype(BF16)
    k_ref[0] = k
    v_ref[0] = v

    @pl.when(t == 0)
    def _():
        ext_ref[0:HALO, :] = jnp.zeros((HALO, POOL_WIDTH), F32)

    ext_ref[HALO:, :] = u
    pos = t * ROW_TILE + lax.broadcasted_iota(I32, (ROW_TILE, 1), 0)
    for gi, w in enumerate(POOL_WINDOWS):
        cols = slice(gi * POOL_GROUP, (gi + 1) * POOL_GROUP)
        win = _window_sum(ext_ref[:, cols], w)[HALO:]
        inv_cnt = 1.0 / jnp.minimum(pos + 1, w).astype(F32)
        pooled = win * inv_cnt - u[:, cols]
        y = _dot(pooled.astype(BF16), wpool_ref[gi]) * ps_ref[:, cols]
        ypool_ref[0, :, cols] = y.astype(BF16)
    last = u[ROW_TILE - HALO:, :]
    ext_ref[0:HALO, :] = last

    @pl.when(t == nt - 1)
    def _():
        tail_ref[0] = last


def _inproj_prompt(x, mod_p, g1, win, bd, gq, gk, wpool, ps):
    B, T, D = x.shape
    nt = T // ROW_TILE
    const2 = lambda b, t: (0, 0)
    tok = lambda b, t: (b, t, 0)
    return pl.pallas_call(
        _inproj_prompt_kernel,
        grid=(B, nt),
        in_specs=[pl.BlockSpec((1, ROW_TILE, D), tok),
                  pl.BlockSpec((1, 1, 6 * D), lambda b, t: (b, 0, 0)),
                  pl.BlockSpec((1, D), const2),
                  pl.BlockSpec((D, MIX_IN), const2),
                  pl.BlockSpec((ATTN_WIDTH, ATTN_WIDTH), const2),
                  pl.BlockSpec((1, ATTN_WIDTH), const2),
                  pl.BlockSpec((1, ATTN_WIDTH), const2),
                  pl.BlockSpec((4, POOL_GROUP, POOL_GROUP), lambda b, t: (0, 0, 0)),
                  pl.BlockSpec((1, POOL_WIDTH), const2)],
        out_specs=[pl.BlockSpec((1, ROW_TILE, POOL_WIDTH), tok),
                   pl.BlockSpec((1, ROW_TILE, ATTN_WIDTH), tok),
                   pl.BlockSpec((1, ROW_TILE, ATTN_WIDTH), tok),
                   pl.BlockSpec((1, ROW_TILE, ATTN_WIDTH), tok),
                   pl.BlockSpec((1, HALO, POOL_WIDTH), lambda b, t: (b, 0, 0))],
        out_shape=[jax.ShapeDtypeStruct((B, T, POOL_WIDTH), BF16),
                   jax.ShapeDtypeStruct((B, T, ATTN_WIDTH), BF16),
                   jax.ShapeDtypeStruct((B, T, ATTN_WIDTH), F32),
                   jax.ShapeDtypeStruct((B, T, ATTN_WIDTH), F32),
                   jax.ShapeDtypeStruct((B, HALO, POOL_WIDTH), F32)],
        scratch_shapes=[pltpu.VMEM((HALO + ROW_TILE, POOL_WIDTH), F32)],
        compiler_params=pltpu.CompilerParams(
            dimension_semantics=("arbitrary", "arbitrary"), vmem_limit_bytes=VMEM_LIMIT),
        name="inproj_prompt",
    )(x, mod_p, g1, win, bd, gq, gk, wpool, ps)


def _inproj_sample_kernel(x_ref, sh_ref, sc_ref, g_ref, win_ref, bd_ref, gq_ref, gk_ref, hist_ref, wpool_ref,
                          ps_ref, ypool_ref, q_ref, k_ref, v_ref, tail_ref, ext_ref):
    n = DEC_BATCH * DEC_SEQ
    ext_rows = HALO + DEC_SEQ
    u, q, k, v = _inproj_core(x_ref[...], sh_ref[...], sc_ref[...], g_ref[...], win_ref, bd_ref, gq_ref, gk_ref)
    q_ref[...] = q * ATTN_SCALE
    k_ref[...] = k
    v_ref[...] = v
    ext_ref[:, 0:HALO, :] = hist_ref[...]
    ext_ref[:, HALO:, :] = u.reshape(DEC_BATCH, DEC_SEQ, POOL_WIDTH)
    tail_ref[...] = ext_ref[:, ext_rows - HALO:, :]
    pos = PAST_LEN + lax.broadcasted_iota(I32, (DEC_BATCH, DEC_SEQ, 1), 1).reshape(n, 1)
    for gi, w in enumerate(POOL_WINDOWS):
        cols = slice(gi * POOL_GROUP, (gi + 1) * POOL_GROUP)
        e = ext_ref[:, :, cols].reshape(DEC_BATCH * ext_rows, POOL_GROUP)
        win = _window_sum(e, w).reshape(DEC_BATCH, ext_rows, POOL_GROUP)[:, HALO:, :].reshape(n, POOL_GROUP)
        inv_cnt = 1.0 / jnp.minimum(pos + 1, w).astype(F32)
        pooled = win * inv_cnt - u[:, cols]
        y = _dot(pooled.astype(BF16), wpool_ref[gi]) * ps_ref[:, cols]
        ypool_ref[:, cols] = y.astype(BF16)


def _inproj_sample(x, sh1, sc1, g1, win, bd, gq, gk, hist, wpool, ps):
    n = x.shape[0]
    return pl.pallas_call(
        _inproj_sample_kernel,
        out_shape=[jax.ShapeDtypeStruct((n, POOL_WIDTH), BF16),
                   jax.ShapeDtypeStruct((n, ATTN_WIDTH), F32),
                   jax.ShapeDtypeStruct((n, ATTN_WIDTH), F32),
                   jax.ShapeDtypeStruct((n, ATTN_WIDTH), F32),
                   jax.ShapeDtypeStruct((DEC_BATCH, HALO, POOL_WIDTH), F32)],
        scratch_shapes=[pltpu.VMEM((DEC_BATCH, HALO + DEC_SEQ, POOL_WIDTH), F32)],
        compiler_params=pltpu.CompilerParams(vmem_limit_bytes=VMEM_LIMIT),
        name="inproj_sample",
    )(x, sh1, sc1, g1, win, bd, gq, gk, hist, wpool, ps)


N_BIAS_LANES = 3


def _moba_prompt_kernel(slopes_ref, q_ref, k_ref, v_ref, o_ref,
                        ka_ref, va_ref, kmf_ref, s_ref, mx_ref, acc_ref):
    S = MOBA_BLOCK
    T = k_ref.shape[1]
    nb = T // S
    hp = pl.program_id(1)
    cb = pl.program_id(2)
    lane = lax.broadcasted_iota(I32, (1, LANES), 1)
    real = (lane < HEAD_DIM, lane >= HEAD_DIM)
    extra = (lane - HEAD_DIM, lane)

    @pl.when(cb == 0)
    def _():
        kf = k_ref[0]
        vf = v_ref[0]
        kmf_ref[...] = jnp.zeros((LANES, LANES), F32)
        for j in range(nb):
            mean = jnp.sum(kf[j * S:(j + 1) * S], axis=0, keepdims=True) * (1.0 / S)
            kmf_ref[j:j + 1, :] = jnp.where(real[0], mean, 0.0)
            kmf_ref[nb + j:nb + j + 1, :] = jnp.where(real[0], 0.0, mean)
        key_i = lax.broadcasted_iota(I32, (T, 1), 0)
        key_blk = key_i // S
        for h in range(2):
            b = slopes_ref[2 * hp + h] * key_i.astype(F32)
            p0 = b.astype(BF16).astype(F32)
            p1 = (b - p0).astype(BF16).astype(F32)
            p2 = b - p0 - p1
            e = extra[h]
            onehot = jnp.where((e >= N_BIAS_LANES) & (e - N_BIAS_LANES == key_blk), 1.0, 0.0)
            feat = jnp.where(e == 0, p0, jnp.where(e == 1, p1, jnp.where(e == 2, p2, onehot)))
            ka_ref[h] = jnp.where(real[h], kf, feat).astype(BF16)
            va_ref[h] = jnp.where(real[h], vf, 1.0).astype(BF16)

    q2 = q_ref[0]

    gt = _dot_nt(kmf_ref[...].astype(BF16), q2)[0:2 * nb]
    row = lax.broadcasted_iota(I32, (2 * nb, S), 0)
    blk = row % nb
    cnt = jnp.zeros((2 * nb, S), I32)
    for m in range(nb):
        gm = jnp.where(row < nb, gt[m:m + 1, :], gt[nb + m:nb + m + 1, :])
        beats = (gm > gt) | ((gm == gt) & (m < blk))
        cnt = cnt + jnp.where(beats & (m < cb), 1, 0)
    keep = (((cnt < MOBA_TOPK) & (blk < cb)) | (blk == cb)).astype(F32)
    keepq = jnp.concatenate([keep, jnp.zeros((LANES - 2 * nb, S), F32)], axis=0).T
    maskv = jnp.where(keepq > 0.5, 0.0, NEG)
    mask_lanes = (pltpu.roll(maskv, HEAD_DIM + N_BIAS_LANES, axis=1),
                  pltpu.roll(maskv, (N_BIAS_LANES - nb) % LANES, axis=1))
    qf = q2.astype(F32)
    qa = []
    for h in range(2):
        e = extra[h]
        feat = jnp.where(e < N_BIAS_LANES, 1.0, jnp.where(e < N_BIAS_LANES + nb, mask_lanes[h], 0.0))
        qa.append(jnp.where(real[h], qf, feat).astype(BF16))

    r_i = lax.broadcasted_iota(I32, (S, S), 0)
    c_i = lax.broadcasted_iota(I32, (S, S), 1)
    causal = c_i <= r_i

    def halves_max(s):
        return jnp.maximum(s[:, 0:LANES], s[:, LANES:])

    own = pl.multiple_of(cb * S, S)
    for h in range(2):
        s = jnp.where(causal, _dot_nt(qa[h], ka_ref[h, pl.ds(own, S), :]), NEG)
        s_ref[h, cb] = s
        mx_ref[h] = halves_max(s)

    for j in range(nb - 1):
        @pl.when(j < cb)
        def _():
            for h in range(2):
                s = _dot_nt(qa[h], ka_ref[h, j * S:(j + 1) * S, :])
                s_ref[h, j] = s
                mx_ref[h] = jnp.maximum(mx_ref[h], halves_max(s))

    m = [jnp.max(mx_ref[h], axis=-1, keepdims=True) for h in range(2)]
    acc_ref[...] = jnp.zeros_like(acc_ref)
    for j in range(nb):
        @pl.when(j <= cb)
        def _():
            for h in range(2):
                p = jnp.exp(s_ref[h, j] - m[h])
                acc_ref[h] += _dot(p.astype(BF16), va_ref[h, j * S:(j + 1) * S, :])

    a0 = acc_ref[0]
    a1 = acc_ref[1]
    o0 = a0 * (1.0 / a0[:, HEAD_DIM:HEAD_DIM + 1])
    o1 = a1 * (1.0 / a1[:, 0:1])
    o_ref[0] = jnp.where(real[0], o0, o1).astype(BF16)


def _moba_prompt(slopes, q, k, v):
    B, T, _ = q.shape
    S = MOBA_BLOCK
    return pl.pallas_call(
        _moba_prompt_kernel,
        grid=(B, N_HEADS // 2, T // S),
        in_specs=[pl.BlockSpec(memory_space=pltpu.SMEM),
                  pl.BlockSpec((1, S, LANES), lambda b, hp, c: (b, c, hp)),
                  pl.BlockSpec((1, T, LANES), lambda b, hp, c: (b, 0, hp)),
                  pl.BlockSpec((1, T, LANES), lambda b, hp, c: (b, 0, hp))],
        out_specs=pl.BlockSpec((1, S, LANES), lambda b, hp, c: (b, c, hp)),
        out_shape=jax.ShapeDtypeStruct((B, T, ATTN_WIDTH), BF16),
        scratch_shapes=[pltpu.VMEM((2, T, LANES), BF16),
                        pltpu.VMEM((2, T, LANES), BF16),
                        pltpu.VMEM((LANES, LANES), F32),
                        pltpu.VMEM((2, T // S, S, S), F32),
                        pltpu.VMEM((2, S, LANES), F32),
                        pltpu.VMEM((2, S, LANES), F32)],
        compiler_params=pltpu.CompilerParams(
            dimension_semantics=("arbitrary", "arbitrary", "arbitrary"), vmem_limit_bytes=VMEM_LIMIT),
        name="moba_prompt",
    )(slopes, q, k, v)


def _moba_sample_kernel(pt_ref, q_ref, kn_ref, vn_ref, slope_ref, lq_ref, *rest):
    P = PAGES_PER_STEP
    kpages = rest[0:P]
    vpages = rest[P:2 * P]
    o_ref = rest[2 * P]
    s_ref, p_ref, qp_ref, kc_ref, vc_ref, acc_ref, gate_ref, linv_ref = rest[2 * P + 1:]
    S = MOBA_BLOCK
    n_blocks = PAST_LEN // S
    chunk = P * PAGE_SIZE
    n_chunks = PAST_LEN // chunk
    bpc = chunk // S
    st = pl.program_id(1)
    slope = slope_ref[...]
    lq = lq_ref[...]
    lane = lax.broadcasted_iota(I32, (1, LANES), 1)

    @pl.when(st == 0)
    def _():
        q8 = q_ref[0]
        head = lax.broadcasted_iota(I32, (DEC_SEQ, ATTN_WIDTH), 1) // HEAD_DIM
        parts = [jnp.where(head == h, q8, 0.0) for h in range(N_HEADS)]
        parts.append(jnp.zeros((LANES - N_HEADS * DEC_SEQ, ATTN_WIDTH), F32))
        qp_ref[...] = jnp.concatenate(parts, axis=0).astype(BF16)
        gate_ref[...] = jnp.zeros((LANES, LANES), F32)

    @pl.when(st < n_chunks)
    def _():
        for r in range(P):
            kc_ref[:, r * PAGE_SIZE:(r + 1) * PAGE_SIZE] = kpages[r][0].astype(BF16)
        sc = _dot(qp_ref[...], kc_ref[...])
        g = gate_ref[...]
        for r2 in range(bpc):
            gs = jnp.sum(sc[:, r2 * S:(r2 + 1) * S], axis=-1, keepdims=True) * (1.0 / S)
            g = jnp.where(lane == st * bpc + r2, gs, g)
        gate_ref[...] = g
        keypos = st * chunk + lax.broadcasted_iota(I32, (1, chunk), 1)
        dist = (keypos - PAST_LEN).astype(F32) - lq
        s_ref[st] = sc + slope * dist

    @pl.when(st == n_chunks)
    def _():
        gate = gate_ref[...]
        cnt = jnp.zeros((LANES, LANES), I32)
        for m in range(n_blocks):
            gm = gate[:, m:m + 1]
            beats = (gm > gate) | ((gm == gate) & (m < lane))
            cnt = cnt + jnp.where(beats, 1, 0)
        selq = ((cnt < min(MOBA_TOPK, n_blocks)) & (lane < n_blocks)).astype(F32)
        keeps = [selq[:, n:n + 1] > 0.5 for n in range(n_blocks)]

        kn = jnp.concatenate([kn_ref[0], jnp.zeros((LANES - DEC_SEQ, ATTN_WIDTH), F32)], axis=0)
        lane_f = lane.astype(F32)
        sn = _dot_nt(qp_ref[...], kn.astype(BF16)) + slope * (lane_f - lq)
        sn = jnp.where(lane_f <= lq, sn, NEG)

        mxv = sn
        for n in range(n_blocks):
            c, r2 = divmod(n, bpc)
            blk = jnp.where(keeps[n], s_ref[c, :, r2 * S:(r2 + 1) * S], NEG)
            for i in range(S // LANES):
                mxv = jnp.maximum(mxv, blk[:, i * LANES:(i + 1) * LANES])
        mx = jnp.max(mxv, axis=-1, keepdims=True)
        pn = jnp.exp(sn - mx)
        lsv = pn
        for n in range(n_blocks):
            c, r2 = divmod(n, bpc)
            p = jnp.exp(jnp.where(keeps[n], s_ref[c, :, r2 * S:(r2 + 1) * S] - mx, NEG))
            p_ref[c, :, r2 * S:(r2 + 1) * S] = p.astype(BF16)
            for i in range(S // LANES):
                lsv = lsv + p[:, i * LANES:(i + 1) * LANES]
        linv_ref[...] = 1.0 / jnp.sum(lsv, axis=-1, keepdims=True)
        vn = jnp.concatenate([vn_ref[0], jnp.zeros((LANES - DEC_SEQ, ATTN_WIDTH), F32)], axis=0)
        acc_ref[...] = _dot(pn.astype(BF16), vn.astype(BF16))

    @pl.when(st >= n_chunks)
    def _():
        for r in range(P):
            vc_ref[:, r * PAGE_SIZE:(r + 1) * PAGE_SIZE] = vpages[r][0].astype(BF16)
        acc_ref[...] += _dot_nt(p_ref[st - n_chunks], vc_ref[...])

    @pl.when(st == 2 * n_chunks - 1)
    def _():
        o = acc_ref[...] * linv_ref[...]
        head = lax.broadcasted_iota(I32, (DEC_SEQ, ATTN_WIDTH), 1) // HEAD_DIM
        res = jnp.zeros((DEC_SEQ, ATTN_WIDTH), F32)
        for h in range(N_HEADS):
            res = res + jnp.where(head == h, o[h * DEC_SEQ:(h + 1) * DEC_SEQ, :], 0.0)
        o_ref[0] = res.astype(BF16)


def _moba_sample(page_table, q, kn, vn, slope_col, lq_col, cache_kt, cache_vt):
    P = PAGES_PER_STEP
    chunk = P * PAGE_SIZE
    n_chunks = PAST_LEN // chunk
    B = DEC_BATCH

    def kmap(r):
        return lambda b, s, pt: (pt[b, jnp.minimum(s, n_chunks - 1) * P + r], 0, 0)

    def vmap(r):
        return lambda b, s, pt: (pt[b, jnp.maximum(s - n_chunks, 0) * P + r], 0, 0)

    row3 = lambda b, s, pt: (b, 0, 0)
    const2 = lambda b, s, pt: (0, 0)
    page = (1, ATTN_WIDTH, PAGE_SIZE)
    in_specs = [pl.BlockSpec((1, DEC_SEQ, ATTN_WIDTH), row3),
                pl.BlockSpec((1, DEC_SEQ, ATTN_WIDTH), row3),
                pl.BlockSpec((1, DEC_SEQ, ATTN_WIDTH), row3),
                pl.BlockSpec((LANES, 1), const2),
                pl.BlockSpec((LANES, 1), const2)]
    in_specs += [pl.BlockSpec(page, kmap(r)) for r in range(P)]
    in_specs += [pl.BlockSpec(page, vmap(r)) for r in range(P)]
    grid_spec = pltpu.PrefetchScalarGridSpec(
        num_scalar_prefetch=1,
        grid=(B, 2 * n_chunks),
        in_specs=in_specs,
        out_specs=pl.BlockSpec((1, DEC_SEQ, ATTN_WIDTH), row3),
        scratch_shapes=[pltpu.VMEM((n_chunks, LANES, chunk), F32),
                        pltpu.VMEM((n_chunks, LANES, chunk), BF16),
                        pltpu.VMEM((LANES, ATTN_WIDTH), BF16),
                        pltpu.VMEM((ATTN_WIDTH, chunk), BF16),
                        pltpu.VMEM((ATTN_WIDTH, chunk), BF16),
                        pltpu.VMEM((LANES, ATTN_WIDTH), F32),
                        pltpu.VMEM((LANES, LANES), F32),
                        pltpu.VMEM((LANES, 1), F32)])
    return pl.pallas_call(
        _moba_sample_kernel,
        grid_spec=grid_spec,
        out_shape=jax.ShapeDtypeStruct((B, DEC_SEQ, ATTN_WIDTH), BF16),
        compiler_params=pltpu.CompilerParams(
            dimension_semantics=("arbitrary", "arbitrary"), vmem_limit_bytes=VMEM_LIMIT),
        name="moba_sample",
    )(page_table, q, kn, vn, slope_col, lq_col, *([cache_kt] * P), *([cache_vt] * P))


def _mod_rows(ref):
    return ref[...].reshape(-1, ref.shape[-1])


def _outproj_kernel(yp_ref, ya_ref, x_ref, gt1_ref, sh2_ref, sc2_ref, g2_ref, wout_ref, wr_ref, wrhi_ref, br_ref,
                    tri_ref, cntin_ref, x1_ref, h2_ref, meta_ref, ew_ref, cntout_ref, cnt_ref):
    rows = x_ref.shape[0]

    @pl.when(pl.program_id(0) == 0)
    def _():
        cnt_ref[...] = cntin_ref[...]

    mix = _dot(yp_ref[...], wout_ref[0:POOL_WIDTH, :]) + _dot(ya_ref[...], wout_ref[POOL_WIDTH:, :])
    x1 = x_ref[...] + _mod_rows(gt1_ref) * mix
    x1_ref[...] = x1
    h2 = _rms_mod(x1, g2_ref[...], _mod_rows(sc2_ref), _mod_rows(sh2_ref))
    h2_ref[...] = h2
    hh = h2.astype(BF16)
    hl = (h2 - hh.astype(F32)).astype(BF16)
    both = _dot(hh, wr_ref[...])
    lt = (both[:, 0:LANES] + both[:, LANES:] + _dot(hl, wrhi_ref[...])).T + br_ref[...]
    row8 = lax.broadcasted_iota(I32, (8, rows), 0)
    g8 = lt[0:8]
    gmax = jnp.max(g8, axis=0, keepdims=True)
    gsum = jnp.sum(jnp.exp(g8 - gmax), axis=0, keepdims=True)
    g_w = 1.0 / gsum
    g_idx = jnp.min(jnp.where(g8 == gmax, row8, 8), axis=0, keepdims=True)
    e_in = jnp.zeros((8, rows), F32)
    for g in range(N_GROUPS):
        e_in = e_in + jnp.where(g_idx == g, lt[8 + 8 * g:16 + 8 * g], 0.0)
    m1 = jnp.max(e_in, axis=0, keepdims=True)
    i1 = jnp.min(jnp.where(e_in == m1, row8, 8), axis=0, keepdims=True)
    e_rest = jnp.where(row8 == i1, NEG, e_in)
    m2 = jnp.max(e_rest, axis=0, keepdims=True)
    i2 = jnp.min(jnp.where(e_rest == m2, row8, 8), axis=0, keepdims=True)
    r = jnp.exp(m2 - m1)
    w1 = g_w / (1.0 + r)
    w2 = g_w * r / (1.0 + r)
    e1 = g_idx * EXPERTS_PER_GROUP + i1
    e2 = g_idx * EXPERTS_PER_GROUP + i2
    rowe = lax.broadcasted_iota(I32, (N_EXPERTS, rows), 0)
    oh1 = (rowe == e1).astype(F32)
    oh2 = (rowe == e2).astype(F32)
    both_oh = oh1 + oh2
    before = _dot(both_oh.astype(BF16), tri_ref[...]) + cnt_ref[:, 0:1]
    rank1 = jnp.sum(oh1 * before, axis=0, keepdims=True).astype(I32)
    rank2 = jnp.sum(oh2 * before, axis=0, keepdims=True).astype(I32)
    cnt_ref[...] = cnt_ref[...] + jnp.sum(both_oh, axis=-1, keepdims=True)
    cntout_ref[...] = cnt_ref[...]
    meta_ref[...] = jnp.where(row8 == 0, e1, jnp.where(row8 == 1, e2, jnp.where(row8 == 2, rank1,
                                                                                 jnp.where(row8 == 3, rank2, 0))))
    ew_ref[...] = jnp.where(row8 == 0, w1, jnp.where(row8 == 1, w2, 0.0))


def _mod_spec(tile, per_batch_rows):
    if per_batch_rows is None:
        return pl.BlockSpec((tile, D_MODEL), lambda i: (i, 0))
    per = per_batch_rows // tile
    return pl.BlockSpec((1, 1, D_MODEL), lambda i: (i // per, 0, 0))


def _outproj(yp, ya, x, gt1, sh2, sc2, g2, wout, wr, wrhi, br, cnt_in, per_batch_rows):
    n = x.shape[0]
    tile = min(ROW_TILE, n)
    tok = lambda i: (i, 0)
    const2 = lambda i: (0, 0)
    mod_spec = _mod_spec(tile, per_batch_rows)
    tri = (jnp.arange(tile)[:, None] < jnp.arange(tile)[None, :]).astype(BF16)
    return pl.pallas_call(
        _outproj_kernel,
        grid=(n // tile,),
        in_specs=[pl.BlockSpec((tile, POOL_WIDTH), tok),
                  pl.BlockSpec((tile, ATTN_WIDTH), tok),
                  pl.BlockSpec((tile, D_MODEL), tok),
                  mod_spec, mod_spec, mod_spec,
                  pl.BlockSpec((1, D_MODEL), const2),
                  pl.BlockSpec((2 * POOL_WIDTH, D_MODEL), const2),
                  pl.BlockSpec((D_MODEL, 2 * LANES), const2),
                  pl.BlockSpec((D_MODEL, LANES), const2),
                  pl.BlockSpec((LANES, 1), const2),
                  pl.BlockSpec((tile, tile), const2),
                  pl.BlockSpec((N_EXPERTS, LANES), const2)],
        out_specs=[pl.BlockSpec((tile, D_MODEL), tok),
                   pl.BlockSpec((tile, D_MODEL), tok),
                   pl.BlockSpec((8, tile), lambda i: (0, i)),
                   pl.BlockSpec((8, tile), lambda i: (0, i)),
                   pl.BlockSpec((N_EXPERTS, LANES), const2)],
        out_shape=[jax.ShapeDtypeStruct((n, D_MODEL), F32),
                   jax.ShapeDtypeStruct((n, D_MODEL), F32),
                   jax.ShapeDtypeStruct((8, n), I32),
                   jax.ShapeDtypeStruct((8, n), F32),
                   jax.ShapeDtypeStruct((N_EXPERTS, LANES), F32)],
        scratch_shapes=[pltpu.VMEM((N_EXPERTS, LANES), F32)],
        compiler_params=pltpu.CompilerParams(dimension_semantics=("arbitrary",), vmem_limit_bytes=VMEM_LIMIT),
        name="outproj",
    )(yp, ya, x, gt1, sh2, sc2, g2, wout, wr, wrhi, br, tri, cnt_in)


def _row_dest(pstart_ref, meta_ref, slot, t):
    return pstart_ref[meta_ref[slot, t]] + meta_ref[EXPERT_TOPK + slot, t]


def _dispatch_kernel(pstart_ref, meta_ref, h_ref, xr_in_ref, xr_ref, sem):
    rows = h_ref.shape[0]

    def row_copy(slot, t):
        d = _row_dest(pstart_ref, meta_ref, slot, t)
        return pltpu.make_async_copy(h_ref.at[pl.ds(t, 1), :], xr_ref.at[pl.ds(d, 1), :], sem)

    def issue(t, carry):
        for slot in range(EXPERT_TOPK):
            row_copy(slot, t).start()
        return carry

    lax.fori_loop(0, rows, issue, 0, unroll=4)
    for slot in range(EXPERT_TOPK):
        pltpu.make_async_copy(h_ref, xr_ref.at[pl.ds(0, rows), :], sem).wait()


def _dispatch_rows(pstart, meta, h2, xr):
    n = h2.shape[0]
    tile = min(ROW_TILE, n)
    return pl.pallas_call(
        _dispatch_kernel,
        grid=(n // tile,),
        in_specs=[pl.BlockSpec(memory_space=pltpu.SMEM),
                  pl.BlockSpec((8, tile), lambda i: (0, i), memory_space=pltpu.SMEM),
                  pl.BlockSpec((tile, D_MODEL), lambda i: (i, 0)),
                  pl.BlockSpec(memory_space=pl.ANY)],
        out_specs=pl.BlockSpec(memory_space=pl.ANY),
        out_shape=jax.ShapeDtypeStruct(xr.shape, xr.dtype),
        scratch_shapes=[pltpu.SemaphoreType.DMA(())],
        input_output_aliases={3: 0},
        compiler_params=pltpu.CompilerParams(dimension_semantics=("arbitrary",), vmem_limit_bytes=VMEM_LIMIT),
        name="dispatch",
    )(pstart, meta, h2, xr)


def _moe_kernel(blk_e_ref, nact_ref, x_ref, wg_ref, wu_ref, wd_ref, y_ref, wgb_ref, wub_ref, wdb_ref):
    i = pl.program_id(0)
    prev = blk_e_ref[jnp.maximum(i - 1, 0)]
    active = i < nact_ref[0]

    @pl.when(active & ((i == 0) | (blk_e_ref[i] != prev)))
    def _():
        wgb_ref[...] = wg_ref[0].astype(BF16)
        wub_ref[...] = wu_ref[0].astype(BF16)
        wdb_ref[...] = wd_ref[0].astype(BF16)

    @pl.when(active)
    def _():
        x = x_ref[...].astype(BF16)
        g = _dot(x, wgb_ref[...])
        u = _dot(x, wub_ref[...])
        a = g / (1.0 + jnp.exp(-g)) * u
        y_ref[...] = _dot(a.astype(BF16), wdb_ref[...])

    @pl.when(jnp.logical_not(active))
    def _():
        y_ref[...] = jnp.zeros_like(y_ref)


def _moe(blk_e, n_active, xr, w_gate, w_up, w_down):
    n_rows = xr.shape[0]
    n_blocks = n_rows // MOE_ROWS
    grid_spec = pltpu.PrefetchScalarGridSpec(
        num_scalar_prefetch=2,
        grid=(n_blocks,),
        in_specs=[pl.BlockSpec((MOE_ROWS, D_MODEL), lambda i, be, na: (i, 0)),
                  pl.BlockSpec((1, D_MODEL, D_FF), lambda i, be, na: (be[i], 0, 0)),
                  pl.BlockSpec((1, D_MODEL, D_FF), lambda i, be, na: (be[i], 0, 0)),
                  pl.BlockSpec((1, D_FF, D_MODEL), lambda i, be, na: (be[i], 0, 0))],
        out_specs=pl.BlockSpec((MOE_ROWS, D_MODEL), lambda i, be, na: (i, 0)),
        scratch_shapes=[pltpu.VMEM((D_MODEL, D_FF), BF16),
                        pltpu.VMEM((D_MODEL, D_FF), BF16),
                        pltpu.VMEM((D_FF, D_MODEL), BF16)])
    return pl.pallas_call(
        _moe_kernel,
        grid_spec=grid_spec,
        out_shape=jax.ShapeDtypeStruct((n_rows, D_MODEL), F32),
        compiler_params=pltpu.CompilerParams(
            dimension_semantics=("arbitrary",), vmem_limit_bytes=VMEM_LIMIT),
        name="moe",
    )(blk_e, n_active, xr, w_gate, w_up, w_down)


def _final_kernel(pstart_ref, meta_ref, x1_ref, w_ref, gt2_ref, yr_ref, y_ref, o_ref, sem):
    rows = x1_ref.shape[0]

    def row_copy(slot, t):
        d = _row_dest(pstart_ref, meta_ref, slot, t)
        return pltpu.make_async_copy(yr_ref.at[pl.ds(d, 1), :], o_ref.at[slot, pl.ds(t, 1), :], sem)

    def issue(t, carry):
        for slot in range(EXPERT_TOPK):
            row_copy(slot, t).start()
        return carry

    lax.fori_loop(0, rows, issue, 0, unroll=4)
    for slot in range(EXPERT_TOPK):
        pltpu.make_async_copy(yr_ref.at[pl.ds(0, rows), :], o_ref.at[slot], sem).wait()
    w = w_ref[...]
    moe = o_ref[0] * w[:, 0:1] + o_ref[1] * w[:, 1:2]
    y_ref[...] = x1_ref[...] + _mod_rows(gt2_ref) * moe


def _final(pstart, meta, x1, w2, gt2, yr, per_batch_rows):
    n = x1.shape[0]
    tile = min(ROW_TILE, n)
    tok = lambda i: (i, 0)
    return pl.pallas_call(
        _final_kernel,
        grid=(n // tile,),
        in_specs=[pl.BlockSpec(memory_space=pltpu.SMEM),
                  pl.BlockSpec((8, tile), lambda i: (0, i), memory_space=pltpu.SMEM),
                  pl.BlockSpec((tile, D_MODEL), tok),
                  pl.BlockSpec((tile, EXPERT_TOPK), tok),
                  _mod_spec(tile, per_batch_rows),
                  pl.BlockSpec(memory_space=pl.ANY)],
        out_specs=pl.BlockSpec((tile, D_MODEL), tok),
        out_shape=jax.ShapeDtypeStruct((n, D_MODEL), F32),
        scratch_shapes=[pltpu.VMEM((EXPERT_TOPK, tile, D_MODEL), F32),
                        pltpu.SemaphoreType.DMA(())],
        compiler_params=pltpu.CompilerParams(dimension_semantics=("arbitrary",), vmem_limit_bytes=VMEM_LIMIT),
        name="final",
    )(pstart, meta, x1, w2, gt2, yr)


def _expert_layout(counts, n_blocks):
    padded = (counts + MOE_ROWS - 1) // MOE_ROWS * MOE_ROWS
    pend = jnp.cumsum(padded)
    pstart = (pend - padded).astype(I32)
    blk_start = jnp.arange(n_blocks, dtype=I32) * MOE_ROWS
    blk_e = jnp.minimum(jnp.sum((pend[None, :] <= blk_start[:, None]).astype(I32), axis=1), N_EXPERTS - 1)
    n_active = (pend[-1] // MOE_ROWS).astype(I32).reshape(1)
    return pstart, blk_e.astype(I32), n_active


def kernel(x_prompt, x_sample, cache_k, cache_v, state_pool, page_table, c_prompt, c_sample, w_ada, b_ada,
           g_attn_norm, w_in, g_q, g_k, w_pool, pool_scale, w_out, g_ffn_norm, w_group, b_group, w_expert,
           b_expert, w_gate, w_up, w_down):
    D = D_MODEL
    B, T, _ = x_prompt.shape
    n_s = DEC_BATCH * DEC_SEQ
    n_p = B * T
    layer = 0

    win = w_in[layer].astype(BF16)
    wout = w_out[layer].astype(BF16)
    wpool = w_pool[layer].astype(BF16)
    g1 = g_attn_norm[layer].reshape(1, D)
    g2 = g_ffn_norm[layer].reshape(1, D)
    gq = jnp.tile(g_q[layer], N_HEADS).reshape(1, ATTN_WIDTH)
    gk = jnp.tile(g_k[layer], N_HEADS).reshape(1, ATTN_WIDTH)
    ps = pool_scale[layer].reshape(1, POOL_WIDTH)
    hd = jnp.arange(ATTN_WIDTH) // HEAD_DIM
    bd = (hd[:, None] == hd[None, :]).astype(BF16)
    slopes = jnp.exp2(-8.0 * (jnp.arange(N_HEADS, dtype=F32) + 1.0) / N_HEADS)
    qlane = jnp.arange(LANES)
    slope_col = slopes[jnp.minimum(qlane // DEC_SEQ, N_HEADS - 1)].reshape(LANES, 1)
    lq_col = (qlane % DEC_SEQ).astype(F32).reshape(LANES, 1)
    wr = jnp.zeros((D, LANES), F32).at[:, 0:N_GROUPS].set(w_group[layer]).at[:, 8:8 + N_EXPERTS].set(w_expert[layer])
    wr_hi = wr.astype(BF16)
    wr_lo = (wr - wr_hi.astype(F32)).astype(BF16)
    wr_both = jnp.concatenate([wr_hi, wr_lo], axis=1)
    br = jnp.zeros((LANES,), F32).at[0:N_GROUPS].set(b_group[layer]).at[N_GROUPS:8].set(NEG)
    br = br.at[8:8 + N_EXPERTS].set(b_expert[layer]).reshape(LANES, 1)

    mod = _ada(jnp.concatenate([c_prompt, c_sample], axis=0), w_ada[layer], b_ada[layer])
    mod_p = mod[:B].reshape(B, 1, 6 * D)
    mod_s = jnp.repeat(mod[B:], DEC_SEQ, axis=0)

    ypool_p, q_p, k_p, v_p, tail_p = _inproj_prompt(x_prompt, mod_p, g1, win, bd, gq, gk, wpool, ps)
    yattn_p = _moba_prompt(slopes, q_p, k_p, v_p)
    mp = lambda j: mod_p[:, :, j * D:(j + 1) * D]
    cnt0 = jnp.zeros((N_EXPERTS, LANES), F32)
    x1_p, h2_p, meta_p, ew_p, cnt_p = _outproj(
        ypool_p.reshape(n_p, POOL_WIDTH), yattn_p.reshape(n_p, ATTN_WIDTH), x_prompt.reshape(n_p, D),
        mp(2), mp(3), mp(4), g2, wout, wr_both, wr_hi, br, cnt0, T)

    ms = lambda j: mod_s[:, j * D:(j + 1) * D]
    hist = jnp.concatenate([jnp.zeros((DEC_BATCH, HALO - POOL_HIST, POOL_WIDTH), F32), state_pool[layer]], axis=1)
    xs = x_sample.reshape(n_s, D)
    ypool_s, q_s, k_s, v_s, tail_s = _inproj_sample(xs, ms(0), ms(1), g1, win, bd, gq, gk, hist, wpool, ps)
    r3 = lambda a: a.reshape(DEC_BATCH, DEC_SEQ, ATTN_WIDTH)
    n_phys = cache_k.shape[1]
    ck = jnp.transpose(cache_k[layer], (0, 2, 3, 1)).reshape(n_phys, ATTN_WIDTH, PAGE_SIZE)
    cv = jnp.transpose(cache_v[layer], (0, 2, 3, 1)).reshape(n_phys, ATTN_WIDTH, PAGE_SIZE)
    yattn_s = _moba_sample(page_table, r3(q_s), r3(k_s), r3(v_s), slope_col, lq_col, ck, cv)
    x1_s, h2_s, meta_s, ew_s, cnt_s = _outproj(ypool_s, yattn_s.reshape(n_s, ATTN_WIDTH), xs, ms(2), ms(3), ms(4),
                                               g2, wout, wr_both, wr_hi, br, cnt_p, None)

    n_blocks = (n_p + n_s) * EXPERT_TOPK // MOE_ROWS + N_EXPERTS
    pstart, blk_e, n_active = _expert_layout(cnt_s[:, 0].astype(I32), n_blocks)
    xr = jnp.zeros((n_blocks * MOE_ROWS, D), F32)
    xr = _dispatch_rows(pstart, meta_p, h2_p, xr)
    xr = _dispatch_rows(pstart, meta_s, h2_s, xr)
    yr = _moe(blk_e, n_active, xr, w_gate[layer], w_up[layer], w_down[layer])
    y_p = _final(pstart, meta_p, x1_p, ew_p[0:EXPERT_TOPK].T, mp(5), yr, T)
    y_s = _final(pstart, meta_s, x1_s, ew_s[0:EXPERT_TOPK].T, ms(5), yr, None)

    k4 = lambda a, b, l: a.reshape(1, b, l, N_HEADS, HEAD_DIM)
    return (y_p.reshape(B, T, D), y_s.reshape(DEC_BATCH, DEC_SEQ, D),
            k4(k_p, B, T), k4(v_p, B, T), tail_p[None, :, HALO - POOL_HIST:, :],
            k4(k_s, DEC_BATCH, DEC_SEQ), k4(v_s, DEC_BATCH, DEC_SEQ), tail_s[None, :, HALO - POOL_HIST:, :])
```

```python
import functools

import jax
import jax.numpy as jnp
from jax import lax
from jax.experimental import pallas as pl
from jax.experimental.pallas import tpu as pltpu

F32 = jnp.float32
BF16 = jnp.bfloat16
I32 = jnp.int32

D_MODEL = 1024
BATCH = 8
SEQ = 2048
DEC_BATCH = 32
DEC_SEQ = 8
PAST_LEN = 16384
PAGE_SIZE = 128
POOL_WIDTH = 512
POOL_WINDOWS = (2, 4, 8, 16)
POOL_GROUP = 128
POOL_HIST = 15
HALO = 16
N_HEADS = 8
HEAD_DIM = 64
ATTN_WIDTH = 512
MOBA_BLOCK = 256
MOBA_TOPK = 3
ATTN_SCALE = HEAD_DIM ** -0.5
MIX_IN = POOL_WIDTH + 3 * ATTN_WIDTH
N_GROUPS = 4
EXPERTS_PER_GROUP = 8
N_EXPERTS = 32
EXPERT_TOPK = 2
D_FF = 512
EPS = 1e-6
NEG = -1e30

LANES = 128
ROW_TILE = 512
MOE_ROWS = 256
PAGES_PER_STEP = 16
VMEM_LIMIT = 56 * 1024 * 1024

_NT = (((1,), (1,)), ((), ()))


def _dot(a, b):
    return jnp.dot(a, b, preferred_element_type=F32)


def _dot_nt(a, b):
    return lax.dot_general(a, b, _NT, preferred_element_type=F32)


def _split_dot(a, b01):
    hi = a.astype(BF16)
    lo = (a - hi.astype(F32)).astype(BF16)
    return _dot(hi, b01) + _dot(lo, b01)


def _rms_mod(x, g, sc, sh):
    ms = jnp.mean(x * x, axis=-1, keepdims=True)
    return x * lax.rsqrt(ms + EPS) * g * (1.0 + sc) + sh


def _ada_kernel(c_ref, w_ref, b_ref, o_ref):
    c = c_ref[...]
    a = c / (1.0 + jnp.exp(-c))
    o_ref[...] = _dot(a.astype(BF16), w_ref[...].astype(BF16)) + b_ref[...]


def _ada(c_all, w_ada, b_ada):
    n = c_all.shape[0]
    tn = 1536
    return pl.pallas_call(
        _ada_kernel,
        grid=(6 * D_MODEL // tn,),
        in_specs=[pl.BlockSpec((n, D_MODEL), lambda j: (0, 0)),
                  pl.BlockSpec((D_MODEL, tn), lambda j: (0, j)),
                  pl.BlockSpec((1, tn), lambda j: (0, j))],
        out_specs=pl.BlockSpec((n, tn), lambda j: (0, j)),
        out_shape=jax.ShapeDtypeStruct((n, 6 * D_MODEL), F32),
        compiler_params=pltpu.CompilerParams(vmem_limit_bytes=VMEM_LIMIT),
        name="ada",
    )(c_all, w_ada, b_ada.reshape(1, -1))


def _inproj_core(x, sh1, sc1, g, win_ref, bd_ref, gq_ref, gk_ref):
    h = _rms_mod(x, g, sc1, sh1)
    z = _dot(h.astype(BF16), win_ref[...])
    u = z[:, 0:POOL_WIDTH]
    q = z[:, POOL_WIDTH:POOL_WIDTH + ATTN_WIDTH]
    k = z[:, POOL_WIDTH + ATTN_WIDTH:POOL_WIDTH + 2 * ATTN_WIDTH]
    v = z[:, POOL_WIDTH + 2 * ATTN_WIDTH:]
    bd = bd_ref[...]
    q = q * lax.rsqrt(_split_dot(q * q, bd) * (1.0 / HEAD_DIM) + EPS) * gq_ref[...]
    k = k * lax.rsqrt(_split_dot(k * k, bd) * (1.0 / HEAD_DIM) + EPS) * gk_ref[...]
    return u, q, k, v


def _window_sum(e, w):
    s = e
    sh = 1
    while sh < w:
        s = s + pltpu.roll(s, sh, axis=0)
        sh *= 2
    return s


def _inproj_prompt_kernel(x_ref, mod_ref, g_ref, win_ref, bd_ref, gq_ref, gk_ref, wpool_ref, ps_ref,
                          ypool_ref, q_ref, k_ref, v_ref, tail_ref, ext_ref):
    t = pl.program_id(1)
    nt = pl.num_programs(1)
    x = x_ref[0]
    sh1 = mod_ref[0, :, 0:D_MODEL]
    sc1 = mod_ref[0, :, D_MODEL:2 * D_MODEL]
    u, q, k, v = _inproj_core(x, sh1, sc1, g_ref[...], win_ref, bd_ref, gq_ref, gk_ref)
    q_ref[0] = (q * ATTN_SCALE).astype(BF16)
    k_ref[0] = k
    v_ref[0] = v

    @pl.when(t == 0)
    def _():
        ext_ref[0:HALO, :] = jnp.zeros((HALO, POOL_WIDTH), F32)

    ext_ref[HALO:, :] = u
    pos = t * ROW_TILE + lax.broadcasted_iota(I32, (ROW_TILE, 1), 0)
    for gi, w in enumerate(POOL_WINDOWS):
        cols = slice(gi * POOL_GROUP, (gi + 1) * POOL_GROUP)
        win = _window_sum(ext_ref[:, cols], w)[HALO:]
        inv_cnt = 1.0 / jnp.minimum(pos + 1, w).astype(F32)
        pooled = win * inv_cnt - u[:, cols]
        y = _dot(pooled.astype(BF16), wpool_ref[gi]) * ps_ref[:, cols]
        ypool_ref[0, :, cols] = y.astype(BF16)
    last = u[ROW_TILE - HALO:, :]
    ext_ref[0:HALO, :] = last

    @pl.when(t == nt - 1)
    def _():
        tail_ref[0] = last


def _inproj_prompt(x, mod_p, g1, win, bd, gq, gk, wpool, ps):
    B, T, D = x.shape
    nt = T // ROW_TILE
    const2 = lambda b, t: (0, 0)
    tok = lambda b, t: (b, t, 0)
    return pl.pallas_call(
        _inproj_prompt_kernel,
        grid=(B, nt),
        in_specs=[pl.BlockSpec((1, ROW_TILE, D), tok),
                  pl.BlockSpec((1, 1, 6 * D), lambda b, t: (b, 0, 0)),
                  pl.BlockSpec((1, D), const2),
                  pl.BlockSpec((D, MIX_IN), const2),
                  pl.BlockSpec((ATTN_WIDTH, ATTN_WIDTH), const2),
                  pl.BlockSpec((1, ATTN_WIDTH), const2),
                  pl.BlockSpec((1, ATTN_WIDTH), const2),
                  pl.BlockSpec((4, POOL_GROUP, POOL_GROUP), lambda b, t: (0, 0, 0)),
                  pl.BlockSpec((1, POOL_WIDTH), const2)],
        out_specs=[pl.BlockSpec((1, ROW_TILE, POOL_WIDTH), tok),
                   pl.BlockSpec((1, ROW_TILE, ATTN_WIDTH), tok),
                   pl.BlockSpec((1, ROW_TILE, ATTN_WIDTH), tok),
                   pl.BlockSpec((1, ROW_TILE, ATTN_WIDTH), tok),
                   pl.BlockSpec((1, HALO, POOL_WIDTH), lambda b, t: (b, 0, 0))],
        out_shape=[jax.ShapeDtypeStruct((B, T, POOL_WIDTH), BF16),
                   jax.ShapeDtypeStruct((B, T, ATTN_WIDTH), BF16),
                   jax.ShapeDtypeStruct((B, T, ATTN_WIDTH), F32),
                   jax.ShapeDtypeStruct((B, T, ATTN_WIDTH), F32),
                   jax.ShapeDtypeStruct((B, HALO, POOL_WIDTH), F32)],
        scratch_shapes=[pltpu.VMEM((HALO + ROW_TILE, POOL_WIDTH), F32)],
        compiler_params=pltpu.CompilerParams(
            dimension_semantics=("arbitrary", "arbitrary"), vmem_limit_bytes=VMEM_LIMIT),
        name="inproj_prompt",
    )(x, mod_p, g1, win, bd, gq, gk, wpool, ps)


def _inproj_sample_kernel(x_ref, sh_ref, sc_ref, g_ref, win_ref, bd_ref, gq_ref, gk_ref, hist_ref, wpool_ref,
                          ps_ref, ypool_ref, q_ref, k_ref, v_ref, tail_ref, ext_ref):
    n = DEC_BATCH * DEC_SEQ
    ext_rows = HALO + DEC_SEQ
    u, q, k, v = _inproj_core(x_ref[...], sh_ref[...], sc_ref[...], g_ref[...], win_ref, bd_ref, gq_ref, gk_ref)
    q_ref[...] = q * ATTN_SCALE
    k_ref[...] = k
    v_ref[...] = v
    ext_ref[:, 0:HALO, :] = hist_ref[...]
    ext_ref[:, HALO:, :] = u.reshape(DEC_BATCH, DEC_SEQ, POOL_WIDTH)
    tail_ref[...] = ext_ref[:, ext_rows - HALO:, :]
    pos = PAST_LEN + lax.broadcasted_iota(I32, (DEC_BATCH, DEC_SEQ, 1), 1).reshape(n, 1)
    for gi, w in enumerate(POOL_WINDOWS):
        cols = slice(gi * POOL_GROUP, (gi + 1) * POOL_GROUP)
        e = ext_ref[:, :, cols].reshape(DEC_BATCH * ext_rows, POOL_GROUP)
        win = _window_sum(e, w).reshape(DEC_BATCH, ext_rows, POOL_GROUP)[:, HALO:, :].reshape(n, POOL_GROUP)
        inv_cnt = 1.0 / jnp.minimum(pos + 1, w).astype(F32)
        pooled = win * inv_cnt - u[:, cols]
        y = _dot(pooled.astype(BF16), wpool_ref[gi]) * ps_ref[:, cols]
        ypool_ref[:, cols] = y.astype(BF16)


def _inproj_sample(x, sh1, sc1, g1, win, bd, gq, gk, hist, wpool, ps):
    n = x.shape[0]
    return pl.pallas_call(
        _inproj_sample_kernel,
        out_shape=[jax.ShapeDtypeStruct((n, POOL_WIDTH), BF16),
                   jax.ShapeDtypeStruct((n, ATTN_WIDTH), F32),
                   jax.ShapeDtypeStruct((n, ATTN_WIDTH), F32),
                   jax.ShapeDtypeStruct((n, ATTN_WIDTH), F32),
                   jax.ShapeDtypeStruct((DEC_BATCH, HALO, POOL_WIDTH), F32)],
        scratch_shapes=[pltpu.VMEM((DEC_BATCH, HALO + DEC_SEQ, POOL_WIDTH), F32)],
        compiler_params=pltpu.CompilerParams(vmem_limit_bytes=VMEM_LIMIT),
        name="inproj_sample",
    )(x, sh1, sc1, g1, win, bd, gq, gk, hist, wpool, ps)


N_BIAS_LANES = 3


def _moba_prompt_kernel(slopes_ref, q_ref, k_ref, v_ref, o_ref, ka_ref, va_ref, qa_ref, kmf_ref):
    S = MOBA_BLOCK
    T = k_ref.shape[1]
    nb = T // S
    hp = pl.program_id(1)
    lane = lax.broadcasted_iota(I32, (1, LANES), 1)
    real = (lane < HEAD_DIM, lane >= HEAD_DIM)
    extra = (lane - HEAD_DIM, lane)

    kf = k_ref[0]
    vf = v_ref[0]
    kmf_ref[...] = jnp.zeros((LANES, LANES), F32)
    for j in range(nb):
        mean = jnp.sum(kf[j * S:(j + 1) * S], axis=0, keepdims=True) * (1.0 / S)
        kmf_ref[j:j + 1, :] = jnp.where(real[0], mean, 0.0)
        kmf_ref[nb + j:nb + j + 1, :] = jnp.where(real[0], 0.0, mean)
    key_i = lax.broadcasted_iota(I32, (T, 1), 0)
    key_blk = key_i // S
    for h in range(2):
        b = slopes_ref[2 * hp + h] * key_i.astype(F32)
        p0 = b.astype(BF16).astype(F32)
        p1 = (b - p0).astype(BF16).astype(F32)
        p2 = b - p0 - p1
        e = extra[h]
        onehot = jnp.where((e >= N_BIAS_LANES) & (e - N_BIAS_LANES == key_blk), 1.0, 0.0)
        feat = jnp.where(e == 0, p0, jnp.where(e == 1, p1, jnp.where(e == 2, p2, onehot)))
        ka_ref[h] = jnp.where(real[h], kf, feat).astype(BF16)
        va_ref[h] = jnp.where(real[h], vf, 1.0).astype(BF16)

    q2 = q_ref[0]

    gt = _dot_nt(kmf_ref[...].astype(BF16), q2)[0:2 * nb]
    row = lax.broadcasted_iota(I32, (2 * nb, T), 0)
    blk = row % nb
    cbq = lax.broadcasted_iota(I32, (2 * nb, T), 1) // S
    cnt = jnp.zeros((2 * nb, T), I32)
    for m in range(nb):
        gm = jnp.where(row < nb, gt[m:m + 1, :], gt[nb + m:nb + m + 1, :])
        beats = (gm > gt) | ((gm == gt) & (m < blk))
        cnt = cnt + jnp.where(beats & (m < cbq), 1, 0)
    keep = (((cnt < MOBA_TOPK) & (blk < cbq)) | (blk == cbq)).astype(F32)
    keepq = jnp.concatenate([keep, jnp.zeros((LANES - 2 * nb, T), F32)], axis=0).T
    maskv = jnp.where(keepq > 0.5, 0.0, NEG)
    mask_lanes = (pltpu.roll(maskv, HEAD_DIM + N_BIAS_LANES, axis=1),
                  pltpu.roll(maskv, (N_BIAS_LANES - nb) % LANES, axis=1))
    qf = q2.astype(F32)
    for h in range(2):
        e = extra[h]
        feat = jnp.where(e < N_BIAS_LANES, 1.0, jnp.where(e < N_BIAS_LANES + nb, mask_lanes[h], 0.0))
        qa_ref[h] = jnp.where(real[h], qf, feat).astype(BF16)

    causal = lax.broadcasted_iota(I32, (S, S), 1) <= lax.broadcasted_iota(I32, (S, S), 0)
    for cb in range(nb):
        rows = slice(cb * S, (cb + 1) * S)
        n = (cb + 1) * S
        outs = []
        for h in range(2):
            s = _dot_nt(qa_ref[h, rows, :], ka_ref[h, 0:n, :])
            s_own = jnp.where(causal, s[:, cb * S:n], NEG)
            s = s_own if cb == 0 else jnp.concatenate([s[:, 0:cb * S], s_own], axis=1)
            p = jnp.exp(s - jnp.max(s, axis=-1, keepdims=True))
            outs.append(_dot(p.astype(BF16), va_ref[h, 0:n, :]))
        a0, a1 = outs
        o0 = a0 * (1.0 / a0[:, HEAD_DIM:HEAD_DIM + 1])
        o1 = a1 * (1.0 / a1[:, 0:1])
        o_ref[0, rows, :] = jnp.where(real[0], o0, o1).astype(BF16)


def _moba_prompt(slopes, q, k, v):
    B, T, _ = q.shape
    seq = lambda b, hp: (b, 0, hp)
    return pl.pallas_call(
        _moba_prompt_kernel,
        grid=(B, N_HEADS // 2),
        in_specs=[pl.BlockSpec(memory_space=pltpu.SMEM),
                  pl.BlockSpec((1, T, LANES), seq),
                  pl.BlockSpec((1, T, LANES), seq),
                  pl.BlockSpec((1, T, LANES), seq)],
        out_specs=pl.BlockSpec((1, T, LANES), seq),
        out_shape=jax.ShapeDtypeStruct((B, T, ATTN_WIDTH), BF16),
        scratch_shapes=[pltpu.VMEM((2, T, LANES), BF16),
                        pltpu.VMEM((2, T, LANES), BF16),
                        pltpu.VMEM((2, T, LANES), BF16),
                        pltpu.VMEM((LANES, LANES), F32)],
        compiler_params=pltpu.CompilerParams(
            dimension_semantics=("arbitrary", "arbitrary"), vmem_limit_bytes=VMEM_LIMIT),
        name="moba_prompt",
    )(slopes, q, k, v)


CHUNK_SLOTS = 4


def _moba_sample_kernel(pt_ref, q_ref, kn_ref, vn_ref, slope_ref, lq_ref, ck_ref, cv_ref, o_ref,
                        buf_ref, sem, s_ref, p_ref, kc_ref, acc_ref):
    P = PAGES_PER_STEP
    S = MOBA_BLOCK
    n_blocks = PAST_LEN // S
    chunk = P * PAGE_SIZE
    n_chunks = PAST_LEN // chunk
    bpc = chunk // S
    n_loads = 2 * n_chunks
    ahead = CHUNK_SLOTS - 1
    assert n_loads % CHUNK_SLOTS == 0
    nq = N_HEADS * DEC_SEQ
    b = pl.program_id(0)
    slope = slope_ref[...]
    lq = lq_ref[...]
    lane = lax.broadcasted_iota(I32, (1, LANES), 1)

    def start_load(bb, i):
        src = ck_ref if i < n_chunks else cv_ref
        slot = i % CHUNK_SLOTS
        for r in range(P):
            page = pt_ref[bb, (i % n_chunks) * P + r]
            pltpu.make_async_copy(src.at[page], buf_ref.at[slot, r], sem.at[slot]).start()

    def wait_load(i):
        slot = i % CHUNK_SLOTS
        pltpu.make_async_copy(ck_ref.at[pl.ds(0, P)], buf_ref.at[slot], sem.at[slot]).wait()

    def load_chunk(i):
        nxt = i + ahead
        if nxt < n_loads:
            start_load(b, nxt)
        else:
            @pl.when(b + 1 < pl.num_programs(0))
            def _():
                start_load(b + 1, nxt - n_loads)
        wait_load(i)
        slot = i % CHUNK_SLOTS
        for r in range(P):
            kc_ref[:, r * PAGE_SIZE:(r + 1) * PAGE_SIZE] = buf_ref[slot, r].astype(BF16)
        return kc_ref[...]

    @pl.when(b == 0)
    def _():
        for i in range(ahead):
            start_load(b, i)

    q8 = q_ref[0]
    head = lax.broadcasted_iota(I32, (DEC_SEQ, ATTN_WIDTH), 1) // HEAD_DIM
    qp = jnp.concatenate([jnp.where(head == h, q8, 0.0) for h in range(N_HEADS)], axis=0).astype(BF16)

    gate = jnp.zeros((nq, LANES), F32)
    for c in range(n_chunks):
        sc = _dot(qp, load_chunk(c))
        for r2 in range(bpc):
            gs = jnp.sum(sc[:, r2 * S:(r2 + 1) * S], axis=-1, keepdims=True) * (1.0 / S)
            gate = jnp.where(lane == c * bpc + r2, gs, gate)
        keypos = c * chunk + lax.broadcasted_iota(I32, (1, chunk), 1)
        dist = (keypos - PAST_LEN).astype(F32) - lq
        s_ref[c] = sc + slope * dist

    cnt = jnp.zeros((nq, LANES), I32)
    for m in range(n_blocks):
        gm = gate[:, m:m + 1]
        beats = (gm > gate) | ((gm == gate) & (m < lane))
        cnt = cnt + jnp.where(beats, 1, 0)
    selq = ((cnt < min(MOBA_TOPK, n_blocks)) & (lane < n_blocks)).astype(F32)
    keeps = [selq[:, n:n + 1] > 0.5 for n in range(n_blocks)]

    kn = jnp.concatenate([kn_ref[0], jnp.zeros((LANES - DEC_SEQ, ATTN_WIDTH), F32)], axis=0)
    lane_f = lane.astype(F32)
    sn = _dot_nt(qp, kn.astype(BF16)) + slope * (lane_f - lq)
    sn = jnp.where(lane_f <= lq, sn, NEG)

    mxv = sn
    for n in range(n_blocks):
        c, r2 = divmod(n, bpc)
        blk = jnp.where(keeps[n], s_ref[c, :, r2 * S:(r2 + 1) * S], NEG)
        for i in range(S // LANES):
            mxv = jnp.maximum(mxv, blk[:, i * LANES:(i + 1) * LANES])
    mx = jnp.max(mxv, axis=-1, keepdims=True)
    pn = jnp.exp(sn - mx)
    lsv = pn
    for n in range(n_blocks):
        c, r2 = divmod(n, bpc)
        p = jnp.exp(jnp.where(keeps[n], s_ref[c, :, r2 * S:(r2 + 1) * S] - mx, NEG))
        p_ref[c, :, r2 * S:(r2 + 1) * S] = p.astype(BF16)
        for i in range(S // LANES):
            lsv = lsv + p[:, i * LANES:(i + 1) * LANES]
    linv = 1.0 / jnp.sum(lsv, axis=-1, keepdims=True)
    vn = jnp.concatenate([vn_ref[0], jnp.zeros((LANES - DEC_SEQ, ATTN_WIDTH), F32)], axis=0)
    acc_ref[...] = _dot(pn.astype(BF16), vn.astype(BF16))

    for c in range(n_chunks):
        acc_ref[...] += _dot_nt(p_ref[c], load_chunk(n_chunks + c))

    o = acc_ref[...] * linv
    res = jnp.zeros((DEC_SEQ, ATTN_WIDTH), F32)
    for h in range(N_HEADS):
        res = res + jnp.where(head == h, o[h * DEC_SEQ:(h + 1) * DEC_SEQ, :], 0.0)
    o_ref[0] = res.astype(BF16)


def _moba_sample(page_table, q, kn, vn, slope_col, lq_col, cache_kt, cache_vt):
    P = PAGES_PER_STEP
    chunk = P * PAGE_SIZE
    n_chunks = PAST_LEN // chunk
    nq = N_HEADS * DEC_SEQ
    row3 = lambda b, pt: (b, 0, 0)
    const2 = lambda b, pt: (0, 0)
    grid_spec = pltpu.PrefetchScalarGridSpec(
        num_scalar_prefetch=1,
        grid=(DEC_BATCH,),
        in_specs=[pl.BlockSpec((1, DEC_SEQ, ATTN_WIDTH), row3),
                  pl.BlockSpec((1, DEC_SEQ, ATTN_WIDTH), row3),
                  pl.BlockSpec((1, DEC_SEQ, ATTN_WIDTH), row3),
                  pl.BlockSpec((nq, 1), const2),
                  pl.BlockSpec((nq, 1), const2),
                  pl.BlockSpec(memory_space=pl.ANY),
                  pl.BlockSpec(memory_space=pl.ANY)],
        out_specs=pl.BlockSpec((1, DEC_SEQ, ATTN_WIDTH), row3),
        scratch_shapes=[pltpu.VMEM((CHUNK_SLOTS, P, ATTN_WIDTH, PAGE_SIZE), F32),
                        pltpu.SemaphoreType.DMA((CHUNK_SLOTS,)),
                        pltpu.VMEM((n_chunks, nq, chunk), F32),
                        pltpu.VMEM((n_chunks, nq, chunk), BF16),
                        pltpu.VMEM((ATTN_WIDTH, chunk), BF16),
                        pltpu.VMEM((nq, ATTN_WIDTH), F32)])
    return pl.pallas_call(
        _moba_sample_kernel,
        grid_spec=grid_spec,
        out_shape=jax.ShapeDtypeStruct((DEC_BATCH, DEC_SEQ, ATTN_WIDTH), BF16),
        compiler_params=pltpu.CompilerParams(dimension_semantics=("arbitrary",), vmem_limit_bytes=VMEM_LIMIT),
        name="moba_sample",
    )(page_table, q, kn, vn, slope_col, lq_col, cache_kt, cache_vt)


def _mod_rows(ref):
    return ref[...].reshape(-1, ref.shape[-1])


def _outproj_kernel(yp_ref, ya_ref, x_ref, gt1_ref, sh2_ref, sc2_ref, g2_ref, wout_ref, wr_ref, wrhi_ref, br_ref,
                    tri_ref, cntin_ref, x1_ref, h2_ref, meta_ref, ew_ref, cntout_ref, cnt_ref):
    rows = x_ref.shape[0]

    @pl.when(pl.program_id(0) == 0)
    def _():
        cnt_ref[...] = cntin_ref[...]

    mix = _dot(yp_ref[...], wout_ref[0:POOL_WIDTH, :]) + _dot(ya_ref[...], wout_ref[POOL_WIDTH:, :])
    x1 = x_ref[...] + _mod_rows(gt1_ref) * mix
    x1_ref[...] = x1
    h2 = _rms_mod(x1, g2_ref[...], _mod_rows(sc2_ref), _mod_rows(sh2_ref))
    h2_ref[...] = h2
    hh = h2.astype(BF16)
    hl = (h2 - hh.astype(F32)).astype(BF16)
    both = _dot(hh, wr_ref[...])
    lt = (both[:, 0:LANES] + both[:, LANES:] + _dot(hl, wrhi_ref[...])).T + br_ref[...]
    row8 = lax.broadcasted_iota(I32, (8, rows), 0)
    g8 = lt[0:8]
    gmax = jnp.max(g8, axis=0, keepdims=True)
    gsum = jnp.sum(jnp.exp(g8 - gmax), axis=0, keepdims=True)
    g_w = 1.0 / gsum
    g_idx = jnp.min(jnp.where(g8 == gmax, row8, 8), axis=0, keepdims=True)
    e_in = jnp.zeros((8, rows), F32)
    for g in range(N_GROUPS):
        e_in = e_in + jnp.where(g_idx == g, lt[8 + 8 * g:16 + 8 * g], 0.0)
    m1 = jnp.max(e_in, axis=0, keepdims=True)
    i1 = jnp.min(jnp.where(e_in == m1, row8, 8), axis=0, keepdims=True)
    e_rest = jnp.where(row8 == i1, NEG, e_in)
    m2 = jnp.max(e_rest, axis=0, keepdims=True)
    i2 = jnp.min(jnp.where(e_rest == m2, row8, 8), axis=0, keepdims=True)
    r = jnp.exp(m2 - m1)
    w1 = g_w / (1.0 + r)
    w2 = g_w * r / (1.0 + r)
    e1 = g_idx * EXPERTS_PER_GROUP + i1
    e2 = g_idx * EXPERTS_PER_GROUP + i2
    rowe = lax.broadcasted_iota(I32, (N_EXPERTS, rows), 0)
    oh1 = (rowe == e1).astype(F32)
    oh2 = (rowe == e2).astype(F32)
    both_oh = oh1 + oh2
    before = _dot(both_oh.astype(BF16), tri_ref[...]) + cnt_ref[:, 0:1]
    rank1 = jnp.sum(oh1 * before, axis=0, keepdims=True).astype(I32)
    rank2 = jnp.sum(oh2 * before, axis=0, keepdims=True).astype(I32)
    cnt_ref[...] = cnt_ref[...] + jnp.sum(both_oh, axis=-1, keepdims=True)
    cntout_ref[...] = cnt_ref[...]
    meta_ref[...] = jnp.where(row8 == 0, e1, jnp.where(row8 == 1, e2, jnp.where(row8 == 2, rank1,
                                                                                 jnp.where(row8 == 3, rank2, 0))))
    ew_ref[...] = jnp.where(row8 == 0, w1, jnp.where(row8 == 1, w2, 0.0))


def _mod_spec(tile, per_batch_rows):
    if per_batch_rows is None:
        return pl.BlockSpec((tile, D_MODEL), lambda i: (i, 0))
    per = per_batch_rows // tile
    return pl.BlockSpec((1, 1, D_MODEL), lambda i: (i // per, 0, 0))


def _outproj(yp, ya, x, gt1, sh2, sc2, g2, wout, wr, wrhi, br, cnt_in, per_batch_rows):
    n = x.shape[0]
    tile = min(ROW_TILE, n)
    tok = lambda i: (i, 0)
    const2 = lambda i: (0, 0)
    mod_spec = _mod_spec(tile, per_batch_rows)
    tri = (jnp.arange(tile)[:, None] < jnp.arange(tile)[None, :]).astype(BF16)
    return pl.pallas_call(
        _outproj_kernel,
        grid=(n // tile,),
        in_specs=[pl.BlockSpec((tile, POOL_WIDTH), tok),
                  pl.BlockSpec((tile, ATTN_WIDTH), tok),
                  pl.BlockSpec((tile, D_MODEL), tok),
                  mod_spec, mod_spec, mod_spec,
                  pl.BlockSpec((1, D_MODEL), const2),
                  pl.BlockSpec((2 * POOL_WIDTH, D_MODEL), const2),
                  pl.BlockSpec((D_MODEL, 2 * LANES), const2),
                  pl.BlockSpec((D_MODEL, LANES), const2),
                  pl.BlockSpec((LANES, 1), const2),
                  pl.BlockSpec((tile, tile), const2),
                  pl.BlockSpec((N_EXPERTS, LANES), const2)],
        out_specs=[pl.BlockSpec((tile, D_MODEL), tok),
                   pl.BlockSpec((tile, D_MODEL), tok),
                   pl.BlockSpec((8, tile), lambda i: (0, i)),
                   pl.BlockSpec((8, tile), lambda i: (0, i)),
                   pl.BlockSpec((N_EXPERTS, LANES), const2)],
        out_shape=[jax.ShapeDtypeStruct((n, D_MODEL), F32),
                   jax.ShapeDtypeStruct((n, D_MODEL), F32),
                   jax.ShapeDtypeStruct((8, n), I32),
                   jax.ShapeDtypeStruct((8, n), F32),
                   jax.ShapeDtypeStruct((N_EXPERTS, LANES), F32)],
        scratch_shapes=[pltpu.VMEM((N_EXPERTS, LANES), F32)],
        compiler_params=pltpu.CompilerParams(dimension_semantics=("arbitrary",), vmem_limit_bytes=VMEM_LIMIT),
        name="outproj",
    )(yp, ya, x, gt1, sh2, sc2, g2, wout, wr, wrhi, br, tri, cnt_in)


def _row_dest(pstart_ref, meta_ref, slot, t):
    return pstart_ref[meta_ref[slot, t]] + meta_ref[EXPERT_TOPK + slot, t]


def _dispatch_kernel(pstart_ref, meta_ref, h_ref, xr_in_ref, xr_ref, sem):
    rows = h_ref.shape[0]

    def row_copy(slot, t):
        d = _row_dest(pstart_ref, meta_ref, slot, t)
        return pltpu.make_async_copy(h_ref.at[pl.ds(t, 1), :], xr_ref.at[pl.ds(d, 1), :], sem)

    def issue(t, carry):
        for slot in range(EXPERT_TOPK):
            row_copy(slot, t).start()
        return carry

    lax.fori_loop(0, rows, issue, 0, unroll=4)
    for slot in range(EXPERT_TOPK):
        pltpu.make_async_copy(h_ref, xr_ref.at[pl.ds(0, rows), :], sem).wait()


def _dispatch_rows(pstart, meta, h2, xr):
    n = h2.shape[0]
    tile = min(ROW_TILE, n)
    return pl.pallas_call(
        _dispatch_kernel,
        grid=(n // tile,),
        in_specs=[pl.BlockSpec(memory_space=pltpu.SMEM),
                  pl.BlockSpec((8, tile), lambda i: (0, i), memory_space=pltpu.SMEM),
                  pl.BlockSpec((tile, D_MODEL), lambda i: (i, 0)),
                  pl.BlockSpec(memory_space=pl.ANY)],
        out_specs=pl.BlockSpec(memory_space=pl.ANY),
        out_shape=jax.ShapeDtypeStruct(xr.shape, xr.dtype),
        scratch_shapes=[pltpu.SemaphoreType.DMA(())],
        input_output_aliases={3: 0},
        compiler_params=pltpu.CompilerParams(dimension_semantics=("arbitrary",), vmem_limit_bytes=VMEM_LIMIT),
        name="dispatch",
    )(pstart, meta, h2, xr)


def _moe_kernel(blk_e_ref, nact_ref, x_ref, wg_ref, wu_ref, wd_ref, y_ref, wgb_ref, wub_ref, wdb_ref):
    i = pl.program_id(0)
    prev = blk_e_ref[jnp.maximum(i - 1, 0)]
    active = i < nact_ref[0]

    @pl.when(active & ((i == 0) | (blk_e_ref[i] != prev)))
    def _():
        wgb_ref[...] = wg_ref[0].astype(BF16)
        wub_ref[...] = wu_ref[0].astype(BF16)
        wdb_ref[...] = wd_ref[0].astype(BF16)

    @pl.when(active)
    def _():
        x = x_ref[...].astype(BF16)
        g = _dot(x, wgb_ref[...])
        u = _dot(x, wub_ref[...])
        a = g / (1.0 + jnp.exp(-g)) * u
        y_ref[...] = _dot(a.astype(BF16), wdb_ref[...])

    @pl.when(jnp.logical_not(active))
    def _():
        y_ref[...] = jnp.zeros_like(y_ref)


def _moe(blk_e, n_active, xr, w_gate, w_up, w_down):
    n_rows = xr.shape[0]
    n_blocks = n_rows // MOE_ROWS
    grid_spec = pltpu.PrefetchScalarGridSpec(
        num_scalar_prefetch=2,
        grid=(n_blocks,),
        in_specs=[pl.BlockSpec((MOE_ROWS, D_MODEL), lambda i, be, na: (i, 0)),
                  pl.BlockSpec((1, D_MODEL, D_FF), lambda i, be, na: (be[i], 0, 0)),
                  pl.BlockSpec((1, D_MODEL, D_FF), lambda i, be, na: (be[i], 0, 0)),
                  pl.BlockSpec((1, D_FF, D_MODEL), lambda i, be, na: (be[i], 0, 0))],
        out_specs=pl.BlockSpec((MOE_ROWS, D_MODEL), lambda i, be, na: (i, 0)),
        scratch_shapes=[pltpu.VMEM((D_MODEL, D_FF), BF16),
                        pltpu.VMEM((D_MODEL, D_FF), BF16),
                        pltpu.VMEM((D_FF, D_MODEL), BF16)])
    return pl.pallas_call(
        _moe_kernel,
        grid_spec=grid_spec,
        out_shape=jax.ShapeDtypeStruct((n_rows, D_MODEL), F32),
        compiler_params=pltpu.CompilerParams(
            dimension_semantics=("arbitrary",), vmem_limit_bytes=VMEM_LIMIT),
        name="moe",
    )(blk_e, n_active, xr, w_gate, w_up, w_down)


def _final_kernel(pstart_ref, meta_ref, x1_ref, w_ref, gt2_ref, yr_ref, y_ref, o_ref, sem):
    rows = x1_ref.shape[0]

    def row_copy(slot, t):
        d = _row_dest(pstart_ref, meta_ref, slot, t)
        return pltpu.make_async_copy(yr_ref.at[pl.ds(d, 1), :], o_ref.at[slot, pl.ds(t, 1), :], sem)

    def issue(t, carry):
        for slot in range(EXPERT_TOPK):
            row_copy(slot, t).start()
        return carry

    lax.fori_loop(0, rows, issue, 0, unroll=4)
    for slot in range(EXPERT_TOPK):
        pltpu.make_async_copy(yr_ref.at[pl.ds(0, rows), :], o_ref.at[slot], sem).wait()
    w = w_ref[...]
    moe = o_ref[0] * w[:, 0:1] + o_ref[1] * w[:, 1:2]
    y_ref[...] = x1_ref[...] + _mod_rows(gt2_ref) * moe


def _final(pstart, meta, x1, w2, gt2, yr, per_batch_rows):
    n = x1.shape[0]
    tile = min(ROW_TILE, n)
    tok = lambda i: (i, 0)
    return pl.pallas_call(
        _final_kernel,
        grid=(n // tile,),
        in_specs=[pl.BlockSpec(memory_space=pltpu.SMEM),
                  pl.BlockSpec((8, tile), lambda i: (0, i), memory_space=pltpu.SMEM),
                  pl.BlockSpec((tile, D_MODEL), tok),
                  pl.BlockSpec((tile, EXPERT_TOPK), tok),
                  _mod_spec(tile, per_batch_rows),
                  pl.BlockSpec(memory_space=pl.ANY)],
        out_specs=pl.BlockSpec((tile, D_MODEL), tok),
        out_shape=jax.ShapeDtypeStruct((n, D_MODEL), F32),
        scratch_shapes=[pltpu.VMEM((EXPERT_TOPK, tile, D_MODEL), F32),
                        pltpu.SemaphoreType.DMA(())],
        compiler_params=pltpu.CompilerParams(dimension_semantics=("arbitrary",), vmem_limit_bytes=VMEM_LIMIT),
        name="final",
    )(pstart, meta, x1, w2, gt2, yr)


def _expert_layout(counts, n_blocks):
    padded = (counts + MOE_ROWS - 1) // MOE_ROWS * MOE_ROWS
    pend = jnp.cumsum(padded)
    pstart = (pend - padded).astype(I32)
    blk_start = jnp.arange(n_blocks, dtype=I32) * MOE_ROWS
    blk_e = jnp.minimum(jnp.sum((pend[None, :] <= blk_start[:, None]).astype(I32), axis=1), N_EXPERTS - 1)
    n_active = (pend[-1] // MOE_ROWS).astype(I32).reshape(1)
    return pstart, blk_e.astype(I32), n_active


def kernel(x_prompt, x_sample, cache_k, cache_v, state_pool, page_table, c_prompt, c_sample, w_ada, b_ada,
           g_attn_norm, w_in, g_q, g_k, w_pool, pool_scale, w_out, g_ffn_norm, w_group, b_group, w_expert,
           b_expert, w_gate, w_up, w_down):
    D = D_MODEL
    B, T, _ = x_prompt.shape
    n_s = DEC_BATCH * DEC_SEQ
    n_p = B * T
    layer = 0

    win = w_in[layer].astype(BF16)
    wout = w_out[layer].astype(BF16)
    wpool = w_pool[layer].astype(BF16)
    g1 = g_attn_norm[layer].reshape(1, D)
    g2 = g_ffn_norm[layer].reshape(1, D)
    gq = jnp.tile(g_q[layer], N_HEADS).reshape(1, ATTN_WIDTH)
    gk = jnp.tile(g_k[layer], N_HEADS).reshape(1, ATTN_WIDTH)
    ps = pool_scale[layer].reshape(1, POOL_WIDTH)
    hd = jnp.arange(ATTN_WIDTH) // HEAD_DIM
    bd = (hd[:, None] == hd[None, :]).astype(BF16)
    slopes = jnp.exp2(-8.0 * (jnp.arange(N_HEADS, dtype=F32) + 1.0) / N_HEADS)
    qrow = jnp.arange(N_HEADS * DEC_SEQ)
    slope_col = slopes[qrow // DEC_SEQ].reshape(-1, 1)
    lq_col = (qrow % DEC_SEQ).astype(F32).reshape(-1, 1)
    wr = jnp.zeros((D, LANES), F32).at[:, 0:N_GROUPS].set(w_group[layer]).at[:, 8:8 + N_EXPERTS].set(w_expert[layer])
    wr_hi = wr.astype(BF16)
    wr_lo = (wr - wr_hi.astype(F32)).astype(BF16)
    wr_both = jnp.concatenate([wr_hi, wr_lo], axis=1)
    br = jnp.zeros((LANES,), F32).at[0:N_GROUPS].set(b_group[layer]).at[N_GROUPS:8].set(NEG)
    br = br.at[8:8 + N_EXPERTS].set(b_expert[layer]).reshape(LANES, 1)

    mod = _ada(jnp.concatenate([c_prompt, c_sample], axis=0), w_ada[layer], b_ada[layer])
    mod_p = mod[:B].reshape(B, 1, 6 * D)
    mod_s = jnp.repeat(mod[B:], DEC_SEQ, axis=0)

    ypool_p, q_p, k_p, v_p, tail_p = _inproj_prompt(x_prompt, mod_p, g1, win, bd, gq, gk, wpool, ps)
    yattn_p = _moba_prompt(slopes, q_p, k_p, v_p)
    mp = lambda j: mod_p[:, :, j * D:(j + 1) * D]
    cnt0 = jnp.zeros((N_EXPERTS, LANES), F32)
    x1_p, h2_p, meta_p, ew_p, cnt_p = _outproj(
        ypool_p.reshape(n_p, POOL_WIDTH), yattn_p.reshape(n_p, ATTN_WIDTH), x_prompt.reshape(n_p, D),
        mp(2), mp(3), mp(4), g2, wout, wr_both, wr_hi, br, cnt0, T)

    ms = lambda j: mod_s[:, j * D:(j + 1) * D]
    hist = jnp.concatenate([jnp.zeros((DEC_BATCH, HALO - POOL_HIST, POOL_WIDTH), F32), state_pool[layer]], axis=1)
    xs = x_sample.reshape(n_s, D)
    ypool_s, q_s, k_s, v_s, tail_s = _inproj_sample(xs, ms(0), ms(1), g1, win, bd, gq, gk, hist, wpool, ps)
    r3 = lambda a: a.reshape(DEC_BATCH, DEC_SEQ, ATTN_WIDTH)
    n_phys = cache_k.shape[1]
    ck = jnp.transpose(cache_k[layer], (0, 2, 3, 1)).reshape(n_phys, ATTN_WIDTH, PAGE_SIZE)
    cv = jnp.transpose(cache_v[layer], (0, 2, 3, 1)).reshape(n_phys, ATTN_WIDTH, PAGE_SIZE)
    yattn_s = _moba_sample(page_table, r3(q_s), r3(k_s), r3(v_s), slope_col, lq_col, ck, cv)
    x1_s, h2_s, meta_s, ew_s, cnt_s = _outproj(ypool_s, yattn_s.reshape(n_s, ATTN_WIDTH), xs, ms(2), ms(3), ms(4),
                                               g2, wout, wr_both, wr_hi, br, cnt_p, None)

    n_blocks = (n_p + n_s) * EXPERT_TOPK // MOE_ROWS + N_EXPERTS
    pstart, blk_e, n_active = _expert_layout(cnt_s[:, 0].astype(I32), n_blocks)
    xr = jnp.zeros((n_blocks * MOE_ROWS, D), F32)
    xr = _dispatch_rows(pstart, meta_p, h2_p, xr)
    xr = _dispatch_rows(pstart, meta_s, h2_s, xr)
    yr = _moe(blk_e, n_active, xr, w_gate[layer], w_up[layer], w_down[layer])
    y_p = _final(pstart, meta_p, x1_p, ew_p[0:EXPERT_TOPK].T, mp(5), yr, T)
    y_s = _final(pstart, meta_s, x1_s, ew_s[0:EXPERT_TOPK].T, ms(5), yr, None)

    k4 = lambda a, b, l: a.reshape(1, b, l, N_HEADS, HEAD_DIM)
    return (y_p.reshape(B, T, D), y_s.reshape(DEC_BATCH, DEC_SEQ, D),
            k4(k_p, B, T), k4(v_p, B, T), tail_p[None, :, HALO - POOL_HIST:, :],
            k4(k_s, DEC_BATCH, DEC_SEQ), k4(v_s, DEC_BATCH, DEC_SEQ), tail_s[None, :, HALO - POOL_HIST:, :])
```

```python
import functools

import jax
import jax.numpy as jnp
from jax import lax
from jax.experimental import pallas as pl
from jax.experimental.pallas import tpu as pltpu

F32 = jnp.float32
BF16 = jnp.bfloat16
I32 = jnp.int32

D_MODEL = 1024
BATCH = 8
SEQ = 2048
DEC_BATCH = 32
DEC_SEQ = 8
PAST_LEN = 16384
PAGE_SIZE = 128
POOL_WIDTH = 512
POOL_WINDOWS = (2, 4, 8, 16)
POOL_GROUP = 128
POOL_HIST = 15
HALO = 16
N_HEADS = 8
HEAD_DIM = 64
ATTN_WIDTH = 512
MOBA_BLOCK = 256
MOBA_TOPK = 3
ATTN_SCALE = HEAD_DIM ** -0.5
MIX_IN = POOL_WIDTH + 3 * ATTN_WIDTH
N_GROUPS = 4
EXPERTS_PER_GROUP = 8
N_EXPERTS = 32
EXPERT_TOPK = 2
D_FF = 512
EPS = 1e-6
NEG = -1e30

LANES = 128
ROW_TILE = 512
MOE_ROWS = 256
PAGES_PER_STEP = 16
VMEM_LIMIT = 56 * 1024 * 1024

_NT = (((1,), (1,)), ((), ()))


def _dot(a, b):
    return jnp.dot(a, b, preferred_element_type=F32)


def _dot_nt(a, b):
    return lax.dot_general(a, b, _NT, preferred_element_type=F32)


def _split_dot(a, b01):
    hi = a.astype(BF16)
    lo = (a - hi.astype(F32)).astype(BF16)
    return _dot(hi, b01) + _dot(lo, b01)


def _rms_mod(x, g, sc, sh):
    ms = jnp.mean(x * x, axis=-1, keepdims=True)
    return x * lax.rsqrt(ms + EPS) * g * (1.0 + sc) + sh


def _ada_kernel(c_ref, w_ref, b_ref, o_ref):
    c = c_ref[...]
    a = c / (1.0 + jnp.exp(-c))
    o_ref[...] = _dot(a.astype(BF16), w_ref[...].astype(BF16)) + b_ref[...]


def _ada(c_all, w_ada, b_ada):
    n = c_all.shape[0]
    tn = 1536
    return pl.pallas_call(
        _ada_kernel,
        grid=(6 * D_MODEL // tn,),
        in_specs=[pl.BlockSpec((n, D_MODEL), lambda j: (0, 0)),
                  pl.BlockSpec((D_MODEL, tn), lambda j: (0, j)),
                  pl.BlockSpec((1, tn), lambda j: (0, j))],
        out_specs=pl.BlockSpec((n, tn), lambda j: (0, j)),
        out_shape=jax.ShapeDtypeStruct((n, 6 * D_MODEL), F32),
        compiler_params=pltpu.CompilerParams(vmem_limit_bytes=VMEM_LIMIT),
        name="ada",
    )(c_all, w_ada, b_ada.reshape(1, -1))


def _inproj_core(x, sh1, sc1, g, win_ref, bd_ref, gq_ref, gk_ref):
    h = _rms_mod(x, g, sc1, sh1)
    z = _dot(h.astype(BF16), win_ref[...])
    u = z[:, 0:POOL_WIDTH]
    q = z[:, POOL_WIDTH:POOL_WIDTH + ATTN_WIDTH]
    k = z[:, POOL_WIDTH + ATTN_WIDTH:POOL_WIDTH + 2 * ATTN_WIDTH]
    v = z[:, POOL_WIDTH + 2 * ATTN_WIDTH:]
    bd = bd_ref[...]
    q = q * lax.rsqrt(_split_dot(q * q, bd) * (1.0 / HEAD_DIM) + EPS) * gq_ref[...]
    k = k * lax.rsqrt(_split_dot(k * k, bd) * (1.0 / HEAD_DIM) + EPS) * gk_ref[...]
    return u, q, k, v


def _window_sum(e, w):
    s = e
    sh = 1
    while sh < w:
        s = s + pltpu.roll(s, sh, axis=0)
        sh *= 2
    return s


def _inproj_prompt_kernel(x_ref, mod_ref, g_ref, win_ref, bd_ref, gq_ref, gk_ref, wpool_ref, ps_ref,
                          ypool_ref, q_ref, k_ref, v_ref, tail_ref, ext_ref):
    t = pl.program_id(1)
    nt = pl.num_programs(1)
    x = x_ref[0]
    sh1 = mod_ref[0, :, 0:D_MODEL]
    sc1 = mod_ref[0, :, D_MODEL:2 * D_MODEL]
    u, q, k, v = _inproj_core(x, sh1, sc1, g_ref[...], win_ref, bd_ref, gq_ref, gk_ref)
    q_ref[0] = (q * ATTN_SCALE).astype(BF16)
    k_ref[0] = k
    v_ref[0] = v

    @pl.when(t == 0)
    def _():
        ext_ref[0:HALO, :] = jnp.zeros((HALO, POOL_WIDTH), F32)

    ext_ref[HALO:, :] = u
    pos = t * ROW_TILE + lax.broadcasted_iota(I32, (ROW_TILE, 1), 0)
    for gi, w in enumerate(POOL_WINDOWS):
        cols = slice(gi * POOL_GROUP, (gi + 1) * POOL_GROUP)
        win = _window_sum(ext_ref[:, cols], w)[HALO:]
        inv_cnt = 1.0 / jnp.minimum(pos + 1, w).astype(F32)
        pooled = win * inv_cnt - u[:, cols]
        y = _dot(pooled.astype(BF16), wpool_ref[gi]) * ps_ref[:, cols]
        ypool_ref[0, :, cols] = y.astype(BF16)
    last = u[ROW_TILE - HALO:, :]
    ext_ref[0:HALO, :] = last

    @pl.when(t == nt - 1)
    def _():
        tail_ref[0] = last


def _inproj_prompt(x, mod_p, g1, win, bd, gq, gk, wpool, ps):
    B, T, D = x.shape
    nt = T // ROW_TILE
    const2 = lambda b, t: (0, 0)
    tok = lambda b, t: (b, t, 0)
    return pl.pallas_call(
        _inproj_prompt_kernel,
        grid=(B, nt),
        in_specs=[pl.BlockSpec((1, ROW_TILE, D), tok),
                  pl.BlockSpec((1, 1, 6 * D), lambda b, t: (b, 0, 0)),
                  pl.BlockSpec((1, D), const2),
                  pl.BlockSpec((D, MIX_IN), const2),
                  pl.BlockSpec((ATTN_WIDTH, ATTN_WIDTH), const2),
                  pl.BlockSpec((1, ATTN_WIDTH), const2),
                  pl.BlockSpec((1, ATTN_WIDTH), const2),
                  pl.BlockSpec((4, POOL_GROUP, POOL_GROUP), lambda b, t: (0, 0, 0)),
                  pl.BlockSpec((1, POOL_WIDTH), const2)],
        out_specs=[pl.BlockSpec((1, ROW_TILE, POOL_WIDTH), tok),
                   pl.BlockSpec((1, ROW_TILE, ATTN_WIDTH), tok),
                   pl.BlockSpec((1, ROW_TILE, ATTN_WIDTH), tok),
                   pl.BlockSpec((1, ROW_TILE, ATTN_WIDTH), tok),
                   pl.BlockSpec((1, HALO, POOL_WIDTH), lambda b, t: (b, 0, 0))],
        out_shape=[jax.ShapeDtypeStruct((B, T, POOL_WIDTH), BF16),
                   jax.ShapeDtypeStruct((B, T, ATTN_WIDTH), BF16),
                   jax.ShapeDtypeStruct((B, T, ATTN_WIDTH), F32),
                   jax.ShapeDtypeStruct((B, T, ATTN_WIDTH), F32),
                   jax.ShapeDtypeStruct((B, HALO, POOL_WIDTH), F32)],
        scratch_shapes=[pltpu.VMEM((HALO + ROW_TILE, POOL_WIDTH), F32)],
        compiler_params=pltpu.CompilerParams(
            dimension_semantics=("arbitrary", "arbitrary"), vmem_limit_bytes=VMEM_LIMIT),
        name="inproj_prompt",
    )(x, mod_p, g1, win, bd, gq, gk, wpool, ps)


def _inproj_sample_kernel(x_ref, sh_ref, sc_ref, g_ref, win_ref, bd_ref, gq_ref, gk_ref, hist_ref, wpool_ref,
                          ps_ref, ypool_ref, q_ref, k_ref, v_ref, tail_ref, ext_ref):
    n = DEC_BATCH * DEC_SEQ
    ext_rows = HALO + DEC_SEQ
    u, q, k, v = _inproj_core(x_ref[...], sh_ref[...], sc_ref[...], g_ref[...], win_ref, bd_ref, gq_ref, gk_ref)
    q_ref[...] = q * ATTN_SCALE
    k_ref[...] = k
    v_ref[...] = v
    ext_ref[:, 0:HALO, :] = hist_ref[...]
    ext_ref[:, HALO:, :] = u.reshape(DEC_BATCH, DEC_SEQ, POOL_WIDTH)
    tail_ref[...] = ext_ref[:, ext_rows - HALO:, :]
    pos = PAST_LEN + lax.broadcasted_iota(I32, (DEC_BATCH, DEC_SEQ, 1), 1).reshape(n, 1)
    for gi, w in enumerate(POOL_WINDOWS):
        cols = slice(gi * POOL_GROUP, (gi + 1) * POOL_GROUP)
        e = ext_ref[:, :, cols].reshape(DEC_BATCH * ext_rows, POOL_GROUP)
        win = _window_sum(e, w).reshape(DEC_BATCH, ext_rows, POOL_GROUP)[:, HALO:, :].reshape(n, POOL_GROUP)
        inv_cnt = 1.0 / jnp.minimum(pos + 1, w).astype(F32)
        pooled = win * inv_cnt - u[:, cols]
        y = _dot(pooled.astype(BF16), wpool_ref[gi]) * ps_ref[:, cols]
        ypool_ref[:, cols] = y.astype(BF16)


def _inproj_sample(x, sh1, sc1, g1, win, bd, gq, gk, hist, wpool, ps):
    n = x.shape[0]
    return pl.pallas_call(
        _inproj_sample_kernel,
        out_shape=[jax.ShapeDtypeStruct((n, POOL_WIDTH), BF16),
                   jax.ShapeDtypeStruct((n, ATTN_WIDTH), F32),
                   jax.ShapeDtypeStruct((n, ATTN_WIDTH), F32),
                   jax.ShapeDtypeStruct((n, ATTN_WIDTH), F32),
                   jax.ShapeDtypeStruct((DEC_BATCH, HALO, POOL_WIDTH), F32)],
        scratch_shapes=[pltpu.VMEM((DEC_BATCH, HALO + DEC_SEQ, POOL_WIDTH), F32)],
        compiler_params=pltpu.CompilerParams(vmem_limit_bytes=VMEM_LIMIT),
        name="inproj_sample",
    )(x, sh1, sc1, g1, win, bd, gq, gk, hist, wpool, ps)


N_BIAS_LANES = 3


def _moba_prompt_kernel(slopes_ref, q_ref, k_ref, v_ref, o_ref, ka_ref, va_ref, qa_ref, kmf_ref):
    S = MOBA_BLOCK
    T = k_ref.shape[1]
    nb = T // S
    hp = pl.program_id(1)
    lane = lax.broadcasted_iota(I32, (1, LANES), 1)
    real = (lane < HEAD_DIM, lane >= HEAD_DIM)
    extra = (lane - HEAD_DIM, lane)

    kf = k_ref[0]
    vf = v_ref[0]
    kmf_ref[...] = jnp.zeros((LANES, LANES), F32)
    for j in range(nb):
        mean = jnp.sum(kf[j * S:(j + 1) * S], axis=0, keepdims=True) * (1.0 / S)
        kmf_ref[j:j + 1, :] = jnp.where(real[0], mean, 0.0)
        kmf_ref[nb + j:nb + j + 1, :] = jnp.where(real[0], 0.0, mean)
    key_i = lax.broadcasted_iota(I32, (T, 1), 0)
    key_blk = key_i // S
    for h in range(2):
        b = slopes_ref[2 * hp + h] * key_i.astype(F32)
        p0 = b.astype(BF16).astype(F32)
        p1 = (b - p0).astype(BF16).astype(F32)
        p2 = b - p0 - p1
        e = extra[h]
        onehot = jnp.where((e >= N_BIAS_LANES) & (e - N_BIAS_LANES == key_blk), 1.0, 0.0)
        feat = jnp.where(e == 0, p0, jnp.where(e == 1, p1, jnp.where(e == 2, p2, onehot)))
        ka_ref[h] = jnp.where(real[h], kf, feat).astype(BF16)
        va_ref[h] = jnp.where(real[h], vf, 1.0).astype(BF16)

    q2 = q_ref[0]

    gt = _dot_nt(kmf_ref[...].astype(BF16), q2)[0:2 * nb]
    row = lax.broadcasted_iota(I32, (2 * nb, T), 0)
    blk = row % nb
    cbq = lax.broadcasted_iota(I32, (2 * nb, T), 1) // S
    cnt = jnp.zeros((2 * nb, T), I32)
    for m in range(nb):
        gm = jnp.where(row < nb, gt[m:m + 1, :], gt[nb + m:nb + m + 1, :])
        beats = (gm > gt) | ((gm == gt) & (m < blk))
        cnt = cnt + jnp.where(beats & (m < cbq), 1, 0)
    keep = (((cnt < MOBA_TOPK) & (blk < cbq)) | (blk == cbq)).astype(F32)
    keepq = jnp.concatenate([keep, jnp.zeros((LANES - 2 * nb, T), F32)], axis=0).T
    maskv = jnp.where(keepq > 0.5, 0.0, NEG)
    mask_lanes = (pltpu.roll(maskv, HEAD_DIM + N_BIAS_LANES, axis=1),
                  pltpu.roll(maskv, (N_BIAS_LANES - nb) % LANES, axis=1))
    qf = q2.astype(F32)
    for h in range(2):
        e = extra[h]
        feat = jnp.where(e < N_BIAS_LANES, 1.0, jnp.where(e < N_BIAS_LANES + nb, mask_lanes[h], 0.0))
        qa_ref[h] = jnp.where(real[h], qf, feat).astype(BF16)

    causal = lax.broadcasted_iota(I32, (S, S), 1) <= lax.broadcasted_iota(I32, (S, S), 0)
    for cb in range(nb):
        rows = slice(cb * S, (cb + 1) * S)
        n = (cb + 1) * S
        outs = []
        for h in range(2):
            s = _dot_nt(qa_ref[h, rows, :], ka_ref[h, 0:n, :])
            s_own = jnp.where(causal, s[:, cb * S:n], NEG)
            s = s_own if cb == 0 else jnp.concatenate([s[:, 0:cb * S], s_own], axis=1)
            p = jnp.exp(s - jnp.max(s, axis=-1, keepdims=True))
            outs.append(_dot(p.astype(BF16), va_ref[h, 0:n, :]))
        a0, a1 = outs
        o0 = a0 * (1.0 / a0[:, HEAD_DIM:HEAD_DIM + 1])
        o1 = a1 * (1.0 / a1[:, 0:1])
        o_ref[0, rows, :] = jnp.where(real[0], o0, o1).astype(BF16)


def _moba_prompt(slopes, q, k, v):
    B, T, _ = q.shape
    seq = lambda b, hp: (b, 0, hp)
    return pl.pallas_call(
        _moba_prompt_kernel,
        grid=(B, N_HEADS // 2),
        in_specs=[pl.BlockSpec(memory_space=pltpu.SMEM),
                  pl.BlockSpec((1, T, LANES), seq),
                  pl.BlockSpec((1, T, LANES), seq),
                  pl.BlockSpec((1, T, LANES), seq)],
        out_specs=pl.BlockSpec((1, T, LANES), seq),
        out_shape=jax.ShapeDtypeStruct((B, T, ATTN_WIDTH), BF16),
        scratch_shapes=[pltpu.VMEM((2, T, LANES), BF16),
                        pltpu.VMEM((2, T, LANES), BF16),
                        pltpu.VMEM((2, T, LANES), BF16),
                        pltpu.VMEM((LANES, LANES), F32)],
        compiler_params=pltpu.CompilerParams(
            dimension_semantics=("arbitrary", "arbitrary"), vmem_limit_bytes=VMEM_LIMIT),
        name="moba_prompt",
    )(slopes, q, k, v)


CHUNK_SLOTS = 4


def _moba_sample_kernel(pt_ref, q_ref, kn_ref, vn_ref, slope_ref, lq_ref, ck_ref, cv_ref, o_ref,
                        buf_ref, sem, s_ref, p_ref, kc_ref, acc_ref):
    P = PAGES_PER_STEP
    S = MOBA_BLOCK
    n_blocks = PAST_LEN // S
    chunk = P * PAGE_SIZE
    n_chunks = PAST_LEN // chunk
    bpc = chunk // S
    n_loads = 2 * n_chunks
    ahead = CHUNK_SLOTS - 1
    assert n_loads % CHUNK_SLOTS == 0
    nq = N_HEADS * DEC_SEQ
    b = pl.program_id(0)
    slope = slope_ref[...]
    lq = lq_ref[...]
    lane = lax.broadcasted_iota(I32, (1, LANES), 1)

    def start_load(bb, i):
        src = ck_ref if i < n_chunks else cv_ref
        slot = i % CHUNK_SLOTS
        for r in range(P):
            page = pt_ref[bb, (i % n_chunks) * P + r]
            pltpu.make_async_copy(src.at[page], buf_ref.at[slot, r], sem.at[slot]).start()

    def wait_load(i):
        slot = i % CHUNK_SLOTS
        pltpu.make_async_copy(ck_ref.at[pl.ds(0, P)], buf_ref.at[slot], sem.at[slot]).wait()

    def load_chunk(i):
        nxt = i + ahead
        if nxt < n_loads:
            start_load(b, nxt)
        else:
            @pl.when(b + 1 < pl.num_programs(0))
            def _():
                start_load(b + 1, nxt - n_loads)
        wait_load(i)
        slot = i % CHUNK_SLOTS
        for r in range(P):
            kc_ref[:, r * PAGE_SIZE:(r + 1) * PAGE_SIZE] = buf_ref[slot, r].astype(BF16)
        return kc_ref[...]

    @pl.when(b == 0)
    def _():
        for i in range(ahead):
            start_load(b, i)

    q8 = q_ref[0]
    head = lax.broadcasted_iota(I32, (DEC_SEQ, ATTN_WIDTH), 1) // HEAD_DIM
    qp = jnp.concatenate([jnp.where(head == h, q8, 0.0) for h in range(N_HEADS)], axis=0).astype(BF16)

    gate = jnp.zeros((nq, LANES), F32)
    for c in range(n_chunks):
        sc = _dot(qp, load_chunk(c))
        for r2 in range(bpc):
            gs = jnp.sum(sc[:, r2 * S:(r2 + 1) * S], axis=-1, keepdims=True) * (1.0 / S)
            gate = jnp.where(lane == c * bpc + r2, gs, gate)
        keypos = c * chunk + lax.broadcasted_iota(I32, (1, chunk), 1)
        dist = (keypos - PAST_LEN).astype(F32) - lq
        s_ref[c] = sc + slope * dist

    cnt = jnp.zeros((nq, LANES), I32)
    for m in range(n_blocks):
        gm = gate[:, m:m + 1]
        beats = (gm > gate) | ((gm == gate) & (m < lane))
        cnt = cnt + jnp.where(beats, 1, 0)
    selq = ((cnt < min(MOBA_TOPK, n_blocks)) & (lane < n_blocks)).astype(F32)
    keeps = [selq[:, n:n + 1] > 0.5 for n in range(n_blocks)]

    kn = jnp.concatenate([kn_ref[0], jnp.zeros((LANES - DEC_SEQ, ATTN_WIDTH), F32)], axis=0)
    lane_f = lane.astype(F32)
    sn = _dot_nt(qp, kn.astype(BF16)) + slope * (lane_f - lq)
    sn = jnp.where(lane_f <= lq, sn, NEG)

    mxv = sn
    for n in range(n_blocks):
        c, r2 = divmod(n, bpc)
        blk = jnp.where(keeps[n], s_ref[c, :, r2 * S:(r2 + 1) * S], NEG)
        for i in range(S // LANES):
            mxv = jnp.maximum(mxv, blk[:, i * LANES:(i + 1) * LANES])
    mx = jnp.max(mxv, axis=-1, keepdims=True)
    pn = jnp.exp(sn - mx)
    lsv = pn
    for n in range(n_blocks):
        c, r2 = divmod(n, bpc)
        p = jnp.exp(jnp.where(keeps[n], s_ref[c, :, r2 * S:(r2 + 1) * S] - mx, NEG))
        p_ref[c, :, r2 * S:(r2 + 1) * S] = p.astype(BF16)
        for i in range(S // LANES):
            lsv = lsv + p[:, i * LANES:(i + 1) * LANES]
    linv = 1.0 / jnp.sum(lsv, axis=-1, keepdims=True)
    vn = jnp.concatenate([vn_ref[0], jnp.zeros((LANES - DEC_SEQ, ATTN_WIDTH), F32)], axis=0)
    acc_ref[...] = _dot(pn.astype(BF16), vn.astype(BF16))

    for c in range(n_chunks):
        acc_ref[...] += _dot_nt(p_ref[c], load_chunk(n_chunks + c))

    o = acc_ref[...] * linv
    res = jnp.zeros((DEC_SEQ, ATTN_WIDTH), F32)
    for h in range(N_HEADS):
        res = res + jnp.where(head == h, o[h * DEC_SEQ:(h + 1) * DEC_SEQ, :], 0.0)
    o_ref[0] = res.astype(BF16)


def _moba_sample(page_table, q, kn, vn, slope_col, lq_col, cache_kt, cache_vt):
    P = PAGES_PER_STEP
    chunk = P * PAGE_SIZE
    n_chunks = PAST_LEN // chunk
    nq = N_HEADS * DEC_SEQ
    row3 = lambda b, pt: (b, 0, 0)
    const2 = lambda b, pt: (0, 0)
    grid_spec = pltpu.PrefetchScalarGridSpec(
        num_scalar_prefetch=1,
        grid=(DEC_BATCH,),
        in_specs=[pl.BlockSpec((1, DEC_SEQ, ATTN_WIDTH), row3),
                  pl.BlockSpec((1, DEC_SEQ, ATTN_WIDTH), row3),
                  pl.BlockSpec((1, DEC_SEQ, ATTN_WIDTH), row3),
                  pl.BlockSpec((nq, 1), const2),
                  pl.BlockSpec((nq, 1), const2),
                  pl.BlockSpec(memory_space=pl.ANY),
                  pl.BlockSpec(memory_space=pl.ANY)],
        out_specs=pl.BlockSpec((1, DEC_SEQ, ATTN_WIDTH), row3),
        scratch_shapes=[pltpu.VMEM((CHUNK_SLOTS, P, ATTN_WIDTH, PAGE_SIZE), F32),
                        pltpu.SemaphoreType.DMA((CHUNK_SLOTS,)),
                        pltpu.VMEM((n_chunks, nq, chunk), F32),
                        pltpu.VMEM((n_chunks, nq, chunk), BF16),
                        pltpu.VMEM((ATTN_WIDTH, chunk), BF16),
                        pltpu.VMEM((nq, ATTN_WIDTH), F32)])
    return pl.pallas_call(
        _moba_sample_kernel,
        grid_spec=grid_spec,
        out_shape=jax.ShapeDtypeStruct((DEC_BATCH, DEC_SEQ, ATTN_WIDTH), BF16),
        compiler_params=pltpu.CompilerParams(dimension_semantics=("arbitrary",), vmem_limit_bytes=VMEM_LIMIT),
        name="moba_sample",
    )(page_table, q, kn, vn, slope_col, lq_col, cache_kt, cache_vt)


def _mod_rows(ref):
    return ref[...].reshape(-1, ref.shape[-1])


def _outproj_kernel(yp_ref, ya_ref, x_ref, gt1_ref, sh2_ref, sc2_ref, g2_ref, wout_ref, wr_ref, wrhi_ref, br_ref,
                    tri_ref, cntin_ref, x1_ref, h2_ref, meta_ref, ew_ref, cntout_ref, cnt_ref):
    rows = x_ref.shape[0]

    @pl.when(pl.program_id(0) == 0)
    def _():
        cnt_ref[...] = cntin_ref[...]

    mix = _dot(yp_ref[...], wout_ref[0:POOL_WIDTH, :]) + _dot(ya_ref[...], wout_ref[POOL_WIDTH:, :])
    x1 = x_ref[...] + _mod_rows(gt1_ref) * mix
    x1_ref[...] = x1
    h2 = _rms_mod(x1, g2_ref[...], _mod_rows(sc2_ref), _mod_rows(sh2_ref))
    h2_ref[...] = h2
    hh = h2.astype(BF16)
    hl = (h2 - hh.astype(F32)).astype(BF16)
    both = _dot(hh, wr_ref[...])
    lt = (both[:, 0:LANES] + both[:, LANES:] + _dot(hl, wrhi_ref[...])).T + br_ref[...]
    row8 = lax.broadcasted_iota(I32, (8, rows), 0)
    g8 = lt[0:8]
    gmax = jnp.max(g8, axis=0, keepdims=True)
    gsum = jnp.sum(jnp.exp(g8 - gmax), axis=0, keepdims=True)
    g_w = 1.0 / gsum
    g_idx = jnp.min(jnp.where(g8 == gmax, row8, 8), axis=0, keepdims=True)
    e_in = jnp.zeros((8, rows), F32)
    for g in range(N_GROUPS):
        e_in = e_in + jnp.where(g_idx == g, lt[8 + 8 * g:16 + 8 * g], 0.0)
    m1 = jnp.max(e_in, axis=0, keepdims=True)
    i1 = jnp.min(jnp.where(e_in == m1, row8, 8), axis=0, keepdims=True)
    e_rest = jnp.where(row8 == i1, NEG, e_in)
    m2 = jnp.max(e_rest, axis=0, keepdims=True)
    i2 = jnp.min(jnp.where(e_rest == m2, row8, 8), axis=0, keepdims=True)
    r = jnp.exp(m2 - m1)
    w1 = g_w / (1.0 + r)
    w2 = g_w * r / (1.0 + r)
    e1 = g_idx * EXPERTS_PER_GROUP + i1
    e2 = g_idx * EXPERTS_PER_GROUP + i2
    rowe = lax.broadcasted_iota(I32, (N_EXPERTS, rows), 0)
    oh1 = (rowe == e1).astype(F32)
    oh2 = (rowe == e2).astype(F32)
    both_oh = oh1 + oh2
    before = _dot(both_oh.astype(BF16), tri_ref[...]) + cnt_ref[:, 0:1]
    rank1 = jnp.sum(oh1 * before, axis=0, keepdims=True).astype(I32)
    rank2 = jnp.sum(oh2 * before, axis=0, keepdims=True).astype(I32)
    cnt_ref[...] = cnt_ref[...] + jnp.sum(both_oh, axis=-1, keepdims=True)
    cntout_ref[...] = cnt_ref[...]
    meta_ref[...] = jnp.where(row8 == 0, e1, jnp.where(row8 == 1, e2, jnp.where(row8 == 2, rank1,
                                                                                 jnp.where(row8 == 3, rank2, 0))))
    ew_ref[...] = jnp.where(row8 == 0, w1, jnp.where(row8 == 1, w2, 0.0))


def _mod_spec(tile, per_batch_rows):
    if per_batch_rows is None:
        return pl.BlockSpec((tile, D_MODEL), lambda i: (i, 0))
    per = per_batch_rows // tile
    return pl.BlockSpec((1, 1, D_MODEL), lambda i: (i // per, 0, 0))


def _outproj(yp, ya, x, gt1, sh2, sc2, g2, wout, wr, wrhi, br, cnt_in, per_batch_rows):
    n = x.shape[0]
    tile = min(ROW_TILE, n)
    tok = lambda i: (i, 0)
    const2 = lambda i: (0, 0)
    mod_spec = _mod_spec(tile, per_batch_rows)
    tri = (jnp.arange(tile)[:, None] < jnp.arange(tile)[None, :]).astype(BF16)
    return pl.pallas_call(
        _outproj_kernel,
        grid=(n // tile,),
        in_specs=[pl.BlockSpec((tile, POOL_WIDTH), tok),
                  pl.BlockSpec((tile, ATTN_WIDTH), tok),
                  pl.BlockSpec((tile, D_MODEL), tok),
                  mod_spec, mod_spec, mod_spec,
                  pl.BlockSpec((1, D_MODEL), const2),
                  pl.BlockSpec((2 * POOL_WIDTH, D_MODEL), const2),
                  pl.BlockSpec((D_MODEL, 2 * LANES), const2),
                  pl.BlockSpec((D_MODEL, LANES), const2),
                  pl.BlockSpec((LANES, 1), const2),
                  pl.BlockSpec((tile, tile), const2),
                  pl.BlockSpec((N_EXPERTS, LANES), const2)],
        out_specs=[pl.BlockSpec((tile, D_MODEL), tok),
                   pl.BlockSpec((tile, D_MODEL), tok),
                   pl.BlockSpec((8, tile), lambda i: (0, i)),
                   pl.BlockSpec((8, tile), lambda i: (0, i)),
                   pl.BlockSpec((N_EXPERTS, LANES), const2)],
        out_shape=[jax.ShapeDtypeStruct((n, D_MODEL), F32),
                   jax.ShapeDtypeStruct((n, D_MODEL), F32),
                   jax.ShapeDtypeStruct((8, n), I32),
                   jax.ShapeDtypeStruct((8, n), F32),
                   jax.ShapeDtypeStruct((N_EXPERTS, LANES), F32)],
        scratch_shapes=[pltpu.VMEM((N_EXPERTS, LANES), F32)],
        compiler_params=pltpu.CompilerParams(dimension_semantics=("arbitrary",), vmem_limit_bytes=VMEM_LIMIT),
        name="outproj",
    )(yp, ya, x, gt1, sh2, sc2, g2, wout, wr, wrhi, br, tri, cnt_in)


def _dest_kernel(pstart_ref, meta_ref, dest_ref):
    n = meta_ref.shape[1]
    eid = meta_ref[0:EXPERT_TOPK, :]
    start = jnp.zeros((EXPERT_TOPK, n), I32)
    for e in range(N_EXPERTS):
        start = jnp.where(eid == e, pstart_ref[e], start)
    dest_ref[...] = start + meta_ref[EXPERT_TOPK:2 * EXPERT_TOPK, :]


def _dest_rows(pstart, meta):
    n = meta.shape[1]
    dest = pl.pallas_call(
        _dest_kernel,
        in_specs=[pl.BlockSpec(memory_space=pltpu.SMEM), pl.BlockSpec(memory_space=pltpu.VMEM)],
        out_specs=pl.BlockSpec(memory_space=pltpu.VMEM),
        out_shape=jax.ShapeDtypeStruct((EXPERT_TOPK, n), I32),
        name="dest",
    )(pstart, meta)
    return [dest[slot] for slot in range(EXPERT_TOPK)]


DMA_ISSUE_UNROLL = 8


def _dispatch_kernel(*refs):
    dest_refs = refs[0:EXPERT_TOPK]
    h_ref, xr_in_ref, xr_ref, sem = refs[EXPERT_TOPK:]
    rows = h_ref.shape[0]

    def issue(t, carry):
        for d_ref in dest_refs:
            pltpu.make_async_copy(h_ref.at[pl.ds(t, 1), :], xr_ref.at[pl.ds(d_ref[t], 1), :], sem).start()
        return carry

    lax.fori_loop(0, rows, issue, 0, unroll=DMA_ISSUE_UNROLL)
    for _ in dest_refs:
        pltpu.make_async_copy(h_ref, xr_ref.at[pl.ds(0, rows), :], sem).wait()


def _dispatch_rows(dests, h2, xr):
    n = h2.shape[0]
    tile = min(ROW_TILE, n)
    smem_rows = pl.BlockSpec((tile,), lambda i: (i,), memory_space=pltpu.SMEM)
    return pl.pallas_call(
        _dispatch_kernel,
        grid=(n // tile,),
        in_specs=[smem_rows] * EXPERT_TOPK + [pl.BlockSpec((tile, D_MODEL), lambda i: (i, 0)),
                                             pl.BlockSpec(memory_space=pl.ANY)],
        out_specs=pl.BlockSpec(memory_space=pl.ANY),
        out_shape=jax.ShapeDtypeStruct(xr.shape, xr.dtype),
        scratch_shapes=[pltpu.SemaphoreType.DMA(())],
        input_output_aliases={EXPERT_TOPK + 1: 0},
        compiler_params=pltpu.CompilerParams(dimension_semantics=("arbitrary",), vmem_limit_bytes=VMEM_LIMIT),
        name="dispatch",
    )(*dests, h2, xr)


def _moe_kernel(blk_e_ref, nact_ref, x_ref, wg_ref, wu_ref, wd_ref, y_ref, wgb_ref, wub_ref, wdb_ref):
    i = pl.program_id(0)
    prev = blk_e_ref[jnp.maximum(i - 1, 0)]
    active = i < nact_ref[0]

    @pl.when(active & ((i == 0) | (blk_e_ref[i] != prev)))
    def _():
        wgb_ref[...] = wg_ref[0].astype(BF16)
        wub_ref[...] = wu_ref[0].astype(BF16)
        wdb_ref[...] = wd_ref[0].astype(BF16)

    @pl.when(active)
    def _():
        x = x_ref[...].astype(BF16)
        g = _dot(x, wgb_ref[...])
        u = _dot(x, wub_ref[...])
        a = g / (1.0 + jnp.exp(-g)) * u
        y_ref[...] = _dot(a.astype(BF16), wdb_ref[...])

    @pl.when(jnp.logical_not(active))
    def _():
        y_ref[...] = jnp.zeros_like(y_ref)


def _moe(blk_e, n_active, xr, w_gate, w_up, w_down):
    n_rows = xr.shape[0]
    n_blocks = n_rows // MOE_ROWS
    grid_spec = pltpu.PrefetchScalarGridSpec(
        num_scalar_prefetch=2,
        grid=(n_blocks,),
        in_specs=[pl.BlockSpec((MOE_ROWS, D_MODEL), lambda i, be, na: (i, 0)),
                  pl.BlockSpec((1, D_MODEL, D_FF), lambda i, be, na: (be[i], 0, 0)),
                  pl.BlockSpec((1, D_MODEL, D_FF), lambda i, be, na: (be[i], 0, 0)),
                  pl.BlockSpec((1, D_FF, D_MODEL), lambda i, be, na: (be[i], 0, 0))],
        out_specs=pl.BlockSpec((MOE_ROWS, D_MODEL), lambda i, be, na: (i, 0)),
        scratch_shapes=[pltpu.VMEM((D_MODEL, D_FF), BF16),
                        pltpu.VMEM((D_MODEL, D_FF), BF16),
                        pltpu.VMEM((D_FF, D_MODEL), BF16)])
    return pl.pallas_call(
        _moe_kernel,
        grid_spec=grid_spec,
        out_shape=jax.ShapeDtypeStruct((n_rows, D_MODEL), F32),
        compiler_params=pltpu.CompilerParams(
            dimension_semantics=("arbitrary",), vmem_limit_bytes=VMEM_LIMIT),
        name="moe",
    )(blk_e, n_active, xr, w_gate, w_up, w_down)


def _final_kernel(*refs):
    dest_refs = refs[0:EXPERT_TOPK]
    x1_ref, w_ref, gt2_ref, yr_ref, y_ref, o_ref, sem = refs[EXPERT_TOPK:]
    rows = x1_ref.shape[0]

    def issue(t, carry):
        for slot, d_ref in enumerate(dest_refs):
            pltpu.make_async_copy(yr_ref.at[pl.ds(d_ref[t], 1), :], o_ref.at[slot, pl.ds(t, 1), :], sem).start()
        return carry

    lax.fori_loop(0, rows, issue, 0, unroll=DMA_ISSUE_UNROLL)
    for slot in range(EXPERT_TOPK):
        pltpu.make_async_copy(yr_ref.at[pl.ds(0, rows), :], o_ref.at[slot], sem).wait()
    w = w_ref[...]
    moe = o_ref[0] * w[:, 0:1] + o_ref[1] * w[:, 1:2]
    y_ref[...] = x1_ref[...] + _mod_rows(gt2_ref) * moe


def _final(dests, x1, w2, gt2, yr, per_batch_rows):
    n = x1.shape[0]
    tile = min(ROW_TILE, n)
    tok = lambda i: (i, 0)
    smem_rows = pl.BlockSpec((tile,), lambda i: (i,), memory_space=pltpu.SMEM)
    return pl.pallas_call(
        _final_kernel,
        grid=(n // tile,),
        in_specs=[smem_rows] * EXPERT_TOPK + [pl.BlockSpec((tile, D_MODEL), tok),
                                             pl.BlockSpec((tile, EXPERT_TOPK), tok),
                                             _mod_spec(tile, per_batch_rows),
                                             pl.BlockSpec(memory_space=pl.ANY)],
        out_specs=pl.BlockSpec((tile, D_MODEL), tok),
        out_shape=jax.ShapeDtypeStruct((n, D_MODEL), F32),
        scratch_shapes=[pltpu.VMEM((EXPERT_TOPK, tile, D_MODEL), F32),
                        pltpu.SemaphoreType.DMA(())],
        compiler_params=pltpu.CompilerParams(dimension_semantics=("arbitrary",), vmem_limit_bytes=VMEM_LIMIT),
        name="final",
    )(*dests, x1, w2, gt2, yr)


def _expert_layout(counts, n_blocks):
    padded = (counts + MOE_ROWS - 1) // MOE_ROWS * MOE_ROWS
    pend = jnp.cumsum(padded)
    pstart = (pend - padded).astype(I32)
    blk_start = jnp.arange(n_blocks, dtype=I32) * MOE_ROWS
    blk_e = jnp.minimum(jnp.sum((pend[None, :] <= blk_start[:, None]).astype(I32), axis=1), N_EXPERTS - 1)
    n_active = (pend[-1] // MOE_ROWS).astype(I32).reshape(1)
    return pstart, blk_e.astype(I32), n_active


def kernel(x_prompt, x_sample, cache_k, cache_v, state_pool, page_table, c_prompt, c_sample, w_ada, b_ada,
           g_attn_norm, w_in, g_q, g_k, w_pool, pool_scale, w_out, g_ffn_norm, w_group, b_group, w_expert,
           b_expert, w_gate, w_up, w_down):
    D = D_MODEL
    B, T, _ = x_prompt.shape
    n_s = DEC_BATCH * DEC_SEQ
    n_p = B * T
    layer = 0

    win = w_in[layer].astype(BF16)
    wout = w_out[layer].astype(BF16)
    wpool = w_pool[layer].astype(BF16)
    g1 = g_attn_norm[layer].reshape(1, D)
    g2 = g_ffn_norm[layer].reshape(1, D)
    gq = jnp.tile(g_q[layer], N_HEADS).reshape(1, ATTN_WIDTH)
    gk = jnp.tile(g_k[layer], N_HEADS).reshape(1, ATTN_WIDTH)
    ps = pool_scale[layer].reshape(1, POOL_WIDTH)
    hd = jnp.arange(ATTN_WIDTH) // HEAD_DIM
    bd = (hd[:, None] == hd[None, :]).astype(BF16)
    slopes = jnp.exp2(-8.0 * (jnp.arange(N_HEADS, dtype=F32) + 1.0) / N_HEADS)
    qrow = jnp.arange(N_HEADS * DEC_SEQ)
    slope_col = slopes[qrow // DEC_SEQ].reshape(-1, 1)
    lq_col = (qrow % DEC_SEQ).astype(F32).reshape(-1, 1)
    wr = jnp.zeros((D, LANES), F32).at[:, 0:N_GROUPS].set(w_group[layer]).at[:, 8:8 + N_EXPERTS].set(w_expert[layer])
    wr_hi = wr.astype(BF16)
    wr_lo = (wr - wr_hi.astype(F32)).astype(BF16)
    wr_both = jnp.concatenate([wr_hi, wr_lo], axis=1)
    br = jnp.zeros((LANES,), F32).at[0:N_GROUPS].set(b_group[layer]).at[N_GROUPS:8].set(NEG)
    br = br.at[8:8 + N_EXPERTS].set(b_expert[layer]).reshape(LANES, 1)

    mod = _ada(jnp.concatenate([c_prompt, c_sample], axis=0), w_ada[layer], b_ada[layer])
    mod_p = mod[:B].reshape(B, 1, 6 * D)
    mod_s = jnp.repeat(mod[B:], DEC_SEQ, axis=0)

    ypool_p, q_p, k_p, v_p, tail_p = _inproj_prompt(x_prompt, mod_p, g1, win, bd, gq, gk, wpool, ps)
    yattn_p = _moba_prompt(slopes, q_p, k_p, v_p)
    mp = lambda j: mod_p[:, :, j * D:(j + 1) * D]
    cnt0 = jnp.zeros((N_EXPERTS, LANES), F32)
    x1_p, h2_p, meta_p, ew_p, cnt_p = _outproj(
        ypool_p.reshape(n_p, POOL_WIDTH), yattn_p.reshape(n_p, ATTN_WIDTH), x_prompt.reshape(n_p, D),
        mp(2), mp(3), mp(4), g2, wout, wr_both, wr_hi, br, cnt0, T)

    ms = lambda j: mod_s[:, j * D:(j + 1) * D]
    hist = jnp.concatenate([jnp.zeros((DEC_BATCH, HALO - POOL_HIST, POOL_WIDTH), F32), state_pool[layer]], axis=1)
    xs = x_sample.reshape(n_s, D)
    ypool_s, q_s, k_s, v_s, tail_s = _inproj_sample(xs, ms(0), ms(1), g1, win, bd, gq, gk, hist, wpool, ps)
    r3 = lambda a: a.reshape(DEC_BATCH, DEC_SEQ, ATTN_WIDTH)
    n_phys = cache_k.shape[1]
    ck = jnp.transpose(cache_k[layer], (0, 2, 3, 1)).reshape(n_phys, ATTN_WIDTH, PAGE_SIZE)
    cv = jnp.transpose(cache_v[layer], (0, 2, 3, 1)).reshape(n_phys, ATTN_WIDTH, PAGE_SIZE)
    yattn_s = _moba_sample(page_table, r3(q_s), r3(k_s), r3(v_s), slope_col, lq_col, ck, cv)
    x1_s, h2_s, meta_s, ew_s, cnt_s = _outproj(ypool_s, yattn_s.reshape(n_s, ATTN_WIDTH), xs, ms(2), ms(3), ms(4),
                                               g2, wout, wr_both, wr_hi, br, cnt_p, None)

    n_blocks = (n_p + n_s) * EXPERT_TOPK // MOE_ROWS + N_EXPERTS
    pstart, blk_e, n_active = _expert_layout(cnt_s[:, 0].astype(I32), n_blocks)
    xr = jnp.zeros((n_blocks * MOE_ROWS, D), F32)
    dests_p = _dest_rows(pstart, meta_p)
    dests_s = _dest_rows(pstart, meta_s)
    xr = _dispatch_rows(dests_p, h2_p, xr)
    xr = _dispatch_rows(dests_s, h2_s, xr)
    yr = _moe(blk_e, n_active, xr, w_gate[layer], w_up[layer], w_down[layer])
    y_p = _final(dests_p, x1_p, ew_p[0:EXPERT_TOPK].T, mp(5), yr, T)
    y_s = _final(dests_s, x1_s, ew_s[0:EXPERT_TOPK].T, ms(5), yr, None)

    k4 = lambda a, b, l: a.reshape(1, b, l, N_HEADS, HEAD_DIM)
    return (y_p.reshape(B, T, D), y_s.reshape(DEC_BATCH, DEC_SEQ, D),
            k4(k_p, B, T), k4(v_p, B, T), tail_p[None, :, HALO - POOL_HIST:, :],
            k4(k_s, DEC_BATCH, DEC_SEQ), k4(v_s, DEC_BATCH, DEC_SEQ), tail_s[None, :, HALO - POOL_HIST:, :])
```

```python
import functools

import jax
import jax.numpy as jnp
from jax import lax
from jax.experimental import pallas as pl
from jax.experimental.pallas import tpu as pltpu

F32 = jnp.float32
BF16 = jnp.bfloat16
I32 = jnp.int32

D_MODEL = 1024
BATCH = 8
SEQ = 2048
DEC_BATCH = 32
DEC_SEQ = 8
PAST_LEN = 16384
PAGE_SIZE = 128
POOL_WIDTH = 512
POOL_WINDOWS = (2, 4, 8, 16)
POOL_GROUP = 128
POOL_HIST = 15
HALO = 16
N_HEADS = 8
HEAD_DIM = 64
ATTN_WIDTH = 512
MOBA_BLOCK = 256
MOBA_TOPK = 3
ATTN_SCALE = HEAD_DIM ** -0.5
MIX_IN = POOL_WIDTH + 3 * ATTN_WIDTH
N_GROUPS = 4
EXPERTS_PER_GROUP = 8
N_EXPERTS = 32
EXPERT_TOPK = 2
D_FF = 512
EPS = 1e-6
NEG = -1e30

LANES = 128
ROW_TILE = 512
MOE_ROWS = 256
PAGES_PER_STEP = 16
VMEM_LIMIT = 56 * 1024 * 1024

_NT = (((1,), (1,)), ((), ()))


def _dot(a, b):
    return jnp.dot(a, b, preferred_element_type=F32)


def _dot_nt(a, b):
    return lax.dot_general(a, b, _NT, preferred_element_type=F32)


def _split_dot(a, b01):
    hi = a.astype(BF16)
    lo = (a - hi.astype(F32)).astype(BF16)
    return _dot(hi, b01) + _dot(lo, b01)


def _rms_mod(x, g, sc, sh):
    ms = jnp.mean(x * x, axis=-1, keepdims=True)
    return x * lax.rsqrt(ms + EPS) * g * (1.0 + sc) + sh


def _ada_kernel(c_ref, w_ref, b_ref, o_ref):
    c = c_ref[...]
    a = c / (1.0 + jnp.exp(-c))
    o_ref[...] = _dot(a.astype(BF16), w_ref[...].astype(BF16)) + b_ref[...]


def _ada(c_all, w_ada, b_ada):
    n = c_all.shape[0]
    tn = 1536
    return pl.pallas_call(
        _ada_kernel,
        grid=(6 * D_MODEL // tn,),
        in_specs=[pl.BlockSpec((n, D_MODEL), lambda j: (0, 0)),
                  pl.BlockSpec((D_MODEL, tn), lambda j: (0, j)),
                  pl.BlockSpec((1, tn), lambda j: (0, j))],
        out_specs=pl.BlockSpec((n, tn), lambda j: (0, j)),
        out_shape=jax.ShapeDtypeStruct((n, 6 * D_MODEL), F32),
        compiler_params=pltpu.CompilerParams(vmem_limit_bytes=VMEM_LIMIT),
        name="ada",
    )(c_all, w_ada, b_ada.reshape(1, -1))


def _inproj_core(x, sh1, sc1, g, win_ref, bd_ref, gq_ref, gk_ref):
    h = _rms_mod(x, g, sc1, sh1)
    z = _dot(h.astype(BF16), win_ref[...])
    u = z[:, 0:POOL_WIDTH]
    q = z[:, POOL_WIDTH:POOL_WIDTH + ATTN_WIDTH]
    k = z[:, POOL_WIDTH + ATTN_WIDTH:POOL_WIDTH + 2 * ATTN_WIDTH]
    v = z[:, POOL_WIDTH + 2 * ATTN_WIDTH:]
    bd = bd_ref[...]
    q = q * lax.rsqrt(_split_dot(q * q, bd) * (1.0 / HEAD_DIM) + EPS) * gq_ref[...]
    k = k * lax.rsqrt(_split_dot(k * k, bd) * (1.0 / HEAD_DIM) + EPS) * gk_ref[...]
    return u, q, k, v


def _window_sum(e, w):
    s = e
    sh = 1
    while sh < w:
        s = s + pltpu.roll(s, sh, axis=0)
        sh *= 2
    return s


def _inproj_prompt_kernel(x_ref, mod_ref, g_ref, win_ref, bd_ref, gq_ref, gk_ref, wpool_ref, ps_ref,
                          ypool_ref, q_ref, k_ref, v_ref, tail_ref, ext_ref):
    t = pl.program_id(1)
    nt = pl.num_programs(1)
    x = x_ref[0]
    sh1 = mod_ref[0, :, 0:D_MODEL]
    sc1 = mod_ref[0, :, D_MODEL:2 * D_MODEL]
    u, q, k, v = _inproj_core(x, sh1, sc1, g_ref[...], win_ref, bd_ref, gq_ref, gk_ref)
    q_ref[0] = (q * ATTN_SCALE).astype(BF16)
    k_ref[0] = k
    v_ref[0] = v

    @pl.when(t == 0)
    def _():
        ext_ref[0:HALO, :] = jnp.zeros((HALO, POOL_WIDTH), F32)

    ext_ref[HALO:, :] = u
    pos = t * ROW_TILE + lax.broadcasted_iota(I32, (ROW_TILE, 1), 0)
    for gi, w in enumerate(POOL_WINDOWS):
        cols = slice(gi * POOL_GROUP, (gi + 1) * POOL_GROUP)
        win = _window_sum(ext_ref[:, cols], w)[HALO:]
        inv_cnt = 1.0 / jnp.minimum(pos + 1, w).astype(F32)
        pooled = win * inv_cnt - u[:, cols]
        y = _dot(pooled.astype(BF16), wpool_ref[gi]) * ps_ref[:, cols]
        ypool_ref[0, :, cols] = y.astype(BF16)
    last = u[ROW_TILE - HALO:, :]
    ext_ref[0:HALO, :] = last

    @pl.when(t == nt - 1)
    def _():
        tail_ref[0] = last


def _inproj_prompt(x, mod_p, g1, win, bd, gq, gk, wpool, ps):
    B, T, D = x.shape
    nt = T // ROW_TILE
    const2 = lambda b, t: (0, 0)
    tok = lambda b, t: (b, t, 0)
    return pl.pallas_call(
        _inproj_prompt_kernel,
        grid=(B, nt),
        in_specs=[pl.BlockSpec((1, ROW_TILE, D), tok),
                  pl.BlockSpec((1, 1, 6 * D), lambda b, t: (b, 0, 0)),
                  pl.BlockSpec((1, D), const2),
                  pl.BlockSpec((D, MIX_IN), const2),
                  pl.BlockSpec((ATTN_WIDTH, ATTN_WIDTH), const2),
                  pl.BlockSpec((1, ATTN_WIDTH), const2),
                  pl.BlockSpec((1, ATTN_WIDTH), const2),
                  pl.BlockSpec((4, POOL_GROUP, POOL_GROUP), lambda b, t: (0, 0, 0)),
                  pl.BlockSpec((1, POOL_WIDTH), const2)],
        out_specs=[pl.BlockSpec((1, ROW_TILE, POOL_WIDTH), tok),
                   pl.BlockSpec((1, ROW_TILE, ATTN_WIDTH), tok),
                   pl.BlockSpec((1, ROW_TILE, ATTN_WIDTH), tok),
                   pl.BlockSpec((1, ROW_TILE, ATTN_WIDTH), tok),
                   pl.BlockSpec((1, HALO, POOL_WIDTH), lambda b, t: (b, 0, 0))],
        out_shape=[jax.ShapeDtypeStruct((B, T, POOL_WIDTH), BF16),
                   jax.ShapeDtypeStruct((B, T, ATTN_WIDTH), BF16),
                   jax.ShapeDtypeStruct((B, T, ATTN_WIDTH), F32),
                   jax.ShapeDtypeStruct((B, T, ATTN_WIDTH), F32),
                   jax.ShapeDtypeStruct((B, HALO, POOL_WIDTH), F32)],
        scratch_shapes=[pltpu.VMEM((HALO + ROW_TILE, POOL_WIDTH), F32)],
        compiler_params=pltpu.CompilerParams(
            dimension_semantics=("arbitrary", "arbitrary"), vmem_limit_bytes=VMEM_LIMIT),
        name="inproj_prompt",
    )(x, mod_p, g1, win, bd, gq, gk, wpool, ps)


def _inproj_sample_kernel(x_ref, sh_ref, sc_ref, g_ref, win_ref, bd_ref, gq_ref, gk_ref, hist_ref, wpool_ref,
                          ps_ref, ypool_ref, q_ref, k_ref, v_ref, tail_ref, ext_ref):
    n = DEC_BATCH * DEC_SEQ
    ext_rows = HALO + DEC_SEQ
    u, q, k, v = _inproj_core(x_ref[...], sh_ref[...], sc_ref[...], g_ref[...], win_ref, bd_ref, gq_ref, gk_ref)
    q_ref[...] = q * ATTN_SCALE
    k_ref[...] = k
    v_ref[...] = v
    ext_ref[:, 0:HALO, :] = hist_ref[...]
    ext_ref[:, HALO:, :] = u.reshape(DEC_BATCH, DEC_SEQ, POOL_WIDTH)
    tail_ref[...] = ext_ref[:, ext_rows - HALO:, :]
    pos = PAST_LEN + lax.broadcasted_iota(I32, (DEC_BATCH, DEC_SEQ, 1), 1).reshape(n, 1)
    for gi, w in enumerate(POOL_WINDOWS):
        cols = slice(gi * POOL_GROUP, (gi + 1) * POOL_GROUP)
        e = ext_ref[:, :, cols].reshape(DEC_BATCH * ext_rows, POOL_GROUP)
        win = _window_sum(e, w).reshape(DEC_BATCH, ext_rows, POOL_GROUP)[:, HALO:, :].reshape(n, POOL_GROUP)
        inv_cnt = 1.0 / jnp.minimum(pos + 1, w).astype(F32)
        pooled = win * inv_cnt - u[:, cols]
        y = _dot(pooled.astype(BF16), wpool_ref[gi]) * ps_ref[:, cols]
        ypool_ref[:, cols] = y.astype(BF16)


def _inproj_sample(x, sh1, sc1, g1, win, bd, gq, gk, hist, wpool, ps):
    n = x.shape[0]
    return pl.pallas_call(
        _inproj_sample_kernel,
        out_shape=[jax.ShapeDtypeStruct((n, POOL_WIDTH), BF16),
                   jax.ShapeDtypeStruct((n, ATTN_WIDTH), F32),
                   jax.ShapeDtypeStruct((n, ATTN_WIDTH), F32),
                   jax.ShapeDtypeStruct((n, ATTN_WIDTH), F32),
                   jax.ShapeDtypeStruct((DEC_BATCH, HALO, POOL_WIDTH), F32)],
        scratch_shapes=[pltpu.VMEM((DEC_BATCH, HALO + DEC_SEQ, POOL_WIDTH), F32)],
        compiler_params=pltpu.CompilerParams(vmem_limit_bytes=VMEM_LIMIT),
        name="inproj_sample",
    )(x, sh1, sc1, g1, win, bd, gq, gk, hist, wpool, ps)


N_BIAS_LANES = 3


def _moba_prompt_kernel(slopes_ref, q_ref, k_ref, v_ref, o_ref, ka_ref, va_ref, qa_ref, kmf_ref):
    S = MOBA_BLOCK
    T = k_ref.shape[1]
    nb = T // S
    hp = pl.program_id(1)
    lane = lax.broadcasted_iota(I32, (1, LANES), 1)
    real = (lane < HEAD_DIM, lane >= HEAD_DIM)
    extra = (lane - HEAD_DIM, lane)

    kf = k_ref[0]
    vf = v_ref[0]
    kmf_ref[...] = jnp.zeros((LANES, LANES), F32)
    for j in range(nb):
        mean = jnp.sum(kf[j * S:(j + 1) * S], axis=0, keepdims=True) * (1.0 / S)
        kmf_ref[j:j + 1, :] = jnp.where(real[0], mean, 0.0)
        kmf_ref[nb + j:nb + j + 1, :] = jnp.where(real[0], 0.0, mean)
    key_i = lax.broadcasted_iota(I32, (T, 1), 0)
    key_blk = key_i // S
    for h in range(2):
        b = slopes_ref[2 * hp + h] * key_i.astype(F32)
        p0 = b.astype(BF16).astype(F32)
        p1 = (b - p0).astype(BF16).astype(F32)
        p2 = b - p0 - p1
        e = extra[h]
        onehot = jnp.where((e >= N_BIAS_LANES) & (e - N_BIAS_LANES == key_blk), 1.0, 0.0)
        feat = jnp.where(e == 0, p0, jnp.where(e == 1, p1, jnp.where(e == 2, p2, onehot)))
        ka_ref[h] = jnp.where(real[h], kf, feat).astype(BF16)
        va_ref[h] = jnp.where(real[h], vf, 1.0).astype(BF16)

    q2 = q_ref[0]

    gt = _dot_nt(kmf_ref[...].astype(BF16), q2)[0:2 * nb]
    row = lax.broadcasted_iota(I32, (2 * nb, T), 0)
    blk = row % nb
    cbq = lax.broadcasted_iota(I32, (2 * nb, T), 1) // S
    cnt = jnp.zeros((2 * nb, T), I32)
    for m in range(nb):
        gm = jnp.where(row < nb, gt[m:m + 1, :], gt[nb + m:nb + m + 1, :])
        beats = (gm > gt) | ((gm == gt) & (m < blk))
        cnt = cnt + jnp.where(beats & (m < cbq), 1, 0)
    keep = (((cnt < MOBA_TOPK) & (blk < cbq)) | (blk == cbq)).astype(F32)
    keepq = jnp.concatenate([keep, jnp.zeros((LANES - 2 * nb, T), F32)], axis=0).T
    maskv = jnp.where(keepq > 0.5, 0.0, NEG)
    mask_lanes = (pltpu.roll(maskv, HEAD_DIM + N_BIAS_LANES, axis=1),
                  pltpu.roll(maskv, (N_BIAS_LANES - nb) % LANES, axis=1))
    qf = q2.astype(F32)
    for h in range(2):
        e = extra[h]
        feat = jnp.where(e < N_BIAS_LANES, 1.0, jnp.where(e < N_BIAS_LANES + nb, mask_lanes[h], 0.0))
        qa_ref[h] = jnp.where(real[h], qf, feat).astype(BF16)

    causal = lax.broadcasted_iota(I32, (S, S), 1) <= lax.broadcasted_iota(I32, (S, S), 0)
    for cb in range(nb):
        rows = slice(cb * S, (cb + 1) * S)
        n = (cb + 1) * S
        outs = []
        for h in range(2):
            s = _dot_nt(qa_ref[h, rows, :], ka_ref[h, 0:n, :])
            s_own = jnp.where(causal, s[:, cb * S:n], NEG)
            s = s_own if cb == 0 else jnp.concatenate([s[:, 0:cb * S], s_own], axis=1)
            p = jnp.exp(s - jnp.max(s, axis=-1, keepdims=True))
            outs.append(_dot(p.astype(BF16), va_ref[h, 0:n, :]))
        a0, a1 = outs
        o0 = a0 * (1.0 / a0[:, HEAD_DIM:HEAD_DIM + 1])
        o1 = a1 * (1.0 / a1[:, 0:1])
        o_ref[0, rows, :] = jnp.where(real[0], o0, o1).astype(BF16)


def _moba_prompt(slopes, q, k, v):
    B, T, _ = q.shape
    seq = lambda b, hp: (b, 0, hp)
    return pl.pallas_call(
        _moba_prompt_kernel,
        grid=(B, N_HEADS // 2),
        in_specs=[pl.BlockSpec(memory_space=pltpu.SMEM),
                  pl.BlockSpec((1, T, LANES), seq),
                  pl.BlockSpec((1, T, LANES), seq),
                  pl.BlockSpec((1, T, LANES), seq)],
        out_specs=pl.BlockSpec((1, T, LANES), seq),
        out_shape=jax.ShapeDtypeStruct((B, T, ATTN_WIDTH), BF16),
        scratch_shapes=[pltpu.VMEM((2, T, LANES), BF16),
                        pltpu.VMEM((2, T, LANES), BF16),
                        pltpu.VMEM((2, T, LANES), BF16),
                        pltpu.VMEM((LANES, LANES), F32)],
        compiler_params=pltpu.CompilerParams(
            dimension_semantics=("arbitrary", "arbitrary"), vmem_limit_bytes=VMEM_LIMIT),
        name="moba_prompt",
    )(slopes, q, k, v)


CHUNK_SLOTS = 4


def _moba_sample_kernel(pt_ref, q_ref, kn_ref, vn_ref, slope_ref, lq_ref, ck_ref, cv_ref, o_ref,
                        buf_ref, sem, s_ref, p3_ref, kc_ref, idx_v, idx_s, idx_sem, vbuf_ref, vsem, vb16_ref, ph_ref):
    P = PAGES_PER_STEP
    S = MOBA_BLOCK
    n_blocks = PAST_LEN // S
    chunk = P * PAGE_SIZE
    n_chunks = PAST_LEN // chunk
    bpc = chunk // S
    ppb = S // PAGE_SIZE
    n_loads = n_chunks
    ahead = CHUNK_SLOTS - 1
    assert n_loads % CHUNK_SLOTS == 0
    nq = N_HEADS * DEC_SEQ
    ktop = min(MOBA_TOPK, n_blocks)
    b = pl.program_id(0)
    slope = slope_ref[...]
    lq = lq_ref[...]
    lane = lax.broadcasted_iota(I32, (1, LANES), 1)

    def start_load(bb, i):
        slot = i % CHUNK_SLOTS
        for r in range(P):
            page = pt_ref[bb, i * P + r]
            pltpu.make_async_copy(ck_ref.at[page], buf_ref.at[slot, r], sem.at[slot]).start()

    def wait_load(i):
        slot = i % CHUNK_SLOTS
        pltpu.make_async_copy(ck_ref.at[pl.ds(0, P)], buf_ref.at[slot], sem.at[slot]).wait()

    def load_chunk(i):
        nxt = i + ahead
        if nxt < n_loads:
            start_load(b, nxt)
        else:
            @pl.when(b + 1 < pl.num_programs(0))
            def _():
                start_load(b + 1, nxt - n_loads)
        wait_load(i)
        slot = i % CHUNK_SLOTS
        for r in range(P):
            kc_ref[:, r * PAGE_SIZE:(r + 1) * PAGE_SIZE] = buf_ref[slot, r].astype(BF16)
        return kc_ref[...]

    @pl.when(b == 0)
    def _():
        for i in range(ahead):
            start_load(b, i)
        ph_ref[...] = jnp.zeros_like(ph_ref)
        vb16_ref[...] = jnp.zeros_like(vb16_ref)

    q8 = q_ref[0]
    head = lax.broadcasted_iota(I32, (DEC_SEQ, ATTN_WIDTH), 1) // HEAD_DIM
    qp = jnp.concatenate([jnp.where(head == h, q8, 0.0) for h in range(N_HEADS)], axis=0).astype(BF16)

    gate = jnp.zeros((nq, LANES), F32)
    for c in range(n_chunks):
        sc = _dot(qp, load_chunk(c))
        for r2 in range(bpc):
            gs = jnp.sum(sc[:, r2 * S:(r2 + 1) * S], axis=-1, keepdims=True) * (1.0 / S)
            gate = jnp.where(lane == c * bpc + r2, gs, gate)
        keypos = c * chunk + lax.broadcasted_iota(I32, (1, chunk), 1)
        dist = (keypos - PAST_LEN).astype(F32) - lq
        s_ref[c] = sc + slope * dist

    gate_t = jnp.concatenate([gate, jnp.zeros((LANES - nq, LANES), F32)], axis=0).T[0:n_blocks]
    blk_i = lax.broadcasted_iota(I32, (n_blocks, LANES), 0)
    cnt = jnp.zeros((n_blocks, LANES), I32)
    for m in range(n_blocks):
        gm = gate_t[m:m + 1, :]
        beats = (gm > gate_t) | ((gm == gate_t) & (m < blk_i))
        cnt = cnt + jnp.where(beats, 1, 0)
    sel_t = cnt < ktop
    row8 = lax.broadcasted_iota(I32, (8, LANES), 0)
    idx_tile = jnp.zeros((8, LANES), I32)
    left = sel_t
    for j in range(ktop):
        pick = jnp.min(jnp.where(left, blk_i, n_blocks), axis=0, keepdims=True)
        idx_tile = jnp.where(row8 == j, pick, idx_tile)
        left = left & (blk_i != pick)
    idx_v[...] = idx_tile
    idx_copy = pltpu.make_async_copy(idx_v, idx_s, idx_sem)
    idx_copy.start()
    selq = jnp.concatenate([sel_t.astype(F32), jnp.zeros((LANES - n_blocks, LANES), F32)], axis=0).T[0:nq]
    keeps = [selq[:, n:n + 1] > 0.5 for n in range(n_blocks)]

    kn = jnp.concatenate([kn_ref[0], jnp.zeros((LANES - DEC_SEQ, ATTN_WIDTH), F32)], axis=0)
    lane_f = lane.astype(F32)
    sn = _dot_nt(qp, kn.astype(BF16)) + slope * (lane_f - lq)
    sn = jnp.where(lane_f <= lq, sn, NEG)

    mxv = sn
    for n in range(n_blocks):
        c, r2 = divmod(n, bpc)
        blk = jnp.where(keeps[n], s_ref[c, :, r2 * S:(r2 + 1) * S], NEG)
        for i in range(S // LANES):
            mxv = jnp.maximum(mxv, blk[:, i * LANES:(i + 1) * LANES])
    mx = jnp.max(mxv, axis=-1, keepdims=True)
    pn = jnp.exp(sn - mx)
    lsv = pn
    for n in range(n_blocks):
        c, r2 = divmod(n, bpc)
        p = jnp.exp(jnp.where(keeps[n], s_ref[c, :, r2 * S:(r2 + 1) * S] - mx, NEG))
        p3_ref[n] = p
        for i in range(S // LANES):
            lsv = lsv + p[:, i * LANES:(i + 1) * LANES]
    linv = 1.0 / jnp.sum(lsv, axis=-1, keepdims=True)

    idx_copy.wait()
    span = ktop * S
    for h in range(N_HEADS):
        for l in range(DEC_SEQ):
            q_i = h * DEC_SEQ + l
            for j in range(ktop):
                n = idx_s[j, q_i]
                for half in range(ppb):
                    page = pt_ref[b, n * ppb + half]
                    col = l * span + j * S + half * PAGE_SIZE
                    pltpu.make_async_copy(cv_ref.at[page, pl.ds(h * HEAD_DIM, HEAD_DIM), :],
                                          vbuf_ref.at[h, :, pl.ds(col, PAGE_SIZE)], vsem).start()
                ph_ref[h, l:l + 1, l * span + j * S:l * span + (j + 1) * S] = (
                    p3_ref[n, q_i:q_i + 1, :] * linv[q_i:q_i + 1, :])

    vn = jnp.concatenate([vn_ref[0], jnp.zeros((LANES - DEC_SEQ, ATTN_WIDTH), F32)], axis=0)
    o_new = _dot(pn.astype(BF16), vn.astype(BF16)) * linv
    res = jnp.zeros((DEC_SEQ, ATTN_WIDTH), F32)
    for h in range(N_HEADS):
        res = res + jnp.where(head == h, o_new[h * DEC_SEQ:(h + 1) * DEC_SEQ, :], 0.0)

    pltpu.make_async_copy(vbuf_ref, vbuf_ref, vsem).wait()
    pieces = []
    for h in range(N_HEADS):
        vb16_ref[0:HEAD_DIM, :] = vbuf_ref[h].astype(BF16)
        pieces.append(_dot_nt(ph_ref[h].astype(BF16), vb16_ref[...])[0:DEC_SEQ])
    pairs = [pieces[2 * i] + pltpu.roll(pieces[2 * i + 1], HEAD_DIM, axis=1) for i in range(N_HEADS // 2)]
    o_ref[0] = (res + jnp.concatenate(pairs, axis=1)).astype(BF16)


def _moba_sample(page_table, q, kn, vn, slope_col, lq_col, cache_kt, cache_vt):
    P = PAGES_PER_STEP
    chunk = P * PAGE_SIZE
    n_chunks = PAST_LEN // chunk
    n_blocks = PAST_LEN // MOBA_BLOCK
    nq = N_HEADS * DEC_SEQ
    fetched = DEC_SEQ * min(MOBA_TOPK, n_blocks) * MOBA_BLOCK
    row3 = lambda b, pt: (b, 0, 0)
    const2 = lambda b, pt: (0, 0)
    grid_spec = pltpu.PrefetchScalarGridSpec(
        num_scalar_prefetch=1,
        grid=(DEC_BATCH,),
        in_specs=[pl.BlockSpec((1, DEC_SEQ, ATTN_WIDTH), row3),
                  pl.BlockSpec((1, DEC_SEQ, ATTN_WIDTH), row3),
                  pl.BlockSpec((1, DEC_SEQ, ATTN_WIDTH), row3),
                  pl.BlockSpec((nq, 1), const2),
                  pl.BlockSpec((nq, 1), const2),
                  pl.BlockSpec(memory_space=pl.ANY),
                  pl.BlockSpec(memory_space=pl.ANY)],
        out_specs=pl.BlockSpec((1, DEC_SEQ, ATTN_WIDTH), row3),
        scratch_shapes=[pltpu.VMEM((CHUNK_SLOTS, P, ATTN_WIDTH, PAGE_SIZE), F32),
                        pltpu.SemaphoreType.DMA((CHUNK_SLOTS,)),
                        pltpu.VMEM((n_chunks, nq, chunk), F32),
                        pltpu.VMEM((n_blocks, nq, MOBA_BLOCK), F32),
                        pltpu.VMEM((ATTN_WIDTH, chunk), BF16),
                        pltpu.VMEM((8, LANES), I32),
                        pltpu.SMEM((8, LANES), I32),
                        pltpu.SemaphoreType.DMA(()),
                        pltpu.VMEM((N_HEADS, HEAD_DIM, fetched), F32),
                        pltpu.SemaphoreType.DMA(()),
                        pltpu.VMEM((LANES, fetched), BF16),
                        pltpu.VMEM((N_HEADS, 2 * DEC_SEQ, fetched), F32)])
    return pl.pallas_call(
        _moba_sample_kernel,
        grid_spec=grid_spec,
        out_shape=jax.ShapeDtypeStruct((DEC_BATCH, DEC_SEQ, ATTN_WIDTH), BF16),
        compiler_params=pltpu.CompilerParams(dimension_semantics=("arbitrary",), vmem_limit_bytes=VMEM_LIMIT),
        name="moba_sample",
    )(page_table, q, kn, vn, slope_col, lq_col, cache_kt, cache_vt)


def _mod_rows(ref):
    return ref[...].reshape(-1, ref.shape[-1])


def _outproj_kernel(yp_ref, ya_ref, x_ref, gt1_ref, sh2_ref, sc2_ref, g2_ref, wout_ref, wr_ref, wrhi_ref, br_ref,
                    tri_ref, cntin_ref, x1_ref, h2_ref, meta_ref, ew_ref, cntout_ref, cnt_ref):
    rows = x_ref.shape[0]

    @pl.when(pl.program_id(0) == 0)
    def _():
        cnt_ref[...] = cntin_ref[...]

    mix = _dot(yp_ref[...], wout_ref[0:POOL_WIDTH, :]) + _dot(ya_ref[...], wout_ref[POOL_WIDTH:, :])
    x1 = x_ref[...] + _mod_rows(gt1_ref) * mix
    x1_ref[...] = x1
    h2 = _rms_mod(x1, g2_ref[...], _mod_rows(sc2_ref), _mod_rows(sh2_ref))
    h2_ref[...] = h2
    hh = h2.astype(BF16)
    hl = (h2 - hh.astype(F32)).astype(BF16)
    both = _dot(hh, wr_ref[...])
    lt = (both[:, 0:LANES] + both[:, LANES:] + _dot(hl, wrhi_ref[...])).T + br_ref[...]
    row8 = lax.broadcasted_iota(I32, (8, rows), 0)
    g8 = lt[0:8]
    gmax = jnp.max(g8, axis=0, keepdims=True)
    gsum = jnp.sum(jnp.exp(g8 - gmax), axis=0, keepdims=True)
    g_w = 1.0 / gsum
    g_idx = jnp.min(jnp.where(g8 == gmax, row8, 8), axis=0, keepdims=True)
    e_in = jnp.zeros((8, rows), F32)
    for g in range(N_GROUPS):
        e_in = e_in + jnp.where(g_idx == g, lt[8 + 8 * g:16 + 8 * g], 0.0)
    m1 = jnp.max(e_in, axis=0, keepdims=True)
    i1 = jnp.min(jnp.where(e_in == m1, row8, 8), axis=0, keepdims=True)
    e_rest = jnp.where(row8 == i1, NEG, e_in)
    m2 = jnp.max(e_rest, axis=0, keepdims=True)
    i2 = jnp.min(jnp.where(e_rest == m2, row8, 8), axis=0, keepdims=True)
    r = jnp.exp(m2 - m1)
    w1 = g_w / (1.0 + r)
    w2 = g_w * r / (1.0 + r)
    e1 = g_idx * EXPERTS_PER_GROUP + i1
    e2 = g_idx * EXPERTS_PER_GROUP + i2
    rowe = lax.broadcasted_iota(I32, (N_EXPERTS, rows), 0)
    oh1 = (rowe == e1).astype(F32)
    oh2 = (rowe == e2).astype(F32)
    both_oh = oh1 + oh2
    before = _dot(both_oh.astype(BF16), tri_ref[...]) + cnt_ref[:, 0:1]
    rank1 = jnp.sum(oh1 * before, axis=0, keepdims=True).astype(I32)
    rank2 = jnp.sum(oh2 * before, axis=0, keepdims=True).astype(I32)
    cnt_ref[...] = cnt_ref[...] + jnp.sum(both_oh, axis=-1, keepdims=True)
    cntout_ref[...] = cnt_ref[...]
    meta_ref[...] = jnp.where(row8 == 0, e1, jnp.where(row8 == 1, e2, jnp.where(row8 == 2, rank1,
                                                                                 jnp.where(row8 == 3, rank2, 0))))
    ew_ref[...] = jnp.where(row8 == 0, w1, jnp.where(row8 == 1, w2, 0.0))


def _mod_spec(tile, per_batch_rows):
    if per_batch_rows is None:
        return pl.BlockSpec((tile, D_MODEL), lambda i: (i, 0))
    per = per_batch_rows // tile
    return pl.BlockSpec((1, 1, D_MODEL), lambda i: (i // per, 0, 0))


def _outproj(yp, ya, x, gt1, sh2, sc2, g2, wout, wr, wrhi, br, cnt_in, per_batch_rows):
    n = x.shape[0]
    tile = min(ROW_TILE, n)
    tok = lambda i: (i, 0)
    const2 = lambda i: (0, 0)
    mod_spec = _mod_spec(tile, per_batch_rows)
    tri = (jnp.arange(tile)[:, None] < jnp.arange(tile)[None, :]).astype(BF16)
    return pl.pallas_call(
        _outproj_kernel,
        grid=(n // tile,),
        in_specs=[pl.BlockSpec((tile, POOL_WIDTH), tok),
                  pl.BlockSpec((tile, ATTN_WIDTH), tok),
                  pl.BlockSpec((tile, D_MODEL), tok),
                  mod_spec, mod_spec, mod_spec,
                  pl.BlockSpec((1, D_MODEL), const2),
                  pl.BlockSpec((2 * POOL_WIDTH, D_MODEL), const2),
                  pl.BlockSpec((D_MODEL, 2 * LANES), const2),
                  pl.BlockSpec((D_MODEL, LANES), const2),
                  pl.BlockSpec((LANES, 1), const2),
                  pl.BlockSpec((tile, tile), const2),
                  pl.BlockSpec((N_EXPERTS, LANES), const2)],
        out_specs=[pl.BlockSpec((tile, D_MODEL), tok),
                   pl.BlockSpec((tile, D_MODEL), tok),
                   pl.BlockSpec((8, tile), lambda i: (0, i)),
                   pl.BlockSpec((8, tile), lambda i: (0, i)),
                   pl.BlockSpec((N_EXPERTS, LANES), const2)],
        out_shape=[jax.ShapeDtypeStruct((n, D_MODEL), F32),
                   jax.ShapeDtypeStruct((n, D_MODEL), F32),
                   jax.ShapeDtypeStruct((8, n), I32),
                   jax.ShapeDtypeStruct((8, n), F32),
                   jax.ShapeDtypeStruct((N_EXPERTS, LANES), F32)],
        scratch_shapes=[pltpu.VMEM((N_EXPERTS, LANES), F32)],
        compiler_params=pltpu.CompilerParams(dimension_semantics=("arbitrary",), vmem_limit_bytes=VMEM_LIMIT),
        name="outproj",
    )(yp, ya, x, gt1, sh2, sc2, g2, wout, wr, wrhi, br, tri, cnt_in)


def _dest_kernel(pstart_ref, meta_ref, dest_ref):
    n = meta_ref.shape[1]
    eid = meta_ref[0:EXPERT_TOPK, :]
    start = jnp.zeros((EXPERT_TOPK, n), I32)
    for e in range(N_EXPERTS):
        start = jnp.where(eid == e, pstart_ref[e], start)
    dest_ref[...] = start + meta_ref[EXPERT_TOPK:2 * EXPERT_TOPK, :]


def _dest_rows(pstart, meta):
    n = meta.shape[1]
    dest = pl.pallas_call(
        _dest_kernel,
        in_specs=[pl.BlockSpec(memory_space=pltpu.SMEM), pl.BlockSpec(memory_space=pltpu.VMEM)],
        out_specs=pl.BlockSpec(memory_space=pltpu.VMEM),
        out_shape=jax.ShapeDtypeStruct((EXPERT_TOPK, n), I32),
        name="dest",
    )(pstart, meta)
    return [dest[slot] for slot in range(EXPERT_TOPK)]


DMA_ISSUE_UNROLL = 8


def _dispatch_kernel(*refs):
    dest_refs = refs[0:EXPERT_TOPK]
    h_ref, xr_in_ref, xr_ref, sem = refs[EXPERT_TOPK:]
    rows = h_ref.shape[0]

    def issue(t, carry):
        for d_ref in dest_refs:
            pltpu.make_async_copy(h_ref.at[pl.ds(t, 1), :], xr_ref.at[pl.ds(d_ref[t], 1), :], sem).start()
        return carry

    lax.fori_loop(0, rows, issue, 0, unroll=DMA_ISSUE_UNROLL)
    for _ in dest_refs:
        pltpu.make_async_copy(h_ref, xr_ref.at[pl.ds(0, rows), :], sem).wait()


def _dispatch_rows(dests, h2, xr):
    n = h2.shape[0]
    tile = min(ROW_TILE, n)
    smem_rows = pl.BlockSpec((tile,), lambda i: (i,), memory_space=pltpu.SMEM)
    return pl.pallas_call(
        _dispatch_kernel,
        grid=(n // tile,),
        in_specs=[smem_rows] * EXPERT_TOPK + [pl.BlockSpec((tile, D_MODEL), lambda i: (i, 0)),
                                             pl.BlockSpec(memory_space=pl.ANY)],
        out_specs=pl.BlockSpec(memory_space=pl.ANY),
        out_shape=jax.ShapeDtypeStruct(xr.shape, xr.dtype),
        scratch_shapes=[pltpu.SemaphoreType.DMA(())],
        input_output_aliases={EXPERT_TOPK + 1: 0},
        compiler_params=pltpu.CompilerParams(dimension_semantics=("arbitrary",), vmem_limit_bytes=VMEM_LIMIT),
        name="dispatch",
    )(*dests, h2, xr)


def _moe_kernel(blk_e_ref, nact_ref, x_ref, wg_ref, wu_ref, wd_ref, y_ref, wgb_ref, wub_ref, wdb_ref):
    i = pl.program_id(0)
    prev = blk_e_ref[jnp.maximum(i - 1, 0)]
    active = i < nact_ref[0]

    @pl.when(active & ((i == 0) | (blk_e_ref[i] != prev)))
    def _():
        wgb_ref[...] = wg_ref[0].astype(BF16)
        wub_ref[...] = wu_ref[0].astype(BF16)
        wdb_ref[...] = wd_ref[0].astype(BF16)

    @pl.when(active)
    def _():
        x = x_ref[...].astype(BF16)
        g = _dot(x, wgb_ref[...])
        u = _dot(x, wub_ref[...])
        a = g / (1.0 + jnp.exp(-g)) * u
        y_ref[...] = _dot(a.astype(BF16), wdb_ref[...])

    @pl.when(jnp.logical_not(active))
    def _():
        y_ref[...] = jnp.zeros_like(y_ref)


def _moe(blk_e, n_active, xr, w_gate, w_up, w_down):
    n_rows = xr.shape[0]
    n_blocks = n_rows // MOE_ROWS
    grid_spec = pltpu.PrefetchScalarGridSpec(
        num_scalar_prefetch=2,
        grid=(n_blocks,),
        in_specs=[pl.BlockSpec((MOE_ROWS, D_MODEL), lambda i, be, na: (i, 0)),
                  pl.BlockSpec((1, D_MODEL, D_FF), lambda i, be, na: (be[i], 0, 0)),
                  pl.BlockSpec((1, D_MODEL, D_FF), lambda i, be, na: (be[i], 0, 0)),
                  pl.BlockSpec((1, D_FF, D_MODEL), lambda i, be, na: (be[i], 0, 0))],
        out_specs=pl.BlockSpec((MOE_ROWS, D_MODEL), lambda i, be, na: (i, 0)),
        scratch_shapes=[pltpu.VMEM((D_MODEL, D_FF), BF16),
                        pltpu.VMEM((D_MODEL, D_FF), BF16),
                        pltpu.VMEM((D_FF, D_MODEL), BF16)])
    return pl.pallas_call(
        _moe_kernel,
        grid_spec=grid_spec,
        out_shape=jax.ShapeDtypeStruct((n_rows, D_MODEL), F32),
        compiler_params=pltpu.CompilerParams(
            dimension_semantics=("arbitrary",), vmem_limit_bytes=VMEM_LIMIT),
        name="moe",
    )(blk_e, n_active, xr, w_gate, w_up, w_down)


def _final_kernel(*refs):
    dest_refs = refs[0:EXPERT_TOPK]
    x1_ref, w_ref, gt2_ref, yr_ref, y_ref, o_ref, sem = refs[EXPERT_TOPK:]
    rows = x1_ref.shape[0]

    def issue(t, carry):
        for slot, d_ref in enumerate(dest_refs):
            pltpu.make_async_copy(yr_ref.at[pl.ds(d_ref[t], 1), :], o_ref.at[slot, pl.ds(t, 1), :], sem).start()
        return carry

    lax.fori_loop(0, rows, issue, 0, unroll=DMA_ISSUE_UNROLL)
    for slot in range(EXPERT_TOPK):
        pltpu.make_async_copy(yr_ref.at[pl.ds(0, rows), :], o_ref.at[slot], sem).wait()
    w = w_ref[...]
    moe = o_ref[0] * w[:, 0:1] + o_ref[1] * w[:, 1:2]
    y_ref[...] = x1_ref[...] + _mod_rows(gt2_ref) * moe


def _final(dests, x1, w2, gt2, yr, per_batch_rows):
    n = x1.shape[0]
    tile = min(ROW_TILE, n)
    tok = lambda i: (i, 0)
    smem_rows = pl.BlockSpec((tile,), lambda i: (i,), memory_space=pltpu.SMEM)
    return pl.pallas_call(
        _final_kernel,
        grid=(n // tile,),
        in_specs=[smem_rows] * EXPERT_TOPK + [pl.BlockSpec((tile, D_MODEL), tok),
                                             pl.BlockSpec((tile, EXPERT_TOPK), tok),
                                             _mod_spec(tile, per_batch_rows),
                                             pl.BlockSpec(memory_space=pl.ANY)],
        out_specs=pl.BlockSpec((tile, D_MODEL), tok),
        out_shape=jax.ShapeDtypeStruct((n, D_MODEL), F32),
        scratch_shapes=[pltpu.VMEM((EXPERT_TOPK, tile, D_MODEL), F32),
                        pltpu.SemaphoreType.DMA(())],
        compiler_params=pltpu.CompilerParams(dimension_semantics=("arbitrary",), vmem_limit_bytes=VMEM_LIMIT),
        name="final",
    )(*dests, x1, w2, gt2, yr)


def _expert_layout(counts, n_blocks):
    padded = (counts + MOE_ROWS - 1) // MOE_ROWS * MOE_ROWS
    pend = jnp.cumsum(padded)
    pstart = (pend - padded).astype(I32)
    blk_start = jnp.arange(n_blocks, dtype=I32) * MOE_ROWS
    blk_e = jnp.minimum(jnp.sum((pend[None, :] <= blk_start[:, None]).astype(I32), axis=1), N_EXPERTS - 1)
    n_active = (pend[-1] // MOE_ROWS).astype(I32).reshape(1)
    return pstart, blk_e.astype(I32), n_active


def kernel(x_prompt, x_sample, cache_k, cache_v, state_pool, page_table, c_prompt, c_sample, w_ada, b_ada,
           g_attn_norm, w_in, g_q, g_k, w_pool, pool_scale, w_out, g_ffn_norm, w_group, b_group, w_expert,
           b_expert, w_gate, w_up, w_down):
    D = D_MODEL
    B, T, _ = x_prompt.shape
    n_s = DEC_BATCH * DEC_SEQ
    n_p = B * T
    layer = 0

    win = w_in[layer].astype(BF16)
    wout = w_out[layer].astype(BF16)
    wpool = w_pool[layer].astype(BF16)
    g1 = g_attn_norm[layer].reshape(1, D)
    g2 = g_ffn_norm[layer].reshape(1, D)
    gq = jnp.tile(g_q[layer], N_HEADS).reshape(1, ATTN_WIDTH)
    gk = jnp.tile(g_k[layer], N_HEADS).reshape(1, ATTN_WIDTH)
    ps = pool_scale[layer].reshape(1, POOL_WIDTH)
    hd = jnp.arange(ATTN_WIDTH) // HEAD_DIM
    bd = (hd[:, None] == hd[None, :]).astype(BF16)
    slopes = jnp.exp2(-8.0 * (jnp.arange(N_HEADS, dtype=F32) + 1.0) / N_HEADS)
    qrow = jnp.arange(N_HEADS * DEC_SEQ)
    slope_col = slopes[qrow // DEC_SEQ].reshape(-1, 1)
    lq_col = (qrow % DEC_SEQ).astype(F32).reshape(-1, 1)
    wr = jnp.zeros((D, LANES), F32).at[:, 0:N_GROUPS].set(w_group[layer]).at[:, 8:8 + N_EXPERTS].set(w_expert[layer])
    wr_hi = wr.astype(BF16)
    wr_lo = (wr - wr_hi.astype(F32)).astype(BF16)
    wr_both = jnp.concatenate([wr_hi, wr_lo], axis=1)
    br = jnp.zeros((LANES,), F32).at[0:N_GROUPS].set(b_group[layer]).at[N_GROUPS:8].set(NEG)
    br = br.at[8:8 + N_EXPERTS].set(b_expert[layer]).reshape(LANES, 1)

    mod = _ada(jnp.concatenate([c_prompt, c_sample], axis=0), w_ada[layer], b_ada[layer])
    mod_p = mod[:B].reshape(B, 1, 6 * D)
    mod_s = jnp.repeat(mod[B:], DEC_SEQ, axis=0)

    ypool_p, q_p, k_p, v_p, tail_p = _inproj_prompt(x_prompt, mod_p, g1, win, bd, gq, gk, wpool, ps)
    yattn_p = _moba_prompt(slopes, q_p, k_p, v_p)
    mp = lambda j: mod_p[:, :, j * D:(j + 1) * D]
    cnt0 = jnp.zeros((N_EXPERTS, LANES), F32)
    x1_p, h2_p, meta_p, ew_p, cnt_p = _outproj(
        ypool_p.reshape(n_p, POOL_WIDTH), yattn_p.reshape(n_p, ATTN_WIDTH), x_prompt.reshape(n_p, D),
        mp(2), mp(3), mp(4), g2, wout, wr_both, wr_hi, br, cnt0, T)

    ms = lambda j: mod_s[:, j * D:(j + 1) * D]
    hist = jnp.concatenate([jnp.zeros((DEC_BATCH, HALO - POOL_HIST, POOL_WIDTH), F32), state_pool[layer]], axis=1)
    xs = x_sample.reshape(n_s, D)
    ypool_s, q_s, k_s, v_s, tail_s = _inproj_sample(xs, ms(0), ms(1), g1, win, bd, gq, gk, hist, wpool, ps)
    r3 = lambda a: a.reshape(DEC_BATCH, DEC_SEQ, ATTN_WIDTH)
    n_phys = cache_k.shape[1]
    ck = jnp.transpose(cache_k[layer], (0, 2, 3, 1)).reshape(n_phys, ATTN_WIDTH, PAGE_SIZE)
    cv = jnp.transpose(cache_v[layer], (0, 2, 3, 1)).reshape(n_phys, ATTN_WIDTH, PAGE_SIZE)
    yattn_s = _moba_sample(page_table, r3(q_s), r3(k_s), r3(v_s), slope_col, lq_col, ck, cv)
    x1_s, h2_s, meta_s, ew_s, cnt_s = _outproj(ypool_s, yattn_s.reshape(n_s, ATTN_WIDTH), xs, ms(2), ms(3), ms(4),
                                               g2, wout, wr_both, wr_hi, br, cnt_p, None)

    n_blocks = (n_p + n_s) * EXPERT_TOPK // MOE_ROWS + N_EXPERTS
    pstart, blk_e, n_active = _expert_layout(cnt_s[:, 0].astype(I32), n_blocks)
    xr = jnp.zeros((n_blocks * MOE_ROWS, D), F32)
    dests_p = _dest_rows(pstart, meta_p)
    dests_s = _dest_rows(pstart, meta_s)
    xr = _dispatch_rows(dests_p, h2_p, xr)
    xr = _dispatch_rows(dests_s, h2_s, xr)
    yr = _moe(blk_e, n_active, xr, w_gate[layer], w_up[layer], w_down[layer])
    y_p = _final(dests_p, x1_p, ew_p[0:EXPERT_TOPK].T, mp(5), yr, T)
    y_s = _final(dests_s, x1_s, ew_s[0:EXPERT_TOPK].T, ms(5), yr, None)

    k4 = lambda a, b, l: a.reshape(1, b, l, N_HEADS, HEAD_DIM)
    return (y_p.reshape(B, T, D), y_s.reshape(DEC_BATCH, DEC_SEQ, D),
            k4(k_p, B, T), k4(v_p, B, T), tail_p[None, :, HALO - POOL_HIST:, :],
            k4(k_s, DEC_BATCH, DEC_SEQ), k4(v_s, DEC_BATCH, DEC_SEQ), tail_s[None, :, HALO - POOL_HIST:, :])
```

```python
import functools

import jax
import jax.numpy as jnp
from jax import lax
from jax.experimental import pallas as pl
from jax.experimental.pallas import tpu as pltpu

F32 = jnp.float32
BF16 = jnp.bfloat16
I32 = jnp.int32

D_MODEL = 1024
BATCH = 8
SEQ = 2048
DEC_BATCH = 32
DEC_SEQ = 8
PAST_LEN = 16384
PAGE_SIZE = 128
POOL_WIDTH = 512
POOL_WINDOWS = (2, 4, 8, 16)
POOL_GROUP = 128
POOL_HIST = 15
HALO = 16
N_HEADS = 8
HEAD_DIM = 64
ATTN_WIDTH = 512
MOBA_BLOCK = 256
MOBA_TOPK = 3
ATTN_SCALE = HEAD_DIM ** -0.5
MIX_IN = POOL_WIDTH + 3 * ATTN_WIDTH
N_GROUPS = 4
EXPERTS_PER_GROUP = 8
N_EXPERTS = 32
EXPERT_TOPK = 2
D_FF = 512
EPS = 1e-6
NEG = -1e30

LANES = 128
ROW_TILE = 512
MOE_ROWS = 256
PAGES_PER_STEP = 16
VMEM_LIMIT = 56 * 1024 * 1024

_NT = (((1,), (1,)), ((), ()))


def _dot(a, b):
    return jnp.dot(a, b, preferred_element_type=F32)


def _dot_nt(a, b):
    return lax.dot_general(a, b, _NT, preferred_element_type=F32)


def _split_dot(a, b01):
    hi = a.astype(BF16)
    lo = (a - hi.astype(F32)).astype(BF16)
    return _dot(hi, b01) + _dot(lo, b01)


def _rms_mod(x, g, sc, sh):
    ms = jnp.mean(x * x, axis=-1, keepdims=True)
    return x * lax.rsqrt(ms + EPS) * g * (1.0 + sc) + sh


def _ada_kernel(c_ref, w_ref, b_ref, o_ref):
    c = c_ref[...]
    a = c / (1.0 + jnp.exp(-c))
    o_ref[...] = _dot(a.astype(BF16), w_ref[...].astype(BF16)) + b_ref[...]


def _ada(c_all, w_ada, b_ada):
    n = c_all.shape[0]
    tn = 1536
    return pl.pallas_call(
        _ada_kernel,
        grid=(6 * D_MODEL // tn,),
        in_specs=[pl.BlockSpec((n, D_MODEL), lambda j: (0, 0)),
                  pl.BlockSpec((D_MODEL, tn), lambda j: (0, j)),
                  pl.BlockSpec((1, tn), lambda j: (0, j))],
        out_specs=pl.BlockSpec((n, tn), lambda j: (0, j)),
        out_shape=jax.ShapeDtypeStruct((n, 6 * D_MODEL), F32),
        compiler_params=pltpu.CompilerParams(vmem_limit_bytes=VMEM_LIMIT),
        name="ada",
    )(c_all, w_ada, b_ada.reshape(1, -1))


def _inproj_core(x, sh1, sc1, g, win_ref, bd_ref, gq_ref, gk_ref):
    h = _rms_mod(x, g, sc1, sh1)
    z = _dot(h.astype(BF16), win_ref[...])
    u = z[:, 0:POOL_WIDTH]
    q = z[:, POOL_WIDTH:POOL_WIDTH + ATTN_WIDTH]
    k = z[:, POOL_WIDTH + ATTN_WIDTH:POOL_WIDTH + 2 * ATTN_WIDTH]
    v = z[:, POOL_WIDTH + 2 * ATTN_WIDTH:]
    bd = bd_ref[...]
    q = q * lax.rsqrt(_split_dot(q * q, bd) * (1.0 / HEAD_DIM) + EPS) * gq_ref[...]
    k = k * lax.rsqrt(_split_dot(k * k, bd) * (1.0 / HEAD_DIM) + EPS) * gk_ref[...]
    return u, q, k, v


def _window_sum(e, w):
    s = e
    sh = 1
    while sh < w:
        s = s + pltpu.roll(s, sh, axis=0)
        sh *= 2
    return s


def _inproj_prompt_kernel(x_ref, mod_ref, g_ref, win_ref, bd_ref, gq_ref, gk_ref, wpool_ref, ps_ref,
                          ypool_ref, q_ref, k_ref, v_ref, tail_ref, ext_ref):
    t = pl.program_id(1)
    nt = pl.num_programs(1)
    x = x_ref[0]
    sh1 = mod_ref[0, :, 0:D_MODEL]
    sc1 = mod_ref[0, :, D_MODEL:2 * D_MODEL]
    u, q, k, v = _inproj_core(x, sh1, sc1, g_ref[...], win_ref, bd_ref, gq_ref, gk_ref)
    q_ref[0] = (q * ATTN_SCALE).astype(BF16)
    k_ref[0] = k
    v_ref[0] = v

    @pl.when(t == 0)
    def _():
        ext_ref[0:HALO, :] = jnp.zeros((HALO, POOL_WIDTH), F32)

    ext_ref[HALO:, :] = u
    pos = t * ROW_TILE + lax.broadcasted_iota(I32, (ROW_TILE, 1), 0)
    for gi, w in enumerate(POOL_WINDOWS):
        cols = slice(gi * POOL_GROUP, (gi + 1) * POOL_GROUP)
        win = _window_sum(ext_ref[:, cols], w)[HALO:]
        inv_cnt = 1.0 / jnp.minimum(pos + 1, w).astype(F32)
        pooled = win * inv_cnt - u[:, cols]
        y = _dot(pooled.astype(BF16), wpool_ref[gi]) * ps_ref[:, cols]
        ypool_ref[0, :, cols] = y.astype(BF16)
    last = u[ROW_TILE - HALO:, :]
    ext_ref[0:HALO, :] = last

    @pl.when(t == nt - 1)
    def _():
        tail_ref[0] = last


def _inproj_prompt(x, mod_p, g1, win, bd, gq, gk, wpool, ps):
    B, T, D = x.shape
    nt = T // ROW_TILE
    const2 = lambda b, t: (0, 0)
    tok = lambda b, t: (b, t, 0)
    return pl.pallas_call(
        _inproj_prompt_kernel,
        grid=(B, nt),
        in_specs=[pl.BlockSpec((1, ROW_TILE, D), tok),
                  pl.BlockSpec((1, 1, 6 * D), lambda b, t: (b, 0, 0)),
                  pl.BlockSpec((1, D), const2),
                  pl.BlockSpec((D, MIX_IN), const2),
                  pl.BlockSpec((ATTN_WIDTH, ATTN_WIDTH), const2),
                  pl.BlockSpec((1, ATTN_WIDTH), const2),
                  pl.BlockSpec((1, ATTN_WIDTH), const2),
                  pl.BlockSpec((4, POOL_GROUP, POOL_GROUP), lambda b, t: (0, 0, 0)),
                  pl.BlockSpec((1, POOL_WIDTH), const2)],
        out_specs=[pl.BlockSpec((1, ROW_TILE, POOL_WIDTH), tok),
                   pl.BlockSpec((1, ROW_TILE, ATTN_WIDTH), tok),
                   pl.BlockSpec((1, ROW_TILE, ATTN_WIDTH), tok),
                   pl.BlockSpec((1, ROW_TILE, ATTN_WIDTH), tok),
                   pl.BlockSpec((1, HALO, POOL_WIDTH), lambda b, t: (b, 0, 0))],
        out_shape=[jax.ShapeDtypeStruct((B, T, POOL_WIDTH), BF16),
                   jax.ShapeDtypeStruct((B, T, ATTN_WIDTH), BF16),
                   jax.ShapeDtypeStruct((B, T, ATTN_WIDTH), F32),
                   jax.ShapeDtypeStruct((B, T, ATTN_WIDTH), F32),
                   jax.ShapeDtypeStruct((B, HALO, POOL_WIDTH), F32)],
        scratch_shapes=[pltpu.VMEM((HALO + ROW_TILE, POOL_WIDTH), F32)],
        compiler_params=pltpu.CompilerParams(
            dimension_semantics=("arbitrary", "arbitrary"), vmem_limit_bytes=VMEM_LIMIT),
        name="inproj_prompt",
    )(x, mod_p, g1, win, bd, gq, gk, wpool, ps)


def _inproj_sample_kernel(x_ref, sh_ref, sc_ref, g_ref, win_ref, bd_ref, gq_ref, gk_ref, hist_ref, wpool_ref,
                          ps_ref, ypool_ref, q_ref, k_ref, v_ref, tail_ref, ext_ref):
    n = DEC_BATCH * DEC_SEQ
    ext_rows = HALO + DEC_SEQ
    u, q, k, v = _inproj_core(x_ref[...], sh_ref[...], sc_ref[...], g_ref[...], win_ref, bd_ref, gq_ref, gk_ref)
    q_ref[...] = q * ATTN_SCALE
    k_ref[...] = k
    v_ref[...] = v
    ext_ref[:, 0:HALO, :] = hist_ref[...]
    ext_ref[:, HALO:, :] = u.reshape(DEC_BATCH, DEC_SEQ, POOL_WIDTH)
    tail_ref[...] = ext_ref[:, ext_rows - HALO:, :]
    pos = PAST_LEN + lax.broadcasted_iota(I32, (DEC_BATCH, DEC_SEQ, 1), 1).reshape(n, 1)
    for gi, w in enumerate(POOL_WINDOWS):
        cols = slice(gi * POOL_GROUP, (gi + 1) * POOL_GROUP)
        e = ext_ref[:, :, cols].reshape(DEC_BATCH * ext_rows, POOL_GROUP)
        win = _window_sum(e, w).reshape(DEC_BATCH, ext_rows, POOL_GROUP)[:, HALO:, :].reshape(n, POOL_GROUP)
        inv_cnt = 1.0 / jnp.minimum(pos + 1, w).astype(F32)
        pooled = win * inv_cnt - u[:, cols]
        y = _dot(pooled.astype(BF16), wpool_ref[gi]) * ps_ref[:, cols]
        ypool_ref[:, cols] = y.astype(BF16)


def _inproj_sample(x, sh1, sc1, g1, win, bd, gq, gk, hist, wpool, ps):
    n = x.shape[0]
    return pl.pallas_call(
        _inproj_sample_kernel,
        out_shape=[jax.ShapeDtypeStruct((n, POOL_WIDTH), BF16),
                   jax.ShapeDtypeStruct((n, ATTN_WIDTH), F32),
                   jax.ShapeDtypeStruct((n, ATTN_WIDTH), F32),
                   jax.ShapeDtypeStruct((n, ATTN_WIDTH), F32),
                   jax.ShapeDtypeStruct((DEC_BATCH, HALO, POOL_WIDTH), F32)],
        scratch_shapes=[pltpu.VMEM((DEC_BATCH, HALO + DEC_SEQ, POOL_WIDTH), F32)],
        compiler_params=pltpu.CompilerParams(vmem_limit_bytes=VMEM_LIMIT),
        name="inproj_sample",
    )(x, sh1, sc1, g1, win, bd, gq, gk, hist, wpool, ps)


N_BIAS_LANES = 3


def _moba_prompt_kernel(slopes_ref, q_ref, k_ref, v_ref, o_ref, ka_ref, va_ref, qa_ref, kmf_ref):
    S = MOBA_BLOCK
    T = k_ref.shape[1]
    nb = T // S
    hp = pl.program_id(1)
    lane = lax.broadcasted_iota(I32, (1, LANES), 1)
    real = (lane < HEAD_DIM, lane >= HEAD_DIM)
    extra = (lane - HEAD_DIM, lane)

    kf = k_ref[0]
    vf = v_ref[0]
    kmf_ref[...] = jnp.zeros((LANES, LANES), F32)
    for j in range(nb):
        mean = jnp.sum(kf[j * S:(j + 1) * S], axis=0, keepdims=True) * (1.0 / S)
        kmf_ref[j:j + 1, :] = jnp.where(real[0], mean, 0.0)
        kmf_ref[nb + j:nb + j + 1, :] = jnp.where(real[0], 0.0, mean)
    key_i = lax.broadcasted_iota(I32, (T, 1), 0)
    key_blk = key_i // S
    for h in range(2):
        b = slopes_ref[2 * hp + h] * key_i.astype(F32)
        p0 = b.astype(BF16).astype(F32)
        p1 = (b - p0).astype(BF16).astype(F32)
        p2 = b - p0 - p1
        e = extra[h]
        onehot = jnp.where((e >= N_BIAS_LANES) & (e - N_BIAS_LANES == key_blk), 1.0, 0.0)
        feat = jnp.where(e == 0, p0, jnp.where(e == 1, p1, jnp.where(e == 2, p2, onehot)))
        ka_ref[h] = jnp.where(real[h], kf, feat).astype(BF16)
        va_ref[h] = jnp.where(real[h], vf, 1.0).astype(BF16)

    q2 = q_ref[0]

    gt = _dot_nt(kmf_ref[...].astype(BF16), q2)[0:2 * nb]
    row = lax.broadcasted_iota(I32, (2 * nb, T), 0)
    blk = row % nb
    cbq = lax.broadcasted_iota(I32, (2 * nb, T), 1) // S
    cnt = jnp.zeros((2 * nb, T), I32)
    for m in range(nb):
        gm = jnp.where(row < nb, gt[m:m + 1, :], gt[nb + m:nb + m + 1, :])
        beats = (gm > gt) | ((gm == gt) & (m < blk))
        cnt = cnt + jnp.where(beats & (m < cbq), 1, 0)
    keep = (((cnt < MOBA_TOPK) & (blk < cbq)) | (blk == cbq)).astype(F32)
    keepq = jnp.concatenate([keep, jnp.zeros((LANES - 2 * nb, T), F32)], axis=0).T
    maskv = jnp.where(keepq > 0.5, 0.0, NEG)
    mask_lanes = (pltpu.roll(maskv, HEAD_DIM + N_BIAS_LANES, axis=1),
                  pltpu.roll(maskv, (N_BIAS_LANES - nb) % LANES, axis=1))
    qf = q2.astype(F32)
    for h in range(2):
        e = extra[h]
        feat = jnp.where(e < N_BIAS_LANES, 1.0, jnp.where(e < N_BIAS_LANES + nb, mask_lanes[h], 0.0))
        qa_ref[h] = jnp.where(real[h], qf, feat).astype(BF16)

    causal = lax.broadcasted_iota(I32, (S, S), 1) <= lax.broadcasted_iota(I32, (S, S), 0)
    for cb in range(nb):
        rows = slice(cb * S, (cb + 1) * S)
        n = (cb + 1) * S
        outs = []
        for h in range(2):
            s = _dot_nt(qa_ref[h, rows, :], ka_ref[h, 0:n, :])
            s_own = jnp.where(causal, s[:, cb * S:n], NEG)
            s = s_own if cb == 0 else jnp.concatenate([s[:, 0:cb * S], s_own], axis=1)
            p = jnp.exp(s - jnp.max(s, axis=-1, keepdims=True))
            outs.append(_dot(p.astype(BF16), va_ref[h, 0:n, :]))
        a0, a1 = outs
        o0 = a0 * (1.0 / a0[:, HEAD_DIM:HEAD_DIM + 1])
        o1 = a1 * (1.0 / a1[:, 0:1])
        o_ref[0, rows, :] = jnp.where(real[0], o0, o1).astype(BF16)


def _moba_prompt(slopes, q, k, v):
    B, T, _ = q.shape
    seq = lambda b, hp: (b, 0, hp)
    return pl.pallas_call(
        _moba_prompt_kernel,
        grid=(B, N_HEADS // 2),
        in_specs=[pl.BlockSpec(memory_space=pltpu.SMEM),
                  pl.BlockSpec((1, T, LANES), seq),
                  pl.BlockSpec((1, T, LANES), seq),
                  pl.BlockSpec((1, T, LANES), seq)],
        out_specs=pl.BlockSpec((1, T, LANES), seq),
        out_shape=jax.ShapeDtypeStruct((B, T, ATTN_WIDTH), BF16),
        scratch_shapes=[pltpu.VMEM((2, T, LANES), BF16),
                        pltpu.VMEM((2, T, LANES), BF16),
                        pltpu.VMEM((2, T, LANES), BF16),
                        pltpu.VMEM((LANES, LANES), F32)],
        compiler_params=pltpu.CompilerParams(
            dimension_semantics=("arbitrary", "arbitrary"), vmem_limit_bytes=VMEM_LIMIT),
        name="moba_prompt",
    )(slopes, q, k, v)


CHUNK_SLOTS = 4


def _moba_sample_kernel(pt_ref, q_ref, kn_ref, vn_ref, slope_ref, lq_ref, ck_ref, cv_ref, o_ref,
                        buf_ref, sem, s_ref, p3_ref, kc_ref, idx_v, idx_s, idx_sem, vbuf_ref, vsem, vb16_ref, ph_ref,
                        gate_ref, res_ref):
    P = PAGES_PER_STEP
    S = MOBA_BLOCK
    chunk = P * PAGE_SIZE
    n_chunks = PAST_LEN // chunk
    bpc = chunk // S
    n_loads = n_chunks
    ahead = CHUNK_SLOTS - 1
    assert n_loads % CHUNK_SLOTS == 0
    nq = N_HEADS * DEC_SEQ
    b = pl.program_id(0)
    n_rows = pl.num_programs(0) - 1
    slope = slope_ref[...]
    lq = lq_ref[...]
    lane = lax.broadcasted_iota(I32, (1, LANES), 1)

    def start_load(bb, i):
        slot = i % CHUNK_SLOTS
        for r in range(P):
            page = pt_ref[bb, i * P + r]
            pltpu.make_async_copy(ck_ref.at[page], buf_ref.at[slot, r], sem.at[slot]).start()

    def wait_load(i):
        slot = i % CHUNK_SLOTS
        pltpu.make_async_copy(ck_ref.at[pl.ds(0, P)], buf_ref.at[slot], sem.at[slot]).wait()

    def load_chunk(i):
        nxt = i + ahead
        if nxt < n_loads:
            start_load(b, nxt)
        else:
            @pl.when(b + 1 < n_rows)
            def _():
                start_load(b + 1, nxt - n_loads)
        wait_load(i)
        slot = i % CHUNK_SLOTS
        for r in range(P):
            kc_ref[:, r * PAGE_SIZE:(r + 1) * PAGE_SIZE] = buf_ref[slot, r].astype(BF16)
        return kc_ref[...]

    @pl.when(b == 0)
    def _():
        for i in range(ahead):
            start_load(b, i)
        ph_ref[...] = jnp.zeros_like(ph_ref)
        vb16_ref[...] = jnp.zeros_like(vb16_ref)

    head = lax.broadcasted_iota(I32, (DEC_SEQ, ATTN_WIDTH), 1) // HEAD_DIM

    def masked_queries():
        q8 = q_ref[0]
        return jnp.concatenate([jnp.where(head == h, q8, 0.0) for h in range(N_HEADS)], axis=0).astype(BF16)

    @pl.when(b < n_rows)
    def _score_keys():
        qp = masked_queries()
        gate = jnp.zeros((nq, LANES), F32)
        for c in range(n_chunks):
            sc = _dot(qp, load_chunk(c))
            for r2 in range(bpc):
                gs = jnp.sum(sc[:, r2 * S:(r2 + 1) * S], axis=-1, keepdims=True) * (1.0 / S)
                gate = jnp.where(lane == c * bpc + r2, gs, gate)
            keypos = c * chunk + lax.broadcasted_iota(I32, (1, chunk), 1)
            dist = (keypos - PAST_LEN).astype(F32) - lq
            s_ref[c] = sc + slope * dist
        gate_ref[...] = gate

    @pl.when(b > 0)
    def _finish_previous_row():
        pltpu.make_async_copy(vbuf_ref, vbuf_ref, vsem).wait()
        pieces = []
        for h in range(N_HEADS):
            vb16_ref[0:HEAD_DIM, :] = vbuf_ref[h].astype(BF16)
            pieces.append(_dot_nt(ph_ref[h].astype(BF16), vb16_ref[...])[0:DEC_SEQ])
        pairs = [pieces[2 * i] + pltpu.roll(pieces[2 * i + 1], HEAD_DIM, axis=1) for i in range(N_HEADS // 2)]
        o_ref[0] = (res_ref[...] + jnp.concatenate(pairs, axis=1)).astype(BF16)

    @pl.when(b < n_rows)
    def _pick_and_fetch():
        _pick_blocks_and_fetch_values(
            b, pt_ref, masked_queries(), gate_ref[...], kn_ref, vn_ref, slope, lq, cv_ref, s_ref, p3_ref,
            idx_v, idx_s, idx_sem, vbuf_ref, vsem, ph_ref, res_ref)


def _pick_blocks_and_fetch_values(b, pt_ref, qp, gate, kn_ref, vn_ref, slope, lq, cv_ref, s_ref, p3_ref,
                                  idx_v, idx_s, idx_sem, vbuf_ref, vsem, ph_ref, res_ref):
    S = MOBA_BLOCK
    n_blocks = PAST_LEN // S
    bpc = PAGES_PER_STEP * PAGE_SIZE // S
    ppb = S // PAGE_SIZE
    nq = N_HEADS * DEC_SEQ
    ktop = min(MOBA_TOPK, n_blocks)
    lane = lax.broadcasted_iota(I32, (1, LANES), 1)
    head = lax.broadcasted_iota(I32, (DEC_SEQ, ATTN_WIDTH), 1) // HEAD_DIM

    gate_t = jnp.concatenate([gate, jnp.zeros((LANES - nq, LANES), F32)], axis=0).T[0:n_blocks]
    blk_i = lax.broadcasted_iota(I32, (n_blocks, LANES), 0)
    cnt = jnp.zeros((n_blocks, LANES), I32)
    for m in range(n_blocks):
        gm = gate_t[m:m + 1, :]
        beats = (gm > gate_t) | ((gm == gate_t) & (m < blk_i))
        cnt = cnt + jnp.where(beats, 1, 0)
    sel_t = cnt < ktop
    row8 = lax.broadcasted_iota(I32, (8, LANES), 0)
    idx_tile = jnp.zeros((8, LANES), I32)
    left = sel_t
    for j in range(ktop):
        pick = jnp.min(jnp.where(left, blk_i, n_blocks), axis=0, keepdims=True)
        idx_tile = jnp.where(row8 == j, pick, idx_tile)
        left = left & (blk_i != pick)
    idx_v[...] = idx_tile
    idx_copy = pltpu.make_async_copy(idx_v, idx_s, idx_sem)
    idx_copy.start()
    selq = jnp.concatenate([sel_t.astype(F32), jnp.zeros((LANES - n_blocks, LANES), F32)], axis=0).T[0:nq]
    keeps = [selq[:, n:n + 1] > 0.5 for n in range(n_blocks)]

    kn = jnp.concatenate([kn_ref[0], jnp.zeros((LANES - DEC_SEQ, ATTN_WIDTH), F32)], axis=0)
    lane_f = lane.astype(F32)
    sn = _dot_nt(qp, kn.astype(BF16)) + slope * (lane_f - lq)
    sn = jnp.where(lane_f <= lq, sn, NEG)

    mxv = sn
    for n in range(n_blocks):
        c, r2 = divmod(n, bpc)
        blk = jnp.where(keeps[n], s_ref[c, :, r2 * S:(r2 + 1) * S], NEG)
        for i in range(S // LANES):
            mxv = jnp.maximum(mxv, blk[:, i * LANES:(i + 1) * LANES])
    mx = jnp.max(mxv, axis=-1, keepdims=True)
    pn = jnp.exp(sn - mx)
    lsv = pn
    for n in range(n_blocks):
        c, r2 = divmod(n, bpc)
        p = jnp.exp(jnp.where(keeps[n], s_ref[c, :, r2 * S:(r2 + 1) * S] - mx, NEG))
        p3_ref[n] = p
        for i in range(S // LANES):
            lsv = lsv + p[:, i * LANES:(i + 1) * LANES]
    linv = 1.0 / jnp.sum(lsv, axis=-1, keepdims=True)

    idx_copy.wait()
    span = ktop * S
    for h in range(N_HEADS):
        for l in range(DEC_SEQ):
            q_i = h * DEC_SEQ + l
            for j in range(ktop):
                n = idx_s[j, q_i]
                for half in range(ppb):
                    page = pt_ref[b, n * ppb + half]
                    col = l * span + j * S + half * PAGE_SIZE
                    pltpu.make_async_copy(cv_ref.at[page, pl.ds(h * HEAD_DIM, HEAD_DIM), :],
                                          vbuf_ref.at[h, :, pl.ds(col, PAGE_SIZE)], vsem).start()
                ph_ref[h, l:l + 1, l * span + j * S:l * span + (j + 1) * S] = (
                    p3_ref[n, q_i:q_i + 1, :] * linv[q_i:q_i + 1, :])

    vn = jnp.concatenate([vn_ref[0], jnp.zeros((LANES - DEC_SEQ, ATTN_WIDTH), F32)], axis=0)
    o_new = _dot(pn.astype(BF16), vn.astype(BF16)) * linv
    res = jnp.zeros((DEC_SEQ, ATTN_WIDTH), F32)
    for h in range(N_HEADS):
        res = res + jnp.where(head == h, o_new[h * DEC_SEQ:(h + 1) * DEC_SEQ, :], 0.0)
    res_ref[...] = res


def _moba_sample(page_table, q, kn, vn, slope_col, lq_col, cache_kt, cache_vt):
    P = PAGES_PER_STEP
    chunk = P * PAGE_SIZE
    n_chunks = PAST_LEN // chunk
    n_blocks = PAST_LEN // MOBA_BLOCK
    nq = N_HEADS * DEC_SEQ
    fetched = DEC_SEQ * min(MOBA_TOPK, n_blocks) * MOBA_BLOCK
    row3 = lambda b, pt: (jnp.minimum(b, DEC_BATCH - 1), 0, 0)
    prev3 = lambda b, pt: (jnp.maximum(b - 1, 0), 0, 0)
    const2 = lambda b, pt: (0, 0)
    grid_spec = pltpu.PrefetchScalarGridSpec(
        num_scalar_prefetch=1,
        grid=(DEC_BATCH + 1,),
        in_specs=[pl.BlockSpec((1, DEC_SEQ, ATTN_WIDTH), row3),
                  pl.BlockSpec((1, DEC_SEQ, ATTN_WIDTH), row3),
                  pl.BlockSpec((1, DEC_SEQ, ATTN_WIDTH), row3),
                  pl.BlockSpec((nq, 1), const2),
                  pl.BlockSpec((nq, 1), const2),
                  pl.BlockSpec(memory_space=pl.ANY),
                  pl.BlockSpec(memory_space=pl.ANY)],
        out_specs=pl.BlockSpec((1, DEC_SEQ, ATTN_WIDTH), prev3),
        scratch_shapes=[pltpu.VMEM((CHUNK_SLOTS, P, ATTN_WIDTH, PAGE_SIZE), F32),
                        pltpu.SemaphoreType.DMA((CHUNK_SLOTS,)),
                        pltpu.VMEM((n_chunks, nq, chunk), F32),
                        pltpu.VMEM((n_blocks, nq, MOBA_BLOCK), F32),
                        pltpu.VMEM((ATTN_WIDTH, chunk), BF16),
                        pltpu.VMEM((8, LANES), I32),
                        pltpu.SMEM((8, LANES), I32),
                        pltpu.SemaphoreType.DMA(()),
                        pltpu.VMEM((N_HEADS, HEAD_DIM, fetched), F32),
                        pltpu.SemaphoreType.DMA(()),
                        pltpu.VMEM((LANES, fetched), BF16),
                        pltpu.VMEM((N_HEADS, 2 * DEC_SEQ, fetched), F32),
                        pltpu.VMEM((nq, LANES), F32),
                        pltpu.VMEM((DEC_SEQ, ATTN_WIDTH), F32)])
    return pl.pallas_call(
        _moba_sample_kernel,
        grid_spec=grid_spec,
        out_shape=jax.ShapeDtypeStruct((DEC_BATCH, DEC_SEQ, ATTN_WIDTH), BF16),
        compiler_params=pltpu.CompilerParams(dimension_semantics=("arbitrary",), vmem_limit_bytes=VMEM_LIMIT),
        name="moba_sample",
    )(page_table, q, kn, vn, slope_col, lq_col, cache_kt, cache_vt)


def _mod_rows(ref):
    return ref[...].reshape(-1, ref.shape[-1])


def _outproj_kernel(yp_ref, ya_ref, x_ref, gt1_ref, sh2_ref, sc2_ref, g2_ref, wout_ref, wr_ref, wrhi_ref, br_ref,
                    tri_ref, cntin_ref, x1_ref, h2_ref, meta_ref, ew_ref, cntout_ref, cnt_ref):
    rows = x_ref.shape[0]

    @pl.when(pl.program_id(0) == 0)
    def _():
        cnt_ref[...] = cntin_ref[...]

    mix = _dot(yp_ref[...], wout_ref[0:POOL_WIDTH, :]) + _dot(ya_ref[...], wout_ref[POOL_WIDTH:, :])
    x1 = x_ref[...] + _mod_rows(gt1_ref) * mix
    x1_ref[...] = x1
    h2 = _rms_mod(x1, g2_ref[...], _mod_rows(sc2_ref), _mod_rows(sh2_ref))
    h2_ref[...] = h2
    hh = h2.astype(BF16)
    hl = (h2 - hh.astype(F32)).astype(BF16)
    both = _dot(hh, wr_ref[...])
    lt = (both[:, 0:LANES] + both[:, LANES:] + _dot(hl, wrhi_ref[...])).T + br_ref[...]
    row8 = lax.broadcasted_iota(I32, (8, rows), 0)
    g8 = lt[0:8]
    gmax = jnp.max(g8, axis=0, keepdims=True)
    gsum = jnp.sum(jnp.exp(g8 - gmax), axis=0, keepdims=True)
    g_w = 1.0 / gsum
    g_idx = jnp.min(jnp.where(g8 == gmax, row8, 8), axis=0, keepdims=True)
    e_in = jnp.zeros((8, rows), F32)
    for g in range(N_GROUPS):
        e_in = e_in + jnp.where(g_idx == g, lt[8 + 8 * g:16 + 8 * g], 0.0)
    m1 = jnp.max(e_in, axis=0, keepdims=True)
    i1 = jnp.min(jnp.where(e_in == m1, row8, 8), axis=0, keepdims=True)
    e_rest = jnp.where(row8 == i1, NEG, e_in)
    m2 = jnp.max(e_rest, axis=0, keepdims=True)
    i2 = jnp.min(jnp.where(e_rest == m2, row8, 8), axis=0, keepdims=True)
    r = jnp.exp(m2 - m1)
    w1 = g_w / (1.0 + r)
    w2 = g_w * r / (1.0 + r)
    e1 = g_idx * EXPERTS_PER_GROUP + i1
    e2 = g_idx * EXPERTS_PER_GROUP + i2
    rowe = lax.broadcasted_iota(I32, (N_EXPERTS, rows), 0)
    oh1 = (rowe == e1).astype(F32)
    oh2 = (rowe == e2).astype(F32)
    both_oh = oh1 + oh2
    before = _dot(both_oh.astype(BF16), tri_ref[...]) + cnt_ref[:, 0:1]
    rank1 = jnp.sum(oh1 * before, axis=0, keepdims=True).astype(I32)
    rank2 = jnp.sum(oh2 * before, axis=0, keepdims=True).astype(I32)
    cnt_ref[...] = cnt_ref[...] + jnp.sum(both_oh, axis=-1, keepdims=True)
    cntout_ref[...] = cnt_ref[...]
    meta_ref[...] = jnp.where(row8 == 0, e1, jnp.where(row8 == 1, e2, jnp.where(row8 == 2, rank1,
                                                                                 jnp.where(row8 == 3, rank2, 0))))
    ew_ref[...] = jnp.where(row8 == 0, w1, jnp.where(row8 == 1, w2, 0.0))


def _mod_spec(tile, per_batch_rows):
    if per_batch_rows is None:
        return pl.BlockSpec((tile, D_MODEL), lambda i: (i, 0))
    per = per_batch_rows // tile
    return pl.BlockSpec((1, 1, D_MODEL), lambda i: (i // per, 0, 0))


def _outproj(yp, ya, x, gt1, sh2, sc2, g2, wout, wr, wrhi, br, cnt_in, per_batch_rows):
    n = x.shape[0]
    tile = min(ROW_TILE, n)
    tok = lambda i: (i, 0)
    const2 = lambda i: (0, 0)
    mod_spec = _mod_spec(tile, per_batch_rows)
    tri = (jnp.arange(tile)[:, None] < jnp.arange(tile)[None, :]).astype(BF16)
    return pl.pallas_call(
        _outproj_kernel,
        grid=(n // tile,),
        in_specs=[pl.BlockSpec((tile, POOL_WIDTH), tok),
                  pl.BlockSpec((tile, ATTN_WIDTH), tok),
                  pl.BlockSpec((tile, D_MODEL), tok),
                  mod_spec, mod_spec, mod_spec,
                  pl.BlockSpec((1, D_MODEL), const2),
                  pl.BlockSpec((2 * POOL_WIDTH, D_MODEL), const2),
                  pl.BlockSpec((D_MODEL, 2 * LANES), const2),
                  pl.BlockSpec((D_MODEL, LANES), const2),
                  pl.BlockSpec((LANES, 1), const2),
                  pl.BlockSpec((tile, tile), const2),
                  pl.BlockSpec((N_EXPERTS, LANES), const2)],
        out_specs=[pl.BlockSpec((tile, D_MODEL), tok),
                   pl.BlockSpec((tile, D_MODEL), tok),
                   pl.BlockSpec((8, tile), lambda i: (0, i)),
                   pl.BlockSpec((8, tile), lambda i: (0, i)),
                   pl.BlockSpec((N_EXPERTS, LANES), const2)],
        out_shape=[jax.ShapeDtypeStruct((n, D_MODEL), F32),
                   jax.ShapeDtypeStruct((n, D_MODEL), F32),
                   jax.ShapeDtypeStruct((8, n), I32),
                   jax.ShapeDtypeStruct((8, n), F32),
                   jax.ShapeDtypeStruct((N_EXPERTS, LANES), F32)],
        scratch_shapes=[pltpu.VMEM((N_EXPERTS, LANES), F32)],
        compiler_params=pltpu.CompilerParams(dimension_semantics=("arbitrary",), vmem_limit_bytes=VMEM_LIMIT),
        name="outproj",
    )(yp, ya, x, gt1, sh2, sc2, g2, wout, wr, wrhi, br, tri, cnt_in)


def _dest_kernel(pstart_ref, meta_ref, dest_ref):
    n = meta_ref.shape[1]
    eid = meta_ref[0:EXPERT_TOPK, :]
    start = jnp.zeros((EXPERT_TOPK, n), I32)
    for e in range(N_EXPERTS):
        start = jnp.where(eid == e, pstart_ref[e], start)
    dest_ref[...] = start + meta_ref[EXPERT_TOPK:2 * EXPERT_TOPK, :]


def _dest_rows(pstart, meta):
    n = meta.shape[1]
    dest = pl.pallas_call(
        _dest_kernel,
        in_specs=[pl.BlockSpec(memory_space=pltpu.SMEM), pl.BlockSpec(memory_space=pltpu.VMEM)],
        out_specs=pl.BlockSpec(memory_space=pltpu.VMEM),
        out_shape=jax.ShapeDtypeStruct((EXPERT_TOPK, n), I32),
        name="dest",
    )(pstart, meta)
    return [dest[slot] for slot in range(EXPERT_TOPK)]


DMA_ISSUE_UNROLL = 8


def _dispatch_kernel(*refs):
    dest_refs = refs[0:EXPERT_TOPK]
    h_ref, xr_in_ref, xr_ref, sem = refs[EXPERT_TOPK:]
    rows = h_ref.shape[0]

    def issue(t, carry):
        for d_ref in dest_refs:
            pltpu.make_async_copy(h_ref.at[pl.ds(t, 1), :], xr_ref.at[pl.ds(d_ref[t], 1), :], sem).start()
        return carry

    lax.fori_loop(0, rows, issue, 0, unroll=DMA_ISSUE_UNROLL)
    for _ in dest_refs:
        pltpu.make_async_copy(h_ref, xr_ref.at[pl.ds(0, rows), :], sem).wait()


def _dispatch_rows(dests, h2, xr):
    n = h2.shape[0]
    tile = min(ROW_TILE, n)
    smem_rows = pl.BlockSpec((tile,), lambda i: (i,), memory_space=pltpu.SMEM)
    return pl.pallas_call(
        _dispatch_kernel,
        grid=(n // tile,),
        in_specs=[smem_rows] * EXPERT_TOPK + [pl.BlockSpec((tile, D_MODEL), lambda i: (i, 0)),
                                             pl.BlockSpec(memory_space=pl.ANY)],
        out_specs=pl.BlockSpec(memory_space=pl.ANY),
        out_shape=jax.ShapeDtypeStruct(xr.shape, xr.dtype),
        scratch_shapes=[pltpu.SemaphoreType.DMA(())],
        input_output_aliases={EXPERT_TOPK + 1: 0},
        compiler_params=pltpu.CompilerParams(dimension_semantics=("arbitrary",), vmem_limit_bytes=VMEM_LIMIT),
        name="dispatch",
    )(*dests, h2, xr)


def _moe_kernel(blk_e_ref, nact_ref, x_ref, wg_ref, wu_ref, wd_ref, y_ref, wgb_ref, wub_ref, wdb_ref):
    i = pl.program_id(0)
    prev = blk_e_ref[jnp.maximum(i - 1, 0)]
    active = i < nact_ref[0]

    @pl.when(active & ((i == 0) | (blk_e_ref[i] != prev)))
    def _():
        wgb_ref[...] = wg_ref[0].astype(BF16)
        wub_ref[...] = wu_ref[0].astype(BF16)
        wdb_ref[...] = wd_ref[0].astype(BF16)

    @pl.when(active)
    def _():
        x = x_ref[...].astype(BF16)
        g = _dot(x, wgb_ref[...])
        u = _dot(x, wub_ref[...])
        a = g / (1.0 + jnp.exp(-g)) * u
        y_ref[...] = _dot(a.astype(BF16), wdb_ref[...])

    @pl.when(jnp.logical_not(active))
    def _():
        y_ref[...] = jnp.zeros_like(y_ref)


def _moe(blk_e, n_active, xr, w_gate, w_up, w_down):
    n_rows = xr.shape[0]
    n_blocks = n_rows // MOE_ROWS
    grid_spec = pltpu.PrefetchScalarGridSpec(
        num_scalar_prefetch=2,
        grid=(n_blocks,),
        in_specs=[pl.BlockSpec((MOE_ROWS, D_MODEL), lambda i, be, na: (i, 0)),
                  pl.BlockSpec((1, D_MODEL, D_FF), lambda i, be, na: (be[i], 0, 0)),
                  pl.BlockSpec((1, D_MODEL, D_FF), lambda i, be, na: (be[i], 0, 0)),
                  pl.BlockSpec((1, D_FF, D_MODEL), lambda i, be, na: (be[i], 0, 0))],
        out_specs=pl.BlockSpec((MOE_ROWS, D_MODEL), lambda i, be, na: (i, 0)),
        scratch_shapes=[pltpu.VMEM((D_MODEL, D_FF), BF16),
                        pltpu.VMEM((D_MODEL, D_FF), BF16),
                        pltpu.VMEM((D_FF, D_MODEL), BF16)])
    return pl.pallas_call(
        _moe_kernel,
        grid_spec=grid_spec,
        out_shape=jax.ShapeDtypeStruct((n_rows, D_MODEL), F32),
        compiler_params=pltpu.CompilerParams(
            dimension_semantics=("arbitrary",), vmem_limit_bytes=VMEM_LIMIT),
        name="moe",
    )(blk_e, n_active, xr, w_gate, w_up, w_down)


def _final_kernel(*refs):
    dest_refs = refs[0:EXPERT_TOPK]
    x1_ref, w_ref, gt2_ref, yr_ref, y_ref, o_ref, sem = refs[EXPERT_TOPK:]
    rows = x1_ref.shape[0]

    def issue(t, carry):
        for slot, d_ref in enumerate(dest_refs):
            pltpu.make_async_copy(yr_ref.at[pl.ds(d_ref[t], 1), :], o_ref.at[slot, pl.ds(t, 1), :], sem).start()
        return carry

    lax.fori_loop(0, rows, issue, 0, unroll=DMA_ISSUE_UNROLL)
    for slot in range(EXPERT_TOPK):
        pltpu.make_async_copy(yr_ref.at[pl.ds(0, rows), :], o_ref.at[slot], sem).wait()
    w = w_ref[...]
    moe = o_ref[0] * w[:, 0:1] + o_ref[1] * w[:, 1:2]
    y_ref[...] = x1_ref[...] + _mod_rows(gt2_ref) * moe


def _final(dests, x1, w2, gt2, yr, per_batch_rows):
    n = x1.shape[0]
    tile = min(ROW_TILE, n)
    tok = lambda i: (i, 0)
    smem_rows = pl.BlockSpec((tile,), lambda i: (i,), memory_space=pltpu.SMEM)
    return pl.pallas_call(
        _final_kernel,
        grid=(n // tile,),
        in_specs=[smem_rows] * EXPERT_TOPK + [pl.BlockSpec((tile, D_MODEL), tok),
                                             pl.BlockSpec((tile, EXPERT_TOPK), tok),
                                             _mod_spec(tile, per_batch_rows),
                                             pl.BlockSpec(memory_space=pl.ANY)],
        out_specs=pl.BlockSpec((tile, D_MODEL), tok),
        out_shape=jax.ShapeDtypeStruct((n, D_MODEL), F32),
        scratch_shapes=[pltpu.VMEM((EXPERT_TOPK, tile, D_MODEL), F32),
                        pltpu.SemaphoreType.DMA(())],
        compiler_params=pltpu.CompilerParams(dimension_semantics=("arbitrary",), vmem_limit_bytes=VMEM_LIMIT),
        name="final",
    )(*dests, x1, w2, gt2, yr)


def _expert_layout(counts, n_blocks):
    padded = (counts + MOE_ROWS - 1) // MOE_ROWS * MOE_ROWS
    pend = jnp.cumsum(padded)
    pstart = (pend - padded).astype(I32)
    blk_start = jnp.arange(n_blocks, dtype=I32) * MOE_ROWS
    blk_e = jnp.minimum(jnp.sum((pend[None, :] <= blk_start[:, None]).astype(I32), axis=1), N_EXPERTS - 1)
    n_active = (pend[-1] // MOE_ROWS).astype(I32).reshape(1)
    return pstart, blk_e.astype(I32), n_active


def kernel(x_prompt, x_sample, cache_k, cache_v, state_pool, page_table, c_prompt, c_sample, w_ada, b_ada,
           g_attn_norm, w_in, g_q, g_k, w_pool, pool_scale, w_out, g_ffn_norm, w_group, b_group, w_expert,
           b_expert, w_gate, w_up, w_down):
    D = D_MODEL
    B, T, _ = x_prompt.shape
    n_s = DEC_BATCH * DEC_SEQ
    n_p = B * T
    layer = 0

    win = w_in[layer].astype(BF16)
    wout = w_out[layer].astype(BF16)
    wpool = w_pool[layer].astype(BF16)
    g1 = g_attn_norm[layer].reshape(1, D)
    g2 = g_ffn_norm[layer].reshape(1, D)
    gq = jnp.tile(g_q[layer], N_HEADS).reshape(1, ATTN_WIDTH)
    gk = jnp.tile(g_k[layer], N_HEADS).reshape(1, ATTN_WIDTH)
    ps = pool_scale[layer].reshape(1, POOL_WIDTH)
    hd = jnp.arange(ATTN_WIDTH) // HEAD_DIM
    bd = (hd[:, None] == hd[None, :]).astype(BF16)
    slopes = jnp.exp2(-8.0 * (jnp.arange(N_HEADS, dtype=F32) + 1.0) / N_HEADS)
    qrow = jnp.arange(N_HEADS * DEC_SEQ)
    slope_col = slopes[qrow // DEC_SEQ].reshape(-1, 1)
    lq_col = (qrow % DEC_SEQ).astype(F32).reshape(-1, 1)
    wr = jnp.zeros((D, LANES), F32).at[:, 0:N_GROUPS].set(w_group[layer]).at[:, 8:8 + N_EXPERTS].set(w_expert[layer])
    wr_hi = wr.astype(BF16)
    wr_lo = (wr - wr_hi.astype(F32)).astype(BF16)
    wr_both = jnp.concatenate([wr_hi, wr_lo], axis=1)
    br = jnp.zeros((LANES,), F32).at[0:N_GROUPS].set(b_group[layer]).at[N_GROUPS:8].set(NEG)
    br = br.at[8:8 + N_EXPERTS].set(b_expert[layer]).reshape(LANES, 1)

    mod = _ada(jnp.concatenate([c_prompt, c_sample], axis=0), w_ada[layer], b_ada[layer])
    mod_p = mod[:B].reshape(B, 1, 6 * D)
    mod_s = jnp.repeat(mod[B:], DEC_SEQ, axis=0)

    ypool_p, q_p, k_p, v_p, tail_p = _inproj_prompt(x_prompt, mod_p, g1, win, bd, gq, gk, wpool, ps)
    yattn_p = _moba_prompt(slopes, q_p, k_p, v_p)
    mp = lambda j: mod_p[:, :, j * D:(j + 1) * D]
    cnt0 = jnp.zeros((N_EXPERTS, LANES), F32)
    x1_p, h2_p, meta_p, ew_p, cnt_p = _outproj(
        ypool_p.reshape(n_p, POOL_WIDTH), yattn_p.reshape(n_p, ATTN_WIDTH), x_prompt.reshape(n_p, D),
        mp(2), mp(3), mp(4), g2, wout, wr_both, wr_hi, br, cnt0, T)

    ms = lambda j: mod_s[:, j * D:(j + 1) * D]
    hist = jnp.concatenate([jnp.zeros((DEC_BATCH, HALO - POOL_HIST, POOL_WIDTH), F32), state_pool[layer]], axis=1)
    xs = x_sample.reshape(n_s, D)
    ypool_s, q_s, k_s, v_s, tail_s = _inproj_sample(xs, ms(0), ms(1), g1, win, bd, gq, gk, hist, wpool, ps)
    r3 = lambda a: a.reshape(DEC_BATCH, DEC_SEQ, ATTN_WIDTH)
    n_phys = cache_k.shape[1]
    ck = jnp.transpose(cache_k[layer], (0, 2, 3, 1)).reshape(n_phys, ATTN_WIDTH, PAGE_SIZE)
    cv = jnp.transpose(cache_v[layer], (0, 2, 3, 1)).reshape(n_phys, ATTN_WIDTH, PAGE_SIZE)
    yattn_s = _moba_sample(page_table, r3(q_s), r3(k_s), r3(v_s), slope_col, lq_col, ck, cv)
    x1_s, h2_s, meta_s, ew_s, cnt_s = _outproj(ypool_s, yattn_s.reshape(n_s, ATTN_WIDTH), xs, ms(2), ms(3), ms(4),
                                               g2, wout, wr_both, wr_hi, br, cnt_p, None)

    n_blocks = (n_p + n_s) * EXPERT_TOPK // MOE_ROWS + N_EXPERTS
    pstart, blk_e, n_active = _expert_layout(cnt_s[:, 0].astype(I32), n_blocks)
    xr = jnp.zeros((n_blocks * MOE_ROWS, D), F32)
    dests_p = _dest_rows(pstart, meta_p)
    dests_s = _dest_rows(pstart, meta_s)
    xr = _dispatch_rows(dests_p, h2_p, xr)
    xr = _dispatch_rows(dests_s, h2_s, xr)
    yr = _moe(blk_e, n_active, xr, w_gate[layer], w_up[layer], w_down[layer])
    y_p = _final(dests_p, x1_p, ew_p[0:EXPERT_TOPK].T, mp(5), yr, T)
    y_s = _final(dests_s, x1_s, ew_s[0:EXPERT_TOPK].T, ms(5), yr, None)

    k4 = lambda a, b, l: a.reshape(1, b, l, N_HEADS, HEAD_DIM)
    return (y_p.reshape(B, T, D), y_s.reshape(DEC_BATCH, DEC_SEQ, D),
            k4(k_p, B, T), k4(v_p, B, T), tail_p[None, :, HALO - POOL_HIST:, :],
            k4(k_s, DEC_BATCH, DEC_SEQ), k4(v_s, DEC_BATCH, DEC_SEQ), tail_s[None, :, HALO - POOL_HIST:, :])
```

```python
import functools

import jax
import jax.numpy as jnp
from jax import lax
from jax.experimental import pallas as pl
from jax.experimental.pallas import tpu as pltpu

F32 = jnp.float32
BF16 = jnp.bfloat16
I32 = jnp.int32

D_MODEL = 1024
BATCH = 8
SEQ = 2048
DEC_BATCH = 32
DEC_SEQ = 8
PAST_LEN = 16384
PAGE_SIZE = 128
POOL_WIDTH = 512
POOL_WINDOWS = (2, 4, 8, 16)
POOL_GROUP = 128
POOL_HIST = 15
HALO = 16
N_HEADS = 8
HEAD_DIM = 64
ATTN_WIDTH = 512
MOBA_BLOCK = 256
MOBA_TOPK = 3
ATTN_SCALE = HEAD_DIM ** -0.5
MIX_IN = POOL_WIDTH + 3 * ATTN_WIDTH
N_GROUPS = 4
EXPERTS_PER_GROUP = 8
N_EXPERTS = 32
EXPERT_TOPK = 2
D_FF = 512
EPS = 1e-6
NEG = -1e30

LANES = 128
ROW_TILE = 512
MOE_ROWS = 256
PAGES_PER_STEP = 16
VMEM_LIMIT = 56 * 1024 * 1024

_NT = (((1,), (1,)), ((), ()))


def _dot(a, b):
    return jnp.dot(a, b, preferred_element_type=F32)


def _dot_nt(a, b):
    return lax.dot_general(a, b, _NT, preferred_element_type=F32)


def _split_dot(a, b01):
    hi = a.astype(BF16)
    lo = (a - hi.astype(F32)).astype(BF16)
    return _dot(hi, b01) + _dot(lo, b01)


def _rms_mod(x, g, sc, sh):
    ms = jnp.mean(x * x, axis=-1, keepdims=True)
    return x * lax.rsqrt(ms + EPS) * g * (1.0 + sc) + sh


def _ada_kernel(c_ref, w_ref, b_ref, o_ref):
    c = c_ref[...]
    a = c / (1.0 + jnp.exp(-c))
    o_ref[...] = _dot(a.astype(BF16), w_ref[...].astype(BF16)) + b_ref[...]


def _ada(c_all, w_ada, b_ada):
    n = c_all.shape[0]
    tn = 1536
    return pl.pallas_call(
        _ada_kernel,
        grid=(6 * D_MODEL // tn,),
        in_specs=[pl.BlockSpec((n, D_MODEL), lambda j: (0, 0)),
                  pl.BlockSpec((D_MODEL, tn), lambda j: (0, j)),
                  pl.BlockSpec((1, tn), lambda j: (0, j))],
        out_specs=pl.BlockSpec((n, tn), lambda j: (0, j)),
        out_shape=jax.ShapeDtypeStruct((n, 6 * D_MODEL), F32),
        compiler_params=pltpu.CompilerParams(vmem_limit_bytes=VMEM_LIMIT),
        name="ada",
    )(c_all, w_ada, b_ada.reshape(1, -1))


def _inproj_core(x, sh1, sc1, g, win_ref, bd_ref, gq_ref, gk_ref):
    h = _rms_mod(x, g, sc1, sh1)
    z = _dot(h.astype(BF16), win_ref[...])
    u = z[:, 0:POOL_WIDTH]
    q = z[:, POOL_WIDTH:POOL_WIDTH + ATTN_WIDTH]
    k = z[:, POOL_WIDTH + ATTN_WIDTH:POOL_WIDTH + 2 * ATTN_WIDTH]
    v = z[:, POOL_WIDTH + 2 * ATTN_WIDTH:]
    bd = bd_ref[...]
    q = q * lax.rsqrt(_split_dot(q * q, bd) * (1.0 / HEAD_DIM) + EPS) * gq_ref[...]
    k = k * lax.rsqrt(_split_dot(k * k, bd) * (1.0 / HEAD_DIM) + EPS) * gk_ref[...]
    return u, q, k, v


def _window_sum(e, w):
    s = e
    sh = 1
    while sh < w:
        s = s + pltpu.roll(s, sh, axis=0)
        sh *= 2
    return s


def _inproj_prompt_kernel(x_ref, mod_ref, g_ref, win_ref, bd_ref, gq_ref, gk_ref, wpool_ref, ps_ref,
                          ypool_ref, q_ref, k_ref, v_ref, tail_ref, ext_ref):
    t = pl.program_id(1)
    nt = pl.num_programs(1)
    x = x_ref[0]
    sh1 = mod_ref[0, :, 0:D_MODEL]
    sc1 = mod_ref[0, :, D_MODEL:2 * D_MODEL]
    u, q, k, v = _inproj_core(x, sh1, sc1, g_ref[...], win_ref, bd_ref, gq_ref, gk_ref)
    q_ref[0] = (q * ATTN_SCALE).astype(BF16)
    k_ref[0] = k
    v_ref[0] = v

    @pl.when(t == 0)
    def _():
        ext_ref[0:HALO, :] = jnp.zeros((HALO, POOL_WIDTH), F32)

    ext_ref[HALO:, :] = u
    pos = t * ROW_TILE + lax.broadcasted_iota(I32, (ROW_TILE, 1), 0)
    for gi, w in enumerate(POOL_WINDOWS):
        cols = slice(gi * POOL_GROUP, (gi + 1) * POOL_GROUP)
        win = _window_sum(ext_ref[:, cols], w)[HALO:]
        inv_cnt = 1.0 / jnp.minimum(pos + 1, w).astype(F32)
        pooled = win * inv_cnt - u[:, cols]
        y = _dot(pooled.astype(BF16), wpool_ref[gi]) * ps_ref[:, cols]
        ypool_ref[0, :, cols] = y.astype(BF16)
    last = u[ROW_TILE - HALO:, :]
    ext_ref[0:HALO, :] = last

    @pl.when(t == nt - 1)
    def _():
        tail_ref[0] = last


def _inproj_prompt(x, mod_p, g1, win, bd, gq, gk, wpool, ps):
    B, T, D = x.shape
    nt = T // ROW_TILE
    const2 = lambda b, t: (0, 0)
    tok = lambda b, t: (b, t, 0)
    return pl.pallas_call(
        _inproj_prompt_kernel,
        grid=(B, nt),
        in_specs=[pl.BlockSpec((1, ROW_TILE, D), tok),
                  pl.BlockSpec((1, 1, 6 * D), lambda b, t: (b, 0, 0)),
                  pl.BlockSpec((1, D), const2),
                  pl.BlockSpec((D, MIX_IN), const2),
                  pl.BlockSpec((ATTN_WIDTH, ATTN_WIDTH), const2),
                  pl.BlockSpec((1, ATTN_WIDTH), const2),
                  pl.BlockSpec((1, ATTN_WIDTH), const2),
                  pl.BlockSpec((4, POOL_GROUP, POOL_GROUP), lambda b, t: (0, 0, 0)),
                  pl.BlockSpec((1, POOL_WIDTH), const2)],
        out_specs=[pl.BlockSpec((1, ROW_TILE, POOL_WIDTH), tok),
                   pl.BlockSpec((1, ROW_TILE, ATTN_WIDTH), tok),
                   pl.BlockSpec((1, ROW_TILE, ATTN_WIDTH), tok),
                   pl.BlockSpec((1, ROW_TILE, ATTN_WIDTH), tok),
                   pl.BlockSpec((1, HALO, POOL_WIDTH), lambda b, t: (b, 0, 0))],
        out_shape=[jax.ShapeDtypeStruct((B, T, POOL_WIDTH), BF16),
                   jax.ShapeDtypeStruct((B, T, ATTN_WIDTH), BF16),
                   jax.ShapeDtypeStruct((B, T, ATTN_WIDTH), F32),
                   jax.ShapeDtypeStruct((B, T, ATTN_WIDTH), F32),
                   jax.ShapeDtypeStruct((B, HALO, POOL_WIDTH), F32)],
        scratch_shapes=[pltpu.VMEM((HALO + ROW_TILE, POOL_WIDTH), F32)],
        compiler_params=pltpu.CompilerParams(
            dimension_semantics=("arbitrary", "arbitrary"), vmem_limit_bytes=VMEM_LIMIT),
        name="inproj_prompt",
    )(x, mod_p, g1, win, bd, gq, gk, wpool, ps)


def _inproj_sample_kernel(x_ref, sh_ref, sc_ref, g_ref, win_ref, bd_ref, gq_ref, gk_ref, hist_ref, wpool_ref,
                          ps_ref, ypool_ref, q_ref, k_ref, v_ref, tail_ref, ext_ref):
    n = DEC_BATCH * DEC_SEQ
    ext_rows = HALO + DEC_SEQ
    u, q, k, v = _inproj_core(x_ref[...], sh_ref[...], sc_ref[...], g_ref[...], win_ref, bd_ref, gq_ref, gk_ref)
    q_ref[...] = q * ATTN_SCALE
    k_ref[...] = k
    v_ref[...] = v
    ext_ref[:, 0:HALO, :] = hist_ref[...]
    ext_ref[:, HALO:, :] = u.reshape(DEC_BATCH, DEC_SEQ, POOL_WIDTH)
    tail_ref[...] = ext_ref[:, ext_rows - HALO:, :]
    pos = PAST_LEN + lax.broadcasted_iota(I32, (DEC_BATCH, DEC_SEQ, 1), 1).reshape(n, 1)
    for gi, w in enumerate(POOL_WINDOWS):
        cols = slice(gi * POOL_GROUP, (gi + 1) * POOL_GROUP)
        e = ext_ref[:, :, cols].reshape(DEC_BATCH * ext_rows, POOL_GROUP)
        win = _window_sum(e, w).reshape(DEC_BATCH, ext_rows, POOL_GROUP)[:, HALO:, :].reshape(n, POOL_GROUP)
        inv_cnt = 1.0 / jnp.minimum(pos + 1, w).astype(F32)
        pooled = win * inv_cnt - u[:, cols]
        y = _dot(pooled.astype(BF16), wpool_ref[gi]) * ps_ref[:, cols]
        ypool_ref[:, cols] = y.astype(BF16)


def _inproj_sample(x, sh1, sc1, g1, win, bd, gq, gk, hist, wpool, ps):
    n = x.shape[0]
    return pl.pallas_call(
        _inproj_sample_kernel,
        out_shape=[jax.ShapeDtypeStruct((n, POOL_WIDTH), BF16),
                   jax.ShapeDtypeStruct((n, ATTN_WIDTH), F32),
                   jax.ShapeDtypeStruct((n, ATTN_WIDTH), F32),
                   jax.ShapeDtypeStruct((n, ATTN_WIDTH), F32),
                   jax.ShapeDtypeStruct((DEC_BATCH, HALO, POOL_WIDTH), F32)],
        scratch_shapes=[pltpu.VMEM((DEC_BATCH, HALO + DEC_SEQ, POOL_WIDTH), F32)],
        compiler_params=pltpu.CompilerParams(vmem_limit_bytes=VMEM_LIMIT),
        name="inproj_sample",
    )(x, sh1, sc1, g1, win, bd, gq, gk, hist, wpool, ps)


N_BIAS_LANES = 3


def _moba_prompt_kernel(slopes_ref, q_ref, k_ref, v_ref, o_ref, ka_ref, va_ref, qa_ref, kmf_ref):
    S = MOBA_BLOCK
    T = k_ref.shape[1]
    nb = T // S
    hp = pl.program_id(1)
    lane = lax.broadcasted_iota(I32, (1, LANES), 1)
    real = (lane < HEAD_DIM, lane >= HEAD_DIM)
    extra = (lane - HEAD_DIM, lane)

    kf = k_ref[0]
    vf = v_ref[0]
    kmf_ref[...] = jnp.zeros((LANES, LANES), F32)
    for j in range(nb):
        mean = jnp.sum(kf[j * S:(j + 1) * S], axis=0, keepdims=True) * (1.0 / S)
        kmf_ref[j:j + 1, :] = jnp.where(real[0], mean, 0.0)
        kmf_ref[nb + j:nb + j + 1, :] = jnp.where(real[0], 0.0, mean)
    key_i = lax.broadcasted_iota(I32, (T, 1), 0)
    key_blk = key_i // S
    for h in range(2):
        b = slopes_ref[2 * hp + h] * key_i.astype(F32)
        p0 = b.astype(BF16).astype(F32)
        p1 = (b - p0).astype(BF16).astype(F32)
        p2 = b - p0 - p1
        e = extra[h]
        onehot = jnp.where((e >= N_BIAS_LANES) & (e - N_BIAS_LANES == key_blk), 1.0, 0.0)
        feat = jnp.where(e == 0, p0, jnp.where(e == 1, p1, jnp.where(e == 2, p2, onehot)))
        ka_ref[h] = jnp.where(real[h], kf, feat).astype(BF16)
        va_ref[h] = jnp.where(real[h], vf, 1.0).astype(BF16)

    q2 = q_ref[0]

    gt = _dot_nt(kmf_ref[...].astype(BF16), q2)[0:2 * nb]
    row = lax.broadcasted_iota(I32, (2 * nb, T), 0)
    blk = row % nb
    cbq = lax.broadcasted_iota(I32, (2 * nb, T), 1) // S
    cnt = jnp.zeros((2 * nb, T), I32)
    for m in range(nb):
        gm = jnp.where(row < nb, gt[m:m + 1, :], gt[nb + m:nb + m + 1, :])
        beats = (gm > gt) | ((gm == gt) & (m < blk))
        cnt = cnt + jnp.where(beats & (m < cbq), 1, 0)
    keep = (((cnt < MOBA_TOPK) & (blk < cbq)) | (blk == cbq)).astype(F32)
    keepq = jnp.concatenate([keep, jnp.zeros((LANES - 2 * nb, T), F32)], axis=0).T
    maskv = jnp.where(keepq > 0.5, 0.0, NEG)
    mask_lanes = (pltpu.roll(maskv, HEAD_DIM + N_BIAS_LANES, axis=1),
                  pltpu.roll(maskv, (N_BIAS_LANES - nb) % LANES, axis=1))
    qf = q2.astype(F32)
    for h in range(2):
        e = extra[h]
        feat = jnp.where(e < N_BIAS_LANES, 1.0, jnp.where(e < N_BIAS_LANES + nb, mask_lanes[h], 0.0))
        qa_ref[h] = jnp.where(real[h], qf, feat).astype(BF16)

    causal = lax.broadcasted_iota(I32, (S, S), 1) <= lax.broadcasted_iota(I32, (S, S), 0)
    for cb in range(nb):
        rows = slice(cb * S, (cb + 1) * S)
        n = (cb + 1) * S
        outs = []
        for h in range(2):
            s = _dot_nt(qa_ref[h, rows, :], ka_ref[h, 0:n, :])
            s_own = jnp.where(causal, s[:, cb * S:n], NEG)
            s = s_own if cb == 0 else jnp.concatenate([s[:, 0:cb * S], s_own], axis=1)
            p = jnp.exp(s - jnp.max(s, axis=-1, keepdims=True))
            outs.append(_dot(p.astype(BF16), va_ref[h, 0:n, :]))
        a0, a1 = outs
        o0 = a0 * (1.0 / a0[:, HEAD_DIM:HEAD_DIM + 1])
        o1 = a1 * (1.0 / a1[:, 0:1])
        o_ref[0, rows, :] = jnp.where(real[0], o0, o1).astype(BF16)


def _moba_prompt(slopes, q, k, v):
    B, T, _ = q.shape
    seq = lambda b, hp: (b, 0, hp)
    return pl.pallas_call(
        _moba_prompt_kernel,
        grid=(B, N_HEADS // 2),
        in_specs=[pl.BlockSpec(memory_space=pltpu.SMEM),
                  pl.BlockSpec((1, T, LANES), seq),
                  pl.BlockSpec((1, T, LANES), seq),
                  pl.BlockSpec((1, T, LANES), seq)],
        out_specs=pl.BlockSpec((1, T, LANES), seq),
        out_shape=jax.ShapeDtypeStruct((B, T, ATTN_WIDTH), BF16),
        scratch_shapes=[pltpu.VMEM((2, T, LANES), BF16),
                        pltpu.VMEM((2, T, LANES), BF16),
                        pltpu.VMEM((2, T, LANES), BF16),
                        pltpu.VMEM((LANES, LANES), F32)],
        compiler_params=pltpu.CompilerParams(
            dimension_semantics=("arbitrary", "arbitrary"), vmem_limit_bytes=VMEM_LIMIT),
        name="moba_prompt",
    )(slopes, q, k, v)


CHUNK_SLOTS = 6


def _moba_sample_kernel(pt_ref, q_ref, kn_ref, vn_ref, slope_ref, lq_ref, ck_ref, cv_ref, o_ref,
                        buf_ref, sem, s_ref, sc_ref, kc_ref, idx_v, idx_s, idx_sem, vbuf_ref, vsem, vb16_ref, ph_ref,
                        gate_ref, res_ref):
    P = PAGES_PER_STEP
    S = MOBA_BLOCK
    chunk = P * PAGE_SIZE
    n_chunks = PAST_LEN // chunk
    bpc = chunk // S
    n_loads = n_chunks
    ahead = CHUNK_SLOTS - 1
    nq = N_HEADS * DEC_SEQ
    b = pl.program_id(0)
    n_rows = pl.num_programs(0) - 1
    slope = slope_ref[...]
    lq = lq_ref[...]
    lane = lax.broadcasted_iota(I32, (1, LANES), 1)

    def slot_of(bb, i):
        return lax.rem(bb * n_loads + i, CHUNK_SLOTS)

    def start_load(bb, i):
        slot = slot_of(bb, i)
        for r in range(P):
            page = pt_ref[bb, i * P + r]
            pltpu.make_async_copy(ck_ref.at[page], buf_ref.at[slot, r], sem.at[slot]).start()

    def load_chunk(i):
        nxt = i + ahead
        if nxt < n_loads:
            start_load(b, nxt)
        else:
            @pl.when(b + 1 < n_rows)
            def _():
                start_load(b + 1, nxt - n_loads)
        slot = slot_of(b, i)
        pltpu.make_async_copy(ck_ref.at[pl.ds(0, P)], buf_ref.at[slot], sem.at[slot]).wait()
        for r in range(P):
            kc_ref[:, r * PAGE_SIZE:(r + 1) * PAGE_SIZE] = buf_ref[slot, r].astype(BF16)
        return kc_ref[...]

    @pl.when(b == 0)
    def _():
        for i in range(ahead):
            start_load(b, i)
        ph_ref[...] = jnp.zeros_like(ph_ref)
        vb16_ref[...] = jnp.zeros_like(vb16_ref)

    head = lax.broadcasted_iota(I32, (DEC_SEQ, ATTN_WIDTH), 1) // HEAD_DIM

    def masked_queries():
        q8 = q_ref[0]
        return jnp.concatenate([jnp.where(head == h, q8, 0.0) for h in range(N_HEADS)], axis=0).astype(BF16)

    @pl.when(b < n_rows)
    def _score_keys():
        qp = masked_queries()
        gate = jnp.zeros((nq, LANES), F32)
        for c in range(n_chunks):
            sc = _dot(qp, load_chunk(c))
            keypos = c * chunk + lax.broadcasted_iota(I32, (1, chunk), 1)
            dist = (keypos - PAST_LEN).astype(F32) - lq
            logits = sc + slope * dist
            for r2 in range(bpc):
                gs = jnp.sum(sc[:, r2 * S:(r2 + 1) * S], axis=-1, keepdims=True) * (1.0 / S)
                gate = jnp.where(lane == c * bpc + r2, gs, gate)
                s_ref[c * bpc + r2] = logits[:, r2 * S:(r2 + 1) * S]
        gate_ref[...] = gate

    @pl.when(b > 0)
    def _finish_previous_row():
        pltpu.make_async_copy(vbuf_ref, vbuf_ref, vsem).wait()
        pieces = []
        for h in range(N_HEADS):
            vb16_ref[0:HEAD_DIM, :] = vbuf_ref[h].astype(BF16)
            pieces.append(_dot_nt(ph_ref[h].astype(BF16), vb16_ref[...])[0:DEC_SEQ])
        pairs = [pieces[2 * i] + pltpu.roll(pieces[2 * i + 1], HEAD_DIM, axis=1) for i in range(N_HEADS // 2)]
        o_ref[0] = (res_ref[...] + jnp.concatenate(pairs, axis=1)).astype(BF16)

    @pl.when(b < n_rows)
    def _pick_and_fetch():
        _pick_blocks_and_fetch_values(
            b, pt_ref, masked_queries(), gate_ref[...], kn_ref, vn_ref, slope, lq, cv_ref, s_ref, sc_ref,
            idx_v, idx_s, idx_sem, vbuf_ref, vsem, ph_ref, res_ref)


def _pick_blocks_and_fetch_values(b, pt_ref, qp, gate, kn_ref, vn_ref, slope, lq, cv_ref, s_ref, sc_ref,
                                  idx_v, idx_s, idx_sem, vbuf_ref, vsem, ph_ref, res_ref):
    S = MOBA_BLOCK
    n_blocks = PAST_LEN // S
    ppb = S // PAGE_SIZE
    nq = N_HEADS * DEC_SEQ
    ktop = min(MOBA_TOPK, n_blocks)
    lane = lax.broadcasted_iota(I32, (1, LANES), 1)
    head = lax.broadcasted_iota(I32, (DEC_SEQ, ATTN_WIDTH), 1) // HEAD_DIM

    gate_t = jnp.concatenate([gate, jnp.zeros((LANES - nq, LANES), F32)], axis=0).T[0:n_blocks]
    blk_i = lax.broadcasted_iota(I32, (n_blocks, LANES), 0)
    cnt = jnp.zeros((n_blocks, LANES), I32)
    for m in range(n_blocks):
        gm = gate_t[m:m + 1, :]
        beats = (gm > gate_t) | ((gm == gate_t) & (m < blk_i))
        cnt = cnt + jnp.where(beats, 1, 0)
    sel_t = cnt < ktop
    row8 = lax.broadcasted_iota(I32, (8, LANES), 0)
    idx_tile = jnp.zeros((8, LANES), I32)
    left = sel_t
    for j in range(ktop):
        pick = jnp.min(jnp.where(left, blk_i, n_blocks), axis=0, keepdims=True)
        idx_tile = jnp.where(row8 == j, pick, idx_tile)
        left = left & (blk_i != pick)
    idx_v[...] = idx_tile
    idx_copy = pltpu.make_async_copy(idx_v, idx_s, idx_sem)
    idx_copy.start()

    kn = jnp.concatenate([kn_ref[0], jnp.zeros((LANES - DEC_SEQ, ATTN_WIDTH), F32)], axis=0)
    lane_f = lane.astype(F32)
    sn = _dot_nt(qp, kn.astype(BF16)) + slope * (lane_f - lq)
    sn = jnp.where(lane_f <= lq, sn, NEG)

    idx_copy.wait()
    span = ktop * S
    for h in range(N_HEADS):
        for l in range(DEC_SEQ):
            q_i = h * DEC_SEQ + l
            for j in range(ktop):
                n = idx_s[j, q_i]
                for half in range(ppb):
                    page = pt_ref[b, n * ppb + half]
                    col = l * span + j * S + half * PAGE_SIZE
                    pltpu.make_async_copy(cv_ref.at[page, pl.ds(h * HEAD_DIM, HEAD_DIM), :],
                                          vbuf_ref.at[h, :, pl.ds(col, PAGE_SIZE)], vsem).start()
                sc_ref[q_i:q_i + 1, j * S:(j + 1) * S] = s_ref[n, q_i:q_i + 1, :]

    sc = sc_ref[...]
    mx = jnp.maximum(jnp.max(sc, axis=-1, keepdims=True), jnp.max(sn, axis=-1, keepdims=True))
    p = jnp.exp(sc - mx)
    pn = jnp.exp(sn - mx)
    linv = 1.0 / (jnp.sum(p, axis=-1, keepdims=True) + jnp.sum(pn, axis=-1, keepdims=True))
    pw = p * linv
    for h in range(N_HEADS):
        for l in range(DEC_SEQ):
            q_i = h * DEC_SEQ + l
            ph_ref[h, l:l + 1, l * span:(l + 1) * span] = pw[q_i:q_i + 1, :]

    vn = jnp.concatenate([vn_ref[0], jnp.zeros((LANES - DEC_SEQ, ATTN_WIDTH), F32)], axis=0)
    o_new = _dot(pn.astype(BF16), vn.astype(BF16)) * linv
    res = jnp.zeros((DEC_SEQ, ATTN_WIDTH), F32)
    for h in range(N_HEADS):
        res = res + jnp.where(head == h, o_new[h * DEC_SEQ:(h + 1) * DEC_SEQ, :], 0.0)
    res_ref[...] = res


def _moba_sample(page_table, q, kn, vn, slope_col, lq_col, cache_kt, cache_vt):
    P = PAGES_PER_STEP
    chunk = P * PAGE_SIZE
    n_chunks = PAST_LEN // chunk
    n_blocks = PAST_LEN // MOBA_BLOCK
    nq = N_HEADS * DEC_SEQ
    fetched = DEC_SEQ * min(MOBA_TOPK, n_blocks) * MOBA_BLOCK
    row3 = lambda b, pt: (jnp.minimum(b, DEC_BATCH - 1), 0, 0)
    prev3 = lambda b, pt: (jnp.maximum(b - 1, 0), 0, 0)
    const2 = lambda b, pt: (0, 0)
    grid_spec = pltpu.PrefetchScalarGridSpec(
        num_scalar_prefetch=1,
        grid=(DEC_BATCH + 1,),
        in_specs=[pl.BlockSpec((1, DEC_SEQ, ATTN_WIDTH), row3),
                  pl.BlockSpec((1, DEC_SEQ, ATTN_WIDTH), row3),
                  pl.BlockSpec((1, DEC_SEQ, ATTN_WIDTH), row3),
                  pl.BlockSpec((nq, 1), const2),
                  pl.BlockSpec((nq, 1), const2),
                  pl.BlockSpec(memory_space=pl.ANY),
                  pl.BlockSpec(memory_space=pl.ANY)],
        out_specs=pl.BlockSpec((1, DEC_SEQ, ATTN_WIDTH), prev3),
        scratch_shapes=[pltpu.VMEM((CHUNK_SLOTS, P, ATTN_WIDTH, PAGE_SIZE), F32),
                        pltpu.SemaphoreType.DMA((CHUNK_SLOTS,)),
                        pltpu.VMEM((n_blocks, nq, MOBA_BLOCK), F32),
                        pltpu.VMEM((nq, fetched // DEC_SEQ), F32),
                        pltpu.VMEM((ATTN_WIDTH, chunk), BF16),
                        pltpu.VMEM((8, LANES), I32),
                        pltpu.SMEM((8, LANES), I32),
                        pltpu.SemaphoreType.DMA(()),
                        pltpu.VMEM((N_HEADS, HEAD_DIM, fetched), F32),
                        pltpu.SemaphoreType.DMA(()),
                        pltpu.VMEM((LANES, fetched), BF16),
                        pltpu.VMEM((N_HEADS, 2 * DEC_SEQ, fetched), F32),
                        pltpu.VMEM((nq, LANES), F32),
                        pltpu.VMEM((DEC_SEQ, ATTN_WIDTH), F32)])
    return pl.pallas_call(
        _moba_sample_kernel,
        grid_spec=grid_spec,
        out_shape=jax.ShapeDtypeStruct((DEC_BATCH, DEC_SEQ, ATTN_WIDTH), BF16),
        compiler_params=pltpu.CompilerParams(dimension_semantics=("arbitrary",), vmem_limit_bytes=VMEM_LIMIT),
        name="moba_sample",
    )(page_table, q, kn, vn, slope_col, lq_col, cache_kt, cache_vt)


def _mod_rows(ref):
    return ref[...].reshape(-1, ref.shape[-1])


def _outproj_kernel(yp_ref, ya_ref, x_ref, gt1_ref, sh2_ref, sc2_ref, g2_ref, wout_ref, wr_ref, wrhi_ref, br_ref,
                    tri_ref, cntin_ref, x1_ref, h2_ref, meta_ref, ew_ref, cntout_ref, cnt_ref):
    rows = x_ref.shape[0]

    @pl.when(pl.program_id(0) == 0)
    def _():
        cnt_ref[...] = cntin_ref[...]

    mix = _dot(yp_ref[...], wout_ref[0:POOL_WIDTH, :]) + _dot(ya_ref[...], wout_ref[POOL_WIDTH:, :])
    x1 = x_ref[...] + _mod_rows(gt1_ref) * mix
    x1_ref[...] = x1
    h2 = _rms_mod(x1, g2_ref[...], _mod_rows(sc2_ref), _mod_rows(sh2_ref))
    h2_ref[...] = h2
    hh = h2.astype(BF16)
    hl = (h2 - hh.astype(F32)).astype(BF16)
    both = _dot(hh, wr_ref[...])
    lt = (both[:, 0:LANES] + both[:, LANES:] + _dot(hl, wrhi_ref[...])).T + br_ref[...]
    row8 = lax.broadcasted_iota(I32, (8, rows), 0)
    g8 = lt[0:8]
    gmax = jnp.max(g8, axis=0, keepdims=True)
    gsum = jnp.sum(jnp.exp(g8 - gmax), axis=0, keepdims=True)
    g_w = 1.0 / gsum
    g_idx = jnp.min(jnp.where(g8 == gmax, row8, 8), axis=0, keepdims=True)
    e_in = jnp.zeros((8, rows), F32)
    for g in range(N_GROUPS):
        e_in = e_in + jnp.where(g_idx == g, lt[8 + 8 * g:16 + 8 * g], 0.0)
    m1 = jnp.max(e_in, axis=0, keepdims=True)
    i1 = jnp.min(jnp.where(e_in == m1, row8, 8), axis=0, keepdims=True)
    e_rest = jnp.where(row8 == i1, NEG, e_in)
    m2 = jnp.max(e_rest, axis=0, keepdims=True)
    i2 = jnp.min(jnp.where(e_rest == m2, row8, 8), axis=0, keepdims=True)
    r = jnp.exp(m2 - m1)
    w1 = g_w / (1.0 + r)
    w2 = g_w * r / (1.0 + r)
    e1 = g_idx * EXPERTS_PER_GROUP + i1
    e2 = g_idx * EXPERTS_PER_GROUP + i2
    rowe = lax.broadcasted_iota(I32, (N_EXPERTS, rows), 0)
    oh1 = (rowe == e1).astype(F32)
    oh2 = (rowe == e2).astype(F32)
    both_oh = oh1 + oh2
    before = _dot(both_oh.astype(BF16), tri_ref[...]) + cnt_ref[:, 0:1]
    rank1 = jnp.sum(oh1 * before, axis=0, keepdims=True).astype(I32)
    rank2 = jnp.sum(oh2 * before, axis=0, keepdims=True).astype(I32)
    cnt_ref[...] = cnt_ref[...] + jnp.sum(both_oh, axis=-1, keepdims=True)
    cntout_ref[...] = cnt_ref[...]
    meta_ref[...] = jnp.where(row8 == 0, e1, jnp.where(row8 == 1, e2, jnp.where(row8 == 2, rank1,
                                                                                 jnp.where(row8 == 3, rank2, 0))))
    ew_ref[...] = jnp.where(row8 == 0, w1, jnp.where(row8 == 1, w2, 0.0))


def _mod_spec(tile, per_batch_rows):
    if per_batch_rows is None:
        return pl.BlockSpec((tile, D_MODEL), lambda i: (i, 0))
    per = per_batch_rows // tile
    return pl.BlockSpec((1, 1, D_MODEL), lambda i: (i // per, 0, 0))


def _outproj(yp, ya, x, gt1, sh2, sc2, g2, wout, wr, wrhi, br, cnt_in, per_batch_rows):
    n = x.shape[0]
    tile = min(ROW_TILE, n)
    tok = lambda i: (i, 0)
    const2 = lambda i: (0, 0)
    mod_spec = _mod_spec(tile, per_batch_rows)
    tri = (jnp.arange(tile)[:, None] < jnp.arange(tile)[None, :]).astype(BF16)
    return pl.pallas_call(
        _outproj_kernel,
        grid=(n // tile,),
        in_specs=[pl.BlockSpec((tile, POOL_WIDTH), tok),
                  pl.BlockSpec((tile, ATTN_WIDTH), tok),
                  pl.BlockSpec((tile, D_MODEL), tok),
                  mod_spec, mod_spec, mod_spec,
                  pl.BlockSpec((1, D_MODEL), const2),
                  pl.BlockSpec((2 * POOL_WIDTH, D_MODEL), const2),
                  pl.BlockSpec((D_MODEL, 2 * LANES), const2),
                  pl.BlockSpec((D_MODEL, LANES), const2),
                  pl.BlockSpec((LANES, 1), const2),
                  pl.BlockSpec((tile, tile), const2),
                  pl.BlockSpec((N_EXPERTS, LANES), const2)],
        out_specs=[pl.BlockSpec((tile, D_MODEL), tok),
                   pl.BlockSpec((tile, D_MODEL), tok),
                   pl.BlockSpec((8, tile), lambda i: (0, i)),
                   pl.BlockSpec((8, tile), lambda i: (0, i)),
                   pl.BlockSpec((N_EXPERTS, LANES), const2)],
        out_shape=[jax.ShapeDtypeStruct((n, D_MODEL), F32),
                   jax.ShapeDtypeStruct((n, D_MODEL), F32),
                   jax.ShapeDtypeStruct((8, n), I32),
                   jax.ShapeDtypeStruct((8, n), F32),
                   jax.ShapeDtypeStruct((N_EXPERTS, LANES), F32)],
        scratch_shapes=[pltpu.VMEM((N_EXPERTS, LANES), F32)],
        compiler_params=pltpu.CompilerParams(dimension_semantics=("arbitrary",), vmem_limit_bytes=VMEM_LIMIT),
        name="outproj",
    )(yp, ya, x, gt1, sh2, sc2, g2, wout, wr, wrhi, br, tri, cnt_in)


def _dest_kernel(pstart_ref, meta_ref, dest_ref):
    n = meta_ref.shape[1]
    eid = meta_ref[0:EXPERT_TOPK, :]
    start = jnp.zeros((EXPERT_TOPK, n), I32)
    for e in range(N_EXPERTS):
        start = jnp.where(eid == e, pstart_ref[e], start)
    dest_ref[...] = start + meta_ref[EXPERT_TOPK:2 * EXPERT_TOPK, :]


def _dest_rows(pstart, meta):
    n = meta.shape[1]
    dest = pl.pallas_call(
        _dest_kernel,
        in_specs=[pl.BlockSpec(memory_space=pltpu.SMEM), pl.BlockSpec(memory_space=pltpu.VMEM)],
        out_specs=pl.BlockSpec(memory_space=pltpu.VMEM),
        out_shape=jax.ShapeDtypeStruct((EXPERT_TOPK, n), I32),
        name="dest",
    )(pstart, meta)
    return [dest[slot] for slot in range(EXPERT_TOPK)]


DMA_ISSUE_UNROLL = 16


def _dispatch_kernel(*refs):
    dest_refs = refs[0:EXPERT_TOPK]
    h_ref, xr_in_ref, xr_ref, sem = refs[EXPERT_TOPK:]
    rows = h_ref.shape[0]

    def issue(t, carry):
        for d_ref in dest_refs:
            pltpu.make_async_copy(h_ref.at[pl.ds(t, 1), :], xr_ref.at[pl.ds(d_ref[t], 1), :], sem).start()
        return carry

    lax.fori_loop(0, rows, issue, 0, unroll=DMA_ISSUE_UNROLL)
    for _ in dest_refs:
        pltpu.make_async_copy(h_ref, xr_ref.at[pl.ds(0, rows), :], sem).wait()


def _dispatch_rows(dests, h2, xr):
    n = h2.shape[0]
    tile = min(ROW_TILE, n)
    smem_rows = pl.BlockSpec((tile,), lambda i: (i,), memory_space=pltpu.SMEM)
    return pl.pallas_call(
        _dispatch_kernel,
        grid=(n // tile,),
        in_specs=[smem_rows] * EXPERT_TOPK + [pl.BlockSpec((tile, D_MODEL), lambda i: (i, 0)),
                                             pl.BlockSpec(memory_space=pl.ANY)],
        out_specs=pl.BlockSpec(memory_space=pl.ANY),
        out_shape=jax.ShapeDtypeStruct(xr.shape, xr.dtype),
        scratch_shapes=[pltpu.SemaphoreType.DMA(())],
        input_output_aliases={EXPERT_TOPK + 1: 0},
        compiler_params=pltpu.CompilerParams(dimension_semantics=("arbitrary",), vmem_limit_bytes=VMEM_LIMIT),
        name="dispatch",
    )(*dests, h2, xr)


def _moe_kernel(blk_e_ref, nact_ref, x_ref, wg_ref, wu_ref, wd_ref, y_ref, wgb_ref, wub_ref, wdb_ref):
    i = pl.program_id(0)
    prev = blk_e_ref[jnp.maximum(i - 1, 0)]
    active = i < nact_ref[0]

    @pl.when(active & ((i == 0) | (blk_e_ref[i] != prev)))
    def _():
        wgb_ref[...] = wg_ref[0].astype(BF16)
        wub_ref[...] = wu_ref[0].astype(BF16)
        wdb_ref[...] = wd_ref[0].astype(BF16)

    @pl.when(active)
    def _():
        x = x_ref[...].astype(BF16)
        g = _dot(x, wgb_ref[...])
        u = _dot(x, wub_ref[...])
        a = g / (1.0 + jnp.exp(-g)) * u
        y_ref[...] = _dot(a.astype(BF16), wdb_ref[...])

    @pl.when(jnp.logical_not(active))
    def _():
        y_ref[...] = jnp.zeros_like(y_ref)


def _moe(blk_e, n_active, xr, w_gate, w_up, w_down):
    n_rows = xr.shape[0]
    n_blocks = n_rows // MOE_ROWS
    grid_spec = pltpu.PrefetchScalarGridSpec(
        num_scalar_prefetch=2,
        grid=(n_blocks,),
        in_specs=[pl.BlockSpec((MOE_ROWS, D_MODEL), lambda i, be, na: (i, 0)),
                  pl.BlockSpec((1, D_MODEL, D_FF), lambda i, be, na: (be[i], 0, 0)),
                  pl.BlockSpec((1, D_MODEL, D_FF), lambda i, be, na: (be[i], 0, 0)),
                  pl.BlockSpec((1, D_FF, D_MODEL), lambda i, be, na: (be[i], 0, 0))],
        out_specs=pl.BlockSpec((MOE_ROWS, D_MODEL), lambda i, be, na: (i, 0)),
        scratch_shapes=[pltpu.VMEM((D_MODEL, D_FF), BF16),
                        pltpu.VMEM((D_MODEL, D_FF), BF16),
                        pltpu.VMEM((D_FF, D_MODEL), BF16)])
    return pl.pallas_call(
        _moe_kernel,
        grid_spec=grid_spec,
        out_shape=jax.ShapeDtypeStruct((n_rows, D_MODEL), F32),
        compiler_params=pltpu.CompilerParams(
            dimension_semantics=("arbitrary",), vmem_limit_bytes=VMEM_LIMIT),
        name="moe",
    )(blk_e, n_active, xr, w_gate, w_up, w_down)


def _final_kernel(*refs):
    dest_refs = refs[0:EXPERT_TOPK]
    x1_ref, w_ref, gt2_ref, yr_ref, y_ref, o_ref, sem = refs[EXPERT_TOPK:]
    rows = x1_ref.shape[0]

    def issue(t, carry):
        for slot, d_ref in enumerate(dest_refs):
            pltpu.make_async_copy(yr_ref.at[pl.ds(d_ref[t], 1), :], o_ref.at[slot, pl.ds(t, 1), :], sem).start()
        return carry

    lax.fori_loop(0, rows, issue, 0, unroll=DMA_ISSUE_UNROLL)
    for slot in range(EXPERT_TOPK):
        pltpu.make_async_copy(yr_ref.at[pl.ds(0, rows), :], o_ref.at[slot], sem).wait()
    w = w_ref[...]
    moe = o_ref[0] * w[:, 0:1] + o_ref[1] * w[:, 1:2]
    y_ref[...] = x1_ref[...] + _mod_rows(gt2_ref) * moe


def _final(dests, x1, w2, gt2, yr, per_batch_rows):
    n = x1.shape[0]
    tile = min(ROW_TILE, n)
    tok = lambda i: (i, 0)
    smem_rows = pl.BlockSpec((tile,), lambda i: (i,), memory_space=pltpu.SMEM)
    return pl.pallas_call(
        _final_kernel,
        grid=(n // tile,),
        in_specs=[smem_rows] * EXPERT_TOPK + [pl.BlockSpec((tile, D_MODEL), tok),
                                             pl.BlockSpec((tile, EXPERT_TOPK), tok),
                                             _mod_spec(tile, per_batch_rows),
                                             pl.BlockSpec(memory_space=pl.ANY)],
        out_specs=pl.BlockSpec((tile, D_MODEL), tok),
        out_shape=jax.ShapeDtypeStruct((n, D_MODEL), F32),
        scratch_shapes=[pltpu.VMEM((EXPERT_TOPK, tile, D_MODEL), F32),
                        pltpu.SemaphoreType.DMA(())],
        compiler_params=pltpu.CompilerParams(dimension_semantics=("arbitrary",), vmem_limit_bytes=VMEM_LIMIT),
        name="final",
    )(*dests, x1, w2, gt2, yr)


def _expert_layout(counts, n_blocks):
    padded = (counts + MOE_ROWS - 1) // MOE_ROWS * MOE_ROWS
    pend = jnp.cumsum(padded)
    pstart = (pend - padded).astype(I32)
    blk_start = jnp.arange(n_blocks, dtype=I32) * MOE_ROWS
    blk_e = jnp.minimum(jnp.sum((pend[None, :] <= blk_start[:, None]).astype(I32), axis=1), N_EXPERTS - 1)
    n_active = (pend[-1] // MOE_ROWS).astype(I32).reshape(1)
    return pstart, blk_e.astype(I32), n_active


def kernel(x_prompt, x_sample, cache_k, cache_v, state_pool, page_table, c_prompt, c_sample, w_ada, b_ada,
           g_attn_norm, w_in, g_q, g_k, w_pool, pool_scale, w_out, g_ffn_norm, w_group, b_group, w_expert,
           b_expert, w_gate, w_up, w_down):
    D = D_MODEL
    B, T, _ = x_prompt.shape
    n_s = DEC_BATCH * DEC_SEQ
    n_p = B * T
    layer = 0

    win = w_in[layer].astype(BF16)
    wout = w_out[layer].astype(BF16)
    wpool = w_pool[layer].astype(BF16)
    g1 = g_attn_norm[layer].reshape(1, D)
    g2 = g_ffn_norm[layer].reshape(1, D)
    gq = jnp.tile(g_q[layer], N_HEADS).reshape(1, ATTN_WIDTH)
    gk = jnp.tile(g_k[layer], N_HEADS).reshape(1, ATTN_WIDTH)
    ps = pool_scale[layer].reshape(1, POOL_WIDTH)
    hd = jnp.arange(ATTN_WIDTH) // HEAD_DIM
    bd = (hd[:, None] == hd[None, :]).astype(BF16)
    slopes = jnp.exp2(-8.0 * (jnp.arange(N_HEADS, dtype=F32) + 1.0) / N_HEADS)
    qrow = jnp.arange(N_HEADS * DEC_SEQ)
    slope_col = slopes[qrow // DEC_SEQ].reshape(-1, 1)
    lq_col = (qrow % DEC_SEQ).astype(F32).reshape(-1, 1)
    wr = jnp.zeros((D, LANES), F32).at[:, 0:N_GROUPS].set(w_group[layer]).at[:, 8:8 + N_EXPERTS].set(w_expert[layer])
    wr_hi = wr.astype(BF16)
    wr_lo = (wr - wr_hi.astype(F32)).astype(BF16)
    wr_both = jnp.concatenate([wr_hi, wr_lo], axis=1)
    br = jnp.zeros((LANES,), F32).at[0:N_GROUPS].set(b_group[layer]).at[N_GROUPS:8].set(NEG)
    br = br.at[8:8 + N_EXPERTS].set(b_expert[layer]).reshape(LANES, 1)

    mod = _ada(jnp.concatenate([c_prompt, c_sample], axis=0), w_ada[layer], b_ada[layer])
    mod_p = mod[:B].reshape(B, 1, 6 * D)
    mod_s = jnp.repeat(mod[B:], DEC_SEQ, axis=0)

    ypool_p, q_p, k_p, v_p, tail_p = _inproj_prompt(x_prompt, mod_p, g1, win, bd, gq, gk, wpool, ps)
    yattn_p = _moba_prompt(slopes, q_p, k_p, v_p)
    mp = lambda j: mod_p[:, :, j * D:(j + 1) * D]
    cnt0 = jnp.zeros((N_EXPERTS, LANES), F32)
    x1_p, h2_p, meta_p, ew_p, cnt_p = _outproj(
        ypool_p.reshape(n_p, POOL_WIDTH), yattn_p.reshape(n_p, ATTN_WIDTH), x_prompt.reshape(n_p, D),
        mp(2), mp(3), mp(4), g2, wout, wr_both, wr_hi, br, cnt0, T)

    ms = lambda j: mod_s[:, j * D:(j + 1) * D]
    hist = jnp.concatenate([jnp.zeros((DEC_BATCH, HALO - POOL_HIST, POOL_WIDTH), F32), state_pool[layer]], axis=1)
    xs = x_sample.reshape(n_s, D)
    ypool_s, q_s, k_s, v_s, tail_s = _inproj_sample(xs, ms(0), ms(1), g1, win, bd, gq, gk, hist, wpool, ps)
    r3 = lambda a: a.reshape(DEC_BATCH, DEC_SEQ, ATTN_WIDTH)
    n_phys = cache_k.shape[1]
    ck = jnp.transpose(cache_k[layer], (0, 2, 3, 1)).reshape(n_phys, ATTN_WIDTH, PAGE_SIZE)
    cv = jnp.transpose(cache_v[layer], (0, 2, 3, 1)).reshape(n_phys, ATTN_WIDTH, PAGE_SIZE)
    yattn_s = _moba_sample(page_table, r3(q_s), r3(k_s), r3(v_s), slope_col, lq_col, ck, cv)
    x1_s, h2_s, meta_s, ew_s, cnt_s = _outproj(ypool_s, yattn_s.reshape(n_s, ATTN_WIDTH), xs, ms(2), ms(3), ms(4),
                                               g2, wout, wr_both, wr_hi, br, cnt_p, None)

    n_blocks = (n_p + n_s) * EXPERT_TOPK // MOE_ROWS + N_EXPERTS
    pstart, blk_e, n_active = _expert_layout(cnt_s[:, 0].astype(I32), n_blocks)
    xr = jnp.zeros((n_blocks * MOE_ROWS, D), F32)
    dests_p = _dest_rows(pstart, meta_p)
    dests_s = _dest_rows(pstart, meta_s)
    xr = _dispatch_rows(dests_p, h2_p, xr)
    xr = _dispatch_rows(dests_s, h2_s, xr)
    yr = _moe(blk_e, n_active, xr, w_gate[layer], w_up[layer], w_down[layer])
    y_p = _final(dests_p, x1_p, ew_p[0:EXPERT_TOPK].T, mp(5), yr, T)
    y_s = _final(dests_s, x1_s, ew_s[0:EXPERT_TOPK].T, ms(5), yr, None)

    k4 = lambda a, b, l: a.reshape(1, b, l, N_HEADS, HEAD_DIM)
    return (y_p.reshape(B, T, D), y_s.reshape(DEC_BATCH, DEC_SEQ, D),
            k4(k_p, B, T), k4(v_p, B, T), tail_p[None, :, HALO - POOL_HIST:, :],
            k4(k_s, DEC_BATCH, DEC_SEQ), k4(v_s, DEC_BATCH, DEC_SEQ), tail_s[None, :, HALO - POOL_HIST:, :])
```

```python
import functools

import jax
import jax.numpy as jnp
from jax import lax
from jax.experimental import pallas as pl
from jax.experimental.pallas import tpu as pltpu

F32 = jnp.float32
BF16 = jnp.bfloat16
I32 = jnp.int32

D_MODEL = 1024
BATCH = 8
SEQ = 2048
DEC_BATCH = 32
DEC_SEQ = 8
PAST_LEN = 16384
PAGE_SIZE = 128
POOL_WIDTH = 512
POOL_WINDOWS = (2, 4, 8, 16)
POOL_GROUP = 128
POOL_HIST = 15
HALO = 16
N_HEADS = 8
HEAD_DIM = 64
ATTN_WIDTH = 512
MOBA_BLOCK = 256
MOBA_TOPK = 3
ATTN_SCALE = HEAD_DIM ** -0.5
MIX_IN = POOL_WIDTH + 3 * ATTN_WIDTH
N_GROUPS = 4
EXPERTS_PER_GROUP = 8
N_EXPERTS = 32
EXPERT_TOPK = 2
D_FF = 512
EPS = 1e-6
NEG = -1e30

LANES = 128
ROW_TILE = 512
MOE_ROWS = 256
PAGES_PER_STEP = 16
VMEM_LIMIT = 56 * 1024 * 1024

_NT = (((1,), (1,)), ((), ()))


def _dot(a, b):
    return jnp.dot(a, b, preferred_element_type=F32)


def _dot_nt(a, b):
    return lax.dot_general(a, b, _NT, preferred_element_type=F32)


def _split_dot(a, b01):
    hi = a.astype(BF16)
    lo = (a - hi.astype(F32)).astype(BF16)
    return _dot(hi, b01) + _dot(lo, b01)


def _rms_mod(x, g, sc, sh):
    ms = jnp.mean(x * x, axis=-1, keepdims=True)
    return x * lax.rsqrt(ms + EPS) * g * (1.0 + sc) + sh


def _ada_kernel(c_ref, w_ref, b_ref, o_ref):
    c = c_ref[...]
    a = c / (1.0 + jnp.exp(-c))
    o_ref[...] = _dot(a.astype(BF16), w_ref[...].astype(BF16)) + b_ref[...]


def _ada(c_all, w_ada, b_ada):
    n = c_all.shape[0]
    tn = 1536
    return pl.pallas_call(
        _ada_kernel,
        grid=(6 * D_MODEL // tn,),
        in_specs=[pl.BlockSpec((n, D_MODEL), lambda j: (0, 0)),
                  pl.BlockSpec((D_MODEL, tn), lambda j: (0, j)),
                  pl.BlockSpec((1, tn), lambda j: (0, j))],
        out_specs=pl.BlockSpec((n, tn), lambda j: (0, j)),
        out_shape=jax.ShapeDtypeStruct((n, 6 * D_MODEL), F32),
        compiler_params=pltpu.CompilerParams(vmem_limit_bytes=VMEM_LIMIT),
        name="ada",
    )(c_all, w_ada, b_ada.reshape(1, -1))


def _inproj_core(x, sh1, sc1, g, win_ref, bd_ref, gq_ref, gk_ref):
    h = _rms_mod(x, g, sc1, sh1)
    z = _dot(h.astype(BF16), win_ref[...])
    u = z[:, 0:POOL_WIDTH]
    q = z[:, POOL_WIDTH:POOL_WIDTH + ATTN_WIDTH]
    k = z[:, POOL_WIDTH + ATTN_WIDTH:POOL_WIDTH + 2 * ATTN_WIDTH]
    v = z[:, POOL_WIDTH + 2 * ATTN_WIDTH:]
    bd = bd_ref[...]
    q = q * lax.rsqrt(_split_dot(q * q, bd) * (1.0 / HEAD_DIM) + EPS) * gq_ref[...]
    k = k * lax.rsqrt(_split_dot(k * k, bd) * (1.0 / HEAD_DIM) + EPS) * gk_ref[...]
    return u, q, k, v


def _window_sum(e, w):
    s = e
    sh = 1
    while sh < w:
        s = s + pltpu.roll(s, sh, axis=0)
        sh *= 2
    return s


def _inproj_prompt_kernel(x_ref, mod_ref, g_ref, win_ref, bd_ref, gq_ref, gk_ref, wpool_ref, ps_ref,
                          ypool_ref, q_ref, k_ref, v_ref, tail_ref, ext_ref):
    t = pl.program_id(1)
    nt = pl.num_programs(1)
    x = x_ref[0]
    sh1 = mod_ref[0, :, 0:D_MODEL]
    sc1 = mod_ref[0, :, D_MODEL:2 * D_MODEL]
    u, q, k, v = _inproj_core(x, sh1, sc1, g_ref[...], win_ref, bd_ref, gq_ref, gk_ref)
    q_ref[0] = (q * ATTN_SCALE).astype(BF16)
    k_ref[0] = k
    v_ref[0] = v

    @pl.when(t == 0)
    def _():
        ext_ref[0:HALO, :] = jnp.zeros((HALO, POOL_WIDTH), F32)

    ext_ref[HALO:, :] = u
    pos = t * ROW_TILE + lax.broadcasted_iota(I32, (ROW_TILE, 1), 0)
    for gi, w in enumerate(POOL_WINDOWS):
        cols = slice(gi * POOL_GROUP, (gi + 1) * POOL_GROUP)
        win = _window_sum(ext_ref[:, cols], w)[HALO:]
        inv_cnt = 1.0 / jnp.minimum(pos + 1, w).astype(F32)
        pooled = win * inv_cnt - u[:, cols]
        y = _dot(pooled.astype(BF16), wpool_ref[gi]) * ps_ref[:, cols]
        ypool_ref[0, :, cols] = y.astype(BF16)
    last = u[ROW_TILE - HALO:, :]
    ext_ref[0:HALO, :] = last

    @pl.when(t == nt - 1)
    def _():
        tail_ref[0] = last


def _inproj_prompt(x, mod_p, g1, win, bd, gq, gk, wpool, ps):
    B, T, D = x.shape
    nt = T // ROW_TILE
    const2 = lambda b, t: (0, 0)
    tok = lambda b, t: (b, t, 0)
    return pl.pallas_call(
        _inproj_prompt_kernel,
        grid=(B, nt),
        in_specs=[pl.BlockSpec((1, ROW_TILE, D), tok),
                  pl.BlockSpec((1, 1, 6 * D), lambda b, t: (b, 0, 0)),
                  pl.BlockSpec((1, D), const2),
                  pl.BlockSpec((D, MIX_IN), const2),
                  pl.BlockSpec((ATTN_WIDTH, ATTN_WIDTH), const2),
                  pl.BlockSpec((1, ATTN_WIDTH), const2),
                  pl.BlockSpec((1, ATTN_WIDTH), const2),
                  pl.BlockSpec((4, POOL_GROUP, POOL_GROUP), lambda b, t: (0, 0, 0)),
                  pl.BlockSpec((1, POOL_WIDTH), const2)],
        out_specs=[pl.BlockSpec((1, ROW_TILE, POOL_WIDTH), tok),
                   pl.BlockSpec((1, ROW_TILE, ATTN_WIDTH), tok),
                   pl.BlockSpec((1, ROW_TILE, ATTN_WIDTH), tok),
                   pl.BlockSpec((1, ROW_TILE, ATTN_WIDTH), tok),
                   pl.BlockSpec((1, HALO, POOL_WIDTH), lambda b, t: (b, 0, 0))],
        out_shape=[jax.ShapeDtypeStruct((B, T, POOL_WIDTH), BF16),
                   jax.ShapeDtypeStruct((B, T, ATTN_WIDTH), BF16),
                   jax.ShapeDtypeStruct((B, T, ATTN_WIDTH), F32),
                   jax.ShapeDtypeStruct((B, T, ATTN_WIDTH), F32),
                   jax.ShapeDtypeStruct((B, HALO, POOL_WIDTH), F32)],
        scratch_shapes=[pltpu.VMEM((HALO + ROW_TILE, POOL_WIDTH), F32)],
        compiler_params=pltpu.CompilerParams(
            dimension_semantics=("arbitrary", "arbitrary"), vmem_limit_bytes=VMEM_LIMIT),
        name="inproj_prompt",
    )(x, mod_p, g1, win, bd, gq, gk, wpool, ps)


def _inproj_sample_kernel(x_ref, sh_ref, sc_ref, g_ref, win_ref, bd_ref, gq_ref, gk_ref, hist_ref, wpool_ref,
                          ps_ref, ypool_ref, q_ref, k_ref, v_ref, tail_ref, ext_ref):
    n = DEC_BATCH * DEC_SEQ
    ext_rows = HALO + DEC_SEQ
    u, q, k, v = _inproj_core(x_ref[...], sh_ref[...], sc_ref[...], g_ref[...], win_ref, bd_ref, gq_ref, gk_ref)
    q_ref[...] = q * ATTN_SCALE
    k_ref[...] = k
    v_ref[...] = v
    ext_ref[:, 0:HALO, :] = hist_ref[...]
    ext_ref[:, HALO:, :] = u.reshape(DEC_BATCH, DEC_SEQ, POOL_WIDTH)
    tail_ref[...] = ext_ref[:, ext_rows - HALO:, :]
    pos = PAST_LEN + lax.broadcasted_iota(I32, (DEC_BATCH, DEC_SEQ, 1), 1).reshape(n, 1)
    for gi, w in enumerate(POOL_WINDOWS):
        cols = slice(gi * POOL_GROUP, (gi + 1) * POOL_GROUP)
        e = ext_ref[:, :, cols].reshape(DEC_BATCH * ext_rows, POOL_GROUP)
        win = _window_sum(e, w).reshape(DEC_BATCH, ext_rows, POOL_GROUP)[:, HALO:, :].reshape(n, POOL_GROUP)
        inv_cnt = 1.0 / jnp.minimum(pos + 1, w).astype(F32)
        pooled = win * inv_cnt - u[:, cols]
        y = _dot(pooled.astype(BF16), wpool_ref[gi]) * ps_ref[:, cols]
        ypool_ref[:, cols] = y.astype(BF16)


def _inproj_sample(x, sh1, sc1, g1, win, bd, gq, gk, hist, wpool, ps):
    n = x.shape[0]
    return pl.pallas_call(
        _inproj_sample_kernel,
        out_shape=[jax.ShapeDtypeStruct((n, POOL_WIDTH), BF16),
                   jax.ShapeDtypeStruct((n, ATTN_WIDTH), F32),
                   jax.ShapeDtypeStruct((n, ATTN_WIDTH), F32),
                   jax.ShapeDtypeStruct((n, ATTN_WIDTH), F32),
                   jax.ShapeDtypeStruct((DEC_BATCH, HALO, POOL_WIDTH), F32)],
        scratch_shapes=[pltpu.VMEM((DEC_BATCH, HALO + DEC_SEQ, POOL_WIDTH), F32)],
        compiler_params=pltpu.CompilerParams(vmem_limit_bytes=VMEM_LIMIT),
        name="inproj_sample",
    )(x, sh1, sc1, g1, win, bd, gq, gk, hist, wpool, ps)


N_BIAS_LANES = 3
PROMPT_Q_TILE = 2 * MOBA_BLOCK


def _moba_prompt_kernel(slopes_ref, q_ref, k_ref, v_ref, o_ref, ka_ref, va_ref, qa_ref, kmf_ref):
    S = MOBA_BLOCK
    T = k_ref.shape[1]
    nb = T // S
    hp = pl.program_id(1)
    lane = lax.broadcasted_iota(I32, (1, LANES), 1)
    real = (lane < HEAD_DIM, lane >= HEAD_DIM)
    extra = (lane - HEAD_DIM, lane)

    kf = k_ref[0]
    vf = v_ref[0]
    kmf_ref[...] = jnp.zeros((LANES, LANES), F32)
    for j in range(nb):
        mean = jnp.sum(kf[j * S:(j + 1) * S], axis=0, keepdims=True) * (1.0 / S)
        kmf_ref[j:j + 1, :] = jnp.where(real[0], mean, 0.0)
        kmf_ref[nb + j:nb + j + 1, :] = jnp.where(real[0], 0.0, mean)
    key_i = lax.broadcasted_iota(I32, (T, 1), 0)
    key_blk = key_i // S
    for h in range(2):
        b = slopes_ref[2 * hp + h] * key_i.astype(F32)
        p0 = b.astype(BF16).astype(F32)
        p1 = (b - p0).astype(BF16).astype(F32)
        p2 = b - p0 - p1
        e = extra[h]
        onehot = jnp.where((e >= N_BIAS_LANES) & (e - N_BIAS_LANES == key_blk), 1.0, 0.0)
        feat = jnp.where(e == 0, p0, jnp.where(e == 1, p1, jnp.where(e == 2, p2, onehot)))
        ka_ref[h] = jnp.where(real[h], kf, feat).astype(BF16)
        va_ref[h] = jnp.where(real[h], vf, 1.0).astype(BF16)

    q2 = q_ref[0]

    gt = _dot_nt(kmf_ref[...].astype(BF16), q2)[0:2 * nb]
    row = lax.broadcasted_iota(I32, (2 * nb, T), 0)
    blk = row % nb
    cbq = lax.broadcasted_iota(I32, (2 * nb, T), 1) // S
    cnt = jnp.zeros((2 * nb, T), I32)
    for m in range(nb):
        gm = jnp.where(row < nb, gt[m:m + 1, :], gt[nb + m:nb + m + 1, :])
        beats = (gm > gt) | ((gm == gt) & (m < blk))
        cnt = cnt + jnp.where(beats & (m < cbq), 1, 0)
    keep = (((cnt < MOBA_TOPK) & (blk < cbq)) | (blk == cbq)).astype(F32)
    keepq = jnp.concatenate([keep, jnp.zeros((LANES - 2 * nb, T), F32)], axis=0).T
    maskv = jnp.where(keepq > 0.5, 0.0, NEG)
    mask_lanes = (pltpu.roll(maskv, HEAD_DIM + N_BIAS_LANES, axis=1),
                  pltpu.roll(maskv, (N_BIAS_LANES - nb) % LANES, axis=1))
    qf = q2.astype(F32)
    for h in range(2):
        e = extra[h]
        feat = jnp.where(e < N_BIAS_LANES, 1.0, jnp.where(e < N_BIAS_LANES + nb, mask_lanes[h], 0.0))
        qa_ref[h] = jnp.where(real[h], qf, feat).astype(BF16)

    QT = PROMPT_Q_TILE
    causal = lax.broadcasted_iota(I32, (QT, QT), 1) <= lax.broadcasted_iota(I32, (QT, QT), 0)
    for t in range(T // QT):
        rows = slice(t * QT, (t + 1) * QT)
        n = (t + 1) * QT
        outs = []
        for h in range(2):
            s = _dot_nt(qa_ref[h, rows, :], ka_ref[h, 0:n, :])
            s_own = jnp.where(causal, s[:, t * QT:n], NEG)
            s = s_own if t == 0 else jnp.concatenate([s[:, 0:t * QT], s_own], axis=1)
            p = jnp.exp(s - jnp.max(s, axis=-1, keepdims=True))
            outs.append(_dot(p.astype(BF16), va_ref[h, 0:n, :]))
        a0, a1 = outs
        o0 = a0 * (1.0 / a0[:, HEAD_DIM:HEAD_DIM + 1])
        o1 = a1 * (1.0 / a1[:, 0:1])
        o_ref[0, rows, :] = jnp.where(real[0], o0, o1).astype(BF16)


def _moba_prompt(slopes, q, k, v):
    B, T, _ = q.shape
    seq = lambda b, hp: (b, 0, hp)
    return pl.pallas_call(
        _moba_prompt_kernel,
        grid=(B, N_HEADS // 2),
        in_specs=[pl.BlockSpec(memory_space=pltpu.SMEM),
                  pl.BlockSpec((1, T, LANES), seq),
                  pl.BlockSpec((1, T, LANES), seq),
                  pl.BlockSpec((1, T, LANES), seq)],
        out_specs=pl.BlockSpec((1, T, LANES), seq),
        out_shape=jax.ShapeDtypeStruct((B, T, ATTN_WIDTH), BF16),
        scratch_shapes=[pltpu.VMEM((2, T, LANES), BF16),
                        pltpu.VMEM((2, T, LANES), BF16),
                        pltpu.VMEM((2, T, LANES), BF16),
                        pltpu.VMEM((LANES, LANES), F32)],
        compiler_params=pltpu.CompilerParams(
            dimension_semantics=("arbitrary", "arbitrary"), vmem_limit_bytes=VMEM_LIMIT),
        name="moba_prompt",
    )(slopes, q, k, v)


CHUNK_SLOTS = 6


def _moba_sample_kernel(pt_ref, q_ref, kn_ref, vn_ref, slope_ref, lq_ref, ck_ref, cv_ref, o_ref,
                        buf_ref, sem, s_ref, sc_ref, kc_ref, idx_v, idx_s, idx_sem, vbuf_ref, vsem, vb16_ref, ph_ref,
                        gate_ref, res_ref):
    P = PAGES_PER_STEP
    S = MOBA_BLOCK
    chunk = P * PAGE_SIZE
    n_chunks = PAST_LEN // chunk
    bpc = chunk // S
    n_loads = n_chunks
    ahead = CHUNK_SLOTS - 1
    nq = N_HEADS * DEC_SEQ
    b = pl.program_id(0)
    n_rows = pl.num_programs(0) - 1
    slope = slope_ref[...]
    lq = lq_ref[...]
    lane = lax.broadcasted_iota(I32, (1, LANES), 1)

    def slot_of(bb, i):
        return lax.rem(bb * n_loads + i, CHUNK_SLOTS)

    def start_load(bb, i):
        slot = slot_of(bb, i)
        for r in range(P):
            page = pt_ref[bb, i * P + r]
            pltpu.make_async_copy(ck_ref.at[page], buf_ref.at[slot, r], sem.at[slot]).start()

    def load_chunk(i):
        nxt = i + ahead
        if nxt < n_loads:
            start_load(b, nxt)
        else:
            @pl.when(b + 1 < n_rows)
            def _():
                start_load(b + 1, nxt - n_loads)
        slot = slot_of(b, i)
        pltpu.make_async_copy(ck_ref.at[pl.ds(0, P)], buf_ref.at[slot], sem.at[slot]).wait()
        for r in range(P):
            kc_ref[:, r * PAGE_SIZE:(r + 1) * PAGE_SIZE] = buf_ref[slot, r].astype(BF16)
        return kc_ref[...]

    @pl.when(b == 0)
    def _():
        for i in range(ahead):
            start_load(b, i)
        ph_ref[...] = jnp.zeros_like(ph_ref)
        vb16_ref[...] = jnp.zeros_like(vb16_ref)

    head = lax.broadcasted_iota(I32, (DEC_SEQ, ATTN_WIDTH), 1) // HEAD_DIM

    def masked_queries():
        q8 = q_ref[0]
        return jnp.concatenate([jnp.where(head == h, q8, 0.0) for h in range(N_HEADS)], axis=0).astype(BF16)

    @pl.when(b < n_rows)
    def _score_keys():
        qp = masked_queries()
        gate = jnp.zeros((nq, LANES), F32)
        for c in range(n_chunks):
            sc = _dot(qp, load_chunk(c))
            keypos = c * chunk + lax.broadcasted_iota(I32, (1, chunk), 1)
            dist = (keypos - PAST_LEN).astype(F32) - lq
            logits = sc + slope * dist
            for r2 in range(bpc):
                gs = jnp.sum(sc[:, r2 * S:(r2 + 1) * S], axis=-1, keepdims=True) * (1.0 / S)
                gate = jnp.where(lane == c * bpc + r2, gs, gate)
                s_ref[c * bpc + r2] = logits[:, r2 * S:(r2 + 1) * S]
        gate_ref[...] = gate

    @pl.when(b > 0)
    def _finish_previous_row():
        pltpu.make_async_copy(vbuf_ref, vbuf_ref, vsem).wait()
        pieces = []
        for h in range(N_HEADS):
            vb16_ref[0:HEAD_DIM, :] = vbuf_ref[h].astype(BF16)
            pieces.append(_dot_nt(ph_ref[h].astype(BF16), vb16_ref[...])[0:DEC_SEQ])
        pairs = [pieces[2 * i] + pltpu.roll(pieces[2 * i + 1], HEAD_DIM, axis=1) for i in range(N_HEADS // 2)]
        o_ref[0] = (res_ref[...] + jnp.concatenate(pairs, axis=1)).astype(BF16)

    @pl.when(b < n_rows)
    def _pick_and_fetch():
        _pick_blocks_and_fetch_values(
            b, pt_ref, masked_queries(), gate_ref[...], kn_ref, vn_ref, slope, lq, cv_ref, s_ref, sc_ref,
            idx_v, idx_s, idx_sem, vbuf_ref, vsem, ph_ref, res_ref)


def _pick_blocks_and_fetch_values(b, pt_ref, qp, gate, kn_ref, vn_ref, slope, lq, cv_ref, s_ref, sc_ref,
                                  idx_v, idx_s, idx_sem, vbuf_ref, vsem, ph_ref, res_ref):
    S = MOBA_BLOCK
    n_blocks = PAST_LEN // S
    ppb = S // PAGE_SIZE
    nq = N_HEADS * DEC_SEQ
    ktop = min(MOBA_TOPK, n_blocks)
    lane = lax.broadcasted_iota(I32, (1, LANES), 1)
    head = lax.broadcasted_iota(I32, (DEC_SEQ, ATTN_WIDTH), 1) // HEAD_DIM

    gate_t = jnp.concatenate([gate, jnp.zeros((LANES - nq, LANES), F32)], axis=0).T[0:n_blocks]
    blk_i = lax.broadcasted_iota(I32, (n_blocks, LANES), 0)
    cnt = jnp.zeros((n_blocks, LANES), I32)
    for m in range(n_blocks):
        gm = gate_t[m:m + 1, :]
        beats = (gm > gate_t) | ((gm == gate_t) & (m < blk_i))
        cnt = cnt + jnp.where(beats, 1, 0)
    sel_t = cnt < ktop
    row8 = lax.broadcasted_iota(I32, (8, LANES), 0)
    idx_tile = jnp.zeros((8, LANES), I32)
    left = sel_t
    for j in range(ktop):
        pick = jnp.min(jnp.where(left, blk_i, n_blocks), axis=0, keepdims=True)
        idx_tile = jnp.where(row8 == j, pick, idx_tile)
        left = left & (blk_i != pick)
    idx_v[...] = idx_tile
    idx_copy = pltpu.make_async_copy(idx_v, idx_s, idx_sem)
    idx_copy.start()

    kn = jnp.concatenate([kn_ref[0], jnp.zeros((LANES - DEC_SEQ, ATTN_WIDTH), F32)], axis=0)
    lane_f = lane.astype(F32)
    sn = _dot_nt(qp, kn.astype(BF16)) + slope * (lane_f - lq)
    sn = jnp.where(lane_f <= lq, sn, NEG)

    idx_copy.wait()
    span = ktop * S
    for h in range(N_HEADS):
        for l in range(DEC_SEQ):
            q_i = h * DEC_SEQ + l
            for j in range(ktop):
                n = idx_s[j, q_i]
                for half in range(ppb):
                    page = pt_ref[b, n * ppb + half]
                    col = l * span + j * S + half * PAGE_SIZE
                    pltpu.make_async_copy(cv_ref.at[page, pl.ds(h * HEAD_DIM, HEAD_DIM), :],
                                          vbuf_ref.at[h, :, pl.ds(col, PAGE_SIZE)], vsem).start()
                sc_ref[q_i:q_i + 1, j * S:(j + 1) * S] = s_ref[n, q_i:q_i + 1, :]

    sc = sc_ref[...]
    mx = jnp.maximum(jnp.max(sc, axis=-1, keepdims=True), jnp.max(sn, axis=-1, keepdims=True))
    p = jnp.exp(sc - mx)
    pn = jnp.exp(sn - mx)
    linv = 1.0 / (jnp.sum(p, axis=-1, keepdims=True) + jnp.sum(pn, axis=-1, keepdims=True))
    pw = p * linv
    for h in range(N_HEADS):
        for l in range(DEC_SEQ):
            q_i = h * DEC_SEQ + l
            ph_ref[h, l:l + 1, l * span:(l + 1) * span] = pw[q_i:q_i + 1, :]

    vn = jnp.concatenate([vn_ref[0], jnp.zeros((LANES - DEC_SEQ, ATTN_WIDTH), F32)], axis=0)
    o_new = _dot(pn.astype(BF16), vn.astype(BF16)) * linv
    res = jnp.zeros((DEC_SEQ, ATTN_WIDTH), F32)
    for h in range(N_HEADS):
        res = res + jnp.where(head == h, o_new[h * DEC_SEQ:(h + 1) * DEC_SEQ, :], 0.0)
    res_ref[...] = res


def _moba_sample(page_table, q, kn, vn, slope_col, lq_col, cache_kt, cache_vt):
    P = PAGES_PER_STEP
    chunk = P * PAGE_SIZE
    n_chunks = PAST_LEN // chunk
    n_blocks = PAST_LEN // MOBA_BLOCK
    nq = N_HEADS * DEC_SEQ
    fetched = DEC_SEQ * min(MOBA_TOPK, n_blocks) * MOBA_BLOCK
    row3 = lambda b, pt: (jnp.minimum(b, DEC_BATCH - 1), 0, 0)
    prev3 = lambda b, pt: (jnp.maximum(b - 1, 0), 0, 0)
    const2 = lambda b, pt: (0, 0)
    grid_spec = pltpu.PrefetchScalarGridSpec(
        num_scalar_prefetch=1,
        grid=(DEC_BATCH + 1,),
        in_specs=[pl.BlockSpec((1, DEC_SEQ, ATTN_WIDTH), row3),
                  pl.BlockSpec((1, DEC_SEQ, ATTN_WIDTH), row3),
                  pl.BlockSpec((1, DEC_SEQ, ATTN_WIDTH), row3),
                  pl.BlockSpec((nq, 1), const2),
                  pl.BlockSpec((nq, 1), const2),
                  pl.BlockSpec(memory_space=pl.ANY),
                  pl.BlockSpec(memory_space=pl.ANY)],
        out_specs=pl.BlockSpec((1, DEC_SEQ, ATTN_WIDTH), prev3),
        scratch_shapes=[pltpu.VMEM((CHUNK_SLOTS, P, ATTN_WIDTH, PAGE_SIZE), F32),
                        pltpu.SemaphoreType.DMA((CHUNK_SLOTS,)),
                        pltpu.VMEM((n_blocks, nq, MOBA_BLOCK), F32),
                        pltpu.VMEM((nq, fetched // DEC_SEQ), F32),
                        pltpu.VMEM((ATTN_WIDTH, chunk), BF16),
                        pltpu.VMEM((8, LANES), I32),
                        pltpu.SMEM((8, LANES), I32),
                        pltpu.SemaphoreType.DMA(()),
                        pltpu.VMEM((N_HEADS, HEAD_DIM, fetched), F32),
                        pltpu.SemaphoreType.DMA(()),
                        pltpu.VMEM((LANES, fetched), BF16),
                        pltpu.VMEM((N_HEADS, 2 * DEC_SEQ, fetched), F32),
                        pltpu.VMEM((nq, LANES), F32),
                        pltpu.VMEM((DEC_SEQ, ATTN_WIDTH), F32)])
    return pl.pallas_call(
        _moba_sample_kernel,
        grid_spec=grid_spec,
        out_shape=jax.ShapeDtypeStruct((DEC_BATCH, DEC_SEQ, ATTN_WIDTH), BF16),
        compiler_params=pltpu.CompilerParams(dimension_semantics=("arbitrary",), vmem_limit_bytes=VMEM_LIMIT),
        name="moba_sample",
    )(page_table, q, kn, vn, slope_col, lq_col, cache_kt, cache_vt)


def _mod_rows(ref):
    return ref[...].reshape(-1, ref.shape[-1])


def _outproj_kernel(yp_ref, ya_ref, x_ref, gt1_ref, sh2_ref, sc2_ref, g2_ref, wout_ref, wr_ref, wrhi_ref, br_ref,
                    tri_ref, cntin_ref, x1_ref, h2_ref, meta_ref, ew_ref, cntout_ref, cnt_ref):
    rows = x_ref.shape[0]

    @pl.when(pl.program_id(0) == 0)
    def _():
        cnt_ref[...] = cntin_ref[...]

    mix = _dot(yp_ref[...], wout_ref[0:POOL_WIDTH, :]) + _dot(ya_ref[...], wout_ref[POOL_WIDTH:, :])
    x1 = x_ref[...] + _mod_rows(gt1_ref) * mix
    x1_ref[...] = x1
    h2 = _rms_mod(x1, g2_ref[...], _mod_rows(sc2_ref), _mod_rows(sh2_ref))
    h2_ref[...] = h2
    hh = h2.astype(BF16)
    hl = (h2 - hh.astype(F32)).astype(BF16)
    both = _dot(hh, wr_ref[...])
    lt = (both[:, 0:LANES] + both[:, LANES:] + _dot(hl, wrhi_ref[...])).T + br_ref[...]
    row8 = lax.broadcasted_iota(I32, (8, rows), 0)
    g8 = lt[0:8]
    gmax = jnp.max(g8, axis=0, keepdims=True)
    gsum = jnp.sum(jnp.exp(g8 - gmax), axis=0, keepdims=True)
    g_w = 1.0 / gsum
    g_idx = jnp.min(jnp.where(g8 == gmax, row8, 8), axis=0, keepdims=True)
    e_in = jnp.zeros((8, rows), F32)
    for g in range(N_GROUPS):
        e_in = e_in + jnp.where(g_idx == g, lt[8 + 8 * g:16 + 8 * g], 0.0)
    m1 = jnp.max(e_in, axis=0, keepdims=True)
    i1 = jnp.min(jnp.where(e_in == m1, row8, 8), axis=0, keepdims=True)
    e_rest = jnp.where(row8 == i1, NEG, e_in)
    m2 = jnp.max(e_rest, axis=0, keepdims=True)
    i2 = jnp.min(jnp.where(e_rest == m2, row8, 8), axis=0, keepdims=True)
    r = jnp.exp(m2 - m1)
    w1 = g_w / (1.0 + r)
    w2 = g_w * r / (1.0 + r)
    e1 = g_idx * EXPERTS_PER_GROUP + i1
    e2 = g_idx * EXPERTS_PER_GROUP + i2
    rowe = lax.broadcasted_iota(I32, (N_EXPERTS, rows), 0)
    oh1 = (rowe == e1).astype(F32)
    oh2 = (rowe == e2).astype(F32)
    both_oh = oh1 + oh2
    before = _dot(both_oh.astype(BF16), tri_ref[...]) + cnt_ref[:, 0:1]
    rank1 = jnp.sum(oh1 * before, axis=0, keepdims=True).astype(I32)
    rank2 = jnp.sum(oh2 * before, axis=0, keepdims=True).astype(I32)
    cnt_ref[...] = cnt_ref[...] + jnp.sum(both_oh, axis=-1, keepdims=True)
    cntout_ref[...] = cnt_ref[...]
    meta_ref[...] = jnp.where(row8 == 0, e1, jnp.where(row8 == 1, e2, jnp.where(row8 == 2, rank1,
                                                                                 jnp.where(row8 == 3, rank2, 0))))
    ew_ref[...] = jnp.where(row8 == 0, w1, jnp.where(row8 == 1, w2, 0.0))


def _mod_spec(tile, per_batch_rows):
    if per_batch_rows is None:
        return pl.BlockSpec((tile, D_MODEL), lambda i: (i, 0))
    per = per_batch_rows // tile
    return pl.BlockSpec((1, 1, D_MODEL), lambda i: (i // per, 0, 0))


def _outproj(yp, ya, x, gt1, sh2, sc2, g2, wout, wr, wrhi, br, cnt_in, per_batch_rows):
    n = x.shape[0]
    tile = min(ROW_TILE, n)
    tok = lambda i: (i, 0)
    const2 = lambda i: (0, 0)
    mod_spec = _mod_spec(tile, per_batch_rows)
    tri = (jnp.arange(tile)[:, None] < jnp.arange(tile)[None, :]).astype(BF16)
    return pl.pallas_call(
        _outproj_kernel,
        grid=(n // tile,),
        in_specs=[pl.BlockSpec((tile, POOL_WIDTH), tok),
                  pl.BlockSpec((tile, ATTN_WIDTH), tok),
                  pl.BlockSpec((tile, D_MODEL), tok),
                  mod_spec, mod_spec, mod_spec,
                  pl.BlockSpec((1, D_MODEL), const2),
                  pl.BlockSpec((2 * POOL_WIDTH, D_MODEL), const2),
                  pl.BlockSpec((D_MODEL, 2 * LANES), const2),
                  pl.BlockSpec((D_MODEL, LANES), const2),
                  pl.BlockSpec((LANES, 1), const2),
                  pl.BlockSpec((tile, tile), const2),
                  pl.BlockSpec((N_EXPERTS, LANES), const2)],
        out_specs=[pl.BlockSpec((tile, D_MODEL), tok),
                   pl.BlockSpec((tile, D_MODEL), tok),
                   pl.BlockSpec((8, tile), lambda i: (0, i)),
                   pl.BlockSpec((8, tile), lambda i: (0, i)),
                   pl.BlockSpec((N_EXPERTS, LANES), const2)],
        out_shape=[jax.ShapeDtypeStruct((n, D_MODEL), F32),
                   jax.ShapeDtypeStruct((n, D_MODEL), F32),
                   jax.ShapeDtypeStruct((8, n), I32),
                   jax.ShapeDtypeStruct((8, n), F32),
                   jax.ShapeDtypeStruct((N_EXPERTS, LANES), F32)],
        scratch_shapes=[pltpu.VMEM((N_EXPERTS, LANES), F32)],
        compiler_params=pltpu.CompilerParams(dimension_semantics=("arbitrary",), vmem_limit_bytes=VMEM_LIMIT),
        name="outproj",
    )(yp, ya, x, gt1, sh2, sc2, g2, wout, wr, wrhi, br, tri, cnt_in)


def _dest_kernel(pstart_ref, meta_ref, dest_ref):
    n = meta_ref.shape[1]
    eid = meta_ref[0:EXPERT_TOPK, :]
    start = jnp.zeros((EXPERT_TOPK, n), I32)
    for e in range(N_EXPERTS):
        start = jnp.where(eid == e, pstart_ref[e], start)
    dest_ref[...] = start + meta_ref[EXPERT_TOPK:2 * EXPERT_TOPK, :]


def _dest_rows(pstart, meta):
    n = meta.shape[1]
    dest = pl.pallas_call(
        _dest_kernel,
        in_specs=[pl.BlockSpec(memory_space=pltpu.SMEM), pl.BlockSpec(memory_space=pltpu.VMEM)],
        out_specs=pl.BlockSpec(memory_space=pltpu.VMEM),
        out_shape=jax.ShapeDtypeStruct((EXPERT_TOPK, n), I32),
        name="dest",
    )(pstart, meta)
    return [dest[slot] for slot in range(EXPERT_TOPK)]


DMA_ISSUE_UNROLL = 16


SUBLANES = 8
PAD_CHUNKS = (128, 64, 32, 16, 8)


def _scatter_rows(dest_refs, h_ref, xr_ref, sem):
    rows = h_ref.shape[0]

    def issue(t, carry):
        for d_ref in dest_refs:
            pltpu.make_async_copy(h_ref.at[pl.ds(t, 1), :], xr_ref.at[pl.ds(d_ref[t], 1), :], sem).start()
        return carry

    lax.fori_loop(0, rows, issue, 0, unroll=DMA_ISSUE_UNROLL)
    for _ in dest_refs:
        pltpu.make_async_copy(h_ref, xr_ref.at[pl.ds(0, rows), :], sem).wait()


def _dispatch_kernel(*refs):
    dest_refs = refs[0:EXPERT_TOPK]
    pstart_ref, cnt_ref, nact_ref, hp_ref, hs_ref, xr_ref, z_ref, sem = refs[EXPERT_TOPK:]
    i = pl.program_id(0)
    last = pl.num_programs(0) - 1

    @pl.when(i < last)
    def _():
        _scatter_rows(dest_refs, hp_ref, xr_ref, sem)

    @pl.when(i == last)
    def _():
        _scatter_rows(dest_refs, hs_ref, xr_ref, sem)
        z_ref[...] = jnp.zeros_like(z_ref)
        _zero_unassigned_rows(pstart_ref, cnt_ref, nact_ref, xr_ref, z_ref, sem)


def _dispatch_rows(dests, pstart, counts, n_active, h2_p, h2_s, n_rows):
    n_p = h2_p.shape[0]
    assert n_p % ROW_TILE == 0 and h2_s.shape[0] <= ROW_TILE
    p_tiles = n_p // ROW_TILE
    smem_rows = pl.BlockSpec((ROW_TILE,), lambda i: (i,), memory_space=pltpu.SMEM)
    smem = pl.BlockSpec(memory_space=pltpu.SMEM)
    dests = [jnp.pad(d, (0, (p_tiles + 1) * ROW_TILE - d.shape[0])) for d in dests]
    return pl.pallas_call(
        _dispatch_kernel,
        grid=(p_tiles + 1,),
        in_specs=[smem_rows] * EXPERT_TOPK + [
            smem, smem, smem,
            pl.BlockSpec((ROW_TILE, D_MODEL), lambda i: (jnp.minimum(i, p_tiles - 1), 0)),
            pl.BlockSpec(h2_s.shape, lambda i: (0, 0))],
        out_specs=pl.BlockSpec(memory_space=pl.ANY),
        out_shape=jax.ShapeDtypeStruct((n_rows, D_MODEL), F32),
        scratch_shapes=[pltpu.VMEM((MOE_ROWS, D_MODEL), F32), pltpu.SemaphoreType.DMA(())],
        compiler_params=pltpu.CompilerParams(dimension_semantics=("arbitrary",), vmem_limit_bytes=VMEM_LIMIT),
        name="dispatch",
    )(*dests, pstart, counts, n_active, h2_p, h2_s)


def _zero_unassigned_rows(pstart_ref, cnt_ref, nact_ref, xr_ref, z_ref, sem):
    assert MOE_ROWS - 1 == SUBLANES - 1 + sum(PAD_CHUNKS)
    n_blocks = xr_ref.shape[0] // MOE_ROWS

    def trailing(act):
        def body(blk, carry):
            dst = xr_ref.at[pl.ds(pl.multiple_of(blk * MOE_ROWS, MOE_ROWS), MOE_ROWS), :]
            act(pltpu.make_async_copy(z_ref, dst, sem))
            return carry
        lax.fori_loop(nact_ref[0], n_blocks, body, 0)

    def copies(act):
        trailing(act)
        for e in range(N_EXPERTS):
            cnt = cnt_ref[e]
            first = pstart_ref[e] + cnt
            n_pad = (MOE_ROWS - (cnt & (MOE_ROWS - 1))) & (MOE_ROWS - 1)
            head = (SUBLANES - (first & (SUBLANES - 1))) & (SUBLANES - 1)
            head = jnp.minimum(head, n_pad)
            for k in range(SUBLANES - 1):
                @pl.when(k < head)
                def _():
                    act(pltpu.make_async_copy(z_ref.at[pl.ds(0, 1), :], xr_ref.at[pl.ds(first + k, 1), :], sem))
            start = first + head
            body = n_pad - head
            for rows in PAD_CHUNKS:
                @pl.when((body & rows) != 0)
                def _():
                    dst = xr_ref.at[pl.ds(pl.multiple_of(start, SUBLANES), rows), :]
                    act(pltpu.make_async_copy(z_ref.at[pl.ds(0, rows), :], dst, sem))
                start = start + (body & rows)

    copies(lambda c: c.start())
    copies(lambda c: c.wait())


def _moe_kernel(blk_e_ref, nact_ref, x_ref, wg_ref, wu_ref, wd_ref, y_ref, wgb_ref, wub_ref, wdb_ref):
    i = pl.program_id(0)
    prev = blk_e_ref[jnp.maximum(i - 1, 0)]
    active = i < nact_ref[0]

    @pl.when(active & ((i == 0) | (blk_e_ref[i] != prev)))
    def _():
        wgb_ref[...] = wg_ref[0].astype(BF16)
        wub_ref[...] = wu_ref[0].astype(BF16)
        wdb_ref[...] = wd_ref[0].astype(BF16)

    @pl.when(active)
    def _():
        x = x_ref[...].astype(BF16)
        g = _dot(x, wgb_ref[...])
        u = _dot(x, wub_ref[...])
        a = g / (1.0 + jnp.exp(-g)) * u
        y_ref[...] = _dot(a.astype(BF16), wdb_ref[...])

    @pl.when(jnp.logical_not(active))
    def _():
        y_ref[...] = jnp.zeros_like(y_ref)


def _moe(blk_e, n_active, xr, w_gate, w_up, w_down):
    n_rows = xr.shape[0]
    n_blocks = n_rows // MOE_ROWS
    grid_spec = pltpu.PrefetchScalarGridSpec(
        num_scalar_prefetch=2,
        grid=(n_blocks,),
        in_specs=[pl.BlockSpec((MOE_ROWS, D_MODEL), lambda i, be, na: (jnp.minimum(i, na[0] - 1), 0)),
                  pl.BlockSpec((1, D_MODEL, D_FF), lambda i, be, na: (be[i], 0, 0)),
                  pl.BlockSpec((1, D_MODEL, D_FF), lambda i, be, na: (be[i], 0, 0)),
                  pl.BlockSpec((1, D_FF, D_MODEL), lambda i, be, na: (be[i], 0, 0))],
        out_specs=pl.BlockSpec((MOE_ROWS, D_MODEL), lambda i, be, na: (i, 0)),
        scratch_shapes=[pltpu.VMEM((D_MODEL, D_FF), BF16),
                        pltpu.VMEM((D_MODEL, D_FF), BF16),
                        pltpu.VMEM((D_FF, D_MODEL), BF16)])
    return pl.pallas_call(
        _moe_kernel,
        grid_spec=grid_spec,
        out_shape=jax.ShapeDtypeStruct((n_rows, D_MODEL), F32),
        compiler_params=pltpu.CompilerParams(
            dimension_semantics=("arbitrary",), vmem_limit_bytes=VMEM_LIMIT),
        name="moe",
    )(blk_e, n_active, xr, w_gate, w_up, w_down)


def _final_kernel(*refs):
    dest_refs = refs[0:EXPERT_TOPK]
    x1_ref, w_ref, gt2_ref, yr_ref, y_ref, o_ref, sem = refs[EXPERT_TOPK:]
    rows = x1_ref.shape[0]

    def issue(t, carry):
        for slot, d_ref in enumerate(dest_refs):
            pltpu.make_async_copy(yr_ref.at[pl.ds(d_ref[t], 1), :], o_ref.at[slot, pl.ds(t, 1), :], sem).start()
        return carry

    lax.fori_loop(0, rows, issue, 0, unroll=DMA_ISSUE_UNROLL)
    for slot in range(EXPERT_TOPK):
        pltpu.make_async_copy(yr_ref.at[pl.ds(0, rows), :], o_ref.at[slot], sem).wait()
    w = w_ref[...]
    moe = o_ref[0] * w[:, 0:1] + o_ref[1] * w[:, 1:2]
    y_ref[...] = x1_ref[...] + _mod_rows(gt2_ref) * moe


def _final(dests, x1, w2, gt2, yr, per_batch_rows):
    n = x1.shape[0]
    tile = min(ROW_TILE, n)
    tok = lambda i: (i, 0)
    smem_rows = pl.BlockSpec((tile,), lambda i: (i,), memory_space=pltpu.SMEM)
    return pl.pallas_call(
        _final_kernel,
        grid=(n // tile,),
        in_specs=[smem_rows] * EXPERT_TOPK + [pl.BlockSpec((tile, D_MODEL), tok),
                                             pl.BlockSpec((tile, EXPERT_TOPK), tok),
                                             _mod_spec(tile, per_batch_rows),
                                             pl.BlockSpec(memory_space=pl.ANY)],
        out_specs=pl.BlockSpec((tile, D_MODEL), tok),
        out_shape=jax.ShapeDtypeStruct((n, D_MODEL), F32),
        scratch_shapes=[pltpu.VMEM((EXPERT_TOPK, tile, D_MODEL), F32),
                        pltpu.SemaphoreType.DMA(())],
        compiler_params=pltpu.CompilerParams(dimension_semantics=("arbitrary",), vmem_limit_bytes=VMEM_LIMIT),
        name="final",
    )(*dests, x1, w2, gt2, yr)


def _expert_layout(counts, n_blocks):
    padded = (counts + MOE_ROWS - 1) // MOE_ROWS * MOE_ROWS
    pend = jnp.cumsum(padded)
    pstart = (pend - padded).astype(I32)
    blk_start = jnp.arange(n_blocks, dtype=I32) * MOE_ROWS
    blk_e = jnp.minimum(jnp.sum((pend[None, :] <= blk_start[:, None]).astype(I32), axis=1), N_EXPERTS - 1)
    n_active = (pend[-1] // MOE_ROWS).astype(I32).reshape(1)
    return pstart, blk_e.astype(I32), n_active


def kernel(x_prompt, x_sample, cache_k, cache_v, state_pool, page_table, c_prompt, c_sample, w_ada, b_ada,
           g_attn_norm, w_in, g_q, g_k, w_pool, pool_scale, w_out, g_ffn_norm, w_group, b_group, w_expert,
           b_expert, w_gate, w_up, w_down):
    D = D_MODEL
    B, T, _ = x_prompt.shape
    n_s = DEC_BATCH * DEC_SEQ
    n_p = B * T
    layer = 0

    win = w_in[layer].astype(BF16)
    wout = w_out[layer].astype(BF16)
    wpool = w_pool[layer].astype(BF16)
    g1 = g_attn_norm[layer].reshape(1, D)
    g2 = g_ffn_norm[layer].reshape(1, D)
    gq = jnp.tile(g_q[layer], N_HEADS).reshape(1, ATTN_WIDTH)
    gk = jnp.tile(g_k[layer], N_HEADS).reshape(1, ATTN_WIDTH)
    ps = pool_scale[layer].reshape(1, POOL_WIDTH)
    hd = jnp.arange(ATTN_WIDTH) // HEAD_DIM
    bd = (hd[:, None] == hd[None, :]).astype(BF16)
    slopes = jnp.exp2(-8.0 * (jnp.arange(N_HEADS, dtype=F32) + 1.0) / N_HEADS)
    qrow = jnp.arange(N_HEADS * DEC_SEQ)
    slope_col = slopes[qrow // DEC_SEQ].reshape(-1, 1)
    lq_col = (qrow % DEC_SEQ).astype(F32).reshape(-1, 1)
    wr = jnp.zeros((D, LANES), F32).at[:, 0:N_GROUPS].set(w_group[layer]).at[:, 8:8 + N_EXPERTS].set(w_expert[layer])
    wr_hi = wr.astype(BF16)
    wr_lo = (wr - wr_hi.astype(F32)).astype(BF16)
    wr_both = jnp.concatenate([wr_hi, wr_lo], axis=1)
    br = jnp.zeros((LANES,), F32).at[0:N_GROUPS].set(b_group[layer]).at[N_GROUPS:8].set(NEG)
    br = br.at[8:8 + N_EXPERTS].set(b_expert[layer]).reshape(LANES, 1)

    mod = _ada(jnp.concatenate([c_prompt, c_sample], axis=0), w_ada[layer], b_ada[layer])
    mod_p = mod[:B].reshape(B, 1, 6 * D)
    mod_s = jnp.repeat(mod[B:], DEC_SEQ, axis=0)

    ypool_p, q_p, k_p, v_p, tail_p = _inproj_prompt(x_prompt, mod_p, g1, win, bd, gq, gk, wpool, ps)
    yattn_p = _moba_prompt(slopes, q_p, k_p, v_p)
    mp = lambda j: mod_p[:, :, j * D:(j + 1) * D]
    cnt0 = jnp.zeros((N_EXPERTS, LANES), F32)
    x1_p, h2_p, meta_p, ew_p, cnt_p = _outproj(
        ypool_p.reshape(n_p, POOL_WIDTH), yattn_p.reshape(n_p, ATTN_WIDTH), x_prompt.reshape(n_p, D),
        mp(2), mp(3), mp(4), g2, wout, wr_both, wr_hi, br, cnt0, T)

    ms = lambda j: mod_s[:, j * D:(j + 1) * D]
    hist = jnp.concatenate([jnp.zeros((DEC_BATCH, HALO - POOL_HIST, POOL_WIDTH), F32), state_pool[layer]], axis=1)
    xs = x_sample.reshape(n_s, D)
    ypool_s, q_s, k_s, v_s, tail_s = _inproj_sample(xs, ms(0), ms(1), g1, win, bd, gq, gk, hist, wpool, ps)
    r3 = lambda a: a.reshape(DEC_BATCH, DEC_SEQ, ATTN_WIDTH)
    n_phys = cache_k.shape[1]
    ck = jnp.transpose(cache_k[layer], (0, 2, 3, 1)).reshape(n_phys, ATTN_WIDTH, PAGE_SIZE)
    cv = jnp.transpose(cache_v[layer], (0, 2, 3, 1)).reshape(n_phys, ATTN_WIDTH, PAGE_SIZE)
    yattn_s = _moba_sample(page_table, r3(q_s), r3(k_s), r3(v_s), slope_col, lq_col, ck, cv)
    x1_s, h2_s, meta_s, ew_s, cnt_s = _outproj(ypool_s, yattn_s.reshape(n_s, ATTN_WIDTH), xs, ms(2), ms(3), ms(4),
                                               g2, wout, wr_both, wr_hi, br, cnt_p, None)

    n_blocks = (n_p + n_s) * EXPERT_TOPK // MOE_ROWS + N_EXPERTS
    counts = cnt_s[:, 0].astype(I32)
    pstart, blk_e, n_active = _expert_layout(counts, n_blocks)
    dests_p = _dest_rows(pstart, meta_p)
    dests_s = _dest_rows(pstart, meta_s)
    dests = [jnp.concatenate([dp, ds]) for dp, ds in zip(dests_p, dests_s)]
    xr = _dispatch_rows(dests, pstart, counts, n_active, h2_p, h2_s, n_blocks * MOE_ROWS)
    yr = _moe(blk_e, n_active, xr, w_gate[layer], w_up[layer], w_down[layer])
    y_p = _final(dests_p, x1_p, ew_p[0:EXPERT_TOPK].T, mp(5), yr, T)
    y_s = _final(dests_s, x1_s, ew_s[0:EXPERT_TOPK].T, ms(5), yr, None)

    k4 = lambda a, b, l: a.reshape(1, b, l, N_HEADS, HEAD_DIM)
    return (y_p.reshape(B, T, D), y_s.reshape(DEC_BATCH, DEC_SEQ, D),
            k4(k_p, B, T), k4(v_p, B, T), tail_p[None, :, HALO - POOL_HIST:, :],
            k4(k_s, DEC_BATCH, DEC_SEQ), k4(v_s, DEC_BATCH, DEC_SEQ), tail_s[None, :, HALO - POOL_HIST:, :])
```

```python
import functools

import jax
import jax.numpy as jnp
from jax import lax
from jax.experimental import pallas as pl
from jax.experimental.pallas import tpu as pltpu

F32 = jnp.float32
BF16 = jnp.bfloat16
I32 = jnp.int32

D_MODEL = 1024
BATCH = 8
SEQ = 2048
DEC_BATCH = 32
DEC_SEQ = 8
PAST_LEN = 16384
PAGE_SIZE = 128
POOL_WIDTH = 512
POOL_WINDOWS = (2, 4, 8, 16)
POOL_GROUP = 128
POOL_HIST = 15
HALO = 16
N_HEADS = 8
HEAD_DIM = 64
ATTN_WIDTH = 512
MOBA_BLOCK = 256
MOBA_TOPK = 3
ATTN_SCALE = HEAD_DIM ** -0.5
MIX_IN = POOL_WIDTH + 3 * ATTN_WIDTH
N_GROUPS = 4
EXPERTS_PER_GROUP = 8
N_EXPERTS = 32
EXPERT_TOPK = 2
D_FF = 512
EPS = 1e-6
NEG = -1e30

LANES = 128
ROW_TILE = 512
INPROJ_ROWS = 1024
MOE_ROWS = 256
PAGES_PER_STEP = 16
VMEM_LIMIT = 56 * 1024 * 1024

_NT = (((1,), (1,)), ((), ()))


def _dot(a, b):
    return jnp.dot(a, b, preferred_element_type=F32)


def _dot_nt(a, b):
    return lax.dot_general(a, b, _NT, preferred_element_type=F32)


def _split_dot(a, b01):
    hi = a.astype(BF16)
    lo = (a - hi.astype(F32)).astype(BF16)
    return _dot(hi, b01) + _dot(lo, b01)


def _rms_mod(x, g, sc, sh):
    ms = jnp.mean(x * x, axis=-1, keepdims=True)
    return x * lax.rsqrt(ms + EPS) * g * (1.0 + sc) + sh


def _ada_kernel(c_ref, w_ref, b_ref, o_ref):
    c = c_ref[...]
    a = c / (1.0 + jnp.exp(-c))
    o_ref[...] = _dot(a.astype(BF16), w_ref[...].astype(BF16)) + b_ref[...]


def _ada(c_all, w_ada, b_ada):
    n = c_all.shape[0]
    tn = 1536
    return pl.pallas_call(
        _ada_kernel,
        grid=(6 * D_MODEL // tn,),
        in_specs=[pl.BlockSpec((n, D_MODEL), lambda j: (0, 0)),
                  pl.BlockSpec((D_MODEL, tn), lambda j: (0, j)),
                  pl.BlockSpec((1, tn), lambda j: (0, j))],
        out_specs=pl.BlockSpec((n, tn), lambda j: (0, j)),
        out_shape=jax.ShapeDtypeStruct((n, 6 * D_MODEL), F32),
        compiler_params=pltpu.CompilerParams(vmem_limit_bytes=VMEM_LIMIT),
        name="ada",
    )(c_all, w_ada, b_ada.reshape(1, -1))


def _inproj_core(x, sh1, sc1, g, win_ref, bd_ref, gq_ref, gk_ref):
    h = _rms_mod(x, g, sc1, sh1)
    z = _dot(h.astype(BF16), win_ref[...])
    u = z[:, 0:POOL_WIDTH]
    q = z[:, POOL_WIDTH:POOL_WIDTH + ATTN_WIDTH]
    k = z[:, POOL_WIDTH + ATTN_WIDTH:POOL_WIDTH + 2 * ATTN_WIDTH]
    v = z[:, POOL_WIDTH + 2 * ATTN_WIDTH:]
    bd = bd_ref[...]
    q = q * lax.rsqrt(_split_dot(q * q, bd) * (1.0 / HEAD_DIM) + EPS) * gq_ref[...]
    k = k * lax.rsqrt(_split_dot(k * k, bd) * (1.0 / HEAD_DIM) + EPS) * gk_ref[...]
    return u, q, k, v


def _window_sum(e, w):
    s = e
    sh = 1
    while sh < w:
        s = s + pltpu.roll(s, sh, axis=0)
        sh *= 2
    return s


def _inproj_prompt_kernel(x_ref, mod_ref, g_ref, win_ref, bd_ref, gq_ref, gk_ref, wpool_ref, ps_ref,
                          ypool_ref, q_ref, k_ref, v_ref, tail_ref, ext_ref):
    t = pl.program_id(1)
    nt = pl.num_programs(1)
    x = x_ref[0]
    sh1 = mod_ref[0, :, 0:D_MODEL]
    sc1 = mod_ref[0, :, D_MODEL:2 * D_MODEL]
    u, q, k, v = _inproj_core(x, sh1, sc1, g_ref[...], win_ref, bd_ref, gq_ref, gk_ref)
    q_ref[0] = (q * ATTN_SCALE).astype(BF16)
    k_ref[0] = k
    v_ref[0] = v

    @pl.when(t == 0)
    def _():
        ext_ref[0:HALO, :] = jnp.zeros((HALO, POOL_WIDTH), F32)

    ext_ref[HALO:, :] = u
    pos = t * INPROJ_ROWS + lax.broadcasted_iota(I32, (INPROJ_ROWS, 1), 0)
    for gi, w in enumerate(POOL_WINDOWS):
        cols = slice(gi * POOL_GROUP, (gi + 1) * POOL_GROUP)
        win = _window_sum(ext_ref[:, cols], w)[HALO:]
        inv_cnt = 1.0 / jnp.minimum(pos + 1, w).astype(F32)
        pooled = win * inv_cnt - u[:, cols]
        y = _dot(pooled.astype(BF16), wpool_ref[gi]) * ps_ref[:, cols]
        ypool_ref[0, :, cols] = y.astype(BF16)
    last = u[INPROJ_ROWS - HALO:, :]
    ext_ref[0:HALO, :] = last

    @pl.when(t == nt - 1)
    def _():
        tail_ref[0] = last


def _inproj_prompt(x, mod_p, g1, win, bd, gq, gk, wpool, ps):
    B, T, D = x.shape
    nt = T // INPROJ_ROWS
    const2 = lambda b, t: (0, 0)
    tok = lambda b, t: (b, t, 0)
    return pl.pallas_call(
        _inproj_prompt_kernel,
        grid=(B, nt),
        in_specs=[pl.BlockSpec((1, INPROJ_ROWS, D), tok),
                  pl.BlockSpec((1, 1, 6 * D), lambda b, t: (b, 0, 0)),
                  pl.BlockSpec((1, D), const2),
                  pl.BlockSpec((D, MIX_IN), const2),
                  pl.BlockSpec((ATTN_WIDTH, ATTN_WIDTH), const2),
                  pl.BlockSpec((1, ATTN_WIDTH), const2),
                  pl.BlockSpec((1, ATTN_WIDTH), const2),
                  pl.BlockSpec((4, POOL_GROUP, POOL_GROUP), lambda b, t: (0, 0, 0)),
                  pl.BlockSpec((1, POOL_WIDTH), const2)],
        out_specs=[pl.BlockSpec((1, INPROJ_ROWS, POOL_WIDTH), tok),
                   pl.BlockSpec((1, INPROJ_ROWS, ATTN_WIDTH), tok),
                   pl.BlockSpec((1, INPROJ_ROWS, ATTN_WIDTH), tok),
                   pl.BlockSpec((1, INPROJ_ROWS, ATTN_WIDTH), tok),
                   pl.BlockSpec((1, HALO, POOL_WIDTH), lambda b, t: (b, 0, 0))],
        out_shape=[jax.ShapeDtypeStruct((B, T, POOL_WIDTH), BF16),
                   jax.ShapeDtypeStruct((B, T, ATTN_WIDTH), BF16),
                   jax.ShapeDtypeStruct((B, T, ATTN_WIDTH), F32),
                   jax.ShapeDtypeStruct((B, T, ATTN_WIDTH), F32),
                   jax.ShapeDtypeStruct((B, HALO, POOL_WIDTH), F32)],
        scratch_shapes=[pltpu.VMEM((HALO + INPROJ_ROWS, POOL_WIDTH), F32)],
        compiler_params=pltpu.CompilerParams(
            dimension_semantics=("arbitrary", "arbitrary"), vmem_limit_bytes=VMEM_LIMIT),
        name="inproj_prompt",
    )(x, mod_p, g1, win, bd, gq, gk, wpool, ps)


def _inproj_sample_kernel(x_ref, sh_ref, sc_ref, g_ref, win_ref, bd_ref, gq_ref, gk_ref, hist_ref, wpool_ref,
                          ps_ref, ypool_ref, q_ref, k_ref, v_ref, tail_ref, ext_ref):
    n = DEC_BATCH * DEC_SEQ
    ext_rows = HALO + DEC_SEQ
    u, q, k, v = _inproj_core(x_ref[...], sh_ref[...], sc_ref[...], g_ref[...], win_ref, bd_ref, gq_ref, gk_ref)
    q_ref[...] = q * ATTN_SCALE
    k_ref[...] = k
    v_ref[...] = v
    ext_ref[:, 0:HALO, :] = hist_ref[...]
    ext_ref[:, HALO:, :] = u.reshape(DEC_BATCH, DEC_SEQ, POOL_WIDTH)
    tail_ref[...] = ext_ref[:, ext_rows - HALO:, :]
    pos = PAST_LEN + lax.broadcasted_iota(I32, (DEC_BATCH, DEC_SEQ, 1), 1).reshape(n, 1)
    for gi, w in enumerate(POOL_WINDOWS):
        cols = slice(gi * POOL_GROUP, (gi + 1) * POOL_GROUP)
        e = ext_ref[:, :, cols].reshape(DEC_BATCH * ext_rows, POOL_GROUP)
        win = _window_sum(e, w).reshape(DEC_BATCH, ext_rows, POOL_GROUP)[:, HALO:, :].reshape(n, POOL_GROUP)
        inv_cnt = 1.0 / jnp.minimum(pos + 1, w).astype(F32)
        pooled = win * inv_cnt - u[:, cols]
        y = _dot(pooled.astype(BF16), wpool_ref[gi]) * ps_ref[:, cols]
        ypool_ref[:, cols] = y.astype(BF16)


def _inproj_sample(x, sh1, sc1, g1, win, bd, gq, gk, hist, wpool, ps):
    n = x.shape[0]
    return pl.pallas_call(
        _inproj_sample_kernel,
        out_shape=[jax.ShapeDtypeStruct((n, POOL_WIDTH), BF16),
                   jax.ShapeDtypeStruct((n, ATTN_WIDTH), F32),
                   jax.ShapeDtypeStruct((n, ATTN_WIDTH), F32),
                   jax.ShapeDtypeStruct((n, ATTN_WIDTH), F32),
                   jax.ShapeDtypeStruct((DEC_BATCH, HALO, POOL_WIDTH), F32)],
        scratch_shapes=[pltpu.VMEM((DEC_BATCH, HALO + DEC_SEQ, POOL_WIDTH), F32)],
        compiler_params=pltpu.CompilerParams(vmem_limit_bytes=VMEM_LIMIT),
        name="inproj_sample",
    )(x, sh1, sc1, g1, win, bd, gq, gk, hist, wpool, ps)


N_BIAS_LANES = 3
PROMPT_Q_TILE = 2 * MOBA_BLOCK


def _moba_prompt_kernel(slopes_ref, q_ref, k_ref, v_ref, o_ref, ka_ref, va_ref, qa_ref, kmf_ref):
    S = MOBA_BLOCK
    T = k_ref.shape[1]
    nb = T // S
    hp = pl.program_id(1)
    lane = lax.broadcasted_iota(I32, (1, LANES), 1)
    real = (lane < HEAD_DIM, lane >= HEAD_DIM)
    extra = (lane - HEAD_DIM, lane)

    kf = k_ref[0]
    vf = v_ref[0]
    kmf_ref[...] = jnp.zeros((LANES, LANES), F32)
    for j in range(nb):
        mean = jnp.sum(kf[j * S:(j + 1) * S], axis=0, keepdims=True) * (1.0 / S)
        kmf_ref[j:j + 1, :] = jnp.where(real[0], mean, 0.0)
        kmf_ref[nb + j:nb + j + 1, :] = jnp.where(real[0], 0.0, mean)
    key_i = lax.broadcasted_iota(I32, (T, 1), 0)
    key_blk = key_i // S
    for h in range(2):
        b = slopes_ref[2 * hp + h] * key_i.astype(F32)
        p0 = b.astype(BF16).astype(F32)
        p1 = (b - p0).astype(BF16).astype(F32)
        p2 = b - p0 - p1
        e = extra[h]
        onehot = jnp.where((e >= N_BIAS_LANES) & (e - N_BIAS_LANES == key_blk), 1.0, 0.0)
        feat = jnp.where(e == 0, p0, jnp.where(e == 1, p1, jnp.where(e == 2, p2, onehot)))
        ka_ref[h] = jnp.where(real[h], kf, feat).astype(BF16)
        va_ref[h] = jnp.where(real[h], vf, 1.0).astype(BF16)

    q2 = q_ref[0]

    gt = _dot_nt(kmf_ref[...].astype(BF16), q2)[0:2 * nb]
    row = lax.broadcasted_iota(I32, (2 * nb, T), 0)
    blk = row % nb
    cbq = lax.broadcasted_iota(I32, (2 * nb, T), 1) // S
    cnt = jnp.zeros((2 * nb, T), I32)
    for m in range(nb):
        gm = jnp.where(row < nb, gt[m:m + 1, :], gt[nb + m:nb + m + 1, :])
        beats = (gm > gt) | ((gm == gt) & (m < blk))
        cnt = cnt + jnp.where(beats & (m < cbq), 1, 0)
    keep = (((cnt < MOBA_TOPK) & (blk < cbq)) | (blk == cbq)).astype(F32)
    keepq = jnp.concatenate([keep, jnp.zeros((LANES - 2 * nb, T), F32)], axis=0).T
    maskv = jnp.where(keepq > 0.5, 0.0, NEG)
    mask_lanes = (pltpu.roll(maskv, HEAD_DIM + N_BIAS_LANES, axis=1),
                  pltpu.roll(maskv, (N_BIAS_LANES - nb) % LANES, axis=1))
    qf = q2.astype(F32)
    for h in range(2):
        e = extra[h]
        feat = jnp.where(e < N_BIAS_LANES, 1.0, jnp.where(e < N_BIAS_LANES + nb, mask_lanes[h], 0.0))
        qa_ref[h] = jnp.where(real[h], qf, feat).astype(BF16)

    QT = PROMPT_Q_TILE
    causal = lax.broadcasted_iota(I32, (QT, QT), 1) <= lax.broadcasted_iota(I32, (QT, QT), 0)
    for t in range(T // QT):
        rows = slice(t * QT, (t + 1) * QT)
        n = (t + 1) * QT
        outs = []
        for h in range(2):
            s = _dot_nt(qa_ref[h, rows, :], ka_ref[h, 0:n, :])
            s_own = jnp.where(causal, s[:, t * QT:n], NEG)
            s = s_own if t == 0 else jnp.concatenate([s[:, 0:t * QT], s_own], axis=1)
            p = jnp.exp(s - jnp.max(s, axis=-1, keepdims=True))
            outs.append(_dot(p.astype(BF16), va_ref[h, 0:n, :]))
        a0, a1 = outs
        o0 = a0 * (1.0 / a0[:, HEAD_DIM:HEAD_DIM + 1])
        o1 = a1 * (1.0 / a1[:, 0:1])
        o_ref[0, rows, :] = jnp.where(real[0], o0, o1).astype(BF16)


def _moba_prompt(slopes, q, k, v):
    B, T, _ = q.shape
    seq = lambda b, hp: (b, 0, hp)
    return pl.pallas_call(
        _moba_prompt_kernel,
        grid=(B, N_HEADS // 2),
        in_specs=[pl.BlockSpec(memory_space=pltpu.SMEM),
                  pl.BlockSpec((1, T, LANES), seq),
                  pl.BlockSpec((1, T, LANES), seq),
                  pl.BlockSpec((1, T, LANES), seq)],
        out_specs=pl.BlockSpec((1, T, LANES), seq),
        out_shape=jax.ShapeDtypeStruct((B, T, ATTN_WIDTH), BF16),
        scratch_shapes=[pltpu.VMEM((2, T, LANES), BF16),
                        pltpu.VMEM((2, T, LANES), BF16),
                        pltpu.VMEM((2, T, LANES), BF16),
                        pltpu.VMEM((LANES, LANES), F32)],
        compiler_params=pltpu.CompilerParams(
            dimension_semantics=("arbitrary", "arbitrary"), vmem_limit_bytes=VMEM_LIMIT),
        name="moba_prompt",
    )(slopes, q, k, v)


CHUNK_SLOTS = 6


def _moba_sample_kernel(pt_ref, q_ref, kn_ref, vn_ref, slope_ref, lq_ref, ck_ref, cv_ref, o_ref,
                        buf_ref, sem, s_ref, sc_ref, kc_ref, idx_v, idx_s, idx_sem, vbuf_ref, vsem, vb16_ref, ph_ref,
                        gate_ref, res_ref):
    P = PAGES_PER_STEP
    S = MOBA_BLOCK
    chunk = P * PAGE_SIZE
    n_chunks = PAST_LEN // chunk
    bpc = chunk // S
    n_loads = n_chunks
    ahead = CHUNK_SLOTS - 1
    nq = N_HEADS * DEC_SEQ
    b = pl.program_id(0)
    n_rows = pl.num_programs(0) - 1
    slope = slope_ref[...]
    lq = lq_ref[...]
    lane = lax.broadcasted_iota(I32, (1, LANES), 1)

    def slot_of(bb, i):
        return lax.rem(bb * n_loads + i, CHUNK_SLOTS)

    def start_load(bb, i):
        slot = slot_of(bb, i)
        for r in range(P):
            page = pt_ref[bb, i * P + r]
            pltpu.make_async_copy(ck_ref.at[page], buf_ref.at[slot, r], sem.at[slot]).start()

    def load_chunk(i):
        nxt = i + ahead
        if nxt < n_loads:
            start_load(b, nxt)
        else:
            @pl.when(b + 1 < n_rows)
            def _():
                start_load(b + 1, nxt - n_loads)
        slot = slot_of(b, i)
        pltpu.make_async_copy(ck_ref.at[pl.ds(0, P)], buf_ref.at[slot], sem.at[slot]).wait()
        for r in range(P):
            kc_ref[:, r * PAGE_SIZE:(r + 1) * PAGE_SIZE] = buf_ref[slot, r].astype(BF16)
        return kc_ref[...]

    @pl.when(b == 0)
    def _():
        for i in range(ahead):
            start_load(b, i)
        ph_ref[...] = jnp.zeros_like(ph_ref)
        vb16_ref[...] = jnp.zeros_like(vb16_ref)

    head = lax.broadcasted_iota(I32, (DEC_SEQ, ATTN_WIDTH), 1) // HEAD_DIM

    def masked_queries():
        q8 = q_ref[0]
        return jnp.concatenate([jnp.where(head == h, q8, 0.0) for h in range(N_HEADS)], axis=0).astype(BF16)

    @pl.when(b < n_rows)
    def _score_keys():
        qp = masked_queries()
        gate = jnp.zeros((nq, LANES), F32)
        for c in range(n_chunks):
            sc = _dot(qp, load_chunk(c))
            keypos = c * chunk + lax.broadcasted_iota(I32, (1, chunk), 1)
            dist = (keypos - PAST_LEN).astype(F32) - lq
            logits = sc + slope * dist
            for r2 in range(bpc):
                gs = jnp.sum(sc[:, r2 * S:(r2 + 1) * S], axis=-1, keepdims=True) * (1.0 / S)
                gate = jnp.where(lane == c * bpc + r2, gs, gate)
                s_ref[c * bpc + r2] = logits[:, r2 * S:(r2 + 1) * S]
        gate_ref[...] = gate

    @pl.when(b > 0)
    def _finish_previous_row():
        pltpu.make_async_copy(vbuf_ref, vbuf_ref, vsem).wait()
        pieces = []
        for h in range(N_HEADS):
            vb16_ref[0:HEAD_DIM, :] = vbuf_ref[h].astype(BF16)
            pieces.append(_dot_nt(ph_ref[h].astype(BF16), vb16_ref[...])[0:DEC_SEQ])
        pairs = [pieces[2 * i] + pltpu.roll(pieces[2 * i + 1], HEAD_DIM, axis=1) for i in range(N_HEADS // 2)]
        o_ref[0] = (res_ref[...] + jnp.concatenate(pairs, axis=1)).astype(BF16)

    @pl.when(b < n_rows)
    def _pick_and_fetch():
        _pick_blocks_and_fetch_values(
            b, pt_ref, masked_queries(), gate_ref[...], kn_ref, vn_ref, slope, lq, cv_ref, s_ref, sc_ref,
            idx_v, idx_s, idx_sem, vbuf_ref, vsem, ph_ref, res_ref)


def _pick_blocks_and_fetch_values(b, pt_ref, qp, gate, kn_ref, vn_ref, slope, lq, cv_ref, s_ref, sc_ref,
                                  idx_v, idx_s, idx_sem, vbuf_ref, vsem, ph_ref, res_ref):
    S = MOBA_BLOCK
    n_blocks = PAST_LEN // S
    ppb = S // PAGE_SIZE
    nq = N_HEADS * DEC_SEQ
    ktop = min(MOBA_TOPK, n_blocks)
    lane = lax.broadcasted_iota(I32, (1, LANES), 1)
    head = lax.broadcasted_iota(I32, (DEC_SEQ, ATTN_WIDTH), 1) // HEAD_DIM

    gate_t = jnp.concatenate([gate, jnp.zeros((LANES - nq, LANES), F32)], axis=0).T[0:n_blocks]
    blk_i = lax.broadcasted_iota(I32, (n_blocks, LANES), 0)
    cnt = jnp.zeros((n_blocks, LANES), I32)
    for m in range(n_blocks):
        gm = gate_t[m:m + 1, :]
        beats = (gm > gate_t) | ((gm == gate_t) & (m < blk_i))
        cnt = cnt + jnp.where(beats, 1, 0)
    sel_t = cnt < ktop
    row8 = lax.broadcasted_iota(I32, (8, LANES), 0)
    idx_tile = jnp.zeros((8, LANES), I32)
    left = sel_t
    for j in range(ktop):
        pick = jnp.min(jnp.where(left, blk_i, n_blocks), axis=0, keepdims=True)
        idx_tile = jnp.where(row8 == j, pick, idx_tile)
        left = left & (blk_i != pick)
    idx_v[...] = idx_tile
    idx_copy = pltpu.make_async_copy(idx_v, idx_s, idx_sem)
    idx_copy.start()

    kn = jnp.concatenate([kn_ref[0], jnp.zeros((LANES - DEC_SEQ, ATTN_WIDTH), F32)], axis=0)
    lane_f = lane.astype(F32)
    sn = _dot_nt(qp, kn.astype(BF16)) + slope * (lane_f - lq)
    sn = jnp.where(lane_f <= lq, sn, NEG)

    idx_copy.wait()
    span = ktop * S
    for h in range(N_HEADS):
        for l in range(DEC_SEQ):
            q_i = h * DEC_SEQ + l
            for j in range(ktop):
                n = idx_s[j, q_i]
                for half in range(ppb):
                    page = pt_ref[b, n * ppb + half]
                    col = l * span + j * S + half * PAGE_SIZE
                    pltpu.make_async_copy(cv_ref.at[page, pl.ds(h * HEAD_DIM, HEAD_DIM), :],
                                          vbuf_ref.at[h, :, pl.ds(col, PAGE_SIZE)], vsem).start()
                sc_ref[q_i:q_i + 1, j * S:(j + 1) * S] = s_ref[n, q_i:q_i + 1, :]

    sc = sc_ref[...]
    mx = jnp.maximum(jnp.max(sc, axis=-1, keepdims=True), jnp.max(sn, axis=-1, keepdims=True))
    p = jnp.exp(sc - mx)
    pn = jnp.exp(sn - mx)
    linv = 1.0 / (jnp.sum(p, axis=-1, keepdims=True) + jnp.sum(pn, axis=-1, keepdims=True))
    pw = p * linv
    for h in range(N_HEADS):
        for l in range(DEC_SEQ):
            q_i = h * DEC_SEQ + l
            ph_ref[h, l:l + 1, l * span:(l + 1) * span] = pw[q_i:q_i + 1, :]

    vn = jnp.concatenate([vn_ref[0], jnp.zeros((LANES - DEC_SEQ, ATTN_WIDTH), F32)], axis=0)
    o_new = _dot(pn.astype(BF16), vn.astype(BF16)) * linv
    res = jnp.zeros((DEC_SEQ, ATTN_WIDTH), F32)
    for h in range(N_HEADS):
        res = res + jnp.where(head == h, o_new[h * DEC_SEQ:(h + 1) * DEC_SEQ, :], 0.0)
    res_ref[...] = res


def _moba_sample(page_table, q, kn, vn, slope_col, lq_col, cache_kt, cache_vt):
    P = PAGES_PER_STEP
    chunk = P * PAGE_SIZE
    n_chunks = PAST_LEN // chunk
    n_blocks = PAST_LEN // MOBA_BLOCK
    nq = N_HEADS * DEC_SEQ
    fetched = DEC_SEQ * min(MOBA_TOPK, n_blocks) * MOBA_BLOCK
    row3 = lambda b, pt: (jnp.minimum(b, DEC_BATCH - 1), 0, 0)
    prev3 = lambda b, pt: (jnp.maximum(b - 1, 0), 0, 0)
    const2 = lambda b, pt: (0, 0)
    grid_spec = pltpu.PrefetchScalarGridSpec(
        num_scalar_prefetch=1,
        grid=(DEC_BATCH + 1,),
        in_specs=[pl.BlockSpec((1, DEC_SEQ, ATTN_WIDTH), row3),
                  pl.BlockSpec((1, DEC_SEQ, ATTN_WIDTH), row3),
                  pl.BlockSpec((1, DEC_SEQ, ATTN_WIDTH), row3),
                  pl.BlockSpec((nq, 1), const2),
                  pl.BlockSpec((nq, 1), const2),
                  pl.BlockSpec(memory_space=pl.ANY),
                  pl.BlockSpec(memory_space=pl.ANY)],
        out_specs=pl.BlockSpec((1, DEC_SEQ, ATTN_WIDTH), prev3),
        scratch_shapes=[pltpu.VMEM((CHUNK_SLOTS, P, ATTN_WIDTH, PAGE_SIZE), F32),
                        pltpu.SemaphoreType.DMA((CHUNK_SLOTS,)),
                        pltpu.VMEM((n_blocks, nq, MOBA_BLOCK), F32),
                        pltpu.VMEM((nq, fetched // DEC_SEQ), F32),
                        pltpu.VMEM((ATTN_WIDTH, chunk), BF16),
                        pltpu.VMEM((8, LANES), I32),
                        pltpu.SMEM((8, LANES), I32),
                        pltpu.SemaphoreType.DMA(()),
                        pltpu.VMEM((N_HEADS, HEAD_DIM, fetched), F32),
                        pltpu.SemaphoreType.DMA(()),
                        pltpu.VMEM((LANES, fetched), BF16),
                        pltpu.VMEM((N_HEADS, 2 * DEC_SEQ, fetched), F32),
                        pltpu.VMEM((nq, LANES), F32),
                        pltpu.VMEM((DEC_SEQ, ATTN_WIDTH), F32)])
    return pl.pallas_call(
        _moba_sample_kernel,
        grid_spec=grid_spec,
        out_shape=jax.ShapeDtypeStruct((DEC_BATCH, DEC_SEQ, ATTN_WIDTH), BF16),
        compiler_params=pltpu.CompilerParams(dimension_semantics=("arbitrary",), vmem_limit_bytes=VMEM_LIMIT),
        name="moba_sample",
    )(page_table, q, kn, vn, slope_col, lq_col, cache_kt, cache_vt)


def _mod_rows(ref):
    return ref[...].reshape(-1, ref.shape[-1])


def _outproj_kernel(yp_ref, ya_ref, x_ref, gt1_ref, sh2_ref, sc2_ref, g2_ref, wout_ref, wr_ref, wrhi_ref, br_ref,
                    tri_ref, cntin_ref, x1_ref, h2_ref, meta_ref, ew_ref, cntout_ref, cnt_ref):
    rows = x_ref.shape[0]

    @pl.when(pl.program_id(0) == 0)
    def _():
        cnt_ref[...] = cntin_ref[...]

    mix = _dot(yp_ref[...], wout_ref[0:POOL_WIDTH, :]) + _dot(ya_ref[...], wout_ref[POOL_WIDTH:, :])
    x1 = x_ref[...] + _mod_rows(gt1_ref) * mix
    x1_ref[...] = x1
    h2 = _rms_mod(x1, g2_ref[...], _mod_rows(sc2_ref), _mod_rows(sh2_ref))
    h2_ref[...] = h2
    hh = h2.astype(BF16)
    hl = (h2 - hh.astype(F32)).astype(BF16)
    both = _dot(hh, wr_ref[...])
    lt = (both[:, 0:LANES] + both[:, LANES:] + _dot(hl, wrhi_ref[...])).T + br_ref[...]
    row8 = lax.broadcasted_iota(I32, (8, rows), 0)
    g8 = lt[0:8]
    gmax = jnp.max(g8, axis=0, keepdims=True)
    gsum = jnp.sum(jnp.exp(g8 - gmax), axis=0, keepdims=True)
    g_w = 1.0 / gsum
    g_idx = jnp.min(jnp.where(g8 == gmax, row8, 8), axis=0, keepdims=True)
    e_in = jnp.zeros((8, rows), F32)
    for g in range(N_GROUPS):
        e_in = e_in + jnp.where(g_idx == g, lt[8 + 8 * g:16 + 8 * g], 0.0)
    m1 = jnp.max(e_in, axis=0, keepdims=True)
    i1 = jnp.min(jnp.where(e_in == m1, row8, 8), axis=0, keepdims=True)
    e_rest = jnp.where(row8 == i1, NEG, e_in)
    m2 = jnp.max(e_rest, axis=0, keepdims=True)
    i2 = jnp.min(jnp.where(e_rest == m2, row8, 8), axis=0, keepdims=True)
    r = jnp.exp(m2 - m1)
    w1 = g_w / (1.0 + r)
    w2 = g_w * r / (1.0 + r)
    e1 = g_idx * EXPERTS_PER_GROUP + i1
    e2 = g_idx * EXPERTS_PER_GROUP + i2
    rowe = lax.broadcasted_iota(I32, (N_EXPERTS, rows), 0)
    oh1 = (rowe == e1).astype(F32)
    oh2 = (rowe == e2).astype(F32)
    both_oh = oh1 + oh2
    before = _dot(both_oh.astype(BF16), tri_ref[...]) + cnt_ref[:, 0:1]
    rank1 = jnp.sum(oh1 * before, axis=0, keepdims=True).astype(I32)
    rank2 = jnp.sum(oh2 * before, axis=0, keepdims=True).astype(I32)
    cnt_ref[...] = cnt_ref[...] + jnp.sum(both_oh, axis=-1, keepdims=True)
    cntout_ref[...] = cnt_ref[...]
    meta_ref[...] = jnp.where(row8 == 0, e1, jnp.where(row8 == 1, e2, jnp.where(row8 == 2, rank1,
                                                                                 jnp.where(row8 == 3, rank2, 0))))
    ew_ref[...] = jnp.where(row8 == 0, w1, jnp.where(row8 == 1, w2, 0.0))


def _mod_spec(tile, per_batch_rows):
    if per_batch_rows is None:
        return pl.BlockSpec((tile, D_MODEL), lambda i: (i, 0))
    per = per_batch_rows // tile
    return pl.BlockSpec((1, 1, D_MODEL), lambda i: (i // per, 0, 0))


def _outproj(yp, ya, x, gt1, sh2, sc2, g2, wout, wr, wrhi, br, cnt_in, per_batch_rows):
    n = x.shape[0]
    tile = min(ROW_TILE, n)
    tok = lambda i: (i, 0)
    const2 = lambda i: (0, 0)
    mod_spec = _mod_spec(tile, per_batch_rows)
    tri = (jnp.arange(tile)[:, None] < jnp.arange(tile)[None, :]).astype(BF16)
    return pl.pallas_call(
        _outproj_kernel,
        grid=(n // tile,),
        in_specs=[pl.BlockSpec((tile, POOL_WIDTH), tok),
                  pl.BlockSpec((tile, ATTN_WIDTH), tok),
                  pl.BlockSpec((tile, D_MODEL), tok),
                  mod_spec, mod_spec, mod_spec,
                  pl.BlockSpec((1, D_MODEL), const2),
                  pl.BlockSpec((2 * POOL_WIDTH, D_MODEL), const2),
                  pl.BlockSpec((D_MODEL, 2 * LANES), const2),
                  pl.BlockSpec((D_MODEL, LANES), const2),
                  pl.BlockSpec((LANES, 1), const2),
                  pl.BlockSpec((tile, tile), const2),
                  pl.BlockSpec((N_EXPERTS, LANES), const2)],
        out_specs=[pl.BlockSpec((tile, D_MODEL), tok),
                   pl.BlockSpec((tile, D_MODEL), tok),
                   pl.BlockSpec((8, tile), lambda i: (0, i)),
                   pl.BlockSpec((8, tile), lambda i: (0, i)),
                   pl.BlockSpec((N_EXPERTS, LANES), const2)],
        out_shape=[jax.ShapeDtypeStruct((n, D_MODEL), F32),
                   jax.ShapeDtypeStruct((n, D_MODEL), F32),
                   jax.ShapeDtypeStruct((8, n), I32),
                   jax.ShapeDtypeStruct((8, n), F32),
                   jax.ShapeDtypeStruct((N_EXPERTS, LANES), F32)],
        scratch_shapes=[pltpu.VMEM((N_EXPERTS, LANES), F32)],
        compiler_params=pltpu.CompilerParams(dimension_semantics=("arbitrary",), vmem_limit_bytes=VMEM_LIMIT),
        name="outproj",
    )(yp, ya, x, gt1, sh2, sc2, g2, wout, wr, wrhi, br, tri, cnt_in)


def _dest_kernel(pstart_ref, meta_ref, dest_ref):
    n = meta_ref.shape[1]
    eid = meta_ref[0:EXPERT_TOPK, :]
    start = jnp.zeros((EXPERT_TOPK, n), I32)
    for e in range(N_EXPERTS):
        start = jnp.where(eid == e, pstart_ref[e], start)
    dest_ref[...] = start + meta_ref[EXPERT_TOPK:2 * EXPERT_TOPK, :]


def _dest_rows(pstart, meta):
    n = meta.shape[1]
    dest = pl.pallas_call(
        _dest_kernel,
        in_specs=[pl.BlockSpec(memory_space=pltpu.SMEM), pl.BlockSpec(memory_space=pltpu.VMEM)],
        out_specs=pl.BlockSpec(memory_space=pltpu.VMEM),
        out_shape=jax.ShapeDtypeStruct((EXPERT_TOPK, n), I32),
        name="dest",
    )(pstart, meta)
    return [dest[slot] for slot in range(EXPERT_TOPK)]


DMA_ISSUE_UNROLL = 16


SUBLANES = 8
PAD_CHUNKS = (128, 64, 32, 16, 8)


def _issue_row_copies(n_tokens, copy_for):
    group = DMA_ISSUE_UNROLL // SUBLANES

    def issue(g, carry):
        for k in range(DMA_ISSUE_UNROLL):
            for slot in range(EXPERT_TOPK):
                copy_for(slot, g * DMA_ISSUE_UNROLL + k, g * group + k // SUBLANES, k % SUBLANES).start()
        return carry

    lax.fori_loop(0, n_tokens // DMA_ISSUE_UNROLL, issue, 0)


def _scatter_rows(dest_refs, h_ref, xr_ref, sem):
    rows = h_ref.shape[0] * SUBLANES

    def copy_for(slot, t, tile, sub):
        return pltpu.make_async_copy(h_ref.at[tile, pl.ds(sub, 1), :],
                                     xr_ref.at[pl.ds(dest_refs[slot][t], 1), :], sem)

    _issue_row_copies(rows, copy_for)
    for _ in dest_refs:
        tile_rows = xr_ref.at[pl.ds(0, rows), :]
        pltpu.make_async_copy(tile_rows, tile_rows, sem).wait()


def _dispatch_kernel(*refs):
    dest_refs = refs[0:EXPERT_TOPK]
    pstart_ref, cnt_ref, nact_ref, hp_ref, hs_ref, xr_ref, z_ref, sem = refs[EXPERT_TOPK:]
    i = pl.program_id(0)
    last = pl.num_programs(0) - 1

    @pl.when(i < last)
    def _():
        _scatter_rows(dest_refs, hp_ref, xr_ref, sem)

    @pl.when(i == last)
    def _():
        _scatter_rows(dest_refs, hs_ref, xr_ref, sem)
        z_ref[...] = jnp.zeros_like(z_ref)
        _zero_unassigned_rows(pstart_ref, cnt_ref, nact_ref, xr_ref, z_ref, sem)


def _dispatch_rows(dests, pstart, counts, n_active, h2_p, h2_s, n_rows):
    n_p = h2_p.shape[0]
    assert n_p % ROW_TILE == 0 and h2_s.shape[0] <= ROW_TILE
    p_tiles = n_p // ROW_TILE
    smem_rows = pl.BlockSpec((ROW_TILE,), lambda i: (i,), memory_space=pltpu.SMEM)
    smem = pl.BlockSpec(memory_space=pltpu.SMEM)
    dests = [jnp.pad(d, (0, (p_tiles + 1) * ROW_TILE - d.shape[0])) for d in dests]
    h2_p = h2_p.reshape(-1, SUBLANES, D_MODEL)
    h2_s = h2_s.reshape(-1, SUBLANES, D_MODEL)
    return pl.pallas_call(
        _dispatch_kernel,
        grid=(p_tiles + 1,),
        in_specs=[smem_rows] * EXPERT_TOPK + [
            smem, smem, smem,
            pl.BlockSpec((ROW_TILE // SUBLANES, SUBLANES, D_MODEL), lambda i: (jnp.minimum(i, p_tiles - 1), 0, 0)),
            pl.BlockSpec(h2_s.shape, lambda i: (0, 0, 0))],
        out_specs=pl.BlockSpec(memory_space=pl.ANY),
        out_shape=jax.ShapeDtypeStruct((n_rows, D_MODEL), F32),
        scratch_shapes=[pltpu.VMEM((MOE_ROWS, D_MODEL), F32), pltpu.SemaphoreType.DMA(())],
        compiler_params=pltpu.CompilerParams(dimension_semantics=("arbitrary",), vmem_limit_bytes=VMEM_LIMIT),
        name="dispatch",
    )(*dests, pstart, counts, n_active, h2_p, h2_s)


def _zero_unassigned_rows(pstart_ref, cnt_ref, nact_ref, xr_ref, z_ref, sem):
    assert MOE_ROWS - 1 == SUBLANES - 1 + sum(PAD_CHUNKS)
    n_blocks = xr_ref.shape[0] // MOE_ROWS

    def trailing(act):
        def body(blk, carry):
            dst = xr_ref.at[pl.ds(pl.multiple_of(blk * MOE_ROWS, MOE_ROWS), MOE_ROWS), :]
            act(pltpu.make_async_copy(z_ref, dst, sem))
            return carry
        lax.fori_loop(nact_ref[0], n_blocks, body, 0)

    def copies(act):
        trailing(act)
        for e in range(N_EXPERTS):
            cnt = cnt_ref[e]
            first = pstart_ref[e] + cnt
            n_pad = (MOE_ROWS - (cnt & (MOE_ROWS - 1))) & (MOE_ROWS - 1)
            head = (SUBLANES - (first & (SUBLANES - 1))) & (SUBLANES - 1)
            head = jnp.minimum(head, n_pad)
            for k in range(SUBLANES - 1):
                @pl.when(k < head)
                def _():
                    act(pltpu.make_async_copy(z_ref.at[pl.ds(0, 1), :], xr_ref.at[pl.ds(first + k, 1), :], sem))
            start = first + head
            body = n_pad - head
            for rows in PAD_CHUNKS:
                @pl.when((body & rows) != 0)
                def _():
                    dst = xr_ref.at[pl.ds(pl.multiple_of(start, SUBLANES), rows), :]
                    act(pltpu.make_async_copy(z_ref.at[pl.ds(0, rows), :], dst, sem))
                start = start + (body & rows)

    copies(lambda c: c.start())
    copies(lambda c: c.wait())


def _moe_kernel(blk_e_ref, nact_ref, x_ref, wg_ref, wu_ref, wd_ref, y_ref, wgb_ref, wub_ref, wdb_ref):
    i = pl.program_id(0)
    prev = blk_e_ref[jnp.maximum(i - 1, 0)]
    active = i < nact_ref[0]

    @pl.when(active & ((i == 0) | (blk_e_ref[i] != prev)))
    def _():
        wgb_ref[...] = wg_ref[0].astype(BF16)
        wub_ref[...] = wu_ref[0].astype(BF16)
        wdb_ref[...] = wd_ref[0].astype(BF16)

    @pl.when(active)
    def _():
        x = x_ref[...].astype(BF16)
        g = _dot(x, wgb_ref[...])
        u = _dot(x, wub_ref[...])
        a = g / (1.0 + jnp.exp(-g)) * u
        y_ref[...] = _dot(a.astype(BF16), wdb_ref[...])

    @pl.when(jnp.logical_not(active))
    def _():
        y_ref[...] = jnp.zeros_like(y_ref)


def _moe(blk_e, n_active, xr, w_gate, w_up, w_down):
    n_rows = xr.shape[0]
    n_blocks = n_rows // MOE_ROWS
    grid_spec = pltpu.PrefetchScalarGridSpec(
        num_scalar_prefetch=2,
        grid=(n_blocks,),
        in_specs=[pl.BlockSpec((MOE_ROWS, D_MODEL), lambda i, be, na: (jnp.minimum(i, na[0] - 1), 0)),
                  pl.BlockSpec((1, D_MODEL, D_FF), lambda i, be, na: (be[i], 0, 0)),
                  pl.BlockSpec((1, D_MODEL, D_FF), lambda i, be, na: (be[i], 0, 0)),
                  pl.BlockSpec((1, D_FF, D_MODEL), lambda i, be, na: (be[i], 0, 0))],
        out_specs=pl.BlockSpec((MOE_ROWS, D_MODEL), lambda i, be, na: (i, 0)),
        scratch_shapes=[pltpu.VMEM((D_MODEL, D_FF), BF16),
                        pltpu.VMEM((D_MODEL, D_FF), BF16),
                        pltpu.VMEM((D_FF, D_MODEL), BF16)])
    return pl.pallas_call(
        _moe_kernel,
        grid_spec=grid_spec,
        out_shape=jax.ShapeDtypeStruct((n_rows, D_MODEL), F32),
        compiler_params=pltpu.CompilerParams(
            dimension_semantics=("arbitrary",), vmem_limit_bytes=VMEM_LIMIT),
        name="moe",
    )(blk_e, n_active, xr, w_gate, w_up, w_down)


def _final_kernel(*refs):
    dest_refs = refs[0:EXPERT_TOPK]
    x1_ref, w_ref, gt2_ref, yr_ref, y_ref, o_ref, sem = refs[EXPERT_TOPK:]
    rows = x1_ref.shape[0]

    def copy_for(slot, t, tile, sub):
        return pltpu.make_async_copy(yr_ref.at[pl.ds(dest_refs[slot][t], 1), :],
                                     o_ref.at[slot, tile, pl.ds(sub, 1), :], sem)

    _issue_row_copies(rows, copy_for)
    for slot in range(EXPERT_TOPK):
        pltpu.make_async_copy(o_ref.at[slot], o_ref.at[slot], sem).wait()
    w = w_ref[...]
    o = [o_ref[slot].reshape(rows, D_MODEL) for slot in range(EXPERT_TOPK)]
    moe = o[0] * w[:, 0:1] + o[1] * w[:, 1:2]
    y_ref[...] = x1_ref[...] + _mod_rows(gt2_ref) * moe


def _final(dests, x1, w2, gt2, yr, per_batch_rows):
    n = x1.shape[0]
    tile = min(ROW_TILE, n)
    tok = lambda i: (i, 0)
    smem_rows = pl.BlockSpec((tile,), lambda i: (i,), memory_space=pltpu.SMEM)
    return pl.pallas_call(
        _final_kernel,
        grid=(n // tile,),
        in_specs=[smem_rows] * EXPERT_TOPK + [pl.BlockSpec((tile, D_MODEL), tok),
                                             pl.BlockSpec((tile, EXPERT_TOPK), tok),
                                             _mod_spec(tile, per_batch_rows),
                                             pl.BlockSpec(memory_space=pl.ANY)],
        out_specs=pl.BlockSpec((tile, D_MODEL), tok),
        out_shape=jax.ShapeDtypeStruct((n, D_MODEL), F32),
        scratch_shapes=[pltpu.VMEM((EXPERT_TOPK, tile // SUBLANES, SUBLANES, D_MODEL), F32),
                        pltpu.SemaphoreType.DMA(())],
        compiler_params=pltpu.CompilerParams(dimension_semantics=("arbitrary",), vmem_limit_bytes=VMEM_LIMIT),
        name="final",
    )(*dests, x1, w2, gt2, yr)


def _expert_layout(counts, n_blocks):
    padded = (counts + MOE_ROWS - 1) // MOE_ROWS * MOE_ROWS
    pend = jnp.cumsum(padded)
    pstart = (pend - padded).astype(I32)
    blk_start = jnp.arange(n_blocks, dtype=I32) * MOE_ROWS
    blk_e = jnp.minimum(jnp.sum((pend[None, :] <= blk_start[:, None]).astype(I32), axis=1), N_EXPERTS - 1)
    n_active = (pend[-1] // MOE_ROWS).astype(I32).reshape(1)
    return pstart, blk_e.astype(I32), n_active


def kernel(x_prompt, x_sample, cache_k, cache_v, state_pool, page_table, c_prompt, c_sample, w_ada, b_ada,
           g_attn_norm, w_in, g_q, g_k, w_pool, pool_scale, w_out, g_ffn_norm, w_group, b_group, w_expert,
           b_expert, w_gate, w_up, w_down):
    D = D_MODEL
    B, T, _ = x_prompt.shape
    n_s = DEC_BATCH * DEC_SEQ
    n_p = B * T
    layer = 0

    win = w_in[layer].astype(BF16)
    wout = w_out[layer].astype(BF16)
    wpool = w_pool[layer].astype(BF16)
    g1 = g_attn_norm[layer].reshape(1, D)
    g2 = g_ffn_norm[layer].reshape(1, D)
    gq = jnp.tile(g_q[layer], N_HEADS).reshape(1, ATTN_WIDTH)
    gk = jnp.tile(g_k[layer], N_HEADS).reshape(1, ATTN_WIDTH)
    ps = pool_scale[layer].reshape(1, POOL_WIDTH)
    hd = jnp.arange(ATTN_WIDTH) // HEAD_DIM
    bd = (hd[:, None] == hd[None, :]).astype(BF16)
    slopes = jnp.exp2(-8.0 * (jnp.arange(N_HEADS, dtype=F32) + 1.0) / N_HEADS)
    qrow = jnp.arange(N_HEADS * DEC_SEQ)
    slope_col = slopes[qrow // DEC_SEQ].reshape(-1, 1)
    lq_col = (qrow % DEC_SEQ).astype(F32).reshape(-1, 1)
    wr = jnp.zeros((D, LANES), F32).at[:, 0:N_GROUPS].set(w_group[layer]).at[:, 8:8 + N_EXPERTS].set(w_expert[layer])
    wr_hi = wr.astype(BF16)
    wr_lo = (wr - wr_hi.astype(F32)).astype(BF16)
    wr_both = jnp.concatenate([wr_hi, wr_lo], axis=1)
    br = jnp.zeros((LANES,), F32).at[0:N_GROUPS].set(b_group[layer]).at[N_GROUPS:8].set(NEG)
    br = br.at[8:8 + N_EXPERTS].set(b_expert[layer]).reshape(LANES, 1)

    mod = _ada(jnp.concatenate([c_prompt, c_sample], axis=0), w_ada[layer], b_ada[layer])
    mod_p = mod[:B].reshape(B, 1, 6 * D)
    mod_s = jnp.repeat(mod[B:], DEC_SEQ, axis=0)

    ypool_p, q_p, k_p, v_p, tail_p = _inproj_prompt(x_prompt, mod_p, g1, win, bd, gq, gk, wpool, ps)
    yattn_p = _moba_prompt(slopes, q_p, k_p, v_p)
    mp = lambda j: mod_p[:, :, j * D:(j + 1) * D]
    cnt0 = jnp.zeros((N_EXPERTS, LANES), F32)
    x1_p, h2_p, meta_p, ew_p, cnt_p = _outproj(
        ypool_p.reshape(n_p, POOL_WIDTH), yattn_p.reshape(n_p, ATTN_WIDTH), x_prompt.reshape(n_p, D),
        mp(2), mp(3), mp(4), g2, wout, wr_both, wr_hi, br, cnt0, T)

    ms = lambda j: mod_s[:, j * D:(j + 1) * D]
    hist = jnp.concatenate([jnp.zeros((DEC_BATCH, HALO - POOL_HIST, POOL_WIDTH), F32), state_pool[layer]], axis=1)
    xs = x_sample.reshape(n_s, D)
    ypool_s, q_s, k_s, v_s, tail_s = _inproj_sample(xs, ms(0), ms(1), g1, win, bd, gq, gk, hist, wpool, ps)
    r3 = lambda a: a.reshape(DEC_BATCH, DEC_SEQ, ATTN_WIDTH)
    n_phys = cache_k.shape[1]
    ck = jnp.transpose(cache_k[layer], (0, 2, 3, 1)).reshape(n_phys, ATTN_WIDTH, PAGE_SIZE)
    cv = jnp.transpose(cache_v[layer], (0, 2, 3, 1)).reshape(n_phys, ATTN_WIDTH, PAGE_SIZE)
    yattn_s = _moba_sample(page_table, r3(q_s), r3(k_s), r3(v_s), slope_col, lq_col, ck, cv)
    x1_s, h2_s, meta_s, ew_s, cnt_s = _outproj(ypool_s, yattn_s.reshape(n_s, ATTN_WIDTH), xs, ms(2), ms(3), ms(4),
                                               g2, wout, wr_both, wr_hi, br, cnt_p, None)

    n_blocks = (n_p + n_s) * EXPERT_TOPK // MOE_ROWS + N_EXPERTS
    counts = cnt_s[:, 0].astype(I32)
    pstart, blk_e, n_active = _expert_layout(counts, n_blocks)
    dests_p = _dest_rows(pstart, meta_p)
    dests_s = _dest_rows(pstart, meta_s)
    dests = [jnp.concatenate([dp, ds]) for dp, ds in zip(dests_p, dests_s)]
    xr = _dispatch_rows(dests, pstart, counts, n_active, h2_p, h2_s, n_blocks * MOE_ROWS)
    yr = _moe(blk_e, n_active, xr, w_gate[layer], w_up[layer], w_down[layer])
    y_p = _final(dests_p, x1_p, ew_p[0:EXPERT_TOPK].T, mp(5), yr, T)
    y_s = _final(dests_s, x1_s, ew_s[0:EXPERT_TOPK].T, ms(5), yr, None)

    k4 = lambda a, b, l: a.reshape(1, b, l, N_HEADS, HEAD_DIM)
    return (y_p.reshape(B, T, D), y_s.reshape(DEC_BATCH, DEC_SEQ, D),
            k4(k_p, B, T), k4(v_p, B, T), tail_p[None, :, HALO - POOL_HIST:, :],
            k4(k_s, DEC_BATCH, DEC_SEQ), k4(v_s, DEC_BATCH, DEC_SEQ), tail_s[None, :, HALO - POOL_HIST:, :])
```

```python
import jax
import jax.numpy as jnp
from jax import lax
from jax.experimental import pallas as pl
from jax.experimental.pallas import tpu as pltpu

F32 = jnp.float32
BF16 = jnp.bfloat16
I32 = jnp.int32

D_MODEL = 1024
DEC_BATCH = 32
DEC_SEQ = 8
PAST_LEN = 16384
PAGE_SIZE = 128
POOL_WIDTH = 512
POOL_WINDOWS = (2, 4, 8, 16)
POOL_GROUP = 128
POOL_HIST = 15
HALO = 16
N_HEADS = 8
HEAD_DIM = 64
ATTN_WIDTH = 512
MOBA_BLOCK = 256
MOBA_TOPK = 3
ATTN_SCALE = HEAD_DIM ** -0.5
MIX_IN = POOL_WIDTH + 3 * ATTN_WIDTH
N_GROUPS = 4
EXPERTS_PER_GROUP = 8
N_EXPERTS = 32
EXPERT_TOPK = 2
D_FF = 512
EPS = 1e-6
NEG = -1e30

LANES = 128
ROW_TILE = 512
INPROJ_ROWS = 1024
MOE_ROWS = 512
PAGES_PER_STEP = 16
VMEM_LIMIT = 56 * 1024 * 1024

_NT = (((1,), (1,)), ((), ()))


def _dot(a, b):
    return jnp.dot(a, b, preferred_element_type=F32)


def _dot_nt(a, b):
    return lax.dot_general(a, b, _NT, preferred_element_type=F32)


def _split_dot(a, b01):
    hi = a.astype(BF16)
    lo = (a - hi.astype(F32)).astype(BF16)
    return _dot(hi, b01) + _dot(lo, b01)


def _rms_mod(x, g, sc, sh):
    ms = jnp.mean(x * x, axis=-1, keepdims=True)
    return x * lax.rsqrt(ms + EPS) * g * (1.0 + sc) + sh


def _ada_kernel(c_ref, w_ref, b_ref, o_ref):
    c = c_ref[...]
    a = c / (1.0 + jnp.exp(-c))
    o_ref[...] = _dot(a.astype(BF16), w_ref[...].astype(BF16)) + b_ref[...]


def _ada(c_all, w_ada, b_ada):
    n = c_all.shape[0]
    tn = 1536
    return pl.pallas_call(
        _ada_kernel,
        grid=(6 * D_MODEL // tn,),
        in_specs=[pl.BlockSpec((n, D_MODEL), lambda j: (0, 0)),
                  pl.BlockSpec((D_MODEL, tn), lambda j: (0, j)),
                  pl.BlockSpec((1, tn), lambda j: (0, j))],
        out_specs=pl.BlockSpec((n, tn), lambda j: (0, j)),
        out_shape=jax.ShapeDtypeStruct((n, 6 * D_MODEL), F32),
        compiler_params=pltpu.CompilerParams(vmem_limit_bytes=VMEM_LIMIT),
        name="ada",
    )(c_all, w_ada, b_ada.reshape(1, -1))


def _inproj_core(x, sh1, sc1, g, win_ref, bd_ref, gq_ref, gk_ref):
    h = _rms_mod(x, g, sc1, sh1)
    z = _dot(h.astype(BF16), win_ref[...])
    u = z[:, 0:POOL_WIDTH]
    q = z[:, POOL_WIDTH:POOL_WIDTH + ATTN_WIDTH]
    k = z[:, POOL_WIDTH + ATTN_WIDTH:POOL_WIDTH + 2 * ATTN_WIDTH]
    v = z[:, POOL_WIDTH + 2 * ATTN_WIDTH:]
    bd = bd_ref[...]
    q = q * lax.rsqrt(_split_dot(q * q, bd) * (1.0 / HEAD_DIM) + EPS) * gq_ref[...]
    k = k * lax.rsqrt(_split_dot(k * k, bd) * (1.0 / HEAD_DIM) + EPS) * gk_ref[...]
    return u, q, k, v


def _window_sum(e, w):
    s = e
    sh = 1
    while sh < w:
        s = s + pltpu.roll(s, sh, axis=0)
        sh *= 2
    return s


def _inproj_prompt_kernel(x_ref, mod_ref, g_ref, win_ref, bd_ref, gq_ref, gk_ref, wpool_ref, ps_ref,
                          ypool_ref, q_ref, k_ref, v_ref, tail_ref, ext_ref):
    t = pl.program_id(1)
    nt = pl.num_programs(1)
    x = x_ref[0]
    sh1 = mod_ref[0, :, 0:D_MODEL]
    sc1 = mod_ref[0, :, D_MODEL:2 * D_MODEL]
    u, q, k, v = _inproj_core(x, sh1, sc1, g_ref[...], win_ref, bd_ref, gq_ref, gk_ref)
    q_ref[0] = (q * ATTN_SCALE).astype(BF16)
    k_ref[0] = k
    v_ref[0] = v

    @pl.when(t == 0)
    def _():
        ext_ref[0:HALO, :] = jnp.zeros((HALO, POOL_WIDTH), F32)

    ext_ref[HALO:, :] = u
    pos = t * INPROJ_ROWS + lax.broadcasted_iota(I32, (INPROJ_ROWS, 1), 0)
    for gi, w in enumerate(POOL_WINDOWS):
        cols = slice(gi * POOL_GROUP, (gi + 1) * POOL_GROUP)
        win = _window_sum(ext_ref[:, cols], w)[HALO:]
        inv_cnt = 1.0 / jnp.minimum(pos + 1, w).astype(F32)
        pooled = win * inv_cnt - u[:, cols]
        y = _dot(pooled.astype(BF16), wpool_ref[gi]) * ps_ref[:, cols]
        ypool_ref[0, :, cols] = y.astype(BF16)
    last = u[INPROJ_ROWS - HALO:, :]
    ext_ref[0:HALO, :] = last

    @pl.when(t == nt - 1)
    def _():
        tail_ref[0] = last


def _inproj_prompt(x, mod_p, g1, win, bd, gq, gk, wpool, ps):
    B, T, D = x.shape
    nt = T // INPROJ_ROWS
    const2 = lambda b, t: (0, 0)
    tok = lambda b, t: (b, t, 0)
    return pl.pallas_call(
        _inproj_prompt_kernel,
        grid=(B, nt),
        in_specs=[pl.BlockSpec((1, INPROJ_ROWS, D), tok),
                  pl.BlockSpec((1, 1, 6 * D), lambda b, t: (b, 0, 0)),
                  pl.BlockSpec((1, D), const2),
                  pl.BlockSpec((D, MIX_IN), const2),
                  pl.BlockSpec((ATTN_WIDTH, ATTN_WIDTH), const2),
                  pl.BlockSpec((1, ATTN_WIDTH), const2),
                  pl.BlockSpec((1, ATTN_WIDTH), const2),
                  pl.BlockSpec((4, POOL_GROUP, POOL_GROUP), lambda b, t: (0, 0, 0)),
                  pl.BlockSpec((1, POOL_WIDTH), const2)],
        out_specs=[pl.BlockSpec((1, INPROJ_ROWS, POOL_WIDTH), tok),
                   pl.BlockSpec((1, INPROJ_ROWS, ATTN_WIDTH), tok),
                   pl.BlockSpec((1, INPROJ_ROWS, ATTN_WIDTH), tok),
                   pl.BlockSpec((1, INPROJ_ROWS, ATTN_WIDTH), tok),
                   pl.BlockSpec((1, HALO, POOL_WIDTH), lambda b, t: (b, 0, 0))],
        out_shape=[jax.ShapeDtypeStruct((B, T, POOL_WIDTH), BF16),
                   jax.ShapeDtypeStruct((B, T, ATTN_WIDTH), BF16),
                   jax.ShapeDtypeStruct((B, T, ATTN_WIDTH), F32),
                   jax.ShapeDtypeStruct((B, T, ATTN_WIDTH), F32),
                   jax.ShapeDtypeStruct((B, HALO, POOL_WIDTH), F32)],
        scratch_shapes=[pltpu.VMEM((HALO + INPROJ_ROWS, POOL_WIDTH), F32)],
        compiler_params=pltpu.CompilerParams(
            dimension_semantics=("arbitrary", "arbitrary"), vmem_limit_bytes=VMEM_LIMIT),
        name="inproj_prompt",
    )(x, mod_p, g1, win, bd, gq, gk, wpool, ps)


def _inproj_sample_kernel(x_ref, sh_ref, sc_ref, g_ref, win_ref, bd_ref, gq_ref, gk_ref, hist_ref, wpool_ref,
                          ps_ref, ypool_ref, q_ref, k_ref, v_ref, tail_ref, ext_ref):
    n = DEC_BATCH * DEC_SEQ
    ext_rows = HALO + DEC_SEQ
    u, q, k, v = _inproj_core(x_ref[...], sh_ref[...], sc_ref[...], g_ref[...], win_ref, bd_ref, gq_ref, gk_ref)
    q_ref[...] = q * ATTN_SCALE
    k_ref[...] = k
    v_ref[...] = v
    ext_ref[:, 0:HALO, :] = hist_ref[...]
    ext_ref[:, HALO:, :] = u.reshape(DEC_BATCH, DEC_SEQ, POOL_WIDTH)
    tail_ref[...] = ext_ref[:, ext_rows - HALO:, :]
    pos = PAST_LEN + lax.broadcasted_iota(I32, (DEC_BATCH, DEC_SEQ, 1), 1).reshape(n, 1)
    for gi, w in enumerate(POOL_WINDOWS):
        cols = slice(gi * POOL_GROUP, (gi + 1) * POOL_GROUP)
        e = ext_ref[:, :, cols].reshape(DEC_BATCH * ext_rows, POOL_GROUP)
        win = _window_sum(e, w).reshape(DEC_BATCH, ext_rows, POOL_GROUP)[:, HALO:, :].reshape(n, POOL_GROUP)
        inv_cnt = 1.0 / jnp.minimum(pos + 1, w).astype(F32)
        pooled = win * inv_cnt - u[:, cols]
        y = _dot(pooled.astype(BF16), wpool_ref[gi]) * ps_ref[:, cols]
        ypool_ref[:, cols] = y.astype(BF16)


def _inproj_sample(x, sh1, sc1, g1, win, bd, gq, gk, hist, wpool, ps):
    n = x.shape[0]
    return pl.pallas_call(
        _inproj_sample_kernel,
        out_shape=[jax.ShapeDtypeStruct((n, POOL_WIDTH), BF16),
                   jax.ShapeDtypeStruct((n, ATTN_WIDTH), F32),
                   jax.ShapeDtypeStruct((n, ATTN_WIDTH), F32),
                   jax.ShapeDtypeStruct((n, ATTN_WIDTH), F32),
                   jax.ShapeDtypeStruct((DEC_BATCH, HALO, POOL_WIDTH), F32)],
        scratch_shapes=[pltpu.VMEM((DEC_BATCH, HALO + DEC_SEQ, POOL_WIDTH), F32)],
        compiler_params=pltpu.CompilerParams(vmem_limit_bytes=VMEM_LIMIT),
        name="inproj_sample",
    )(x, sh1, sc1, g1, win, bd, gq, gk, hist, wpool, ps)


N_BIAS_LANES = 3
PROMPT_Q_TILE = 2 * MOBA_BLOCK


def _moba_prompt_kernel(slopes_ref, q_ref, k_ref, v_ref, o_ref, ka_ref, va_ref, qa_ref, kmf_ref):
    S = MOBA_BLOCK
    T = k_ref.shape[1]
    nb = T // S
    hp = pl.program_id(1)
    lane = lax.broadcasted_iota(I32, (1, LANES), 1)
    real = (lane < HEAD_DIM, lane >= HEAD_DIM)
    extra = (lane - HEAD_DIM, lane)

    kf = k_ref[0]
    vf = v_ref[0]
    kmf_ref[...] = jnp.zeros((LANES, LANES), F32)
    for j in range(nb):
        mean = jnp.sum(kf[j * S:(j + 1) * S], axis=0, keepdims=True) * (1.0 / S)
        kmf_ref[j:j + 1, :] = jnp.where(real[0], mean, 0.0)
        kmf_ref[nb + j:nb + j + 1, :] = jnp.where(real[0], 0.0, mean)
    key_i = lax.broadcasted_iota(I32, (T, 1), 0)
    key_blk = key_i // S
    for h in range(2):
        b = slopes_ref[2 * hp + h] * key_i.astype(F32)
        p0 = b.astype(BF16).astype(F32)
        p1 = (b - p0).astype(BF16).astype(F32)
        p2 = b - p0 - p1
        e = extra[h]
        onehot = jnp.where((e >= N_BIAS_LANES) & (e - N_BIAS_LANES == key_blk), 1.0, 0.0)
        feat = jnp.where(e == 0, p0, jnp.where(e == 1, p1, jnp.where(e == 2, p2, onehot)))
        ka_ref[h] = jnp.where(real[h], kf, feat).astype(BF16)
        va_ref[h] = jnp.where(real[h], vf, 1.0).astype(BF16)

    q2 = q_ref[0]

    gt = _dot_nt(kmf_ref[...].astype(BF16), q2)[0:2 * nb]
    row = lax.broadcasted_iota(I32, (2 * nb, T), 0)
    blk = row % nb
    cbq = lax.broadcasted_iota(I32, (2 * nb, T), 1) // S
    cnt = jnp.zeros((2 * nb, T), I32)
    for m in range(nb):
        gm = jnp.where(row < nb, gt[m:m + 1, :], gt[nb + m:nb + m + 1, :])
        beats = (gm > gt) | ((gm == gt) & (m < blk))
        cnt = cnt + jnp.where(beats & (m < cbq), 1, 0)
    keep = (((cnt < MOBA_TOPK) & (blk < cbq)) | (blk == cbq)).astype(F32)
    keepq = jnp.concatenate([keep, jnp.zeros((LANES - 2 * nb, T), F32)], axis=0).T
    maskv = jnp.where(keepq > 0.5, 0.0, NEG)
    mask_lanes = (pltpu.roll(maskv, HEAD_DIM + N_BIAS_LANES, axis=1),
                  pltpu.roll(maskv, (N_BIAS_LANES - nb) % LANES, axis=1))
    qf = q2.astype(F32)
    for h in range(2):
        e = extra[h]
        feat = jnp.where(e < N_BIAS_LANES, 1.0, jnp.where(e < N_BIAS_LANES + nb, mask_lanes[h], 0.0))
        qa_ref[h] = jnp.where(real[h], qf, feat).astype(BF16)

    QT = PROMPT_Q_TILE
    causal = lax.broadcasted_iota(I32, (QT, QT), 1) <= lax.broadcasted_iota(I32, (QT, QT), 0)
    for t in range(T // QT):
        rows = slice(t * QT, (t + 1) * QT)
        n = (t + 1) * QT
        outs = []
        for h in range(2):
            s = _dot_nt(qa_ref[h, rows, :], ka_ref[h, 0:n, :])
            s_own = jnp.where(causal, s[:, t * QT:n], NEG)
            s = s_own if t == 0 else jnp.concatenate([s[:, 0:t * QT], s_own], axis=1)
            p = jnp.exp(s - jnp.max(s, axis=-1, keepdims=True))
            outs.append(_dot(p.astype(BF16), va_ref[h, 0:n, :]))
        a0, a1 = outs
        o0 = a0 * (1.0 / a0[:, HEAD_DIM:HEAD_DIM + 1])
        o1 = a1 * (1.0 / a1[:, 0:1])
        o_ref[0, rows, :] = jnp.where(real[0], o0, o1).astype(BF16)


def _moba_prompt(slopes, q, k, v):
    B, T, _ = q.shape
    seq = lambda b, hp: (b, 0, hp)
    return pl.pallas_call(
        _moba_prompt_kernel,
        grid=(B, N_HEADS // 2),
        in_specs=[pl.BlockSpec(memory_space=pltpu.SMEM),
                  pl.BlockSpec((1, T, LANES), seq),
                  pl.BlockSpec((1, T, LANES), seq),
                  pl.BlockSpec((1, T, LANES), seq)],
        out_specs=pl.BlockSpec((1, T, LANES), seq),
        out_shape=jax.ShapeDtypeStruct((B, T, ATTN_WIDTH), BF16),
        scratch_shapes=[pltpu.VMEM((2, T, LANES), BF16),
                        pltpu.VMEM((2, T, LANES), BF16),
                        pltpu.VMEM((2, T, LANES), BF16),
                        pltpu.VMEM((LANES, LANES), F32)],
        compiler_params=pltpu.CompilerParams(
            dimension_semantics=("arbitrary", "arbitrary"), vmem_limit_bytes=VMEM_LIMIT),
        name="moba_prompt",
    )(slopes, q, k, v)


CHUNK_SLOTS = 6


def _moba_sample_kernel(pt_ref, q_ref, kn_ref, vn_ref, slope_ref, lq_ref, ck_ref, cv_ref, o_ref,
                        buf_ref, sem, s_ref, sc_ref, kc_ref, idx_v, idx_s, idx_sem, vbuf_ref, vsem, vb16_ref, ph_ref,
                        gate_ref, res_ref):
    P = PAGES_PER_STEP
    S = MOBA_BLOCK
    chunk = P * PAGE_SIZE
    n_chunks = PAST_LEN // chunk
    bpc = chunk // S
    n_loads = n_chunks
    ahead = CHUNK_SLOTS - 1
    nq = N_HEADS * DEC_SEQ
    b = pl.program_id(0)
    n_rows = pl.num_programs(0) - 1
    slope = slope_ref[...]
    lq = lq_ref[...]
    lane = lax.broadcasted_iota(I32, (1, LANES), 1)

    def slot_of(bb, i):
        return lax.rem(bb * n_loads + i, CHUNK_SLOTS)

    def start_load(bb, i):
        slot = slot_of(bb, i)
        for r in range(P):
            page = pt_ref[bb, i * P + r]
            pltpu.make_async_copy(ck_ref.at[page], buf_ref.at[slot, r], sem.at[slot]).start()

    def load_chunk(i):
        nxt = i + ahead
        if nxt < n_loads:
            start_load(b, nxt)
        else:
            @pl.when(b + 1 < n_rows)
            def _():
                start_load(b + 1, nxt - n_loads)
        slot = slot_of(b, i)
        pltpu.make_async_copy(ck_ref.at[pl.ds(0, P)], buf_ref.at[slot], sem.at[slot]).wait()
        for r in range(P):
            kc_ref[:, r * PAGE_SIZE:(r + 1) * PAGE_SIZE] = buf_ref[slot, r].astype(BF16)
        return kc_ref[...]

    @pl.when(b == 0)
    def _():
        for i in range(ahead):
            start_load(b, i)
        ph_ref[...] = jnp.zeros_like(ph_ref)
        vb16_ref[...] = jnp.zeros_like(vb16_ref)

    head = lax.broadcasted_iota(I32, (DEC_SEQ, ATTN_WIDTH), 1) // HEAD_DIM

    def masked_queries():
        q8 = q_ref[0]
        return jnp.concatenate([jnp.where(head == h, q8, 0.0) for h in range(N_HEADS)], axis=0).astype(BF16)

    @pl.when(b < n_rows)
    def _score_keys():
        qp = masked_queries()
        gate = jnp.zeros((nq, LANES), F32)
        for c in range(n_chunks):
            sc = _dot(qp, load_chunk(c))
            keypos = c * chunk + lax.broadcasted_iota(I32, (1, chunk), 1)
            dist = (keypos - PAST_LEN).astype(F32) - lq
            logits = sc + slope * dist
            for r2 in range(bpc):
                gs = jnp.sum(sc[:, r2 * S:(r2 + 1) * S], axis=-1, keepdims=True) * (1.0 / S)
                gate = jnp.where(lane == c * bpc + r2, gs, gate)
                s_ref[c * bpc + r2] = logits[:, r2 * S:(r2 + 1) * S]
        gate_ref[...] = gate

    @pl.when(b > 0)
    def _finish_previous_row():
        pltpu.make_async_copy(vbuf_ref, vbuf_ref, vsem).wait()
        pieces = []
        for h in range(N_HEADS):
            vb16_ref[0:HEAD_DIM, :] = vbuf_ref[h].astype(BF16)
            pieces.append(_dot_nt(ph_ref[h].astype(BF16), vb16_ref[...])[0:DEC_SEQ])
        pairs = [pieces[2 * i] + pltpu.roll(pieces[2 * i + 1], HEAD_DIM, axis=1) for i in range(N_HEADS // 2)]
        o_ref[0] = (res_ref[...] + jnp.concatenate(pairs, axis=1)).astype(BF16)

    @pl.when(b < n_rows)
    def _pick_and_fetch():
        _pick_blocks_and_fetch_values(
            b, pt_ref, masked_queries(), gate_ref[...], kn_ref, vn_ref, slope, lq, cv_ref, s_ref, sc_ref,
            idx_v, idx_s, idx_sem, vbuf_ref, vsem, ph_ref, res_ref)


def _pick_blocks_and_fetch_values(b, pt_ref, qp, gate, kn_ref, vn_ref, slope, lq, cv_ref, s_ref, sc_ref,
                                  idx_v, idx_s, idx_sem, vbuf_ref, vsem, ph_ref, res_ref):
    S = MOBA_BLOCK
    n_blocks = PAST_LEN // S
    ppb = S // PAGE_SIZE
    nq = N_HEADS * DEC_SEQ
    ktop = min(MOBA_TOPK, n_blocks)
    lane = lax.broadcasted_iota(I32, (1, LANES), 1)
    head = lax.broadcasted_iota(I32, (DEC_SEQ, ATTN_WIDTH), 1) // HEAD_DIM

    gate_t = jnp.concatenate([gate, jnp.zeros((LANES - nq, LANES), F32)], axis=0).T[0:n_blocks]
    blk_i = lax.broadcasted_iota(I32, (n_blocks, LANES), 0)
    cnt = jnp.zeros((n_blocks, LANES), I32)
    for m in range(n_blocks):
        gm = gate_t[m:m + 1, :]
        beats = (gm > gate_t) | ((gm == gate_t) & (m < blk_i))
        cnt = cnt + jnp.where(beats, 1, 0)
    sel_t = cnt < ktop
    row8 = lax.broadcasted_iota(I32, (8, LANES), 0)
    idx_tile = jnp.zeros((8, LANES), I32)
    left = sel_t
    for j in range(ktop):
        pick = jnp.min(jnp.where(left, blk_i, n_blocks), axis=0, keepdims=True)
        idx_tile = jnp.where(row8 == j, pick, idx_tile)
        left = left & (blk_i != pick)
    idx_v[...] = idx_tile
    idx_copy = pltpu.make_async_copy(idx_v, idx_s, idx_sem)
    idx_copy.start()

    kn = jnp.concatenate([kn_ref[0], jnp.zeros((LANES - DEC_SEQ, ATTN_WIDTH), F32)], axis=0)
    lane_f = lane.astype(F32)
    sn = _dot_nt(qp, kn.astype(BF16)) + slope * (lane_f - lq)
    sn = jnp.where(lane_f <= lq, sn, NEG)

    idx_copy.wait()
    span = ktop * S
    for h in range(N_HEADS):
        for l in range(DEC_SEQ):
            q_i = h * DEC_SEQ + l
            for j in range(ktop):
                n = idx_s[j, q_i]
                for half in range(ppb):
                    page = pt_ref[b, n * ppb + half]
                    col = l * span + j * S + half * PAGE_SIZE
                    pltpu.make_async_copy(cv_ref.at[page, pl.ds(h * HEAD_DIM, HEAD_DIM), :],
                                          vbuf_ref.at[h, :, pl.ds(col, PAGE_SIZE)], vsem).start()
                sc_ref[q_i:q_i + 1, j * S:(j + 1) * S] = s_ref[n, q_i:q_i + 1, :]

    sc = sc_ref[...]
    mx = jnp.maximum(jnp.max(sc, axis=-1, keepdims=True), jnp.max(sn, axis=-1, keepdims=True))
    p = jnp.exp(sc - mx)
    pn = jnp.exp(sn - mx)
    linv = 1.0 / (jnp.sum(p, axis=-1, keepdims=True) + jnp.sum(pn, axis=-1, keepdims=True))
    pw = p * linv
    for h in range(N_HEADS):
        for l in range(DEC_SEQ):
            q_i = h * DEC_SEQ + l
            ph_ref[h, l:l + 1, l * span:(l + 1) * span] = pw[q_i:q_i + 1, :]

    vn = jnp.concatenate([vn_ref[0], jnp.zeros((LANES - DEC_SEQ, ATTN_WIDTH), F32)], axis=0)
    o_new = _dot(pn.astype(BF16), vn.astype(BF16)) * linv
    res = jnp.zeros((DEC_SEQ, ATTN_WIDTH), F32)
    for h in range(N_HEADS):
        res = res + jnp.where(head == h, o_new[h * DEC_SEQ:(h + 1) * DEC_SEQ, :], 0.0)
    res_ref[...] = res


def _moba_sample(page_table, q, kn, vn, slope_col, lq_col, cache_kt, cache_vt):
    P = PAGES_PER_STEP
    chunk = P * PAGE_SIZE
    n_chunks = PAST_LEN // chunk
    n_blocks = PAST_LEN // MOBA_BLOCK
    nq = N_HEADS * DEC_SEQ
    fetched = DEC_SEQ * min(MOBA_TOPK, n_blocks) * MOBA_BLOCK
    row3 = lambda b, pt: (jnp.minimum(b, DEC_BATCH - 1), 0, 0)
    prev3 = lambda b, pt: (jnp.maximum(b - 1, 0), 0, 0)
    const2 = lambda b, pt: (0, 0)
    grid_spec = pltpu.PrefetchScalarGridSpec(
        num_scalar_prefetch=1,
        grid=(DEC_BATCH + 1,),
        in_specs=[pl.BlockSpec((1, DEC_SEQ, ATTN_WIDTH), row3),
                  pl.BlockSpec((1, DEC_SEQ, ATTN_WIDTH), row3),
                  pl.BlockSpec((1, DEC_SEQ, ATTN_WIDTH), row3),
                  pl.BlockSpec((nq, 1), const2),
                  pl.BlockSpec((nq, 1), const2),
                  pl.BlockSpec(memory_space=pl.ANY),
                  pl.BlockSpec(memory_space=pl.ANY)],
        out_specs=pl.BlockSpec((1, DEC_SEQ, ATTN_WIDTH), prev3),
        scratch_shapes=[pltpu.VMEM((CHUNK_SLOTS, P, ATTN_WIDTH, PAGE_SIZE), F32),
                        pltpu.SemaphoreType.DMA((CHUNK_SLOTS,)),
                        pltpu.VMEM((n_blocks, nq, MOBA_BLOCK), F32),
                        pltpu.VMEM((nq, fetched // DEC_SEQ), F32),
                        pltpu.VMEM((ATTN_WIDTH, chunk), BF16),
                        pltpu.VMEM((8, LANES), I32),
                        pltpu.SMEM((8, LANES), I32),
                        pltpu.SemaphoreType.DMA(()),
                        pltpu.VMEM((N_HEADS, HEAD_DIM, fetched), F32),
                        pltpu.SemaphoreType.DMA(()),
                        pltpu.VMEM((LANES, fetched), BF16),
                        pltpu.VMEM((N_HEADS, 2 * DEC_SEQ, fetched), F32),
                        pltpu.VMEM((nq, LANES), F32),
                        pltpu.VMEM((DEC_SEQ, ATTN_WIDTH), F32)])
    return pl.pallas_call(
        _moba_sample_kernel,
        grid_spec=grid_spec,
        out_shape=jax.ShapeDtypeStruct((DEC_BATCH, DEC_SEQ, ATTN_WIDTH), BF16),
        compiler_params=pltpu.CompilerParams(dimension_semantics=("arbitrary",), vmem_limit_bytes=VMEM_LIMIT),
        name="moba_sample",
    )(page_table, q, kn, vn, slope_col, lq_col, cache_kt, cache_vt)


def _mod_rows(ref):
    return ref[...].reshape(-1, ref.shape[-1])


def _outproj_kernel(yp_ref, ya_ref, x_ref, gt1_ref, sh2_ref, sc2_ref, g2_ref, wout_ref, wr_ref, wrhi_ref, br_ref,
                    tri_ref, cntin_ref, x1_ref, h2_ref, meta_ref, ew_ref, cntout_ref, cnt_ref):
    rows = x_ref.shape[0]

    @pl.when(pl.program_id(0) == 0)
    def _():
        cnt_ref[...] = cntin_ref[...]

    mix = _dot(yp_ref[...], wout_ref[0:POOL_WIDTH, :]) + _dot(ya_ref[...], wout_ref[POOL_WIDTH:, :])
    x1 = x_ref[...] + _mod_rows(gt1_ref) * mix
    x1_ref[...] = x1
    h2 = _rms_mod(x1, g2_ref[...], _mod_rows(sc2_ref), _mod_rows(sh2_ref))
    h2_ref[...] = h2
    hh = h2.astype(BF16)
    hl = (h2 - hh.astype(F32)).astype(BF16)
    both = _dot(hh, wr_ref[...])
    lt = (both[:, 0:LANES] + both[:, LANES:] + _dot(hl, wrhi_ref[...])).T + br_ref[...]
    row8 = lax.broadcasted_iota(I32, (8, rows), 0)
    g8 = lt[0:8]
    gmax = jnp.max(g8, axis=0, keepdims=True)
    gsum = jnp.sum(jnp.exp(g8 - gmax), axis=0, keepdims=True)
    g_w = 1.0 / gsum
    g_idx = jnp.min(jnp.where(g8 == gmax, row8, 8), axis=0, keepdims=True)
    e_in = jnp.zeros((8, rows), F32)
    for g in range(N_GROUPS):
        e_in = e_in + jnp.where(g_idx == g, lt[8 + 8 * g:16 + 8 * g], 0.0)
    m1 = jnp.max(e_in, axis=0, keepdims=True)
    i1 = jnp.min(jnp.where(e_in == m1, row8, 8), axis=0, keepdims=True)
    e_rest = jnp.where(row8 == i1, NEG, e_in)
    m2 = jnp.max(e_rest, axis=0, keepdims=True)
    i2 = jnp.min(jnp.where(e_rest == m2, row8, 8), axis=0, keepdims=True)
    r = jnp.exp(m2 - m1)
    w1 = g_w / (1.0 + r)
    w2 = g_w * r / (1.0 + r)
    e1 = g_idx * EXPERTS_PER_GROUP + i1
    e2 = g_idx * EXPERTS_PER_GROUP + i2
    rowe = lax.broadcasted_iota(I32, (N_EXPERTS, rows), 0)
    oh1 = (rowe == e1).astype(F32)
    oh2 = (rowe == e2).astype(F32)
    both_oh = oh1 + oh2
    before = _dot(both_oh.astype(BF16), tri_ref[...]) + cnt_ref[:, 0:1]
    rank1 = jnp.sum(oh1 * before, axis=0, keepdims=True).astype(I32)
    rank2 = jnp.sum(oh2 * before, axis=0, keepdims=True).astype(I32)
    cnt_ref[...] = cnt_ref[...] + jnp.sum(both_oh, axis=-1, keepdims=True)
    cntout_ref[...] = cnt_ref[...]
    meta_ref[...] = jnp.where(row8 == 0, e1, jnp.where(row8 == 1, e2, jnp.where(row8 == 2, rank1,
                                                                                 jnp.where(row8 == 3, rank2, 0))))
    ew_ref[...] = jnp.where(row8 == 0, w1, jnp.where(row8 == 1, w2, 0.0))


def _mod_spec(tile, per_batch_rows):
    if per_batch_rows is None:
        return pl.BlockSpec((tile, D_MODEL), lambda i: (i, 0))
    per = per_batch_rows // tile
    return pl.BlockSpec((1, 1, D_MODEL), lambda i: (i // per, 0, 0))


def _outproj(yp, ya, x, gt1, sh2, sc2, g2, wout, wr, wrhi, br, cnt_in, per_batch_rows):
    n = x.shape[0]
    tile = min(ROW_TILE, n)
    tok = lambda i: (i, 0)
    const2 = lambda i: (0, 0)
    mod_spec = _mod_spec(tile, per_batch_rows)
    tri = (jnp.arange(tile)[:, None] < jnp.arange(tile)[None, :]).astype(BF16)
    return pl.pallas_call(
        _outproj_kernel,
        grid=(n // tile,),
        in_specs=[pl.BlockSpec((tile, POOL_WIDTH), tok),
                  pl.BlockSpec((tile, ATTN_WIDTH), tok),
                  pl.BlockSpec((tile, D_MODEL), tok),
                  mod_spec, mod_spec, mod_spec,
                  pl.BlockSpec((1, D_MODEL), const2),
                  pl.BlockSpec((2 * POOL_WIDTH, D_MODEL), const2),
                  pl.BlockSpec((D_MODEL, 2 * LANES), const2),
                  pl.BlockSpec((D_MODEL, LANES), const2),
                  pl.BlockSpec((LANES, 1), const2),
                  pl.BlockSpec((tile, tile), const2),
                  pl.BlockSpec((N_EXPERTS, LANES), const2)],
        out_specs=[pl.BlockSpec((tile, D_MODEL), tok),
                   pl.BlockSpec((tile, D_MODEL), tok),
                   pl.BlockSpec((8, tile), lambda i: (0, i)),
                   pl.BlockSpec((8, tile), lambda i: (0, i)),
                   pl.BlockSpec((N_EXPERTS, LANES), const2)],
        out_shape=[jax.ShapeDtypeStruct((n, D_MODEL), F32),
                   jax.ShapeDtypeStruct((n, D_MODEL), F32),
                   jax.ShapeDtypeStruct((8, n), I32),
                   jax.ShapeDtypeStruct((8, n), F32),
                   jax.ShapeDtypeStruct((N_EXPERTS, LANES), F32)],
        scratch_shapes=[pltpu.VMEM((N_EXPERTS, LANES), F32)],
        compiler_params=pltpu.CompilerParams(dimension_semantics=("arbitrary",), vmem_limit_bytes=VMEM_LIMIT),
        name="outproj",
    )(yp, ya, x, gt1, sh2, sc2, g2, wout, wr, wrhi, br, tri, cnt_in)


def _dest_kernel(pstart_ref, meta_ref, dest_ref):
    n = meta_ref.shape[1]
    eid = meta_ref[0:EXPERT_TOPK, :]
    start = jnp.zeros((EXPERT_TOPK, n), I32)
    for e in range(N_EXPERTS):
        start = jnp.where(eid == e, pstart_ref[e], start)
    dest_ref[...] = start + meta_ref[EXPERT_TOPK:2 * EXPERT_TOPK, :]


def _dest_rows(pstart, meta):
    n = meta.shape[1]
    dest = pl.pallas_call(
        _dest_kernel,
        in_specs=[pl.BlockSpec(memory_space=pltpu.SMEM), pl.BlockSpec(memory_space=pltpu.VMEM)],
        out_specs=pl.BlockSpec(memory_space=pltpu.VMEM),
        out_shape=jax.ShapeDtypeStruct((EXPERT_TOPK, n), I32),
        name="dest",
    )(pstart, meta)
    return [dest[slot] for slot in range(EXPERT_TOPK)]


DMA_ISSUE_UNROLL = 16


SUBLANES = 8
PAD_CHUNKS = tuple(MOE_ROWS >> s for s in range(1, MOE_ROWS.bit_length() - 3))


def _issue_row_copies(n_tokens, copy_for):
    group = DMA_ISSUE_UNROLL // SUBLANES

    def issue(g, carry):
        for k in range(DMA_ISSUE_UNROLL):
            for slot in range(EXPERT_TOPK):
                copy_for(slot, g * DMA_ISSUE_UNROLL + k, g * group + k // SUBLANES, k % SUBLANES).start()
        return carry

    lax.fori_loop(0, n_tokens // DMA_ISSUE_UNROLL, issue, 0)


def _scatter_rows(dest_refs, h_ref, xr_ref, sem):
    rows = h_ref.shape[0] * SUBLANES

    def copy_for(slot, t, tile, sub):
        return pltpu.make_async_copy(h_ref.at[tile, pl.ds(sub, 1), :],
                                     xr_ref.at[pl.ds(dest_refs[slot][t], 1), :], sem)

    _issue_row_copies(rows, copy_for)
    for _ in dest_refs:
        tile_rows = xr_ref.at[pl.ds(0, rows), :]
        pltpu.make_async_copy(tile_rows, tile_rows, sem).wait()


def _dispatch_kernel(*refs):
    dest_refs = refs[0:EXPERT_TOPK]
    pstart_ref, cnt_ref, nact_ref, hp_ref, hs_ref, xr_ref, z_ref, sem = refs[EXPERT_TOPK:]
    i = pl.program_id(0)
    last = pl.num_programs(0) - 1

    @pl.when(i < last)
    def _():
        _scatter_rows(dest_refs, hp_ref, xr_ref, sem)

    @pl.when(i == last)
    def _():
        _scatter_rows(dest_refs, hs_ref, xr_ref, sem)
        z_ref[...] = jnp.zeros_like(z_ref)
        _zero_unassigned_rows(pstart_ref, cnt_ref, nact_ref, xr_ref, z_ref, sem)


def _dispatch_rows(dests, pstart, counts, n_active, h2_p, h2_s, n_rows):
    n_p = h2_p.shape[0]
    assert n_p % ROW_TILE == 0 and h2_s.shape[0] <= ROW_TILE
    p_tiles = n_p // ROW_TILE
    smem_rows = pl.BlockSpec((ROW_TILE,), lambda i: (i,), memory_space=pltpu.SMEM)
    smem = pl.BlockSpec(memory_space=pltpu.SMEM)
    dests = [jnp.pad(d, (0, (p_tiles + 1) * ROW_TILE - d.shape[0])) for d in dests]
    h2_p = h2_p.reshape(-1, SUBLANES, D_MODEL)
    h2_s = h2_s.reshape(-1, SUBLANES, D_MODEL)
    return pl.pallas_call(
        _dispatch_kernel,
        grid=(p_tiles + 1,),
        in_specs=[smem_rows] * EXPERT_TOPK + [
            smem, smem, smem,
            pl.BlockSpec((ROW_TILE // SUBLANES, SUBLANES, D_MODEL), lambda i: (jnp.minimum(i, p_tiles - 1), 0, 0)),
            pl.BlockSpec(h2_s.shape, lambda i: (0, 0, 0))],
        out_specs=pl.BlockSpec(memory_space=pl.ANY),
        out_shape=jax.ShapeDtypeStruct((n_rows, D_MODEL), F32),
        scratch_shapes=[pltpu.VMEM((MOE_ROWS, D_MODEL), F32), pltpu.SemaphoreType.DMA(())],
        compiler_params=pltpu.CompilerParams(dimension_semantics=("arbitrary",), vmem_limit_bytes=VMEM_LIMIT),
        name="dispatch",
    )(*dests, pstart, counts, n_active, h2_p, h2_s)


def _zero_unassigned_rows(pstart_ref, cnt_ref, nact_ref, xr_ref, z_ref, sem):
    assert MOE_ROWS - 1 == SUBLANES - 1 + sum(PAD_CHUNKS)
    n_blocks = xr_ref.shape[0] // MOE_ROWS

    def trailing(act):
        def body(blk, carry):
            dst = xr_ref.at[pl.ds(pl.multiple_of(blk * MOE_ROWS, MOE_ROWS), MOE_ROWS), :]
            act(pltpu.make_async_copy(z_ref, dst, sem))
            return carry
        lax.fori_loop(nact_ref[0], n_blocks, body, 0)

    def copies(act):
        trailing(act)
        for e in range(N_EXPERTS):
            cnt = cnt_ref[e]
            first = pstart_ref[e] + cnt
            n_pad = (MOE_ROWS - (cnt & (MOE_ROWS - 1))) & (MOE_ROWS - 1)
            head = (SUBLANES - (first & (SUBLANES - 1))) & (SUBLANES - 1)
            head = jnp.minimum(head, n_pad)
            for k in range(SUBLANES - 1):
                @pl.when(k < head)
                def _():
                    act(pltpu.make_async_copy(z_ref.at[pl.ds(0, 1), :], xr_ref.at[pl.ds(first + k, 1), :], sem))
            start = first + head
            body = n_pad - head
            for rows in PAD_CHUNKS:
                @pl.when((body & rows) != 0)
                def _():
                    dst = xr_ref.at[pl.ds(pl.multiple_of(start, SUBLANES), rows), :]
                    act(pltpu.make_async_copy(z_ref.at[pl.ds(0, rows), :], dst, sem))
                start = start + (body & rows)

    copies(lambda c: c.start())
    copies(lambda c: c.wait())


def _moe_kernel(blk_e_ref, nact_ref, x_ref, wg_ref, wu_ref, wd_ref, y_ref, wgb_ref, wub_ref, wdb_ref):
    i = pl.program_id(0)
    prev = blk_e_ref[jnp.maximum(i - 1, 0)]
    active = i < nact_ref[0]

    @pl.when(active & ((i == 0) | (blk_e_ref[i] != prev)))
    def _():
        wgb_ref[...] = wg_ref[0].astype(BF16)
        wub_ref[...] = wu_ref[0].astype(BF16)
        wdb_ref[...] = wd_ref[0].astype(BF16)

    @pl.when(active)
    def _():
        x = x_ref[...].astype(BF16)
        g = _dot(x, wgb_ref[...])
        u = _dot(x, wub_ref[...])
        a = g / (1.0 + jnp.exp(-g)) * u
        y_ref[...] = _dot(a.astype(BF16), wdb_ref[...])

    @pl.when(jnp.logical_not(active))
    def _():
        y_ref[...] = jnp.zeros_like(y_ref)


def _moe(blk_e, n_active, xr, w_gate, w_up, w_down):
    n_rows = xr.shape[0]
    n_blocks = n_rows // MOE_ROWS
    grid_spec = pltpu.PrefetchScalarGridSpec(
        num_scalar_prefetch=2,
        grid=(n_blocks,),
        in_specs=[pl.BlockSpec((MOE_ROWS, D_MODEL), lambda i, be, na: (jnp.minimum(i, na[0] - 1), 0)),
                  pl.BlockSpec((1, D_MODEL, D_FF), lambda i, be, na: (be[i], 0, 0)),
                  pl.BlockSpec((1, D_MODEL, D_FF), lambda i, be, na: (be[i], 0, 0)),
                  pl.BlockSpec((1, D_FF, D_MODEL), lambda i, be, na: (be[i], 0, 0))],
        out_specs=pl.BlockSpec((MOE_ROWS, D_MODEL), lambda i, be, na: (i, 0)),
        scratch_shapes=[pltpu.VMEM((D_MODEL, D_FF), BF16),
                        pltpu.VMEM((D_MODEL, D_FF), BF16),
                        pltpu.VMEM((D_FF, D_MODEL), BF16)])
    return pl.pallas_call(
        _moe_kernel,
        grid_spec=grid_spec,
        out_shape=jax.ShapeDtypeStruct((n_rows, D_MODEL), F32),
        compiler_params=pltpu.CompilerParams(
            dimension_semantics=("arbitrary",), vmem_limit_bytes=VMEM_LIMIT),
        name="moe",
    )(blk_e, n_active, xr, w_gate, w_up, w_down)


def _final_kernel(*refs):
    dest_refs = refs[0:EXPERT_TOPK]
    x1_ref, w_ref, gt2_ref, yr_ref, y_ref, o_ref, sem = refs[EXPERT_TOPK:]
    rows = x1_ref.shape[0]

    def copy_for(slot, t, tile, sub):
        return pltpu.make_async_copy(yr_ref.at[pl.ds(dest_refs[slot][t], 1), :],
                                     o_ref.at[slot, tile, pl.ds(sub, 1), :], sem)

    _issue_row_copies(rows, copy_for)
    for slot in range(EXPERT_TOPK):
        pltpu.make_async_copy(o_ref.at[slot], o_ref.at[slot], sem).wait()
    w = w_ref[...]
    o = [o_ref[slot].reshape(rows, D_MODEL) for slot in range(EXPERT_TOPK)]
    moe = o[0] * w[:, 0:1] + o[1] * w[:, 1:2]
    y_ref[...] = x1_ref[...] + _mod_rows(gt2_ref) * moe


def _final(dests, x1, w2, gt2, yr, per_batch_rows):
    n = x1.shape[0]
    tile = min(ROW_TILE, n)
    tok = lambda i: (i, 0)
    smem_rows = pl.BlockSpec((tile,), lambda i: (i,), memory_space=pltpu.SMEM)
    return pl.pallas_call(
        _final_kernel,
        grid=(n // tile,),
        in_specs=[smem_rows] * EXPERT_TOPK + [pl.BlockSpec((tile, D_MODEL), tok),
                                             pl.BlockSpec((tile, EXPERT_TOPK), tok),
                                             _mod_spec(tile, per_batch_rows),
                                             pl.BlockSpec(memory_space=pl.ANY)],
        out_specs=pl.BlockSpec((tile, D_MODEL), tok),
        out_shape=jax.ShapeDtypeStruct((n, D_MODEL), F32),
        scratch_shapes=[pltpu.VMEM((EXPERT_TOPK, tile // SUBLANES, SUBLANES, D_MODEL), F32),
                        pltpu.SemaphoreType.DMA(())],
        compiler_params=pltpu.CompilerParams(dimension_semantics=("arbitrary",), vmem_limit_bytes=VMEM_LIMIT),
        name="final",
    )(*dests, x1, w2, gt2, yr)


def _expert_layout(counts, n_blocks):
    padded = (counts + MOE_ROWS - 1) // MOE_ROWS * MOE_ROWS
    pend = jnp.cumsum(padded)
    pstart = (pend - padded).astype(I32)
    blk_start = jnp.arange(n_blocks, dtype=I32) * MOE_ROWS
    blk_e = jnp.minimum(jnp.sum((pend[None, :] <= blk_start[:, None]).astype(I32), axis=1), N_EXPERTS - 1)
    n_active = (pend[-1] // MOE_ROWS).astype(I32).reshape(1)
    return pstart, blk_e.astype(I32), n_active


def kernel(x_prompt, x_sample, cache_k, cache_v, state_pool, page_table, c_prompt, c_sample, w_ada, b_ada,
           g_attn_norm, w_in, g_q, g_k, w_pool, pool_scale, w_out, g_ffn_norm, w_group, b_group, w_expert,
           b_expert, w_gate, w_up, w_down):
    D = D_MODEL
    B, T, _ = x_prompt.shape
    n_s = DEC_BATCH * DEC_SEQ
    n_p = B * T
    layer = 0

    win = w_in[layer].astype(BF16)
    wout = w_out[layer].astype(BF16)
    wpool = w_pool[layer].astype(BF16)
    g1 = g_attn_norm[layer].reshape(1, D)
    g2 = g_ffn_norm[layer].reshape(1, D)
    gq = jnp.tile(g_q[layer], N_HEADS).reshape(1, ATTN_WIDTH)
    gk = jnp.tile(g_k[layer], N_HEADS).reshape(1, ATTN_WIDTH)
    ps = pool_scale[layer].reshape(1, POOL_WIDTH)
    hd = jnp.arange(ATTN_WIDTH) // HEAD_DIM
    bd = (hd[:, None] == hd[None, :]).astype(BF16)
    slopes = jnp.exp2(-8.0 * (jnp.arange(N_HEADS, dtype=F32) + 1.0) / N_HEADS)
    qrow = jnp.arange(N_HEADS * DEC_SEQ)
    slope_col = slopes[qrow // DEC_SEQ].reshape(-1, 1)
    lq_col = (qrow % DEC_SEQ).astype(F32).reshape(-1, 1)
    wr = jnp.zeros((D, LANES), F32).at[:, 0:N_GROUPS].set(w_group[layer]).at[:, 8:8 + N_EXPERTS].set(w_expert[layer])
    wr_hi = wr.astype(BF16)
    wr_lo = (wr - wr_hi.astype(F32)).astype(BF16)
    wr_both = jnp.concatenate([wr_hi, wr_lo], axis=1)
    br = jnp.zeros((LANES,), F32).at[0:N_GROUPS].set(b_group[layer]).at[N_GROUPS:8].set(NEG)
    br = br.at[8:8 + N_EXPERTS].set(b_expert[layer]).reshape(LANES, 1)

    mod = _ada(jnp.concatenate([c_prompt, c_sample], axis=0), w_ada[layer], b_ada[layer])
    mod_p = mod[:B].reshape(B, 1, 6 * D)
    mod_s = jnp.repeat(mod[B:], DEC_SEQ, axis=0)

    ypool_p, q_p, k_p, v_p, tail_p = _inproj_prompt(x_prompt, mod_p, g1, win, bd, gq, gk, wpool, ps)
    yattn_p = _moba_prompt(slopes, q_p, k_p, v_p)
    mp = lambda j: mod_p[:, :, j * D:(j + 1) * D]
    cnt0 = jnp.zeros((N_EXPERTS, LANES), F32)
    x1_p, h2_p, meta_p, ew_p, cnt_p = _outproj(
        ypool_p.reshape(n_p, POOL_WIDTH), yattn_p.reshape(n_p, ATTN_WIDTH), x_prompt.reshape(n_p, D),
        mp(2), mp(3), mp(4), g2, wout, wr_both, wr_hi, br, cnt0, T)

    ms = lambda j: mod_s[:, j * D:(j + 1) * D]
    hist = jnp.concatenate([jnp.zeros((DEC_BATCH, HALO - POOL_HIST, POOL_WIDTH), F32), state_pool[layer]], axis=1)
    xs = x_sample.reshape(n_s, D)
    ypool_s, q_s, k_s, v_s, tail_s = _inproj_sample(xs, ms(0), ms(1), g1, win, bd, gq, gk, hist, wpool, ps)
    r3 = lambda a: a.reshape(DEC_BATCH, DEC_SEQ, ATTN_WIDTH)
    n_phys = cache_k.shape[1]
    ck = jnp.transpose(cache_k[layer], (0, 2, 3, 1)).reshape(n_phys, ATTN_WIDTH, PAGE_SIZE)
    cv = jnp.transpose(cache_v[layer], (0, 2, 3, 1)).reshape(n_phys, ATTN_WIDTH, PAGE_SIZE)
    yattn_s = _moba_sample(page_table, r3(q_s), r3(k_s), r3(v_s), slope_col, lq_col, ck, cv)
    x1_s, h2_s, meta_s, ew_s, cnt_s = _outproj(ypool_s, yattn_s.reshape(n_s, ATTN_WIDTH), xs, ms(2), ms(3), ms(4),
                                               g2, wout, wr_both, wr_hi, br, cnt_p, None)

    n_blocks = (n_p + n_s) * EXPERT_TOPK // MOE_ROWS + N_EXPERTS
    counts = cnt_s[:, 0].astype(I32)
    pstart, blk_e, n_active = _expert_layout(counts, n_blocks)
    dests_p = _dest_rows(pstart, meta_p)
    dests_s = _dest_rows(pstart, meta_s)
    dests = [jnp.concatenate([dp, ds]) for dp, ds in zip(dests_p, dests_s)]
    xr = _dispatch_rows(dests, pstart, counts, n_active, h2_p, h2_s, n_blocks * MOE_ROWS)
    yr = _moe(blk_e, n_active, xr, w_gate[layer], w_up[layer], w_down[layer])
    y_p = _final(dests_p, x1_p, ew_p[0:EXPERT_TOPK].T, mp(5), yr, T)
    y_s = _final(dests_s, x1_s, ew_s[0:EXPERT_TOPK].T, ms(5), yr, None)

    k4 = lambda a, b, l: a.reshape(1, b, l, N_HEADS, HEAD_DIM)
    return (y_p.reshape(B, T, D), y_s.reshape(DEC_BATCH, DEC_SEQ, D),
            k4(k_p, B, T), k4(v_p, B, T), tail_p[None, :, HALO - POOL_HIST:, :],
            k4(k_s, DEC_BATCH, DEC_SEQ), k4(v_s, DEC_BATCH, DEC_SEQ), tail_s[None, :, HALO - POOL_HIST:, :])
```

```python
import jax
import jax.numpy as jnp
from jax import lax
from jax.experimental import pallas as pl
from jax.experimental.pallas import tpu as pltpu

F32 = jnp.float32
BF16 = jnp.bfloat16
I32 = jnp.int32

D_MODEL = 1024
DEC_BATCH = 32
DEC_SEQ = 8
PAST_LEN = 16384
PAGE_SIZE = 128
POOL_WIDTH = 512
POOL_WINDOWS = (2, 4, 8, 16)
POOL_GROUP = 128
POOL_HIST = 15
HALO = 16
N_HEADS = 8
HEAD_DIM = 64
ATTN_WIDTH = 512
MOBA_BLOCK = 256
MOBA_TOPK = 3
ATTN_SCALE = HEAD_DIM ** -0.5
MIX_IN = POOL_WIDTH + 3 * ATTN_WIDTH
N_GROUPS = 4
EXPERTS_PER_GROUP = 8
N_EXPERTS = 32
EXPERT_TOPK = 2
D_FF = 512
EPS = 1e-6
NEG = -1e30

LANES = 128
ROW_TILE = 512
INPROJ_ROWS = 1024
MOE_ROWS = 512
PAGES_PER_STEP = 16
VMEM_LIMIT = 56 * 1024 * 1024

_NT = (((1,), (1,)), ((), ()))


def _dot(a, b):
    return jnp.dot(a, b, preferred_element_type=F32)


def _dot_nt(a, b):
    return lax.dot_general(a, b, _NT, preferred_element_type=F32)


def _split_dot(a, b01):
    hi = a.astype(BF16)
    lo = (a - hi.astype(F32)).astype(BF16)
    return _dot(hi, b01) + _dot(lo, b01)


def _rms_mod(x, g, sc, sh):
    ms = jnp.mean(x * x, axis=-1, keepdims=True)
    return x * lax.rsqrt(ms + EPS) * g * (1.0 + sc) + sh


def _ada_kernel(c_ref, w_ref, b_ref, o_ref):
    c = c_ref[...]
    a = c / (1.0 + jnp.exp(-c))
    o_ref[...] = _dot(a.astype(BF16), w_ref[...].astype(BF16)) + b_ref[...]


def _ada(c_all, w_ada, b_ada):
    n = c_all.shape[0]
    tn = 1536
    return pl.pallas_call(
        _ada_kernel,
        grid=(6 * D_MODEL // tn,),
        in_specs=[pl.BlockSpec((n, D_MODEL), lambda j: (0, 0)),
                  pl.BlockSpec((D_MODEL, tn), lambda j: (0, j)),
                  pl.BlockSpec((1, tn), lambda j: (0, j))],
        out_specs=pl.BlockSpec((n, tn), lambda j: (0, j)),
        out_shape=jax.ShapeDtypeStruct((n, 6 * D_MODEL), F32),
        compiler_params=pltpu.CompilerParams(vmem_limit_bytes=VMEM_LIMIT),
        name="ada",
    )(c_all, w_ada, b_ada.reshape(1, -1))


def _inproj_core(x, sh1, sc1, g, win_ref, bd_ref, gq_ref, gk_ref):
    h = _rms_mod(x, g, sc1, sh1)
    z = _dot(h.astype(BF16), win_ref[...])
    u = z[:, 0:POOL_WIDTH]
    q = z[:, POOL_WIDTH:POOL_WIDTH + ATTN_WIDTH]
    k = z[:, POOL_WIDTH + ATTN_WIDTH:POOL_WIDTH + 2 * ATTN_WIDTH]
    v = z[:, POOL_WIDTH + 2 * ATTN_WIDTH:]
    bd = bd_ref[...]
    q = q * lax.rsqrt(_split_dot(q * q, bd) * (1.0 / HEAD_DIM) + EPS) * gq_ref[...]
    k = k * lax.rsqrt(_split_dot(k * k, bd) * (1.0 / HEAD_DIM) + EPS) * gk_ref[...]
    return u, q, k, v


def _window_sum(e, w):
    s = e
    sh = 1
    while sh < w:
        s = s + pltpu.roll(s, sh, axis=0)
        sh *= 2
    return s


def _inproj_prompt_kernel(x_ref, mod_ref, g_ref, win_ref, bd_ref, gq_ref, gk_ref, wpool_ref, ps_ref,
                          ypool_ref, q_ref, k_ref, v_ref, tail_ref, ext_ref):
    t = pl.program_id(1)
    nt = pl.num_programs(1)
    x = x_ref[0]
    sh1 = mod_ref[0, :, 0:D_MODEL]
    sc1 = mod_ref[0, :, D_MODEL:2 * D_MODEL]
    u, q, k, v = _inproj_core(x, sh1, sc1, g_ref[...], win_ref, bd_ref, gq_ref, gk_ref)
    q_ref[0] = (q * ATTN_SCALE).astype(BF16)
    k_ref[0] = k
    v_ref[0] = v

    @pl.when(t == 0)
    def _():
        ext_ref[0:HALO, :] = jnp.zeros((HALO, POOL_WIDTH), F32)

    ext_ref[HALO:, :] = u
    pos = t * INPROJ_ROWS + lax.broadcasted_iota(I32, (INPROJ_ROWS, 1), 0)
    for gi, w in enumerate(POOL_WINDOWS):
        cols = slice(gi * POOL_GROUP, (gi + 1) * POOL_GROUP)
        win = _window_sum(ext_ref[:, cols], w)[HALO:]
        inv_cnt = 1.0 / jnp.minimum(pos + 1, w).astype(F32)
        pooled = win * inv_cnt - u[:, cols]
        y = _dot(pooled.astype(BF16), wpool_ref[gi]) * ps_ref[:, cols]
        ypool_ref[0, :, cols] = y.astype(BF16)
    last = u[INPROJ_ROWS - HALO:, :]
    ext_ref[0:HALO, :] = last

    @pl.when(t == nt - 1)
    def _():
        tail_ref[0] = last


def _inproj_prompt(x, mod_p, g1, win, bd, gq, gk, wpool, ps):
    B, T, D = x.shape
    nt = T // INPROJ_ROWS
    const2 = lambda b, t: (0, 0)
    tok = lambda b, t: (b, t, 0)
    return pl.pallas_call(
        _inproj_prompt_kernel,
        grid=(B, nt),
        in_specs=[pl.BlockSpec((1, INPROJ_ROWS, D), tok),
                  pl.BlockSpec((1, 1, 6 * D), lambda b, t: (b, 0, 0)),
                  pl.BlockSpec((1, D), const2),
                  pl.BlockSpec((D, MIX_IN), const2),
                  pl.BlockSpec((ATTN_WIDTH, ATTN_WIDTH), const2),
                  pl.BlockSpec((1, ATTN_WIDTH), const2),
                  pl.BlockSpec((1, ATTN_WIDTH), const2),
                  pl.BlockSpec((4, POOL_GROUP, POOL_GROUP), lambda b, t: (0, 0, 0)),
                  pl.BlockSpec((1, POOL_WIDTH), const2)],
        out_specs=[pl.BlockSpec((1, INPROJ_ROWS, POOL_WIDTH), tok),
                   pl.BlockSpec((1, INPROJ_ROWS, ATTN_WIDTH), tok),
                   pl.BlockSpec((1, INPROJ_ROWS, ATTN_WIDTH), tok),
                   pl.BlockSpec((1, INPROJ_ROWS, ATTN_WIDTH), tok),
                   pl.BlockSpec((1, HALO, POOL_WIDTH), lambda b, t: (b, 0, 0))],
        out_shape=[jax.ShapeDtypeStruct((B, T, POOL_WIDTH), BF16),
                   jax.ShapeDtypeStruct((B, T, ATTN_WIDTH), BF16),
                   jax.ShapeDtypeStruct((B, T, ATTN_WIDTH), F32),
                   jax.ShapeDtypeStruct((B, T, ATTN_WIDTH), F32),
                   jax.ShapeDtypeStruct((B, HALO, POOL_WIDTH), F32)],
        scratch_shapes=[pltpu.VMEM((HALO + INPROJ_ROWS, POOL_WIDTH), F32)],
        compiler_params=pltpu.CompilerParams(
            dimension_semantics=("arbitrary", "arbitrary"), vmem_limit_bytes=VMEM_LIMIT),
        name="inproj_prompt",
    )(x, mod_p, g1, win, bd, gq, gk, wpool, ps)


def _inproj_sample_kernel(x_ref, sh_ref, sc_ref, g_ref, win_ref, bd_ref, gq_ref, gk_ref, hist_ref, wpool_ref,
                          ps_ref, ypool_ref, q_ref, k_ref, v_ref, tail_ref, ext_ref):
    n = DEC_BATCH * DEC_SEQ
    ext_rows = HALO + DEC_SEQ
    u, q, k, v = _inproj_core(x_ref[...], sh_ref[...], sc_ref[...], g_ref[...], win_ref, bd_ref, gq_ref, gk_ref)
    q_ref[...] = q * ATTN_SCALE
    k_ref[...] = k
    v_ref[...] = v
    ext_ref[:, 0:HALO, :] = hist_ref[...]
    ext_ref[:, HALO:, :] = u.reshape(DEC_BATCH, DEC_SEQ, POOL_WIDTH)
    tail_ref[...] = ext_ref[:, ext_rows - HALO:, :]
    pos = PAST_LEN + lax.broadcasted_iota(I32, (DEC_BATCH, DEC_SEQ, 1), 1).reshape(n, 1)
    for gi, w in enumerate(POOL_WINDOWS):
        cols = slice(gi * POOL_GROUP, (gi + 1) * POOL_GROUP)
        e = ext_ref[:, :, cols].reshape(DEC_BATCH * ext_rows, POOL_GROUP)
        win = _window_sum(e, w).reshape(DEC_BATCH, ext_rows, POOL_GROUP)[:, HALO:, :].reshape(n, POOL_GROUP)
        inv_cnt = 1.0 / jnp.minimum(pos + 1, w).astype(F32)
        pooled = win * inv_cnt - u[:, cols]
        y = _dot(pooled.astype(BF16), wpool_ref[gi]) * ps_ref[:, cols]
        ypool_ref[:, cols] = y.astype(BF16)


def _inproj_sample(x, sh1, sc1, g1, win, bd, gq, gk, hist, wpool, ps):
    n = x.shape[0]
    return pl.pallas_call(
        _inproj_sample_kernel,
        out_shape=[jax.ShapeDtypeStruct((n, POOL_WIDTH), BF16),
                   jax.ShapeDtypeStruct((n, ATTN_WIDTH), F32),
                   jax.ShapeDtypeStruct((n, ATTN_WIDTH), F32),
                   jax.ShapeDtypeStruct((n, ATTN_WIDTH), F32),
                   jax.ShapeDtypeStruct((DEC_BATCH, HALO, POOL_WIDTH), F32)],
        scratch_shapes=[pltpu.VMEM((DEC_BATCH, HALO + DEC_SEQ, POOL_WIDTH), F32)],
        compiler_params=pltpu.CompilerParams(vmem_limit_bytes=VMEM_LIMIT),
        name="inproj_sample",
    )(x, sh1, sc1, g1, win, bd, gq, gk, hist, wpool, ps)


N_BIAS_LANES = 3
PROMPT_Q_TILE = 2 * MOBA_BLOCK


def _moba_prompt_kernel(slopes_ref, q_ref, k_ref, v_ref, o_ref, ka_ref, va_ref, qa_ref, kmf_ref):
    S = MOBA_BLOCK
    T = k_ref.shape[1]
    nb = T // S
    hp = pl.program_id(1)
    lane = lax.broadcasted_iota(I32, (1, LANES), 1)
    real = (lane < HEAD_DIM, lane >= HEAD_DIM)
    extra = (lane - HEAD_DIM, lane)

    kf = k_ref[0]
    vf = v_ref[0]
    kmf_ref[...] = jnp.zeros((LANES, LANES), F32)
    for j in range(nb):
        mean = jnp.sum(kf[j * S:(j + 1) * S], axis=0, keepdims=True) * (1.0 / S)
        kmf_ref[j:j + 1, :] = jnp.where(real[0], mean, 0.0)
        kmf_ref[nb + j:nb + j + 1, :] = jnp.where(real[0], 0.0, mean)
    key_i = lax.broadcasted_iota(I32, (T, 1), 0)
    key_blk = key_i // S
    for h in range(2):
        b = slopes_ref[2 * hp + h] * key_i.astype(F32)
        p0 = b.astype(BF16).astype(F32)
        p1 = (b - p0).astype(BF16).astype(F32)
        p2 = b - p0 - p1
        e = extra[h]
        onehot = jnp.where((e >= N_BIAS_LANES) & (e - N_BIAS_LANES == key_blk), 1.0, 0.0)
        feat = jnp.where(e == 0, p0, jnp.where(e == 1, p1, jnp.where(e == 2, p2, onehot)))
        ka_ref[h] = jnp.where(real[h], kf, feat).astype(BF16)
        va_ref[h] = jnp.where(real[h], vf, 1.0).astype(BF16)

    q2 = q_ref[0]

    gt = _dot_nt(kmf_ref[...].astype(BF16), q2)[0:2 * nb]
    row = lax.broadcasted_iota(I32, (2 * nb, T), 0)
    blk = row % nb
    cbq = lax.broadcasted_iota(I32, (2 * nb, T), 1) // S
    cnt = jnp.zeros((2 * nb, T), I32)
    for m in range(nb):
        gm = jnp.where(row < nb, gt[m:m + 1, :], gt[nb + m:nb + m + 1, :])
        beats = (gm > gt) | ((gm == gt) & (m < blk))
        cnt = cnt + jnp.where(beats & (m < cbq), 1, 0)
    keep = (((cnt < MOBA_TOPK) & (blk < cbq)) | (blk == cbq)).astype(F32)
    keepq = jnp.concatenate([keep, jnp.zeros((LANES - 2 * nb, T), F32)], axis=0).T
    maskv = jnp.where(keepq > 0.5, 0.0, NEG)
    mask_lanes = (pltpu.roll(maskv, HEAD_DIM + N_BIAS_LANES, axis=1),
                  pltpu.roll(maskv, (N_BIAS_LANES - nb) % LANES, axis=1))
    qf = q2.astype(F32)
    for h in range(2):
        e = extra[h]
        feat = jnp.where(e < N_BIAS_LANES, 1.0, jnp.where(e < N_BIAS_LANES + nb, mask_lanes[h], 0.0))
        qa_ref[h] = jnp.where(real[h], qf, feat).astype(BF16)

    QT = PROMPT_Q_TILE
    causal = lax.broadcasted_iota(I32, (QT, QT), 1) <= lax.broadcasted_iota(I32, (QT, QT), 0)
    for t in range(T // QT):
        rows = slice(t * QT, (t + 1) * QT)
        n = (t + 1) * QT
        outs = []
        for h in range(2):
            s = _dot_nt(qa_ref[h, rows, :], ka_ref[h, 0:n, :])
            s_own = jnp.where(causal, s[:, t * QT:n], NEG)
            s = s_own if t == 0 else jnp.concatenate([s[:, 0:t * QT], s_own], axis=1)
            p = jnp.exp(s - jnp.max(s, axis=-1, keepdims=True))
            outs.append(_dot(p.astype(BF16), va_ref[h, 0:n, :]))
        a0, a1 = outs
        o0 = a0 * (1.0 / a0[:, HEAD_DIM:HEAD_DIM + 1])
        o1 = a1 * (1.0 / a1[:, 0:1])
        o_ref[0, rows, :] = jnp.where(real[0], o0, o1).astype(BF16)


def _moba_prompt(slopes, q, k, v):
    B, T, _ = q.shape
    seq = lambda b, hp: (b, 0, hp)
    return pl.pallas_call(
        _moba_prompt_kernel,
        grid=(B, N_HEADS // 2),
        in_specs=[pl.BlockSpec(memory_space=pltpu.SMEM),
                  pl.BlockSpec((1, T, LANES), seq),
                  pl.BlockSpec((1, T, LANES), seq),
                  pl.BlockSpec((1, T, LANES), seq)],
        out_specs=pl.BlockSpec((1, T, LANES), seq),
        out_shape=jax.ShapeDtypeStruct((B, T, ATTN_WIDTH), BF16),
        scratch_shapes=[pltpu.VMEM((2, T, LANES), BF16),
                        pltpu.VMEM((2, T, LANES), BF16),
                        pltpu.VMEM((2, T, LANES), BF16),
                        pltpu.VMEM((LANES, LANES), F32)],
        compiler_params=pltpu.CompilerParams(
            dimension_semantics=("arbitrary", "arbitrary"), vmem_limit_bytes=VMEM_LIMIT),
        name="moba_prompt",
    )(slopes, q, k, v)


CHUNK_SLOTS = 6


def _moba_sample_kernel(pt_ref, q_ref, kn_ref, vn_ref, slope_ref, lq_ref, ck_ref, cv_ref, o_ref,
                        buf_ref, sem, s_ref, sc_ref, kc_ref, idx_v, idx_s, idx_sem, vbuf_ref, vsem, vb16_ref, ph_ref,
                        gate_ref, res_ref):
    P = PAGES_PER_STEP
    S = MOBA_BLOCK
    chunk = P * PAGE_SIZE
    n_chunks = PAST_LEN // chunk
    bpc = chunk // S
    n_loads = n_chunks
    ahead = CHUNK_SLOTS - 1
    nq = N_HEADS * DEC_SEQ
    b = pl.program_id(0)
    n_rows = pl.num_programs(0) - 1
    slope = slope_ref[...]
    lq = lq_ref[...]
    lane = lax.broadcasted_iota(I32, (1, LANES), 1)

    def slot_of(bb, i):
        return lax.rem(bb * n_loads + i, CHUNK_SLOTS)

    def start_load(bb, i):
        slot = slot_of(bb, i)
        for r in range(P):
            page = pt_ref[bb, i * P + r]
            pltpu.make_async_copy(ck_ref.at[page], buf_ref.at[slot, r], sem.at[slot]).start()

    def load_chunk(i):
        nxt = i + ahead
        if nxt < n_loads:
            start_load(b, nxt)
        else:
            @pl.when(b + 1 < n_rows)
            def _():
                start_load(b + 1, nxt - n_loads)
        slot = slot_of(b, i)
        pltpu.make_async_copy(ck_ref.at[pl.ds(0, P)], buf_ref.at[slot], sem.at[slot]).wait()
        for r in range(P):
            kc_ref[:, r * PAGE_SIZE:(r + 1) * PAGE_SIZE] = buf_ref[slot, r].astype(BF16)
        return kc_ref[...]

    @pl.when(b == 0)
    def _():
        for i in range(ahead):
            start_load(b, i)
        ph_ref[...] = jnp.zeros_like(ph_ref)
        vb16_ref[...] = jnp.zeros_like(vb16_ref)

    head = lax.broadcasted_iota(I32, (DEC_SEQ, ATTN_WIDTH), 1) // HEAD_DIM

    def masked_queries():
        q8 = q_ref[0]
        return jnp.concatenate([jnp.where(head == h, q8, 0.0) for h in range(N_HEADS)], axis=0).astype(BF16)

    @pl.when(b < n_rows)
    def _score_keys():
        qp = masked_queries()
        gate = jnp.zeros((nq, LANES), F32)
        for c in range(n_chunks):
            sc = _dot(qp, load_chunk(c))
            keypos = c * chunk + lax.broadcasted_iota(I32, (1, chunk), 1)
            dist = (keypos - PAST_LEN).astype(F32) - lq
            logits = sc + slope * dist
            for r2 in range(bpc):
                gs = jnp.sum(sc[:, r2 * S:(r2 + 1) * S], axis=-1, keepdims=True) * (1.0 / S)
                gate = jnp.where(lane == c * bpc + r2, gs, gate)
                s_ref[c * bpc + r2] = logits[:, r2 * S:(r2 + 1) * S]
        gate_ref[...] = gate

    @pl.when(b > 0)
    def _finish_previous_row():
        pltpu.make_async_copy(vbuf_ref, vbuf_ref, vsem).wait()
        pieces = []
        for h in range(N_HEADS):
            vb16_ref[0:HEAD_DIM, :] = vbuf_ref[h].astype(BF16)
            pieces.append(_dot_nt(ph_ref[h].astype(BF16), vb16_ref[...])[0:DEC_SEQ])
        pairs = [pieces[2 * i] + pltpu.roll(pieces[2 * i + 1], HEAD_DIM, axis=1) for i in range(N_HEADS // 2)]
        o_ref[0] = (res_ref[...] + jnp.concatenate(pairs, axis=1)).astype(BF16)

    @pl.when(b < n_rows)
    def _pick_and_fetch():
        _pick_blocks_and_fetch_values(
            b, pt_ref, masked_queries(), gate_ref[...], kn_ref, vn_ref, slope, lq, cv_ref, s_ref, sc_ref,
            idx_v, idx_s, idx_sem, vbuf_ref, vsem, ph_ref, res_ref)


def _pick_blocks_and_fetch_values(b, pt_ref, qp, gate, kn_ref, vn_ref, slope, lq, cv_ref, s_ref, sc_ref,
                                  idx_v, idx_s, idx_sem, vbuf_ref, vsem, ph_ref, res_ref):
    S = MOBA_BLOCK
    n_blocks = PAST_LEN // S
    ppb = S // PAGE_SIZE
    nq = N_HEADS * DEC_SEQ
    ktop = min(MOBA_TOPK, n_blocks)
    lane = lax.broadcasted_iota(I32, (1, LANES), 1)
    head = lax.broadcasted_iota(I32, (DEC_SEQ, ATTN_WIDTH), 1) // HEAD_DIM

    gate_t = jnp.concatenate([gate, jnp.zeros((LANES - nq, LANES), F32)], axis=0).T[0:n_blocks]
    blk_i = lax.broadcasted_iota(I32, (n_blocks, LANES), 0)
    cnt = jnp.zeros((n_blocks, LANES), I32)
    for m in range(n_blocks):
        gm = gate_t[m:m + 1, :]
        beats = (gm > gate_t) | ((gm == gate_t) & (m < blk_i))
        cnt = cnt + jnp.where(beats, 1, 0)
    sel_t = cnt < ktop
    row8 = lax.broadcasted_iota(I32, (8, LANES), 0)
    idx_tile = jnp.zeros((8, LANES), I32)
    left = sel_t
    for j in range(ktop):
        pick = jnp.min(jnp.where(left, blk_i, n_blocks), axis=0, keepdims=True)
        idx_tile = jnp.where(row8 == j, pick, idx_tile)
        left = left & (blk_i != pick)
    idx_v[...] = idx_tile
    idx_copy = pltpu.make_async_copy(idx_v, idx_s, idx_sem)
    idx_copy.start()

    kn = jnp.concatenate([kn_ref[0], jnp.zeros((LANES - DEC_SEQ, ATTN_WIDTH), F32)], axis=0)
    lane_f = lane.astype(F32)
    sn = _dot_nt(qp, kn.astype(BF16)) + slope * (lane_f - lq)
    sn = jnp.where(lane_f <= lq, sn, NEG)

    idx_copy.wait()
    span = ktop * S
    for h in range(N_HEADS):
        for l in range(DEC_SEQ):
            q_i = h * DEC_SEQ + l
            for j in range(ktop):
                n = idx_s[j, q_i]
                for half in range(ppb):
                    page = pt_ref[b, n * ppb + half]
                    col = l * span + j * S + half * PAGE_SIZE
                    pltpu.make_async_copy(cv_ref.at[page, pl.ds(h * HEAD_DIM, HEAD_DIM), :],
                                          vbuf_ref.at[h, :, pl.ds(col, PAGE_SIZE)], vsem).start()
                sc_ref[q_i:q_i + 1, j * S:(j + 1) * S] = s_ref[n, q_i:q_i + 1, :]

    sc = sc_ref[...]
    mx = jnp.maximum(jnp.max(sc, axis=-1, keepdims=True), jnp.max(sn, axis=-1, keepdims=True))
    p = jnp.exp(sc - mx)
    pn = jnp.exp(sn - mx)
    linv = 1.0 / (jnp.sum(p, axis=-1, keepdims=True) + jnp.sum(pn, axis=-1, keepdims=True))
    pw = p * linv
    for h in range(N_HEADS):
        for l in range(DEC_SEQ):
            q_i = h * DEC_SEQ + l
            ph_ref[h, l:l + 1, l * span:(l + 1) * span] = pw[q_i:q_i + 1, :]

    vn = jnp.concatenate([vn_ref[0], jnp.zeros((LANES - DEC_SEQ, ATTN_WIDTH), F32)], axis=0)
    o_new = _dot(pn.astype(BF16), vn.astype(BF16)) * linv
    res = jnp.zeros((DEC_SEQ, ATTN_WIDTH), F32)
    for h in range(N_HEADS):
        res = res + jnp.where(head == h, o_new[h * DEC_SEQ:(h + 1) * DEC_SEQ, :], 0.0)
    res_ref[...] = res


def _moba_sample(page_table, q, kn, vn, slope_col, lq_col, cache_kt, cache_vt):
    P = PAGES_PER_STEP
    chunk = P * PAGE_SIZE
    n_chunks = PAST_LEN // chunk
    n_blocks = PAST_LEN // MOBA_BLOCK
    nq = N_HEADS * DEC_SEQ
    fetched = DEC_SEQ * min(MOBA_TOPK, n_blocks) * MOBA_BLOCK
    row3 = lambda b, pt: (jnp.minimum(b, DEC_BATCH - 1), 0, 0)
    prev3 = lambda b, pt: (jnp.maximum(b - 1, 0), 0, 0)
    const2 = lambda b, pt: (0, 0)
    grid_spec = pltpu.PrefetchScalarGridSpec(
        num_scalar_prefetch=1,
        grid=(DEC_BATCH + 1,),
        in_specs=[pl.BlockSpec((1, DEC_SEQ, ATTN_WIDTH), row3),
                  pl.BlockSpec((1, DEC_SEQ, ATTN_WIDTH), row3),
                  pl.BlockSpec((1, DEC_SEQ, ATTN_WIDTH), row3),
                  pl.BlockSpec((nq, 1), const2),
                  pl.BlockSpec((nq, 1), const2),
                  pl.BlockSpec(memory_space=pl.ANY),
                  pl.BlockSpec(memory_space=pl.ANY)],
        out_specs=pl.BlockSpec((1, DEC_SEQ, ATTN_WIDTH), prev3),
        scratch_shapes=[pltpu.VMEM((CHUNK_SLOTS, P, ATTN_WIDTH, PAGE_SIZE), F32),
                        pltpu.SemaphoreType.DMA((CHUNK_SLOTS,)),
                        pltpu.VMEM((n_blocks, nq, MOBA_BLOCK), F32),
                        pltpu.VMEM((nq, fetched // DEC_SEQ), F32),
                        pltpu.VMEM((ATTN_WIDTH, chunk), BF16),
                        pltpu.VMEM((8, LANES), I32),
                        pltpu.SMEM((8, LANES), I32),
                        pltpu.SemaphoreType.DMA(()),
                        pltpu.VMEM((N_HEADS, HEAD_DIM, fetched), F32),
                        pltpu.SemaphoreType.DMA(()),
                        pltpu.VMEM((LANES, fetched), BF16),
                        pltpu.VMEM((N_HEADS, 2 * DEC_SEQ, fetched), F32),
                        pltpu.VMEM((nq, LANES), F32),
                        pltpu.VMEM((DEC_SEQ, ATTN_WIDTH), F32)])
    return pl.pallas_call(
        _moba_sample_kernel,
        grid_spec=grid_spec,
        out_shape=jax.ShapeDtypeStruct((DEC_BATCH, DEC_SEQ, ATTN_WIDTH), BF16),
        compiler_params=pltpu.CompilerParams(dimension_semantics=("arbitrary",), vmem_limit_bytes=VMEM_LIMIT),
        name="moba_sample",
    )(page_table, q, kn, vn, slope_col, lq_col, cache_kt, cache_vt)


def _mod_rows(ref):
    return ref[...].reshape(-1, ref.shape[-1])


def _outproj_kernel(yp_ref, ya_ref, x_ref, gt1_ref, sh2_ref, sc2_ref, g2_ref, wout_ref, wr_ref, wrhi_ref, br_ref,
                    tri_ref, cntin_ref, x1_ref, h2_ref, meta_ref, ew_ref, cntout_ref, cnt_ref):
    rows = x_ref.shape[0]

    @pl.when(pl.program_id(0) == 0)
    def _():
        cnt_ref[...] = cntin_ref[...]

    mix = _dot(yp_ref[...], wout_ref[0:POOL_WIDTH, :]) + _dot(ya_ref[...], wout_ref[POOL_WIDTH:, :])
    x1 = x_ref[...] + _mod_rows(gt1_ref) * mix
    x1_ref[...] = x1
    h2 = _rms_mod(x1, g2_ref[...], _mod_rows(sc2_ref), _mod_rows(sh2_ref))
    h2_ref[...] = h2
    hh = h2.astype(BF16)
    hl = (h2 - hh.astype(F32)).astype(BF16)
    both = _dot(hh, wr_ref[...])
    lt = (both[:, 0:LANES] + both[:, LANES:] + _dot(hl, wrhi_ref[...])).T + br_ref[...]
    row8 = lax.broadcasted_iota(I32, (8, rows), 0)
    g8 = lt[0:8]
    gmax = jnp.max(g8, axis=0, keepdims=True)
    gsum = jnp.sum(jnp.exp(g8 - gmax), axis=0, keepdims=True)
    g_w = 1.0 / gsum
    g_idx = jnp.min(jnp.where(g8 == gmax, row8, 8), axis=0, keepdims=True)
    e_in = jnp.zeros((8, rows), F32)
    for g in range(N_GROUPS):
        e_in = e_in + jnp.where(g_idx == g, lt[8 + 8 * g:16 + 8 * g], 0.0)
    m1 = jnp.max(e_in, axis=0, keepdims=True)
    i1 = jnp.min(jnp.where(e_in == m1, row8, 8), axis=0, keepdims=True)
    e_rest = jnp.where(row8 == i1, NEG, e_in)
    m2 = jnp.max(e_rest, axis=0, keepdims=True)
    i2 = jnp.min(jnp.where(e_rest == m2, row8, 8), axis=0, keepdims=True)
    r = jnp.exp(m2 - m1)
    w1 = g_w / (1.0 + r)
    w2 = g_w * r / (1.0 + r)
    e1 = g_idx * EXPERTS_PER_GROUP + i1
    e2 = g_idx * EXPERTS_PER_GROUP + i2
    rowe = lax.broadcasted_iota(I32, (N_EXPERTS, rows), 0)
    oh1 = (rowe == e1).astype(F32)
    oh2 = (rowe == e2).astype(F32)
    both_oh = oh1 + oh2
    before = _dot(both_oh.astype(BF16), tri_ref[...]) + cnt_ref[:, 0:1]
    rank1 = jnp.sum(oh1 * before, axis=0, keepdims=True).astype(I32)
    rank2 = jnp.sum(oh2 * before, axis=0, keepdims=True).astype(I32)
    cnt_ref[...] = cnt_ref[...] + jnp.sum(both_oh, axis=-1, keepdims=True)
    cntout_ref[...] = cnt_ref[...]
    meta_ref[...] = jnp.where(row8 == 0, e1, jnp.where(row8 == 1, e2, jnp.where(row8 == 2, rank1,
                                                                                 jnp.where(row8 == 3, rank2, 0))))
    ew_ref[...] = jnp.where(row8 == 0, w1, jnp.where(row8 == 1, w2, 0.0))


def _mod_spec(tile, per_batch_rows):
    if per_batch_rows is None:
        return pl.BlockSpec((tile, D_MODEL), lambda i: (i, 0))
    per = per_batch_rows // tile
    return pl.BlockSpec((1, 1, D_MODEL), lambda i: (i // per, 0, 0))


def _outproj(yp, ya, x, gt1, sh2, sc2, g2, wout, wr, wrhi, br, cnt_in, per_batch_rows):
    n = x.shape[0]
    tile = min(ROW_TILE, n)
    tok = lambda i: (i, 0)
    const2 = lambda i: (0, 0)
    mod_spec = _mod_spec(tile, per_batch_rows)
    tri = (jnp.arange(tile)[:, None] < jnp.arange(tile)[None, :]).astype(BF16)
    return pl.pallas_call(
        _outproj_kernel,
        grid=(n // tile,),
        in_specs=[pl.BlockSpec((tile, POOL_WIDTH), tok),
                  pl.BlockSpec((tile, ATTN_WIDTH), tok),
                  pl.BlockSpec((tile, D_MODEL), tok),
                  mod_spec, mod_spec, mod_spec,
                  pl.BlockSpec((1, D_MODEL), const2),
                  pl.BlockSpec((2 * POOL_WIDTH, D_MODEL), const2),
                  pl.BlockSpec((D_MODEL, 2 * LANES), const2),
                  pl.BlockSpec((D_MODEL, LANES), const2),
                  pl.BlockSpec((LANES, 1), const2),
                  pl.BlockSpec((tile, tile), const2),
                  pl.BlockSpec((N_EXPERTS, LANES), const2)],
        out_specs=[pl.BlockSpec((tile, D_MODEL), tok),
                   pl.BlockSpec((tile, D_MODEL), tok),
                   pl.BlockSpec((8, tile), lambda i: (0, i)),
                   pl.BlockSpec((8, tile), lambda i: (0, i)),
                   pl.BlockSpec((N_EXPERTS, LANES), const2)],
        out_shape=[jax.ShapeDtypeStruct((n, D_MODEL), F32),
                   jax.ShapeDtypeStruct((n, D_MODEL), F32),
                   jax.ShapeDtypeStruct((8, n), I32),
                   jax.ShapeDtypeStruct((8, n), F32),
                   jax.ShapeDtypeStruct((N_EXPERTS, LANES), F32)],
        scratch_shapes=[pltpu.VMEM((N_EXPERTS, LANES), F32)],
        compiler_params=pltpu.CompilerParams(dimension_semantics=("arbitrary",), vmem_limit_bytes=VMEM_LIMIT),
        name="outproj",
    )(yp, ya, x, gt1, sh2, sc2, g2, wout, wr, wrhi, br, tri, cnt_in)


def _dest_kernel(pstart_ref, meta_ref, dest_ref):
    n = meta_ref.shape[1]
    eid = meta_ref[0:EXPERT_TOPK, :]
    start = jnp.zeros((EXPERT_TOPK, n), I32)
    for e in range(N_EXPERTS):
        start = jnp.where(eid == e, pstart_ref[e], start)
    dest_ref[...] = start + meta_ref[EXPERT_TOPK:2 * EXPERT_TOPK, :]


def _dest_rows(pstart, meta):
    n = meta.shape[1]
    dest = pl.pallas_call(
        _dest_kernel,
        in_specs=[pl.BlockSpec(memory_space=pltpu.SMEM), pl.BlockSpec(memory_space=pltpu.VMEM)],
        out_specs=pl.BlockSpec(memory_space=pltpu.VMEM),
        out_shape=jax.ShapeDtypeStruct((EXPERT_TOPK, n), I32),
        name="dest",
    )(pstart, meta)
    return [dest[slot] for slot in range(EXPERT_TOPK)]


DMA_ISSUE_UNROLL = 16


SUBLANES = 8
PAD_CHUNKS = tuple(MOE_ROWS >> s for s in range(1, MOE_ROWS.bit_length() - 3))


def _issue_row_copies(n_tokens, copy_for):
    group = DMA_ISSUE_UNROLL // SUBLANES

    def issue(g, carry):
        for k in range(DMA_ISSUE_UNROLL):
            for slot in range(EXPERT_TOPK):
                copy_for(slot, g * DMA_ISSUE_UNROLL + k, g * group + k // SUBLANES, k % SUBLANES).start()
        return carry

    lax.fori_loop(0, n_tokens // DMA_ISSUE_UNROLL, issue, 0)


def _scatter_rows(dest_refs, h_ref, xr_ref, sem):
    rows = h_ref.shape[0] * SUBLANES

    def copy_for(slot, t, tile, sub):
        return pltpu.make_async_copy(h_ref.at[tile, pl.ds(sub, 1), :],
                                     xr_ref.at[pl.ds(dest_refs[slot][t], 1), :], sem)

    _issue_row_copies(rows, copy_for)
    for _ in dest_refs:
        tile_rows = xr_ref.at[pl.ds(0, rows), :]
        pltpu.make_async_copy(tile_rows, tile_rows, sem).wait()


def _dispatch_kernel(*refs):
    dest_refs = refs[0:EXPERT_TOPK]
    pstart_ref, cnt_ref, nact_ref, hp_ref, hs_ref, xr_ref, z_ref, sem, zsem = refs[EXPERT_TOPK:]
    i = pl.program_id(0)
    last = pl.num_programs(0) - 1

    @pl.when(i == 0)
    def _():
        z_ref[...] = jnp.zeros_like(z_ref)
        _zero_unassigned_rows(pstart_ref, cnt_ref, nact_ref, xr_ref, z_ref, zsem, lambda c: c.start())

    @pl.when(i < last)
    def _():
        _scatter_rows(dest_refs, hp_ref, xr_ref, sem)

    @pl.when(i == last)
    def _():
        _scatter_rows(dest_refs, hs_ref, xr_ref, sem)
        _zero_unassigned_rows(pstart_ref, cnt_ref, nact_ref, xr_ref, z_ref, zsem, lambda c: c.wait())


def _dispatch_rows(dests, pstart, counts, n_active, h2_p, h2_s, n_rows):
    n_p = h2_p.shape[0]
    assert n_p % ROW_TILE == 0 and h2_s.shape[0] <= ROW_TILE
    p_tiles = n_p // ROW_TILE
    smem_rows = pl.BlockSpec((ROW_TILE,), lambda i: (i,), memory_space=pltpu.SMEM)
    smem = pl.BlockSpec(memory_space=pltpu.SMEM)
    dests = [jnp.pad(d, (0, (p_tiles + 1) * ROW_TILE - d.shape[0])) for d in dests]
    h2_p = h2_p.reshape(-1, SUBLANES, D_MODEL)
    h2_s = h2_s.reshape(-1, SUBLANES, D_MODEL)
    return pl.pallas_call(
        _dispatch_kernel,
        grid=(p_tiles + 1,),
        in_specs=[smem_rows] * EXPERT_TOPK + [
            smem, smem, smem,
            pl.BlockSpec((ROW_TILE // SUBLANES, SUBLANES, D_MODEL), lambda i: (jnp.minimum(i, p_tiles - 1), 0, 0)),
            pl.BlockSpec(h2_s.shape, lambda i: (0, 0, 0))],
        out_specs=pl.BlockSpec(memory_space=pl.ANY),
        out_shape=jax.ShapeDtypeStruct((n_rows, D_MODEL), F32),
        scratch_shapes=[pltpu.VMEM((MOE_ROWS, D_MODEL), F32),
                        pltpu.SemaphoreType.DMA(()),
                        pltpu.SemaphoreType.DMA(())],
        compiler_params=pltpu.CompilerParams(dimension_semantics=("arbitrary",), vmem_limit_bytes=VMEM_LIMIT),
        name="dispatch",
    )(*dests, pstart, counts, n_active, h2_p, h2_s)


def _zero_unassigned_rows(pstart_ref, cnt_ref, nact_ref, xr_ref, z_ref, sem, act):
    assert MOE_ROWS - 1 == SUBLANES - 1 + sum(PAD_CHUNKS)
    n_blocks = xr_ref.shape[0] // MOE_ROWS

    def trailing_block(blk, carry):
        dst = xr_ref.at[pl.ds(pl.multiple_of(blk * MOE_ROWS, MOE_ROWS), MOE_ROWS), :]
        act(pltpu.make_async_copy(z_ref, dst, sem))
        return carry

    lax.fori_loop(nact_ref[0], n_blocks, trailing_block, 0)
    for e in range(N_EXPERTS):
        cnt = cnt_ref[e]
        first = pstart_ref[e] + cnt
        n_pad = (MOE_ROWS - (cnt & (MOE_ROWS - 1))) & (MOE_ROWS - 1)
        head = (SUBLANES - (first & (SUBLANES - 1))) & (SUBLANES - 1)
        head = jnp.minimum(head, n_pad)
        for k in range(SUBLANES - 1):
            @pl.when(k < head)
            def _():
                act(pltpu.make_async_copy(z_ref.at[pl.ds(0, 1), :], xr_ref.at[pl.ds(first + k, 1), :], sem))
        start = first + head
        body = n_pad - head
        for rows in PAD_CHUNKS:
            @pl.when((body & rows) != 0)
            def _():
                dst = xr_ref.at[pl.ds(pl.multiple_of(start, SUBLANES), rows), :]
                act(pltpu.make_async_copy(z_ref.at[pl.ds(0, rows), :], dst, sem))
            start = start + (body & rows)


def _moe_kernel(blk_e_ref, nact_ref, x_ref, wg_ref, wu_ref, wd_ref, y_ref, wgb_ref, wub_ref, wdb_ref):
    i = pl.program_id(0)
    prev = blk_e_ref[jnp.maximum(i - 1, 0)]
    active = i < nact_ref[0]

    @pl.when(active & ((i == 0) | (blk_e_ref[i] != prev)))
    def _():
        wgb_ref[...] = wg_ref[0].astype(BF16)
        wub_ref[...] = wu_ref[0].astype(BF16)
        wdb_ref[...] = wd_ref[0].astype(BF16)

    @pl.when(active)
    def _():
        x = x_ref[...].astype(BF16)
        g = _dot(x, wgb_ref[...])
        u = _dot(x, wub_ref[...])
        a = g / (1.0 + jnp.exp(-g)) * u
        y_ref[...] = _dot(a.astype(BF16), wdb_ref[...])

    @pl.when(jnp.logical_not(active))
    def _():
        y_ref[...] = jnp.zeros_like(y_ref)


def _moe(blk_e, n_active, xr, w_gate, w_up, w_down):
    n_rows = xr.shape[0]
    n_blocks = n_rows // MOE_ROWS
    grid_spec = pltpu.PrefetchScalarGridSpec(
        num_scalar_prefetch=2,
        grid=(n_blocks,),
        in_specs=[pl.BlockSpec((MOE_ROWS, D_MODEL), lambda i, be, na: (jnp.minimum(i, na[0] - 1), 0)),
                  pl.BlockSpec((1, D_MODEL, D_FF), lambda i, be, na: (be[i], 0, 0)),
                  pl.BlockSpec((1, D_MODEL, D_FF), lambda i, be, na: (be[i], 0, 0)),
                  pl.BlockSpec((1, D_FF, D_MODEL), lambda i, be, na: (be[i], 0, 0))],
        out_specs=pl.BlockSpec((MOE_ROWS, D_MODEL), lambda i, be, na: (i, 0)),
        scratch_shapes=[pltpu.VMEM((D_MODEL, D_FF), BF16),
                        pltpu.VMEM((D_MODEL, D_FF), BF16),
                        pltpu.VMEM((D_FF, D_MODEL), BF16)])
    return pl.pallas_call(
        _moe_kernel,
        grid_spec=grid_spec,
        out_shape=jax.ShapeDtypeStruct((n_rows, D_MODEL), F32),
        compiler_params=pltpu.CompilerParams(
            dimension_semantics=("arbitrary",), vmem_limit_bytes=VMEM_LIMIT),
        name="moe",
    )(blk_e, n_active, xr, w_gate, w_up, w_down)


def _final_kernel(*refs):
    dest_refs = refs[0:EXPERT_TOPK]
    x1_ref, w_ref, gt2_ref, yr_ref, y_ref, o_ref, sem = refs[EXPERT_TOPK:]
    rows = x1_ref.shape[0]

    def copy_for(slot, t, tile, sub):
        return pltpu.make_async_copy(yr_ref.at[pl.ds(dest_refs[slot][t], 1), :],
                                     o_ref.at[slot, tile, pl.ds(sub, 1), :], sem)

    _issue_row_copies(rows, copy_for)
    for slot in range(EXPERT_TOPK):
        pltpu.make_async_copy(o_ref.at[slot], o_ref.at[slot], sem).wait()
    w = w_ref[...]
    o = [o_ref[slot].reshape(rows, D_MODEL) for slot in range(EXPERT_TOPK)]
    moe = o[0] * w[:, 0:1] + o[1] * w[:, 1:2]
    y_ref[...] = x1_ref[...] + _mod_rows(gt2_ref) * moe


def _final(dests, x1, w2, gt2, yr, per_batch_rows):
    n = x1.shape[0]
    tile = min(ROW_TILE, n)
    tok = lambda i: (i, 0)
    smem_rows = pl.BlockSpec((tile,), lambda i: (i,), memory_space=pltpu.SMEM)
    return pl.pallas_call(
        _final_kernel,
        grid=(n // tile,),
        in_specs=[smem_rows] * EXPERT_TOPK + [pl.BlockSpec((tile, D_MODEL), tok),
                                             pl.BlockSpec((tile, EXPERT_TOPK), tok),
                                             _mod_spec(tile, per_batch_rows),
                                             pl.BlockSpec(memory_space=pl.ANY)],
        out_specs=pl.BlockSpec((tile, D_MODEL), tok),
        out_shape=jax.ShapeDtypeStruct((n, D_MODEL), F32),
        scratch_shapes=[pltpu.VMEM((EXPERT_TOPK, tile // SUBLANES, SUBLANES, D_MODEL), F32),
                        pltpu.SemaphoreType.DMA(())],
        compiler_params=pltpu.CompilerParams(dimension_semantics=("arbitrary",), vmem_limit_bytes=VMEM_LIMIT),
        name="final",
    )(*dests, x1, w2, gt2, yr)


def _expert_layout(counts, n_blocks):
    padded = (counts + MOE_ROWS - 1) // MOE_ROWS * MOE_ROWS
    pend = jnp.cumsum(padded)
    pstart = (pend - padded).astype(I32)
    blk_start = jnp.arange(n_blocks, dtype=I32) * MOE_ROWS
    blk_e = jnp.minimum(jnp.sum((pend[None, :] <= blk_start[:, None]).astype(I32), axis=1), N_EXPERTS - 1)
    n_active = (pend[-1] // MOE_ROWS).astype(I32).reshape(1)
    return pstart, blk_e.astype(I32), n_active


def kernel(x_prompt, x_sample, cache_k, cache_v, state_pool, page_table, c_prompt, c_sample, w_ada, b_ada,
           g_attn_norm, w_in, g_q, g_k, w_pool, pool_scale, w_out, g_ffn_norm, w_group, b_group, w_expert,
           b_expert, w_gate, w_up, w_down):
    D = D_MODEL
    B, T, _ = x_prompt.shape
    n_s = DEC_BATCH * DEC_SEQ
    n_p = B * T
    layer = 0

    win = w_in[layer].astype(BF16)
    wout = w_out[layer].astype(BF16)
    wpool = w_pool[layer].astype(BF16)
    g1 = g_attn_norm[layer].reshape(1, D)
    g2 = g_ffn_norm[layer].reshape(1, D)
    gq = jnp.tile(g_q[layer], N_HEADS).reshape(1, ATTN_WIDTH)
    gk = jnp.tile(g_k[layer], N_HEADS).reshape(1, ATTN_WIDTH)
    ps = pool_scale[layer].reshape(1, POOL_WIDTH)
    hd = jnp.arange(ATTN_WIDTH) // HEAD_DIM
    bd = (hd[:, None] == hd[None, :]).astype(BF16)
    slopes = jnp.exp2(-8.0 * (jnp.arange(N_HEADS, dtype=F32) + 1.0) / N_HEADS)
    qrow = jnp.arange(N_HEADS * DEC_SEQ)
    slope_col = slopes[qrow // DEC_SEQ].reshape(-1, 1)
    lq_col = (qrow % DEC_SEQ).astype(F32).reshape(-1, 1)
    wr = jnp.zeros((D, LANES), F32).at[:, 0:N_GROUPS].set(w_group[layer]).at[:, 8:8 + N_EXPERTS].set(w_expert[layer])
    wr_hi = wr.astype(BF16)
    wr_lo = (wr - wr_hi.astype(F32)).astype(BF16)
    wr_both = jnp.concatenate([wr_hi, wr_lo], axis=1)
    br = jnp.zeros((LANES,), F32).at[0:N_GROUPS].set(b_group[layer]).at[N_GROUPS:8].set(NEG)
    br = br.at[8:8 + N_EXPERTS].set(b_expert[layer]).reshape(LANES, 1)

    mod = _ada(jnp.concatenate([c_prompt, c_sample], axis=0), w_ada[layer], b_ada[layer])
    mod_p = mod[:B].reshape(B, 1, 6 * D)
    mod_s = jnp.repeat(mod[B:], DEC_SEQ, axis=0)

    ypool_p, q_p, k_p, v_p, tail_p = _inproj_prompt(x_prompt, mod_p, g1, win, bd, gq, gk, wpool, ps)
    yattn_p = _moba_prompt(slopes, q_p, k_p, v_p)
    mp = lambda j: mod_p[:, :, j * D:(j + 1) * D]
    cnt0 = jnp.zeros((N_EXPERTS, LANES), F32)
    x1_p, h2_p, meta_p, ew_p, cnt_p = _outproj(
        ypool_p.reshape(n_p, POOL_WIDTH), yattn_p.reshape(n_p, ATTN_WIDTH), x_prompt.reshape(n_p, D),
        mp(2), mp(3), mp(4), g2, wout, wr_both, wr_hi, br, cnt0, T)

    ms = lambda j: mod_s[:, j * D:(j + 1) * D]
    hist = jnp.concatenate([jnp.zeros((DEC_BATCH, HALO - POOL_HIST, POOL_WIDTH), F32), state_pool[layer]], axis=1)
    xs = x_sample.reshape(n_s, D)
    ypool_s, q_s, k_s, v_s, tail_s = _inproj_sample(xs, ms(0), ms(1), g1, win, bd, gq, gk, hist, wpool, ps)
    r3 = lambda a: a.reshape(DEC_BATCH, DEC_SEQ, ATTN_WIDTH)
    n_phys = cache_k.shape[1]
    ck = jnp.transpose(cache_k[layer], (0, 2, 3, 1)).reshape(n_phys, ATTN_WIDTH, PAGE_SIZE)
    cv = jnp.transpose(cache_v[layer], (0, 2, 3, 1)).reshape(n_phys, ATTN_WIDTH, PAGE_SIZE)
    yattn_s = _moba_sample(page_table, r3(q_s), r3(k_s), r3(v_s), slope_col, lq_col, ck, cv)
    x1_s, h2_s, meta_s, ew_s, cnt_s = _outproj(ypool_s, yattn_s.reshape(n_s, ATTN_WIDTH), xs, ms(2), ms(3), ms(4),
                                               g2, wout, wr_both, wr_hi, br, cnt_p, None)

    n_blocks = (n_p + n_s) * EXPERT_TOPK // MOE_ROWS + N_EXPERTS
    counts = cnt_s[:, 0].astype(I32)
    pstart, blk_e, n_active = _expert_layout(counts, n_blocks)
    dests_p = _dest_rows(pstart, meta_p)
    dests_s = _dest_rows(pstart, meta_s)
    dests = [jnp.concatenate([dp, ds]) for dp, ds in zip(dests_p, dests_s)]
    xr = _dispatch_rows(dests, pstart, counts, n_active, h2_p, h2_s, n_blocks * MOE_ROWS)
    yr = _moe(blk_e, n_active, xr, w_gate[layer], w_up[layer], w_down[layer])
    y_p = _final(dests_p, x1_p, ew_p[0:EXPERT_TOPK].T, mp(5), yr, T)
    y_s = _final(dests_s, x1_s, ew_s[0:EXPERT_TOPK].T, ms(5), yr, None)

    k4 = lambda a, b, l: a.reshape(1, b, l, N_HEADS, HEAD_DIM)
    return (y_p.reshape(B, T, D), y_s.reshape(DEC_BATCH, DEC_SEQ, D),
            k4(k_p, B, T), k4(v_p, B, T), tail_p[None, :, HALO - POOL_HIST:, :],
            k4(k_s, DEC_BATCH, DEC_SEQ), k4(v_s, DEC_BATCH, DEC_SEQ), tail_s[None, :, HALO - POOL_HIST:, :])
```

```python
import jax
import jax.numpy as jnp
from jax import lax
from jax.experimental import pallas as pl
from jax.experimental.pallas import tpu as pltpu

F32 = jnp.float32
BF16 = jnp.bfloat16
I32 = jnp.int32

D_MODEL = 1024
DEC_BATCH = 32
DEC_SEQ = 8
PAST_LEN = 16384
PAGE_SIZE = 128
POOL_WIDTH = 512
POOL_WINDOWS = (2, 4, 8, 16)
POOL_GROUP = 128
POOL_HIST = 15
HALO = 16
N_HEADS = 8
HEAD_DIM = 64
ATTN_WIDTH = 512
MOBA_BLOCK = 256
MOBA_TOPK = 3
ATTN_SCALE = HEAD_DIM ** -0.5
MIX_IN = POOL_WIDTH + 3 * ATTN_WIDTH
N_GROUPS = 4
EXPERTS_PER_GROUP = 8
N_EXPERTS = 32
EXPERT_TOPK = 2
D_FF = 512
EPS = 1e-6
NEG = -1e30

LANES = 128
ROW_TILE = 512
COMBINE_ROWS = 1024
INPROJ_ROWS = 1024
MOE_ROWS = 512
PAGES_PER_STEP = 16
VMEM_LIMIT = 56 * 1024 * 1024

_NT = (((1,), (1,)), ((), ()))


def _dot(a, b):
    return jnp.dot(a, b, preferred_element_type=F32)


def _dot_nt(a, b):
    return lax.dot_general(a, b, _NT, preferred_element_type=F32)


def _split_dot(a, b01):
    hi = a.astype(BF16)
    lo = (a - hi.astype(F32)).astype(BF16)
    return _dot(hi, b01) + _dot(lo, b01)


def _rms_mod(x, g, sc, sh):
    ms = jnp.mean(x * x, axis=-1, keepdims=True)
    return x * lax.rsqrt(ms + EPS) * g * (1.0 + sc) + sh


def _ada_kernel(c_ref, w_ref, b_ref, o_ref):
    c = c_ref[...]
    a = c / (1.0 + jnp.exp(-c))
    o_ref[...] = _dot(a.astype(BF16), w_ref[...].astype(BF16)) + b_ref[...]


def _ada(c_all, w_ada, b_ada):
    n = c_all.shape[0]
    tn = 1536
    return pl.pallas_call(
        _ada_kernel,
        grid=(6 * D_MODEL // tn,),
        in_specs=[pl.BlockSpec((n, D_MODEL), lambda j: (0, 0)),
                  pl.BlockSpec((D_MODEL, tn), lambda j: (0, j)),
                  pl.BlockSpec((1, tn), lambda j: (0, j))],
        out_specs=pl.BlockSpec((n, tn), lambda j: (0, j)),
        out_shape=jax.ShapeDtypeStruct((n, 6 * D_MODEL), F32),
        compiler_params=pltpu.CompilerParams(vmem_limit_bytes=VMEM_LIMIT),
        name="ada",
    )(c_all, w_ada, b_ada.reshape(1, -1))


def _inproj_core(x, sh1, sc1, g, win_ref, bd_ref, gq_ref, gk_ref):
    h = _rms_mod(x, g, sc1, sh1)
    z = _dot(h.astype(BF16), win_ref[...])
    u = z[:, 0:POOL_WIDTH]
    q = z[:, POOL_WIDTH:POOL_WIDTH + ATTN_WIDTH]
    k = z[:, POOL_WIDTH + ATTN_WIDTH:POOL_WIDTH + 2 * ATTN_WIDTH]
    v = z[:, POOL_WIDTH + 2 * ATTN_WIDTH:]
    bd = bd_ref[...]
    q = q * lax.rsqrt(_split_dot(q * q, bd) * (1.0 / HEAD_DIM) + EPS) * gq_ref[...]
    k = k * lax.rsqrt(_split_dot(k * k, bd) * (1.0 / HEAD_DIM) + EPS) * gk_ref[...]
    return u, q, k, v


def _window_sum(e, w):
    s = e
    sh = 1
    while sh < w:
        s = s + pltpu.roll(s, sh, axis=0)
        sh *= 2
    return s


def _inproj_prompt_kernel(x_ref, mod_ref, g_ref, win_ref, bd_ref, gq_ref, gk_ref, wpool_ref, ps_ref,
                          ypool_ref, q_ref, k_ref, v_ref, tail_ref, ext_ref):
    t = pl.program_id(1)
    nt = pl.num_programs(1)
    x = x_ref[0]
    sh1 = mod_ref[0, :, 0:D_MODEL]
    sc1 = mod_ref[0, :, D_MODEL:2 * D_MODEL]
    u, q, k, v = _inproj_core(x, sh1, sc1, g_ref[...], win_ref, bd_ref, gq_ref, gk_ref)
    q_ref[0] = (q * ATTN_SCALE).astype(BF16)
    k_ref[0] = k
    v_ref[0] = v

    @pl.when(t == 0)
    def _():
        ext_ref[0:HALO, :] = jnp.zeros((HALO, POOL_WIDTH), F32)

    ext_ref[HALO:, :] = u
    pos = t * INPROJ_ROWS + lax.broadcasted_iota(I32, (INPROJ_ROWS, 1), 0)
    for gi, w in enumerate(POOL_WINDOWS):
        cols = slice(gi * POOL_GROUP, (gi + 1) * POOL_GROUP)
        win = _window_sum(ext_ref[:, cols], w)[HALO:]
        inv_cnt = 1.0 / jnp.minimum(pos + 1, w).astype(F32)
        pooled = win * inv_cnt - u[:, cols]
        y = _dot(pooled.astype(BF16), wpool_ref[gi]) * ps_ref[:, cols]
        ypool_ref[0, :, cols] = y.astype(BF16)
    last = u[INPROJ_ROWS - HALO:, :]
    ext_ref[0:HALO, :] = last

    @pl.when(t == nt - 1)
    def _():
        tail_ref[0] = last


def _inproj_prompt(x, mod_p, g1, win, bd, gq, gk, wpool, ps):
    B, T, D = x.shape
    nt = T // INPROJ_ROWS
    const2 = lambda b, t: (0, 0)
    tok = lambda b, t: (b, t, 0)
    return pl.pallas_call(
        _inproj_prompt_kernel,
        grid=(B, nt),
        in_specs=[pl.BlockSpec((1, INPROJ_ROWS, D), tok),
                  pl.BlockSpec((1, 1, 6 * D), lambda b, t: (b, 0, 0)),
                  pl.BlockSpec((1, D), const2),
                  pl.BlockSpec((D, MIX_IN), const2),
                  pl.BlockSpec((ATTN_WIDTH, ATTN_WIDTH), const2),
                  pl.BlockSpec((1, ATTN_WIDTH), const2),
                  pl.BlockSpec((1, ATTN_WIDTH), const2),
                  pl.BlockSpec((4, POOL_GROUP, POOL_GROUP), lambda b, t: (0, 0, 0)),
                  pl.BlockSpec((1, POOL_WIDTH), const2)],
        out_specs=[pl.BlockSpec((1, INPROJ_ROWS, POOL_WIDTH), tok),
                   pl.BlockSpec((1, INPROJ_ROWS, ATTN_WIDTH), tok),
                   pl.BlockSpec((1, INPROJ_ROWS, ATTN_WIDTH), tok),
                   pl.BlockSpec((1, INPROJ_ROWS, ATTN_WIDTH), tok),
                   pl.BlockSpec((1, HALO, POOL_WIDTH), lambda b, t: (b, 0, 0))],
        out_shape=[jax.ShapeDtypeStruct((B, T, POOL_WIDTH), BF16),
                   jax.ShapeDtypeStruct((B, T, ATTN_WIDTH), BF16),
                   jax.ShapeDtypeStruct((B, T, ATTN_WIDTH), F32),
                   jax.ShapeDtypeStruct((B, T, ATTN_WIDTH), F32),
                   jax.ShapeDtypeStruct((B, HALO, POOL_WIDTH), F32)],
        scratch_shapes=[pltpu.VMEM((HALO + INPROJ_ROWS, POOL_WIDTH), F32)],
        compiler_params=pltpu.CompilerParams(
            dimension_semantics=("arbitrary", "arbitrary"), vmem_limit_bytes=VMEM_LIMIT),
        name="inproj_prompt",
    )(x, mod_p, g1, win, bd, gq, gk, wpool, ps)


def _inproj_sample_kernel(x_ref, sh_ref, sc_ref, g_ref, win_ref, bd_ref, gq_ref, gk_ref, hist_ref, wpool_ref,
                          ps_ref, ypool_ref, q_ref, k_ref, v_ref, tail_ref, ext_ref):
    n = DEC_BATCH * DEC_SEQ
    ext_rows = HALO + DEC_SEQ
    u, q, k, v = _inproj_core(x_ref[...], sh_ref[...], sc_ref[...], g_ref[...], win_ref, bd_ref, gq_ref, gk_ref)
    q_ref[...] = q * ATTN_SCALE
    k_ref[...] = k
    v_ref[...] = v
    ext_ref[:, 0:HALO, :] = hist_ref[...]
    ext_ref[:, HALO:, :] = u.reshape(DEC_BATCH, DEC_SEQ, POOL_WIDTH)
    tail_ref[...] = ext_ref[:, ext_rows - HALO:, :]
    pos = PAST_LEN + lax.broadcasted_iota(I32, (DEC_BATCH, DEC_SEQ, 1), 1).reshape(n, 1)
    for gi, w in enumerate(POOL_WINDOWS):
        cols = slice(gi * POOL_GROUP, (gi + 1) * POOL_GROUP)
        e = ext_ref[:, :, cols].reshape(DEC_BATCH * ext_rows, POOL_GROUP)
        win = _window_sum(e, w).reshape(DEC_BATCH, ext_rows, POOL_GROUP)[:, HALO:, :].reshape(n, POOL_GROUP)
        inv_cnt = 1.0 / jnp.minimum(pos + 1, w).astype(F32)
        pooled = win * inv_cnt - u[:, cols]
        y = _dot(pooled.astype(BF16), wpool_ref[gi]) * ps_ref[:, cols]
        ypool_ref[:, cols] = y.astype(BF16)


def _inproj_sample(x, sh1, sc1, g1, win, bd, gq, gk, hist, wpool, ps):
    n = x.shape[0]
    return pl.pallas_call(
        _inproj_sample_kernel,
        out_shape=[jax.ShapeDtypeStruct((n, POOL_WIDTH), BF16),
                   jax.ShapeDtypeStruct((n, ATTN_WIDTH), F32),
                   jax.ShapeDtypeStruct((n, ATTN_WIDTH), F32),
                   jax.ShapeDtypeStruct((n, ATTN_WIDTH), F32),
                   jax.ShapeDtypeStruct((DEC_BATCH, HALO, POOL_WIDTH), F32)],
        scratch_shapes=[pltpu.VMEM((DEC_BATCH, HALO + DEC_SEQ, POOL_WIDTH), F32)],
        compiler_params=pltpu.CompilerParams(vmem_limit_bytes=VMEM_LIMIT),
        name="inproj_sample",
    )(x, sh1, sc1, g1, win, bd, gq, gk, hist, wpool, ps)


N_BIAS_LANES = 3
PROMPT_Q_TILE = 2 * MOBA_BLOCK


def _moba_prompt_kernel(slopes_ref, q_ref, k_ref, v_ref, o_ref, ka_ref, va_ref, qa_ref, kmf_ref):
    S = MOBA_BLOCK
    T = k_ref.shape[1]
    nb = T // S
    hp = pl.program_id(1)
    lane = lax.broadcasted_iota(I32, (1, LANES), 1)
    real = (lane < HEAD_DIM, lane >= HEAD_DIM)
    extra = (lane - HEAD_DIM, lane)

    kf = k_ref[0]
    vf = v_ref[0]
    kmf_ref[...] = jnp.zeros((LANES, LANES), F32)
    for j in range(nb):
        mean = jnp.sum(kf[j * S:(j + 1) * S], axis=0, keepdims=True) * (1.0 / S)
        kmf_ref[j:j + 1, :] = jnp.where(real[0], mean, 0.0)
        kmf_ref[nb + j:nb + j + 1, :] = jnp.where(real[0], 0.0, mean)
    key_i = lax.broadcasted_iota(I32, (T, 1), 0)
    key_blk = key_i // S
    for h in range(2):
        b = slopes_ref[2 * hp + h] * key_i.astype(F32)
        p0 = b.astype(BF16).astype(F32)
        p1 = (b - p0).astype(BF16).astype(F32)
        p2 = b - p0 - p1
        e = extra[h]
        onehot = jnp.where((e >= N_BIAS_LANES) & (e - N_BIAS_LANES == key_blk), 1.0, 0.0)
        feat = jnp.where(e == 0, p0, jnp.where(e == 1, p1, jnp.where(e == 2, p2, onehot)))
        ka_ref[h] = jnp.where(real[h], kf, feat).astype(BF16)
        va_ref[h] = jnp.where(real[h], vf, 1.0).astype(BF16)

    q2 = q_ref[0]

    gt = _dot_nt(kmf_ref[...].astype(BF16), q2)[0:2 * nb]
    row = lax.broadcasted_iota(I32, (2 * nb, T), 0)
    blk = row % nb
    cbq = lax.broadcasted_iota(I32, (2 * nb, T), 1) // S
    cnt = jnp.zeros((2 * nb, T), I32)
    for m in range(nb):
        gm = jnp.where(row < nb, gt[m:m + 1, :], gt[nb + m:nb + m + 1, :])
        beats = (gm > gt) | ((gm == gt) & (m < blk))
        cnt = cnt + jnp.where(beats & (m < cbq), 1, 0)
    keep = (((cnt < MOBA_TOPK) & (blk < cbq)) | (blk == cbq)).astype(F32)
    keepq = jnp.concatenate([keep, jnp.zeros((LANES - 2 * nb, T), F32)], axis=0).T
    maskv = jnp.where(keepq > 0.5, 0.0, NEG)
    mask_lanes = (pltpu.roll(maskv, HEAD_DIM + N_BIAS_LANES, axis=1),
                  pltpu.roll(maskv, (N_BIAS_LANES - nb) % LANES, axis=1))
    qf = q2.astype(F32)
    for h in range(2):
        e = extra[h]
        feat = jnp.where(e < N_BIAS_LANES, 1.0, jnp.where(e < N_BIAS_LANES + nb, mask_lanes[h], 0.0))
        qa_ref[h] = jnp.where(real[h], qf, feat).astype(BF16)

    QT = PROMPT_Q_TILE
    causal = lax.broadcasted_iota(I32, (QT, QT), 1) <= lax.broadcasted_iota(I32, (QT, QT), 0)
    for t in range(T // QT):
        rows = slice(t * QT, (t + 1) * QT)
        n = (t + 1) * QT
        outs = []
        for h in range(2):
            s = _dot_nt(qa_ref[h, rows, :], ka_ref[h, 0:n, :])
            s_own = jnp.where(causal, s[:, t * QT:n], NEG)
            s = s_own if t == 0 else jnp.concatenate([s[:, 0:t * QT], s_own], axis=1)
            p = jnp.exp(s - jnp.max(s, axis=-1, keepdims=True))
            outs.append(_dot(p.astype(BF16), va_ref[h, 0:n, :]))
        a0, a1 = outs
        o0 = a0 * (1.0 / a0[:, HEAD_DIM:HEAD_DIM + 1])
        o1 = a1 * (1.0 / a1[:, 0:1])
        o_ref[0, rows, :] = jnp.where(real[0], o0, o1).astype(BF16)


def _moba_prompt(slopes, q, k, v):
    B, T, _ = q.shape
    seq = lambda b, hp: (b, 0, hp)
    return pl.pallas_call(
        _moba_prompt_kernel,
        grid=(B, N_HEADS // 2),
        in_specs=[pl.BlockSpec(memory_space=pltpu.SMEM),
                  pl.BlockSpec((1, T, LANES), seq),
                  pl.BlockSpec((1, T, LANES), seq),
                  pl.BlockSpec((1, T, LANES), seq)],
        out_specs=pl.BlockSpec((1, T, LANES), seq),
        out_shape=jax.ShapeDtypeStruct((B, T, ATTN_WIDTH), BF16),
        scratch_shapes=[pltpu.VMEM((2, T, LANES), BF16),
                        pltpu.VMEM((2, T, LANES), BF16),
                        pltpu.VMEM((2, T, LANES), BF16),
                        pltpu.VMEM((LANES, LANES), F32)],
        compiler_params=pltpu.CompilerParams(
            dimension_semantics=("arbitrary", "arbitrary"), vmem_limit_bytes=VMEM_LIMIT),
        name="moba_prompt",
    )(slopes, q, k, v)


CHUNK_SLOTS = 6


def _moba_sample_kernel(pt_ref, q_ref, kn_ref, vn_ref, slope_ref, lq_ref, ck_ref, cv_ref, o_ref,
                        buf_ref, sem, s_ref, sc_ref, kc_ref, idx_v, idx_s, idx_sem, vbuf_ref, vsem, vb16_ref, ph_ref,
                        gate_ref, res_ref):
    P = PAGES_PER_STEP
    S = MOBA_BLOCK
    chunk = P * PAGE_SIZE
    n_chunks = PAST_LEN // chunk
    bpc = chunk // S
    n_loads = n_chunks
    ahead = CHUNK_SLOTS - 1
    nq = N_HEADS * DEC_SEQ
    b = pl.program_id(0)
    n_rows = pl.num_programs(0) - 1
    slope = slope_ref[...]
    lq = lq_ref[...]
    lane = lax.broadcasted_iota(I32, (1, LANES), 1)

    def slot_of(bb, i):
        return lax.rem(bb * n_loads + i, CHUNK_SLOTS)

    def start_load(bb, i):
        slot = slot_of(bb, i)
        for r in range(P):
            page = pt_ref[bb, i * P + r]
            pltpu.make_async_copy(ck_ref.at[page], buf_ref.at[slot, r], sem.at[slot]).start()

    def load_chunk(i):
        nxt = i + ahead
        if nxt < n_loads:
            start_load(b, nxt)
        else:
            @pl.when(b + 1 < n_rows)
            def _():
                start_load(b + 1, nxt - n_loads)
        slot = slot_of(b, i)
        pltpu.make_async_copy(ck_ref.at[pl.ds(0, P)], buf_ref.at[slot], sem.at[slot]).wait()
        for r in range(P):
            kc_ref[:, r * PAGE_SIZE:(r + 1) * PAGE_SIZE] = buf_ref[slot, r].astype(BF16)
        return kc_ref[...]

    @pl.when(b == 0)
    def _():
        for i in range(ahead):
            start_load(b, i)
        ph_ref[...] = jnp.zeros_like(ph_ref)
        vb16_ref[...] = jnp.zeros_like(vb16_ref)

    head = lax.broadcasted_iota(I32, (DEC_SEQ, ATTN_WIDTH), 1) // HEAD_DIM

    def masked_queries():
        q8 = q_ref[0]
        return jnp.concatenate([jnp.where(head == h, q8, 0.0) for h in range(N_HEADS)], axis=0).astype(BF16)

    @pl.when(b < n_rows)
    def _score_keys():
        qp = masked_queries()
        gate = jnp.zeros((nq, LANES), F32)
        for c in range(n_chunks):
            sc = _dot(qp, load_chunk(c))
            keypos = c * chunk + lax.broadcasted_iota(I32, (1, chunk), 1)
            dist = (keypos - PAST_LEN).astype(F32) - lq
            logits = sc + slope * dist
            for r2 in range(bpc):
                gs = jnp.sum(sc[:, r2 * S:(r2 + 1) * S], axis=-1, keepdims=True) * (1.0 / S)
                gate = jnp.where(lane == c * bpc + r2, gs, gate)
                s_ref[c * bpc + r2] = logits[:, r2 * S:(r2 + 1) * S]
        gate_ref[...] = gate

    @pl.when(b > 0)
    def _finish_previous_row():
        pltpu.make_async_copy(vbuf_ref, vbuf_ref, vsem).wait()
        pieces = []
        for h in range(N_HEADS):
            vb16_ref[0:HEAD_DIM, :] = vbuf_ref[h].astype(BF16)
            pieces.append(_dot_nt(ph_ref[h].astype(BF16), vb16_ref[...])[0:DEC_SEQ])
        pairs = [pieces[2 * i] + pltpu.roll(pieces[2 * i + 1], HEAD_DIM, axis=1) for i in range(N_HEADS // 2)]
        o_ref[0] = (res_ref[...] + jnp.concatenate(pairs, axis=1)).astype(BF16)

    @pl.when(b < n_rows)
    def _pick_and_fetch():
        _pick_blocks_and_fetch_values(
            b, pt_ref, masked_queries(), gate_ref[...], kn_ref, vn_ref, slope, lq, cv_ref, s_ref, sc_ref,
            idx_v, idx_s, idx_sem, vbuf_ref, vsem, ph_ref, res_ref)


def _pick_blocks_and_fetch_values(b, pt_ref, qp, gate, kn_ref, vn_ref, slope, lq, cv_ref, s_ref, sc_ref,
                                  idx_v, idx_s, idx_sem, vbuf_ref, vsem, ph_ref, res_ref):
    S = MOBA_BLOCK
    n_blocks = PAST_LEN // S
    ppb = S // PAGE_SIZE
    nq = N_HEADS * DEC_SEQ
    ktop = min(MOBA_TOPK, n_blocks)
    lane = lax.broadcasted_iota(I32, (1, LANES), 1)
    head = lax.broadcasted_iota(I32, (DEC_SEQ, ATTN_WIDTH), 1) // HEAD_DIM

    gate_t = jnp.concatenate([gate, jnp.zeros((LANES - nq, LANES), F32)], axis=0).T[0:n_blocks]
    blk_i = lax.broadcasted_iota(I32, (n_blocks, LANES), 0)
    cnt = jnp.zeros((n_blocks, LANES), I32)
    for m in range(n_blocks):
        gm = gate_t[m:m + 1, :]
        beats = (gm > gate_t) | ((gm == gate_t) & (m < blk_i))
        cnt = cnt + jnp.where(beats, 1, 0)
    sel_t = cnt < ktop
    row8 = lax.broadcasted_iota(I32, (8, LANES), 0)
    idx_tile = jnp.zeros((8, LANES), I32)
    left = sel_t
    for j in range(ktop):
        pick = jnp.min(jnp.where(left, blk_i, n_blocks), axis=0, keepdims=True)
        idx_tile = jnp.where(row8 == j, pick, idx_tile)
        left = left & (blk_i != pick)
    idx_v[...] = idx_tile
    idx_copy = pltpu.make_async_copy(idx_v, idx_s, idx_sem)
    idx_copy.start()

    kn = jnp.concatenate([kn_ref[0], jnp.zeros((LANES - DEC_SEQ, ATTN_WIDTH), F32)], axis=0)
    lane_f = lane.astype(F32)
    sn = _dot_nt(qp, kn.astype(BF16)) + slope * (lane_f - lq)
    sn = jnp.where(lane_f <= lq, sn, NEG)

    idx_copy.wait()
    span = ktop * S
    for h in range(N_HEADS):
        for l in range(DEC_SEQ):
            q_i = h * DEC_SEQ + l
            for j in range(ktop):
                n = idx_s[j, q_i]
                for half in range(ppb):
                    page = pt_ref[b, n * ppb + half]
                    col = l * span + j * S + half * PAGE_SIZE
                    pltpu.make_async_copy(cv_ref.at[page, pl.ds(h * HEAD_DIM, HEAD_DIM), :],
                                          vbuf_ref.at[h, :, pl.ds(col, PAGE_SIZE)], vsem).start()
                sc_ref[q_i:q_i + 1, j * S:(j + 1) * S] = s_ref[n, q_i:q_i + 1, :]

    sc = sc_ref[...]
    mx = jnp.maximum(jnp.max(sc, axis=-1, keepdims=True), jnp.max(sn, axis=-1, keepdims=True))
    p = jnp.exp(sc - mx)
    pn = jnp.exp(sn - mx)
    linv = 1.0 / (jnp.sum(p, axis=-1, keepdims=True) + jnp.sum(pn, axis=-1, keepdims=True))
    pw = p * linv
    for h in range(N_HEADS):
        for l in range(DEC_SEQ):
            q_i = h * DEC_SEQ + l
            ph_ref[h, l:l + 1, l * span:(l + 1) * span] = pw[q_i:q_i + 1, :]

    vn = jnp.concatenate([vn_ref[0], jnp.zeros((LANES - DEC_SEQ, ATTN_WIDTH), F32)], axis=0)
    o_new = _dot(pn.astype(BF16), vn.astype(BF16)) * linv
    res = jnp.zeros((DEC_SEQ, ATTN_WIDTH), F32)
    for h in range(N_HEADS):
        res = res + jnp.where(head == h, o_new[h * DEC_SEQ:(h + 1) * DEC_SEQ, :], 0.0)
    res_ref[...] = res


def _moba_sample(page_table, q, kn, vn, slope_col, lq_col, cache_kt, cache_vt):
    P = PAGES_PER_STEP
    chunk = P * PAGE_SIZE
    n_chunks = PAST_LEN // chunk
    n_blocks = PAST_LEN // MOBA_BLOCK
    nq = N_HEADS * DEC_SEQ
    fetched = DEC_SEQ * min(MOBA_TOPK, n_blocks) * MOBA_BLOCK
    row3 = lambda b, pt: (jnp.minimum(b, DEC_BATCH - 1), 0, 0)
    prev3 = lambda b, pt: (jnp.maximum(b - 1, 0), 0, 0)
    const2 = lambda b, pt: (0, 0)
    grid_spec = pltpu.PrefetchScalarGridSpec(
        num_scalar_prefetch=1,
        grid=(DEC_BATCH + 1,),
        in_specs=[pl.BlockSpec((1, DEC_SEQ, ATTN_WIDTH), row3),
                  pl.BlockSpec((1, DEC_SEQ, ATTN_WIDTH), row3),
                  pl.BlockSpec((1, DEC_SEQ, ATTN_WIDTH), row3),
                  pl.BlockSpec((nq, 1), const2),
                  pl.BlockSpec((nq, 1), const2),
                  pl.BlockSpec(memory_space=pl.ANY),
                  pl.BlockSpec(memory_space=pl.ANY)],
        out_specs=pl.BlockSpec((1, DEC_SEQ, ATTN_WIDTH), prev3),
        scratch_shapes=[pltpu.VMEM((CHUNK_SLOTS, P, ATTN_WIDTH, PAGE_SIZE), F32),
                        pltpu.SemaphoreType.DMA((CHUNK_SLOTS,)),
                        pltpu.VMEM((n_blocks, nq, MOBA_BLOCK), F32),
                        pltpu.VMEM((nq, fetched // DEC_SEQ), F32),
                        pltpu.VMEM((ATTN_WIDTH, chunk), BF16),
                        pltpu.VMEM((8, LANES), I32),
                        pltpu.SMEM((8, LANES), I32),
                        pltpu.SemaphoreType.DMA(()),
                        pltpu.VMEM((N_HEADS, HEAD_DIM, fetched), F32),
                        pltpu.SemaphoreType.DMA(()),
                        pltpu.VMEM((LANES, fetched), BF16),
                        pltpu.VMEM((N_HEADS, 2 * DEC_SEQ, fetched), F32),
                        pltpu.VMEM((nq, LANES), F32),
                        pltpu.VMEM((DEC_SEQ, ATTN_WIDTH), F32)])
    return pl.pallas_call(
        _moba_sample_kernel,
        grid_spec=grid_spec,
        out_shape=jax.ShapeDtypeStruct((DEC_BATCH, DEC_SEQ, ATTN_WIDTH), BF16),
        compiler_params=pltpu.CompilerParams(dimension_semantics=("arbitrary",), vmem_limit_bytes=VMEM_LIMIT),
        name="moba_sample",
    )(page_table, q, kn, vn, slope_col, lq_col, cache_kt, cache_vt)


def _mod_rows(ref):
    return ref[...].reshape(-1, ref.shape[-1])


def _outproj_kernel(yp_ref, ya_ref, x_ref, gt1_ref, sh2_ref, sc2_ref, g2_ref, wout_ref, wr_ref, wrhi_ref, br_ref,
                    tri_ref, cntin_ref, x1_ref, h2_ref, meta_ref, ew_ref, cntout_ref, cnt_ref):
    rows = x_ref.shape[0]

    @pl.when(pl.program_id(0) == 0)
    def _():
        cnt_ref[...] = cntin_ref[...]

    mix = _dot(yp_ref[...], wout_ref[0:POOL_WIDTH, :]) + _dot(ya_ref[...], wout_ref[POOL_WIDTH:, :])
    x1 = x_ref[...] + _mod_rows(gt1_ref) * mix
    x1_ref[...] = x1
    h2 = _rms_mod(x1, g2_ref[...], _mod_rows(sc2_ref), _mod_rows(sh2_ref))
    h2_ref[...] = h2
    hh = h2.astype(BF16)
    hl = (h2 - hh.astype(F32)).astype(BF16)
    both = _dot(hh, wr_ref[...])
    lt = (both[:, 0:LANES] + both[:, LANES:] + _dot(hl, wrhi_ref[...])).T + br_ref[...]
    row8 = lax.broadcasted_iota(I32, (8, rows), 0)
    g8 = lt[0:8]
    gmax = jnp.max(g8, axis=0, keepdims=True)
    gsum = jnp.sum(jnp.exp(g8 - gmax), axis=0, keepdims=True)
    g_w = 1.0 / gsum
    g_idx = jnp.min(jnp.where(g8 == gmax, row8, 8), axis=0, keepdims=True)
    e_in = jnp.zeros((8, rows), F32)
    for g in range(N_GROUPS):
        e_in = e_in + jnp.where(g_idx == g, lt[8 + 8 * g:16 + 8 * g], 0.0)
    m1 = jnp.max(e_in, axis=0, keepdims=True)
    i1 = jnp.min(jnp.where(e_in == m1, row8, 8), axis=0, keepdims=True)
    e_rest = jnp.where(row8 == i1, NEG, e_in)
    m2 = jnp.max(e_rest, axis=0, keepdims=True)
    i2 = jnp.min(jnp.where(e_rest == m2, row8, 8), axis=0, keepdims=True)
    r = jnp.exp(m2 - m1)
    w1 = g_w / (1.0 + r)
    w2 = g_w * r / (1.0 + r)
    e1 = g_idx * EXPERTS_PER_GROUP + i1
    e2 = g_idx * EXPERTS_PER_GROUP + i2
    rowe = lax.broadcasted_iota(I32, (N_EXPERTS, rows), 0)
    oh1 = (rowe == e1).astype(F32)
    oh2 = (rowe == e2).astype(F32)
    both_oh = oh1 + oh2
    before = _dot(both_oh.astype(BF16), tri_ref[...]) + cnt_ref[:, 0:1]
    rank1 = jnp.sum(oh1 * before, axis=0, keepdims=True).astype(I32)
    rank2 = jnp.sum(oh2 * before, axis=0, keepdims=True).astype(I32)
    cnt_ref[...] = cnt_ref[...] + jnp.sum(both_oh, axis=-1, keepdims=True)
    cntout_ref[...] = cnt_ref[...]
    meta_ref[...] = jnp.where(row8 == 0, e1, jnp.where(row8 == 1, e2, jnp.where(row8 == 2, rank1,
                                                                                 jnp.where(row8 == 3, rank2, 0))))
    ew_ref[...] = jnp.where(row8 == 0, w1, jnp.where(row8 == 1, w2, 0.0))


def _mod_spec(tile, per_batch_rows):
    if per_batch_rows is None:
        return pl.BlockSpec((tile, D_MODEL), lambda i: (i, 0))
    per = per_batch_rows // tile
    return pl.BlockSpec((1, 1, D_MODEL), lambda i: (i // per, 0, 0))


def _outproj(yp, ya, x, gt1, sh2, sc2, g2, wout, wr, wrhi, br, cnt_in, per_batch_rows):
    n = x.shape[0]
    tile = min(ROW_TILE, n)
    tok = lambda i: (i, 0)
    const2 = lambda i: (0, 0)
    mod_spec = _mod_spec(tile, per_batch_rows)
    tri = (jnp.arange(tile)[:, None] < jnp.arange(tile)[None, :]).astype(BF16)
    return pl.pallas_call(
        _outproj_kernel,
        grid=(n // tile,),
        in_specs=[pl.BlockSpec((tile, POOL_WIDTH), tok),
                  pl.BlockSpec((tile, ATTN_WIDTH), tok),
                  pl.BlockSpec((tile, D_MODEL), tok),
                  mod_spec, mod_spec, mod_spec,
                  pl.BlockSpec((1, D_MODEL), const2),
                  pl.BlockSpec((2 * POOL_WIDTH, D_MODEL), const2),
                  pl.BlockSpec((D_MODEL, 2 * LANES), const2),
                  pl.BlockSpec((D_MODEL, LANES), const2),
                  pl.BlockSpec((LANES, 1), const2),
                  pl.BlockSpec((tile, tile), const2),
                  pl.BlockSpec((N_EXPERTS, LANES), const2)],
        out_specs=[pl.BlockSpec((tile, D_MODEL), tok),
                   pl.BlockSpec((tile, D_MODEL), tok),
                   pl.BlockSpec((8, tile), lambda i: (0, i)),
                   pl.BlockSpec((8, tile), lambda i: (0, i)),
                   pl.BlockSpec((N_EXPERTS, LANES), const2)],
        out_shape=[jax.ShapeDtypeStruct((n, D_MODEL), F32),
                   jax.ShapeDtypeStruct((n, D_MODEL), F32),
                   jax.ShapeDtypeStruct((8, n), I32),
                   jax.ShapeDtypeStruct((8, n), F32),
                   jax.ShapeDtypeStruct((N_EXPERTS, LANES), F32)],
        scratch_shapes=[pltpu.VMEM((N_EXPERTS, LANES), F32)],
        compiler_params=pltpu.CompilerParams(dimension_semantics=("arbitrary",), vmem_limit_bytes=VMEM_LIMIT),
        name="outproj",
    )(yp, ya, x, gt1, sh2, sc2, g2, wout, wr, wrhi, br, tri, cnt_in)


def _dest_kernel(pstart_ref, meta_ref, dest_ref):
    n = meta_ref.shape[1]
    eid = meta_ref[0:EXPERT_TOPK, :]
    start = jnp.zeros((EXPERT_TOPK, n), I32)
    for e in range(N_EXPERTS):
        start = jnp.where(eid == e, pstart_ref[e], start)
    dest_ref[...] = start + meta_ref[EXPERT_TOPK:2 * EXPERT_TOPK, :]


def _dest_rows(pstart, meta):
    n = meta.shape[1]
    dest = pl.pallas_call(
        _dest_kernel,
        in_specs=[pl.BlockSpec(memory_space=pltpu.SMEM), pl.BlockSpec(memory_space=pltpu.VMEM)],
        out_specs=pl.BlockSpec(memory_space=pltpu.VMEM),
        out_shape=jax.ShapeDtypeStruct((EXPERT_TOPK, n), I32),
        name="dest",
    )(pstart, meta)
    return [dest[slot] for slot in range(EXPERT_TOPK)]


DMA_ISSUE_UNROLL = 16


SUBLANES = 8
PAD_CHUNKS = tuple(MOE_ROWS >> s for s in range(1, MOE_ROWS.bit_length() - 3))


def _issue_row_copies(n_tokens, copy_for):
    group = DMA_ISSUE_UNROLL // SUBLANES

    def issue(g, carry):
        for k in range(DMA_ISSUE_UNROLL):
            for slot in range(EXPERT_TOPK):
                copy_for(slot, g * DMA_ISSUE_UNROLL + k, g * group + k // SUBLANES, k % SUBLANES).start()
        return carry

    lax.fori_loop(0, n_tokens // DMA_ISSUE_UNROLL, issue, 0)


def _scatter_rows(dest_refs, h_ref, xr_ref, sem):
    rows = h_ref.shape[0] * SUBLANES

    def copy_for(slot, t, tile, sub):
        return pltpu.make_async_copy(h_ref.at[tile, pl.ds(sub, 1), :],
                                     xr_ref.at[pl.ds(dest_refs[slot][t], 1), :], sem)

    _issue_row_copies(rows, copy_for)
    for _ in dest_refs:
        tile_rows = xr_ref.at[pl.ds(0, rows), :]
        pltpu.make_async_copy(tile_rows, tile_rows, sem).wait()


def _dispatch_kernel(*refs):
    dest_refs = refs[0:EXPERT_TOPK]
    pstart_ref, cnt_ref, nact_ref, hp_ref, hs_ref, xr_ref, z_ref, sem, zsem = refs[EXPERT_TOPK:]
    i = pl.program_id(0)
    last = pl.num_programs(0) - 1

    @pl.when(i == 0)
    def _():
        z_ref[...] = jnp.zeros_like(z_ref)
        _zero_unassigned_rows(pstart_ref, cnt_ref, nact_ref, xr_ref, z_ref, zsem, lambda c: c.start())

    @pl.when(i < last)
    def _():
        _scatter_rows(dest_refs, hp_ref, xr_ref, sem)

    @pl.when(i == last)
    def _():
        _scatter_rows(dest_refs, hs_ref, xr_ref, sem)
        _zero_unassigned_rows(pstart_ref, cnt_ref, nact_ref, xr_ref, z_ref, zsem, lambda c: c.wait())


def _dispatch_rows(dests, pstart, counts, n_active, h2_p, h2_s, n_rows):
    n_p = h2_p.shape[0]
    assert n_p % ROW_TILE == 0 and h2_s.shape[0] <= ROW_TILE
    p_tiles = n_p // ROW_TILE
    smem_rows = pl.BlockSpec((ROW_TILE,), lambda i: (i,), memory_space=pltpu.SMEM)
    smem = pl.BlockSpec(memory_space=pltpu.SMEM)
    dests = [jnp.pad(d, (0, (p_tiles + 1) * ROW_TILE - d.shape[0])) for d in dests]
    h2_p = h2_p.reshape(-1, SUBLANES, D_MODEL)
    h2_s = h2_s.reshape(-1, SUBLANES, D_MODEL)
    return pl.pallas_call(
        _dispatch_kernel,
        grid=(p_tiles + 1,),
        in_specs=[smem_rows] * EXPERT_TOPK + [
            smem, smem, smem,
            pl.BlockSpec((ROW_TILE // SUBLANES, SUBLANES, D_MODEL), lambda i: (jnp.minimum(i, p_tiles - 1), 0, 0)),
            pl.BlockSpec(h2_s.shape, lambda i: (0, 0, 0))],
        out_specs=pl.BlockSpec(memory_space=pl.ANY),
        out_shape=jax.ShapeDtypeStruct((n_rows, D_MODEL), F32),
        scratch_shapes=[pltpu.VMEM((MOE_ROWS, D_MODEL), F32),
                        pltpu.SemaphoreType.DMA(()),
                        pltpu.SemaphoreType.DMA(())],
        compiler_params=pltpu.CompilerParams(dimension_semantics=("arbitrary",), vmem_limit_bytes=VMEM_LIMIT),
        name="dispatch",
    )(*dests, pstart, counts, n_active, h2_p, h2_s)


def _zero_unassigned_rows(pstart_ref, cnt_ref, nact_ref, xr_ref, z_ref, sem, act):
    assert MOE_ROWS - 1 == SUBLANES - 1 + sum(PAD_CHUNKS)
    n_blocks = xr_ref.shape[0] // MOE_ROWS

    def trailing_block(blk, carry):
        dst = xr_ref.at[pl.ds(pl.multiple_of(blk * MOE_ROWS, MOE_ROWS), MOE_ROWS), :]
        act(pltpu.make_async_copy(z_ref, dst, sem))
        return carry

    lax.fori_loop(nact_ref[0], n_blocks, trailing_block, 0)
    for e in range(N_EXPERTS):
        cnt = cnt_ref[e]
        first = pstart_ref[e] + cnt
        n_pad = (MOE_ROWS - (cnt & (MOE_ROWS - 1))) & (MOE_ROWS - 1)
        head = (SUBLANES - (first & (SUBLANES - 1))) & (SUBLANES - 1)
        head = jnp.minimum(head, n_pad)
        for k in range(SUBLANES - 1):
            @pl.when(k < head)
            def _():
                act(pltpu.make_async_copy(z_ref.at[pl.ds(0, 1), :], xr_ref.at[pl.ds(first + k, 1), :], sem))
        start = first + head
        body = n_pad - head
        for rows in PAD_CHUNKS:
            @pl.when((body & rows) != 0)
            def _():
                dst = xr_ref.at[pl.ds(pl.multiple_of(start, SUBLANES), rows), :]
                act(pltpu.make_async_copy(z_ref.at[pl.ds(0, rows), :], dst, sem))
            start = start + (body & rows)


def _moe_kernel(blk_e_ref, nact_ref, x_ref, wg_ref, wu_ref, wd_ref, y_ref, wgb_ref, wub_ref, wdb_ref):
    i = pl.program_id(0)
    prev = blk_e_ref[jnp.maximum(i - 1, 0)]
    active = i < nact_ref[0]

    @pl.when(active & ((i == 0) | (blk_e_ref[i] != prev)))
    def _():
        wgb_ref[...] = wg_ref[0].astype(BF16)
        wub_ref[...] = wu_ref[0].astype(BF16)
        wdb_ref[...] = wd_ref[0].astype(BF16)

    @pl.when(active)
    def _():
        x = x_ref[...].astype(BF16)
        g = _dot(x, wgb_ref[...])
        u = _dot(x, wub_ref[...])
        a = g / (1.0 + jnp.exp(-g)) * u
        y_ref[...] = _dot(a.astype(BF16), wdb_ref[...])

    @pl.when(jnp.logical_not(active))
    def _():
        y_ref[...] = jnp.zeros_like(y_ref)


def _moe(blk_e, n_active, xr, w_gate, w_up, w_down):
    n_rows = xr.shape[0]
    n_blocks = n_rows // MOE_ROWS
    grid_spec = pltpu.PrefetchScalarGridSpec(
        num_scalar_prefetch=2,
        grid=(n_blocks,),
        in_specs=[pl.BlockSpec((MOE_ROWS, D_MODEL), lambda i, be, na: (jnp.minimum(i, na[0] - 1), 0)),
                  pl.BlockSpec((1, D_MODEL, D_FF), lambda i, be, na: (be[i], 0, 0)),
                  pl.BlockSpec((1, D_MODEL, D_FF), lambda i, be, na: (be[i], 0, 0)),
                  pl.BlockSpec((1, D_FF, D_MODEL), lambda i, be, na: (be[i], 0, 0))],
        out_specs=pl.BlockSpec((MOE_ROWS, D_MODEL), lambda i, be, na: (i, 0)),
        scratch_shapes=[pltpu.VMEM((D_MODEL, D_FF), BF16),
                        pltpu.VMEM((D_MODEL, D_FF), BF16),
                        pltpu.VMEM((D_FF, D_MODEL), BF16)])
    return pl.pallas_call(
        _moe_kernel,
        grid_spec=grid_spec,
        out_shape=jax.ShapeDtypeStruct((n_rows, D_MODEL), F32),
        compiler_params=pltpu.CompilerParams(
            dimension_semantics=("arbitrary",), vmem_limit_bytes=VMEM_LIMIT),
        name="moe",
    )(blk_e, n_active, xr, w_gate, w_up, w_down)


def _final_kernel(*refs):
    dest_refs = refs[0:EXPERT_TOPK]
    x1_ref, w_ref, gt2_ref, yr_ref, y_ref, o_ref, sem = refs[EXPERT_TOPK:]
    rows = x1_ref.shape[0]

    def copy_for(slot, t, tile, sub):
        return pltpu.make_async_copy(yr_ref.at[pl.ds(dest_refs[slot][t], 1), :],
                                     o_ref.at[slot, tile, pl.ds(sub, 1), :], sem)

    _issue_row_copies(rows, copy_for)
    for slot in range(EXPERT_TOPK):
        pltpu.make_async_copy(o_ref.at[slot], o_ref.at[slot], sem).wait()
    w = w_ref[...]
    o = [o_ref[slot].reshape(rows, D_MODEL) for slot in range(EXPERT_TOPK)]
    moe = o[0] * w[:, 0:1] + o[1] * w[:, 1:2]
    y_ref[...] = x1_ref[...] + _mod_rows(gt2_ref) * moe


def _final(dests, x1, w2, gt2, yr, per_batch_rows):
    n = x1.shape[0]
    tile = min(COMBINE_ROWS, n)
    tok = lambda i: (i, 0)
    smem_rows = pl.BlockSpec((tile,), lambda i: (i,), memory_space=pltpu.SMEM)
    return pl.pallas_call(
        _final_kernel,
        grid=(n // tile,),
        in_specs=[smem_rows] * EXPERT_TOPK + [pl.BlockSpec((tile, D_MODEL), tok),
                                             pl.BlockSpec((tile, EXPERT_TOPK), tok),
                                             _mod_spec(tile, per_batch_rows),
                                             pl.BlockSpec(memory_space=pl.ANY)],
        out_specs=pl.BlockSpec((tile, D_MODEL), tok),
        out_shape=jax.ShapeDtypeStruct((n, D_MODEL), F32),
        scratch_shapes=[pltpu.VMEM((EXPERT_TOPK, tile // SUBLANES, SUBLANES, D_MODEL), F32),
                        pltpu.SemaphoreType.DMA(())],
        compiler_params=pltpu.CompilerParams(dimension_semantics=("arbitrary",), vmem_limit_bytes=VMEM_LIMIT),
        name="final",
    )(*dests, x1, w2, gt2, yr)


def _expert_layout(counts, n_blocks):
    padded = (counts + MOE_ROWS - 1) // MOE_ROWS * MOE_ROWS
    pend = jnp.cumsum(padded)
    pstart = (pend - padded).astype(I32)
    blk_start = jnp.arange(n_blocks, dtype=I32) * MOE_ROWS
    blk_e = jnp.minimum(jnp.sum((pend[None, :] <= blk_start[:, None]).astype(I32), axis=1), N_EXPERTS - 1)
    n_active = (pend[-1] // MOE_ROWS).astype(I32).reshape(1)
    return pstart, blk_e.astype(I32), n_active


def kernel(x_prompt, x_sample, cache_k, cache_v, state_pool, page_table, c_prompt, c_sample, w_ada, b_ada,
           g_attn_norm, w_in, g_q, g_k, w_pool, pool_scale, w_out, g_ffn_norm, w_group, b_group, w_expert,
           b_expert, w_gate, w_up, w_down):
    D = D_MODEL
    B, T, _ = x_prompt.shape
    n_s = DEC_BATCH * DEC_SEQ
    n_p = B * T
    layer = 0

    win = w_in[layer].astype(BF16)
    wout = w_out[layer].astype(BF16)
    wpool = w_pool[layer].astype(BF16)
    g1 = g_attn_norm[layer].reshape(1, D)
    g2 = g_ffn_norm[layer].reshape(1, D)
    gq = jnp.tile(g_q[layer], N_HEADS).reshape(1, ATTN_WIDTH)
    gk = jnp.tile(g_k[layer], N_HEADS).reshape(1, ATTN_WIDTH)
    ps = pool_scale[layer].reshape(1, POOL_WIDTH)
    hd = jnp.arange(ATTN_WIDTH) // HEAD_DIM
    bd = (hd[:, None] == hd[None, :]).astype(BF16)
    slopes = jnp.exp2(-8.0 * (jnp.arange(N_HEADS, dtype=F32) + 1.0) / N_HEADS)
    qrow = jnp.arange(N_HEADS * DEC_SEQ)
    slope_col = slopes[qrow // DEC_SEQ].reshape(-1, 1)
    lq_col = (qrow % DEC_SEQ).astype(F32).reshape(-1, 1)
    wr = jnp.zeros((D, LANES), F32).at[:, 0:N_GROUPS].set(w_group[layer]).at[:, 8:8 + N_EXPERTS].set(w_expert[layer])
    wr_hi = wr.astype(BF16)
    wr_lo = (wr - wr_hi.astype(F32)).astype(BF16)
    wr_both = jnp.concatenate([wr_hi, wr_lo], axis=1)
    br = jnp.zeros((LANES,), F32).at[0:N_GROUPS].set(b_group[layer]).at[N_GROUPS:8].set(NEG)
    br = br.at[8:8 + N_EXPERTS].set(b_expert[layer]).reshape(LANES, 1)

    mod = _ada(jnp.concatenate([c_prompt, c_sample], axis=0), w_ada[layer], b_ada[layer])
    mod_p = mod[:B].reshape(B, 1, 6 * D)
    mod_s = jnp.repeat(mod[B:], DEC_SEQ, axis=0)

    ypool_p, q_p, k_p, v_p, tail_p = _inproj_prompt(x_prompt, mod_p, g1, win, bd, gq, gk, wpool, ps)
    yattn_p = _moba_prompt(slopes, q_p, k_p, v_p)
    mp = lambda j: mod_p[:, :, j * D:(j + 1) * D]
    cnt0 = jnp.zeros((N_EXPERTS, LANES), F32)
    x1_p, h2_p, meta_p, ew_p, cnt_p = _outproj(
        ypool_p.reshape(n_p, POOL_WIDTH), yattn_p.reshape(n_p, ATTN_WIDTH), x_prompt.reshape(n_p, D),
        mp(2), mp(3), mp(4), g2, wout, wr_both, wr_hi, br, cnt0, T)

    ms = lambda j: mod_s[:, j * D:(j + 1) * D]
    hist = jnp.concatenate([jnp.zeros((DEC_BATCH, HALO - POOL_HIST, POOL_WIDTH), F32), state_pool[layer]], axis=1)
    xs = x_sample.reshape(n_s, D)
    ypool_s, q_s, k_s, v_s, tail_s = _inproj_sample(xs, ms(0), ms(1), g1, win, bd, gq, gk, hist, wpool, ps)
    r3 = lambda a: a.reshape(DEC_BATCH, DEC_SEQ, ATTN_WIDTH)
    n_phys = cache_k.shape[1]
    ck = jnp.transpose(cache_k[layer], (0, 2, 3, 1)).reshape(n_phys, ATTN_WIDTH, PAGE_SIZE)
    cv = jnp.transpose(cache_v[layer], (0, 2, 3, 1)).reshape(n_phys, ATTN_WIDTH, PAGE_SIZE)
    yattn_s = _moba_sample(page_table, r3(q_s), r3(k_s), r3(v_s), slope_col, lq_col, ck, cv)
    x1_s, h2_s, meta_s, ew_s, cnt_s = _outproj(ypool_s, yattn_s.reshape(n_s, ATTN_WIDTH), xs, ms(2), ms(3), ms(4),
                                               g2, wout, wr_both, wr_hi, br, cnt_p, None)

    n_blocks = (n_p + n_s) * EXPERT_TOPK // MOE_ROWS + N_EXPERTS
    counts = cnt_s[:, 0].astype(I32)
    pstart, blk_e, n_active = _expert_layout(counts, n_blocks)
    dests_p = _dest_rows(pstart, meta_p)
    dests_s = _dest_rows(pstart, meta_s)
    dests = [jnp.concatenate([dp, ds]) for dp, ds in zip(dests_p, dests_s)]
    xr = _dispatch_rows(dests, pstart, counts, n_active, h2_p, h2_s, n_blocks * MOE_ROWS)
    yr = _moe(blk_e, n_active, xr, w_gate[layer], w_up[layer], w_down[layer])
    y_p = _final(dests_p, x1_p, ew_p[0:EXPERT_TOPK].T, mp(5), yr, T)
    y_s = _final(dests_s, x1_s, ew_s[0:EXPERT_TOPK].T, ms(5), yr, None)

    k4 = lambda a, b, l: a.reshape(1, b, l, N_HEADS, HEAD_DIM)
    return (y_p.reshape(B, T, D), y_s.reshape(DEC_BATCH, DEC_SEQ, D),
            k4(k_p, B, T), k4(v_p, B, T), tail_p[None, :, HALO - POOL_HIST:, :],
            k4(k_s, DEC_BATCH, DEC_SEQ), k4(v_s, DEC_BATCH, DEC_SEQ), tail_s[None, :, HALO - POOL_HIST:, :])
```
